```python
import jax, jax.numpy as jnp
from jax import lax
import numpy as np

D_MODEL = 1024
BATCH = 8
SEQ = 16384
DEPTH = 4

D_MIX = 1024
MLA_HEADS = 8
QK_NOPE = 64
QK_ROPE = 32
QK_HEAD = QK_NOPE + QK_ROPE
V_HEAD = 64
Q_LORA = 384
KV_LORA = 256
MLA_WIDTH = MLA_HEADS * V_HEAD
ROPE_THETA = 10000.0
Q_BLOCK = 128
POOL_WINDOWS = (2, 4, 8, 16)
POOL_GROUPS = len(POOL_WINDOWS)
POOL_WIDTH = D_MIX - MLA_WIDTH
POOL_GROUP_DIM = POOL_WIDTH // POOL_GROUPS
D_IN = Q_LORA + KV_LORA + QK_ROPE + POOL_WIDTH
D_FF = 2816
EPS = 1e-6

kernel_name = "hybrid_mla_pool_macaron_encoder"


def rmsnorm(x, g):
    xf = x.astype(jnp.float32)
    y = xf * lax.rsqrt(jnp.mean(xf * xf, axis=-1, keepdims=True) + EPS)
    return (y * g.astype(jnp.float32)).astype(x.dtype)


def swiglu(h, w_gu, w_down):
    gu = h @ w_gu
    g, u = gu[..., :D_FF], gu[..., D_FF:]
    return (jax.nn.silu(g) * u) @ w_down


def rope_tables(seq):
    pos = jnp.arange(seq, dtype=jnp.float32)
    inv = ROPE_THETA ** (-jnp.arange(0, QK_ROPE, 2, dtype=jnp.float32) / QK_ROPE)
    ang = pos[:, None] * inv[None, :]
    return jnp.cos(ang), jnp.sin(ang)


def apply_rope(t, cos, sin):
    tf = t.astype(jnp.float32)
    half = QK_ROPE // 2
    t1, t2 = tf[..., :half], tf[..., half:]
    c, s = cos[None, :, None, :], sin[None, :, None, :]
    out = jnp.concatenate([t1 * c - t2 * s, t2 * c + t1 * s], axis=-1)
    return out.astype(t.dtype)


def bidir_attention(q, k, v):
    B, S, H, Dq = q.shape
    nb = S // Q_BLOCK
    qb = jnp.moveaxis((q * (Dq ** -0.5)).reshape(B, nb, Q_BLOCK, H, Dq), 1, 0)

    def block(q_blk):
        s = jnp.einsum('bqhd,bkhd->bhqk', q_blk, k, preferred_element_type=jnp.float32)
        p = jax.nn.softmax(s, axis=-1).astype(v.dtype)
        return jnp.einsum('bhqk,bkhd->bqhd', p, v)

    o = lax.map(block, qb)
    return jnp.moveaxis(o, 0, 1).reshape(B, S, H * V_HEAD)


def mla_mixer(c_q, c_kv, k_pe, q_lat_norm, kv_lat_norm, w_uq, w_uk, w_uv, q_norm, k_norm, cos, sin):
    B, S, _ = c_q.shape
    c_q = rmsnorm(c_q, q_lat_norm)
    c_kv = rmsnorm(c_kv, kv_lat_norm)
    q = (c_q @ w_uq).reshape(B, S, MLA_HEADS, QK_HEAD)
    k_nope = (c_kv @ w_uk).reshape(B, S, MLA_HEADS, QK_NOPE)
    v = (c_kv @ w_uv).reshape(B, S, MLA_HEADS, V_HEAD)
    k_pe_h = jnp.broadcast_to(k_pe[:, :, None, :], (B, S, MLA_HEADS, QK_ROPE))
    k = jnp.concatenate([k_nope, k_pe_h], axis=-1)
    q = rmsnorm(q, q_norm)
    k = rmsnorm(k, k_norm)
    q = jnp.concatenate([q[..., :QK_NOPE], apply_rope(q[..., QK_NOPE:], cos, sin)], axis=-1)
    k = jnp.concatenate([k[..., :QK_NOPE], apply_rope(k[..., QK_NOPE:], cos, sin)], axis=-1)
    return bidir_attention(q, k, v)


def pool_mixer(p, w_pool, pool_scale):
    B, S, C = p.shape
    pf = p.astype(jnp.float32)
    cs = jnp.concatenate([jnp.zeros((B, 1, C), jnp.float32), jnp.cumsum(pf, axis=1)], axis=1)
    idx = jnp.arange(S)
    outs = []
    for g, w in enumerate(POOL_WINDOWS):
        left = w // 2
        right = w - 1 - left
        lo = jnp.clip(idx - left, 0, S)
        hi = jnp.clip(idx + right + 1, 0, S)
        csg = cs[..., g * POOL_GROUP_DIM:(g + 1) * POOL_GROUP_DIM]
        wsum = jnp.take(csg, hi, axis=1) - jnp.take(csg, lo, axis=1)
        cnt = (hi - lo).astype(jnp.float32)[None, :, None]
        outs.append(wsum / cnt)
    pooled = jnp.stack(outs, axis=2)
    mixed = (pooled - pf.reshape(B, S, POOL_GROUPS, POOL_GROUP_DIM)).astype(p.dtype)
    y = jnp.einsum('bsgc,gcd->bsgd', mixed, w_pool).reshape(B, S, C)
    return y * pool_scale


def _fwd_setup_inputs(seed: int = 0) -> dict:
    key = jax.random.key(seed)
    ks = jax.random.split(key, 20)
    f32 = jnp.float32

    def w(k, shape, fan_in):
        return jax.random.normal(k, shape, f32) * (fan_in ** -0.5)

    def gain(k, shape):
        return 1.0 + 0.02 * jax.random.normal(k, shape, f32)

    return {
        "x": jax.random.normal(ks[0], (BATCH, SEQ, D_MODEL), f32),
        "ffn1_norm": gain(ks[1], (DEPTH, D_MODEL)),
        "ffn1_w_gu": w(ks[2], (DEPTH, D_MODEL, 2 * D_FF), D_MODEL),
        "ffn1_w_down": w(ks[3], (DEPTH, D_FF, D_MODEL), D_FF),
        "mix_norm": gain(ks[4], (DEPTH, D_MODEL)),
        "w_in": w(ks[5], (DEPTH, D_MODEL, D_IN), D_MODEL),
        "q_lat_norm": gain(ks[6], (DEPTH, Q_LORA)),
        "kv_lat_norm": gain(ks[7], (DEPTH, KV_LORA)),
        "w_uq": w(ks[8], (DEPTH, Q_LORA, MLA_HEADS * QK_HEAD), Q_LORA),
        "w_uk": w(ks[9], (DEPTH, KV_LORA, MLA_HEADS * QK_NOPE), KV_LORA),
        "w_uv": w(ks[10], (DEPTH, KV_LORA, MLA_HEADS * V_HEAD), KV_LORA),
        "q_norm": gain(ks[11], (DEPTH, QK_HEAD)),
        "k_norm": gain(ks[12], (DEPTH, QK_HEAD)),
        "w_pool": w(ks[13], (DEPTH, POOL_GROUPS, POOL_GROUP_DIM, POOL_GROUP_DIM), POOL_GROUP_DIM),
        "pool_scale": gain(ks[14], (DEPTH, POOL_WIDTH)),
        "w_out": w(ks[15], (DEPTH, D_MIX, D_MODEL), D_MIX),
        "ffn2_norm": gain(ks[16], (DEPTH, D_MODEL)),
        "ffn2_w_gu": w(ks[17], (DEPTH, D_MODEL, 2 * D_FF), D_MODEL),
        "ffn2_w_down": w(ks[18], (DEPTH, D_FF, D_MODEL), D_FF),
    }


def _fwd_reference(x, ffn1_norm, ffn1_w_gu, ffn1_w_down, mix_norm, w_in, q_lat_norm, kv_lat_norm,
              w_uq, w_uk, w_uv, q_norm, k_norm, w_pool, pool_scale, w_out,
              ffn2_norm, ffn2_w_gu, ffn2_w_down):
    S = x.shape[1]
    cos, sin = rope_tables(S)
    o_kv = Q_LORA
    o_pe = Q_LORA + KV_LORA
    o_pool = Q_LORA + KV_LORA + QK_ROPE
    for l in range(DEPTH):
        x = x + 0.5 * swiglu(rmsnorm(x, ffn1_norm[l]), ffn1_w_gu[l], ffn1_w_down[l])
        z = rmsnorm(x, mix_norm[l]) @ w_in[l]
        a = mla_mixer(z[..., :o_kv], z[..., o_kv:o_pe], z[..., o_pe:o_pool],
                      q_lat_norm[l], kv_lat_norm[l], w_uq[l], w_uk[l], w_uv[l],
                      q_norm[l], k_norm[l], cos, sin)
        b = pool_mixer(z[..., o_pool:], w_pool[l], pool_scale[l])
        x = x + jnp.concatenate([a, b], axis=-1) @ w_out[l]
        x = x + 0.5 * swiglu(rmsnorm(x, ffn2_norm[l]), ffn2_w_gu[l], ffn2_w_down[l])
    return x


import jax as _jax
import jax.numpy as _jnp

TWIN_FORMAT = 'train_step'
FWD_PARAMS = ['x', 'ffn1_norm', 'ffn1_w_gu', 'ffn1_w_down', 'mix_norm', 'w_in', 'q_lat_norm', 'kv_lat_norm', 'w_uq', 'w_uk', 'w_uv', 'q_norm', 'k_norm', 'w_pool', 'pool_scale', 'w_out', 'ffn2_norm', 'ffn2_w_gu', 'ffn2_w_down']
TWIN_WEIGHTS = ['ffn1_norm', 'ffn1_w_gu', 'ffn1_w_down', 'mix_norm', 'w_in', 'q_lat_norm', 'kv_lat_norm', 'w_uq', 'w_uk', 'w_uv', 'q_norm', 'k_norm', 'w_pool', 'pool_scale', 'w_out', 'ffn2_norm', 'ffn2_w_gu', 'ffn2_w_down']
TWIN_DIFF_INPUT = 'x'
TWIN_INPUTS = ['x', 'ffn1_norm', 'ffn1_w_gu', 'ffn1_w_down', 'mix_norm', 'w_in', 'q_lat_norm', 'kv_lat_norm', 'w_uq', 'w_uk', 'w_uv', 'q_norm', 'k_norm', 'w_pool', 'pool_scale', 'w_out', 'ffn2_norm', 'ffn2_w_gu', 'ffn2_w_down', 'loss_target', 'm_ffn1_norm', 'm_ffn1_w_gu', 'm_ffn1_w_down', 'm_mix_norm', 'm_w_in', 'm_q_lat_norm', 'm_kv_lat_norm', 'm_w_uq', 'm_w_uk', 'm_w_uv', 'm_q_norm', 'm_k_norm', 'm_w_pool', 'm_pool_scale', 'm_w_out', 'm_ffn2_norm', 'm_ffn2_w_gu', 'm_ffn2_w_down', 'v_ffn1_norm', 'v_ffn1_w_gu', 'v_ffn1_w_down', 'v_mix_norm', 'v_w_in', 'v_q_lat_norm', 'v_kv_lat_norm', 'v_w_uq', 'v_w_uk', 'v_w_uv', 'v_q_norm', 'v_k_norm', 'v_w_pool', 'v_pool_scale', 'v_w_out', 'v_ffn2_norm', 'v_ffn2_w_gu', 'v_ffn2_w_down']
TWIN_OUTPUTS = ['loss', 'grad_x', 'grad_ffn1_norm', 'grad_ffn1_w_gu', 'grad_ffn1_w_down', 'grad_mix_norm', 'grad_w_in', 'grad_q_lat_norm', 'grad_kv_lat_norm', 'grad_w_uq', 'grad_w_uk', 'grad_w_uv', 'grad_q_norm', 'grad_k_norm', 'grad_w_pool', 'grad_pool_scale', 'grad_w_out', 'grad_ffn2_norm', 'grad_ffn2_w_gu', 'grad_ffn2_w_down', 'delta_ffn1_norm', 'delta_ffn1_w_gu', 'delta_ffn1_w_down', 'delta_mix_norm', 'delta_w_in', 'delta_q_lat_norm', 'delta_kv_lat_norm', 'delta_w_uq', 'delta_w_uk', 'delta_w_uv', 'delta_q_norm', 'delta_k_norm', 'delta_w_pool', 'delta_pool_scale', 'delta_w_out', 'delta_ffn2_norm', 'delta_ffn2_w_gu', 'delta_ffn2_w_down', 'new_m_ffn1_norm', 'new_m_ffn1_w_gu', 'new_m_ffn1_w_down', 'new_m_mix_norm', 'new_m_w_in', 'new_m_q_lat_norm', 'new_m_kv_lat_norm', 'new_m_w_uq', 'new_m_w_uk', 'new_m_w_uv', 'new_m_q_norm', 'new_m_k_norm', 'new_m_w_pool', 'new_m_pool_scale', 'new_m_w_out', 'new_m_ffn2_norm', 'new_m_ffn2_w_gu', 'new_m_ffn2_w_down', 'new_v_ffn1_norm', 'new_v_ffn1_w_gu', 'new_v_ffn1_w_down', 'new_v_mix_norm', 'new_v_w_in', 'new_v_q_lat_norm', 'new_v_kv_lat_norm', 'new_v_w_uq', 'new_v_w_uk', 'new_v_w_uv', 'new_v_q_norm', 'new_v_k_norm', 'new_v_w_pool', 'new_v_pool_scale', 'new_v_w_out', 'new_v_ffn2_norm', 'new_v_ffn2_w_gu', 'new_v_ffn2_w_down']
TWIN_LEAF_KINDS = {'loss': 'loss', 'grad_x': 'grad_x', 'grad_ffn1_norm': 'grad_w', 'grad_ffn1_w_gu': 'grad_w', 'grad_ffn1_w_down': 'grad_w', 'grad_mix_norm': 'grad_w', 'grad_w_in': 'grad_w', 'grad_q_lat_norm': 'grad_w', 'grad_kv_lat_norm': 'grad_w', 'grad_w_uq': 'grad_w', 'grad_w_uk': 'grad_w', 'grad_w_uv': 'grad_w', 'grad_q_norm': 'grad_w', 'grad_k_norm': 'grad_w', 'grad_w_pool': 'grad_w', 'grad_pool_scale': 'grad_w', 'grad_w_out': 'grad_w', 'grad_ffn2_norm': 'grad_w', 'grad_ffn2_w_gu': 'grad_w', 'grad_ffn2_w_down': 'grad_w', 'delta_ffn1_norm': 'delta_w', 'delta_ffn1_w_gu': 'delta_w', 'delta_ffn1_w_down': 'delta_w', 'delta_mix_norm': 'delta_w', 'delta_w_in': 'delta_w', 'delta_q_lat_norm': 'delta_w', 'delta_kv_lat_norm': 'delta_w', 'delta_w_uq': 'delta_w', 'delta_w_uk': 'delta_w', 'delta_w_uv': 'delta_w', 'delta_q_norm': 'delta_w', 'delta_k_norm': 'delta_w', 'delta_w_pool': 'delta_w', 'delta_pool_scale': 'delta_w', 'delta_w_out': 'delta_w', 'delta_ffn2_norm': 'delta_w', 'delta_ffn2_w_gu': 'delta_w', 'delta_ffn2_w_down': 'delta_w', 'new_m_ffn1_norm': 'new_m', 'new_m_ffn1_w_gu': 'new_m', 'new_m_ffn1_w_down': 'new_m', 'new_m_mix_norm': 'new_m', 'new_m_w_in': 'new_m', 'new_m_q_lat_norm': 'new_m', 'new_m_kv_lat_norm': 'new_m', 'new_m_w_uq': 'new_m', 'new_m_w_uk': 'new_m', 'new_m_w_uv': 'new_m', 'new_m_q_norm': 'new_m', 'new_m_k_norm': 'new_m', 'new_m_w_pool': 'new_m', 'new_m_pool_scale': 'new_m', 'new_m_w_out': 'new_m', 'new_m_ffn2_norm': 'new_m', 'new_m_ffn2_w_gu': 'new_m', 'new_m_ffn2_w_down': 'new_m', 'new_v_ffn1_norm': 'new_v', 'new_v_ffn1_w_gu': 'new_v', 'new_v_ffn1_w_down': 'new_v', 'new_v_mix_norm': 'new_v', 'new_v_w_in': 'new_v', 'new_v_q_lat_norm': 'new_v', 'new_v_kv_lat_norm': 'new_v', 'new_v_w_uq': 'new_v', 'new_v_w_uk': 'new_v', 'new_v_w_uv': 'new_v', 'new_v_q_norm': 'new_v', 'new_v_k_norm': 'new_v', 'new_v_w_pool': 'new_v', 'new_v_pool_scale': 'new_v', 'new_v_w_out': 'new_v', 'new_v_ffn2_norm': 'new_v', 'new_v_ffn2_w_gu': 'new_v', 'new_v_ffn2_w_down': 'new_v'}


def _forward(args):
    return _fwd_reference(*[args[k] for k in FWD_PARAMS])


def _output_shape():
    def fwd():
        inp = _fwd_setup_inputs(0)
        return _fwd_reference(*[inp[k] for k in FWD_PARAMS])
    out = _jax.eval_shape(fwd)
    return out.shape, out.dtype

N_MICROBATCH = 1
ADAM_LR = 0.001
ADAM_B1 = 0.9
ADAM_B2 = 0.999
ADAM_EPS = 1e-08
ADAM_WD = 0.01
ADAM_STEP = 10
PER_EXAMPLE_BATCH_AXIS = {'x': 0, 'loss_target': 0}
SHARED_INPUTS = []
_WEIGHT_DTYPES = {'ffn1_norm': _jnp.float32, 'ffn1_w_gu': _jnp.float32, 'ffn1_w_down': _jnp.float32, 'mix_norm': _jnp.float32, 'w_in': _jnp.float32, 'q_lat_norm': _jnp.float32, 'kv_lat_norm': _jnp.float32, 'w_uq': _jnp.float32, 'w_uk': _jnp.float32, 'w_uv': _jnp.float32, 'q_norm': _jnp.float32, 'k_norm': _jnp.float32, 'w_pool': _jnp.float32, 'pool_scale': _jnp.float32, 'w_out': _jnp.float32, 'ffn2_norm': _jnp.float32, 'ffn2_w_gu': _jnp.float32, 'ffn2_w_down': _jnp.float32}
MOMENT_SCALE = {'ffn1_norm': 2.357146e+01, 'ffn1_w_gu': 3.289032e-01, 'ffn1_w_down': 6.995685e-01, 'mix_norm': 5.286975e+01, 'w_in': 5.354367e+00, 'q_lat_norm': 2.928116e-01, 'kv_lat_norm': 8.220088e-01, 'w_uq': 2.050722e-01, 'w_uk': 2.888514e-01, 'w_uv': 2.457744e-01, 'q_norm': 1.257169e+00, 'k_norm': 1.255400e+00, 'w_pool': 1.224090e+01, 'pool_scale': 1.083557e+02, 'w_out': 6.241816e+00, 'ffn2_norm': 2.448289e+01, 'ffn2_w_gu': 3.208535e-01, 'ffn2_w_down': 7.072934e-01}


def _to_microbatches(a, axis):
    t = _jnp.moveaxis(a, axis, 0)
    t = t.reshape((N_MICROBATCH, t.shape[0] // N_MICROBATCH) + t.shape[1:])
    return _jnp.moveaxis(t, 1, axis + 1)


def setup_inputs(seed: int = 0) -> dict:
    inp = _fwd_setup_inputs(seed)
    key = _jax.random.fold_in(_jax.random.key(seed), 7919)
    shape, _ = _output_shape()
    out = dict(inp)
    out["loss_target"] = _jax.random.normal(_jax.random.fold_in(key, 0), shape, _jnp.float32)
    for i, name in enumerate(TWIN_WEIGHTS):
        w = inp[name].astype(_jnp.float32)
        if MOMENT_SCALE is None:
            s = _jnp.sqrt(_jnp.mean(_jnp.square(w)) + 1e-30)
        else:
            s = MOMENT_SCALE[name]
        km, kv = _jax.random.split(_jax.random.fold_in(key, i + 1))
        out[name] = w
        out["m_" + name] = s * _jax.random.normal(km, w.shape, _jnp.float32)
        out["v_" + name] = (s * s) * _jax.random.uniform(kv, w.shape, _jnp.float32, 0.5, 1.5)
    if N_MICROBATCH > 1:
        for name, axis in PER_EXAMPLE_BATCH_AXIS.items():
            out[name] = _to_microbatches(out[name], axis)
    return {'x': out['x'], 'ffn1_norm': out['ffn1_norm'], 'ffn1_w_gu': out['ffn1_w_gu'], 'ffn1_w_down': out['ffn1_w_down'], 'mix_norm': out['mix_norm'], 'w_in': out['w_in'], 'q_lat_norm': out['q_lat_norm'], 'kv_lat_norm': out['kv_lat_norm'], 'w_uq': out['w_uq'], 'w_uk': out['w_uk'], 'w_uv': out['w_uv'], 'q_norm': out['q_norm'], 'k_norm': out['k_norm'], 'w_pool': out['w_pool'], 'pool_scale': out['pool_scale'], 'w_out': out['w_out'], 'ffn2_norm': out['ffn2_norm'], 'ffn2_w_gu': out['ffn2_w_gu'], 'ffn2_w_down': out['ffn2_w_down'], 'loss_target': out['loss_target'], 'm_ffn1_norm': out['m_ffn1_norm'], 'm_ffn1_w_gu': out['m_ffn1_w_gu'], 'm_ffn1_w_down': out['m_ffn1_w_down'], 'm_mix_norm': out['m_mix_norm'], 'm_w_in': out['m_w_in'], 'm_q_lat_norm': out['m_q_lat_norm'], 'm_kv_lat_norm': out['m_kv_lat_norm'], 'm_w_uq': out['m_w_uq'], 'm_w_uk': out['m_w_uk'], 'm_w_uv': out['m_w_uv'], 'm_q_norm': out['m_q_norm'], 'm_k_norm': out['m_k_norm'], 'm_w_pool': out['m_w_pool'], 'm_pool_scale': out['m_pool_scale'], 'm_w_out': out['m_w_out'], 'm_ffn2_norm': out['m_ffn2_norm'], 'm_ffn2_w_gu': out['m_ffn2_w_gu'], 'm_ffn2_w_down': out['m_ffn2_w_down'], 'v_ffn1_norm': out['v_ffn1_norm'], 'v_ffn1_w_gu': out['v_ffn1_w_gu'], 'v_ffn1_w_down': out['v_ffn1_w_down'], 'v_mix_norm': out['v_mix_norm'], 'v_w_in': out['v_w_in'], 'v_q_lat_norm': out['v_q_lat_norm'], 'v_kv_lat_norm': out['v_kv_lat_norm'], 'v_w_uq': out['v_w_uq'], 'v_w_uk': out['v_w_uk'], 'v_w_uv': out['v_w_uv'], 'v_q_norm': out['v_q_norm'], 'v_k_norm': out['v_k_norm'], 'v_w_pool': out['v_w_pool'], 'v_pool_scale': out['v_pool_scale'], 'v_w_out': out['v_w_out'], 'v_ffn2_norm': out['v_ffn2_norm'], 'v_ffn2_w_gu': out['v_ffn2_w_gu'], 'v_ffn2_w_down': out['v_ffn2_w_down']}


def _loss(weights, diff, rest, loss_target):
    with _jax.named_scope("forward"):
        args = {**rest, TWIN_DIFF_INPUT: diff, **{k: w.astype(_WEIGHT_DTYPES[k]) for k, w in weights.items()}}
        y = _forward(args)
    with _jax.named_scope("loss_head"):
        err = _jnp.square(y.astype(_jnp.float32) - loss_target)
        return 0.5 * _jnp.sum(_jnp.mean(err, axis=-1)) if err.ndim else 0.5 * err


def _adamw(w, g, m, v):
    m = ADAM_B1 * m + (1.0 - ADAM_B1) * g
    v = ADAM_B2 * v + (1.0 - ADAM_B2) * _jnp.square(g)
    m_hat = m / (1.0 - ADAM_B1 ** ADAM_STEP)
    v_hat = v / (1.0 - ADAM_B2 ** ADAM_STEP)
    delta = -ADAM_LR * (m_hat / (_jnp.sqrt(v_hat) + ADAM_EPS) + ADAM_WD * w)
    return delta, m, v


def reference(x, ffn1_norm, ffn1_w_gu, ffn1_w_down, mix_norm, w_in, q_lat_norm, kv_lat_norm, w_uq, w_uk, w_uv, q_norm, k_norm, w_pool, pool_scale, w_out, ffn2_norm, ffn2_w_gu, ffn2_w_down, loss_target, m_ffn1_norm, m_ffn1_w_gu, m_ffn1_w_down, m_mix_norm, m_w_in, m_q_lat_norm, m_kv_lat_norm, m_w_uq, m_w_uk, m_w_uv, m_q_norm, m_k_norm, m_w_pool, m_pool_scale, m_w_out, m_ffn2_norm, m_ffn2_w_gu, m_ffn2_w_down, v_ffn1_norm, v_ffn1_w_gu, v_ffn1_w_down, v_mix_norm, v_w_in, v_q_lat_norm, v_kv_lat_norm, v_w_uq, v_w_uk, v_w_uv, v_q_norm, v_k_norm, v_w_pool, v_pool_scale, v_w_out, v_ffn2_norm, v_ffn2_w_gu, v_ffn2_w_down):
    given = dict(x=x, ffn1_norm=ffn1_norm, ffn1_w_gu=ffn1_w_gu, ffn1_w_down=ffn1_w_down, mix_norm=mix_norm, w_in=w_in, q_lat_norm=q_lat_norm, kv_lat_norm=kv_lat_norm, w_uq=w_uq, w_uk=w_uk, w_uv=w_uv, q_norm=q_norm, k_norm=k_norm, w_pool=w_pool, pool_scale=pool_scale, w_out=w_out, ffn2_norm=ffn2_norm, ffn2_w_gu=ffn2_w_gu, ffn2_w_down=ffn2_w_down, loss_target=loss_target, m_ffn1_norm=m_ffn1_norm, m_ffn1_w_gu=m_ffn1_w_gu, m_ffn1_w_down=m_ffn1_w_down, m_mix_norm=m_mix_norm, m_w_in=m_w_in, m_q_lat_norm=m_q_lat_norm, m_kv_lat_norm=m_kv_lat_norm, m_w_uq=m_w_uq, m_w_uk=m_w_uk, m_w_uv=m_w_uv, m_q_norm=m_q_norm, m_k_norm=m_k_norm, m_w_pool=m_w_pool, m_pool_scale=m_pool_scale, m_w_out=m_w_out, m_ffn2_norm=m_ffn2_norm, m_ffn2_w_gu=m_ffn2_w_gu, m_ffn2_w_down=m_ffn2_w_down, v_ffn1_norm=v_ffn1_norm, v_ffn1_w_gu=v_ffn1_w_gu, v_ffn1_w_down=v_ffn1_w_down, v_mix_norm=v_mix_norm, v_w_in=v_w_in, v_q_lat_norm=v_q_lat_norm, v_kv_lat_norm=v_kv_lat_norm, v_w_uq=v_w_uq, v_w_uk=v_w_uk, v_w_uv=v_w_uv, v_q_norm=v_q_norm, v_k_norm=v_k_norm, v_w_pool=v_w_pool, v_pool_scale=v_pool_scale, v_w_out=v_w_out, v_ffn2_norm=v_ffn2_norm, v_ffn2_w_gu=v_ffn2_w_gu, v_ffn2_w_down=v_ffn2_w_down)
    weights = {n: given[n] for n in TWIN_WEIGHTS}
    shared = {n: given[n] for n in SHARED_INPUTS}
    per_example = {n: given[n] for n in ['x']}
    grad_fn = _jax.value_and_grad(_loss, argnums=(0, 1))

    def one_microbatch(ex, loss_target):
        ex = dict(ex)
        diff = ex.pop(TWIN_DIFF_INPUT)
        return grad_fn(weights, diff, {**shared, **ex}, loss_target)

    if N_MICROBATCH == 1:
        loss, (grad_w, grad_x) = one_microbatch(per_example, given["loss_target"])
    else:
        def body(carry, xs):
            loss_sum, grad_sum = carry
            l_k, (gw_k, gx_k) = one_microbatch(xs[0], xs[1])
            with _jax.named_scope("update"):
                return (loss_sum + l_k, _jax.tree.map(_jnp.add, grad_sum, gw_k)), gx_k

        init = (_jnp.zeros((), _jnp.float32), _jax.tree.map(_jnp.zeros_like, weights))
        (loss, grad_w), grad_x = _jax.lax.scan(body, init, (per_example, given["loss_target"]))
    with _jax.named_scope("update"):
        delta_w, new_m, new_v = {}, {}, {}
        for n in TWIN_WEIGHTS:
            delta_w[n], new_m[n], new_v[n] = _adamw(weights[n], grad_w[n], given["m_" + n], given["v_" + n])
    return (loss, grad_x, *[grad_w[n] for n in TWIN_WEIGHTS], *[delta_w[n] for n in TWIN_WEIGHTS],
            *[new_m[n] for n in TWIN_WEIGHTS], *[new_v[n] for n in TWIN_WEIGHTS])
```

```python
import functools

import jax
import jax.numpy as jnp
from jax import lax
from jax.experimental import pallas as pl
from jax.experimental.pallas import tpu as pltpu

F32 = jnp.float32
BF16 = jnp.bfloat16
MESH = pl.DeviceIdType.MESH

DEPTH = 4
D_MODEL = 1024
D_FF = 2816
FF_BLOCK = D_FF // 2
N_CHIPS = 4
HEADS = 8
HEAD_PAD = 128
QK_NOPE = 64
QK_ROPE = 32
QK_HEAD = QK_NOPE + QK_ROPE
V_HEAD = 64
Q_LORA = 384
KV_LORA = 256
POOL_W = 512
POOL_WINDOWS = (2, 4, 8, 16)
POOL_HALO = 8
GROUP = 128
D_IN = Q_LORA + KV_LORA + QK_ROPE + POOL_W
Z_W = 1280
Z_CKV = Q_LORA
Z_POOL = Q_LORA + KV_LORA
Z_KPE = Z_POOL + POOL_W
ATT_W = HEADS * HEAD_PAD
CAT_W = ATT_W + POOL_W
EPS = 1e-6
ROPE_THETA = 10000.0
ADAM_LR, ADAM_B1, ADAM_B2, ADAM_EPS, ADAM_WD, ADAM_STEP = 0.001, 0.9, 0.999, 1e-08, 0.01, 10

VMEM_LIMIT = 56 * 1024 * 1024


def _params(*sem):
    return pltpu.CompilerParams(dimension_semantics=sem, vmem_limit_bytes=VMEM_LIMIT)


def _dot(a, b):
    return jnp.dot(a, b, preferred_element_type=F32)


def _dot_nt(a, b):
    return lax.dot_general(a, b, (((1,), (1,)), ((), ())), preferred_element_type=F32)


def _dot_tn(a, b):
    return lax.dot_general(a, b, (((0,), (0,)), ((), ())), preferred_element_type=F32)


def _norm_fwd(x, n):
    rstd = lax.rsqrt(jnp.sum(x * x, axis=-1, keepdims=True) * (1.0 / n) + EPS)
    return x * rstd, rstd


def _norm_bwd(dy, g, xhat, rstd, n):
    dxhat = dy * g
    dx = rstd * (dxhat - xhat * (jnp.sum(dxhat * xhat, axis=-1, keepdims=True) * (1.0 / n)))
    return dx, dy * xhat


def _row_tile(s, want):
    t = min(s, want)
    assert s % t == 0
    return t


def ffn_fwd(x, gn, wgu4, wd, layer, tm=512):
    s = x.shape[0]
    tm = _row_tile(s, tm)

    def body(x_ref, gn_ref, wg_ref, wu_ref, wd_ref, o_ref, h_s, acc_s):
        j = pl.program_id(1)

        @pl.when(j == 0)
        def _():
            xhat, _ = _norm_fwd(x_ref[...], D_MODEL)
            h_s[...] = (xhat * gn_ref[...]).astype(BF16)
            acc_s[...] = jnp.zeros_like(acc_s)

        h = h_s[...]
        g = _dot(h, wg_ref[...])
        u = _dot(h, wu_ref[...])
        a = (g * jax.nn.sigmoid(g) * u).astype(BF16)
        acc_s[...] += _dot(a, wd_ref[...])

        @pl.when(j == 1)
        def _():
            o_ref[...] = x_ref[...] + 0.5 * acc_s[...]

    return pl.pallas_call(
        body, name="ffn_fwd",
        grid=(s // tm, 2),
        in_specs=[
            pl.BlockSpec((tm, D_MODEL), lambda i, j: (i, 0)),
            pl.BlockSpec((1, D_MODEL), lambda i, j: (0, 0)),
            pl.BlockSpec((None, None, D_MODEL, FF_BLOCK), lambda i, j: (j, layer, 0, 0)),
            pl.BlockSpec((None, None, D_MODEL, FF_BLOCK), lambda i, j: (j + 2, layer, 0, 0)),
            pl.BlockSpec((None, FF_BLOCK, D_MODEL), lambda i, j: (layer, j, 0)),
        ],
        out_specs=pl.BlockSpec((tm, D_MODEL), lambda i, j: (i, 0)),
        out_shape=jax.ShapeDtypeStruct((s, D_MODEL), F32),
        scratch_shapes=[pltpu.VMEM((tm, D_MODEL), BF16), pltpu.VMEM((tm, D_MODEL), F32)],
        compiler_params=_params("parallel", "arbitrary"),
    )(x, gn, wgu4, wgu4, wd)


def ffn_bwd(x, dy, gn, wgu4, wd, layer, tm=256):
    s = x.shape[0]
    tm = _row_tile(s, tm)

    def body(x_ref, dy_ref, gn_ref, wg_ref, wu_ref, wd_ref,
             dx_ref, h_ref, a_ref, dg_ref, du_ref, dyh_ref, dgn_ref, dh_s):
        i, j = pl.program_id(0), pl.program_id(1)

        @pl.when(j == 0)
        def _():
            xhat, _ = _norm_fwd(x_ref[...], D_MODEL)
            h_ref[...] = (xhat * gn_ref[...]).astype(BF16)
            dyh_ref[...] = (0.5 * dy_ref[...]).astype(BF16)
            dh_s[...] = jnp.zeros_like(dh_s)

        h = h_ref[...]
        g = _dot(h, wg_ref[...])
        u = _dot(h, wu_ref[...])
        sg = jax.nn.sigmoid(g)
        silu = g * sg
        a_ref[...] = (silu * u).astype(BF16)
        da = _dot_nt(dyh_ref[...], wd_ref[...])
        dg = (da * u * (sg * (1.0 + g * (1.0 - sg)))).astype(BF16)
        du = (da * silu).astype(BF16)
        dg_ref[...] = dg
        du_ref[...] = du
        dh_s[...] += _dot_nt(dg, wg_ref[...]) + _dot_nt(du, wu_ref[...])

        @pl.when(j == 1)
        def _():
            xhat, rstd = _norm_fwd(x_ref[...], D_MODEL)
            dxn, dgrow = _norm_bwd(dh_s[...], gn_ref[...], xhat, rstd, D_MODEL)
            dx_ref[...] = dy_ref[...] + dxn
            part = jnp.sum(dgrow, axis=0, keepdims=True)

            @pl.when(i == 0)
            def _():
                dgn_ref[...] = part

            @pl.when(i > 0)
            def _():
                dgn_ref[...] += part

    row = lambda i, j: (i, 0)
    return pl.pallas_call(
        body, name="ffn_bwd",
        grid=(s // tm, 2),
        in_specs=[
            pl.BlockSpec((tm, D_MODEL), row),
            pl.BlockSpec((tm, D_MODEL), row),
            pl.BlockSpec((1, D_MODEL), lambda i, j: (0, 0)),
            pl.BlockSpec((None, None, D_MODEL, FF_BLOCK), lambda i, j: (j, layer, 0, 0)),
            pl.BlockSpec((None, None, D_MODEL, FF_BLOCK), lambda i, j: (j + 2, layer, 0, 0)),
            pl.BlockSpec((None, FF_BLOCK, D_MODEL), lambda i, j: (layer, j, 0)),
        ],
        out_specs=[
            pl.BlockSpec((tm, D_MODEL), row),
            pl.BlockSpec((tm, D_MODEL), row),
            pl.BlockSpec((tm, FF_BLOCK), lambda i, j: (i, j)),
            pl.BlockSpec((tm, FF_BLOCK), lambda i, j: (i, j)),
            pl.BlockSpec((tm, FF_BLOCK), lambda i, j: (i, j)),
            pl.BlockSpec((tm, D_MODEL), row),
            pl.BlockSpec((1, D_MODEL), lambda i, j: (0, 0)),
        ],
        out_shape=[
            jax.ShapeDtypeStruct((s, D_MODEL), F32),
            jax.ShapeDtypeStruct((s, D_MODEL), BF16),
            jax.ShapeDtypeStruct((s, D_FF), BF16),
            jax.ShapeDtypeStruct((s, D_FF), BF16),
            jax.ShapeDtypeStruct((s, D_FF), BF16),
            jax.ShapeDtypeStruct((s, D_MODEL), BF16),
            jax.ShapeDtypeStruct((1, D_MODEL), F32),
        ],
        scratch_shapes=[pltpu.VMEM((tm, D_MODEL), F32)],
        compiler_params=_params("arbitrary", "arbitrary"),
    )(x, dy, gn, wgu4, wgu4, wd)


def matmul_tn(a, b, tk=512, tn_max=1536):
    s, m = a.shape
    n = b.shape[1]
    tk = _row_tile(s, tk)
    tn = n
    if n > tn_max:
        tn = n // 2
    assert n % tn == 0 and tn % 128 == 0
    nk = s // tk

    def body(a_ref, b_ref, o_ref, acc_s):
        k = pl.program_id(1)

        @pl.when(k == 0)
        def _():
            acc_s[...] = jnp.zeros_like(acc_s)

        acc_s[...] += _dot_tn(a_ref[...].astype(BF16), b_ref[...].astype(BF16))

        @pl.when(k == nk - 1)
        def _():
            o_ref[...] = acc_s[...]

    return pl.pallas_call(
        body, name="matmul_tn",
        grid=(n // tn, nk),
        in_specs=[pl.BlockSpec((tk, m), lambda j, k: (k, 0)), pl.BlockSpec((tk, tn), lambda j, k: (k, j))],
        out_specs=pl.BlockSpec((m, tn), lambda j, k: (0, j)),
        out_shape=jax.ShapeDtypeStruct((m, n), F32),
        scratch_shapes=[pltpu.VMEM((m, tn), F32)],
        compiler_params=_params("parallel", "arbitrary"),
    )(a, b)


def _rope_fwd(y, c, s1, s2):
    return y * c + pltpu.roll(y, HEAD_PAD - 16, 1) * s1 + pltpu.roll(y, 16, 1) * s2


def _rope_bwd(dy, c, s1, s2):
    return dy * c + pltpu.roll(dy * s1, 16, 1) + pltpu.roll(dy * s2, HEAD_PAD - 16, 1)


def rope_tables(s):
    pos = jnp.arange(s, dtype=F32)
    inv = ROPE_THETA ** (-jnp.arange(0, QK_ROPE, 2, dtype=F32) / QK_ROPE)
    ang = pos[:, None] * inv[None, :]
    cos, sin = jnp.cos(ang), jnp.sin(ang)
    z16 = jnp.zeros((s, 16), F32)
    c = jnp.concatenate([jnp.ones((s, QK_NOPE), F32), cos, cos, z16, z16], axis=1)
    s1 = jnp.concatenate([jnp.zeros((s, QK_NOPE), F32), -sin, z16, z16, z16], axis=1)
    s2 = jnp.concatenate([jnp.zeros((s, QK_NOPE), F32), z16, sin, z16, z16], axis=1)
    return c, s1, s2


def mix_in_fwd(x, gm, win, qln, kvln, wuq, wuk, wuv, qn, kn, rc, rs1, rs2, tm=512):
    s = x.shape[0]
    tm = _row_tile(s, tm)
    scale = QK_HEAD ** -0.5

    def body(x_ref, gm_ref, win_ref, qln_ref, kvln_ref, wuq_ref, wuk_ref, wuv_ref, qn_ref, kn_ref,
             rc_ref, rs1_ref, rs2_ref, z_ref, q_ref, k_ref, v_ref):
        xhat, _ = _norm_fwd(x_ref[...], D_MODEL)
        h = (xhat * gm_ref[...]).astype(BF16)
        z = _dot(h, win_ref[...])
        z_ref[...] = z
        cq, _ = _norm_fwd(z[:, :Q_LORA], Q_LORA)
        cqn = (cq * qln_ref[...]).astype(BF16)
        ckv, _ = _norm_fwd(z[:, Z_CKV:Z_POOL], KV_LORA)
        ckvn = (ckv * kvln_ref[...]).astype(BF16)
        kpe = z[:, Z_KPE:]
        q_raw = _dot(cqn, wuq_ref[...])
        k_raw = _dot(ckvn, wuk_ref[...])
        v_ref[...] = _dot(ckvn, wuv_ref[...]).astype(BF16)
        c, s1, s2 = rc_ref[...], rs1_ref[...], rs2_ref[...]
        for hd in range(HEADS):
            lanes = slice(hd * HEAD_PAD, (hd + 1) * HEAD_PAD)
            qh, _ = _norm_fwd(q_raw[:, lanes], QK_HEAD)
            q_ref[:, lanes] = (_rope_fwd(qh * qn_ref[...], c, s1, s2) * scale).astype(BF16)
            kh, _ = _norm_fwd(k_raw[:, lanes] + kpe, QK_HEAD)
            k_ref[:, lanes] = _rope_fwd(kh * kn_ref[...], c, s1, s2).astype(BF16)

    full = lambda shape: pl.BlockSpec(shape, lambda i: (0,) * len(shape))
    row = lambda w: pl.BlockSpec((tm, w), lambda i: (i, 0))
    return pl.pallas_call(
        body, name="mix_in_fwd",
        grid=(s // tm,),
        in_specs=[row(D_MODEL), full((1, D_MODEL)), full((D_MODEL, Z_W)), full((1, Q_LORA)), full((1, KV_LORA)),
                  full((Q_LORA, ATT_W)), full((KV_LORA, ATT_W)), full((KV_LORA, ATT_W)),
                  full((1, HEAD_PAD)), full((1, HEAD_PAD)), row(HEAD_PAD), row(HEAD_PAD), row(HEAD_PAD)],
        out_specs=[row(Z_W), row(ATT_W), row(ATT_W), row(ATT_W)],
        out_shape=[jax.ShapeDtypeStruct((s, Z_W), F32)] + [jax.ShapeDtypeStruct((s, ATT_W), BF16)] * 3,
        compiler_params=_params("parallel"),
    )(x, gm, win, qln, kvln, wuq, wuk, wuv, qn, kn, rc, rs1, rs2)


def mla_bwd(z, dq_t, dk, dv, dp, qln, kvln, wuq, wuk, wuv, qn, kn, rc, rs1, rs2, tm=512):
    s = z.shape[0]
    tm = _row_tile(s, tm)
    scale = QK_HEAD ** -0.5

    def body(z_ref, dqt_ref, dk_ref, dv_ref, dp_ref, qln_ref, kvln_ref, wuq_ref, wuk_ref, wuv_ref, qn_ref, kn_ref,
             rc_ref, rs1_ref, rs2_ref,
             dz_ref, cqn_ref, ckvn_ref, dqr_ref, dkr_ref, dvb_ref, dqln_ref, dkvln_ref, dqn_ref, dkn_ref):
        i = pl.program_id(0)
        z = z_ref[...]
        cq, cq_rstd = _norm_fwd(z[:, :Q_LORA], Q_LORA)
        cqn = (cq * qln_ref[...]).astype(BF16)
        ckv, ckv_rstd = _norm_fwd(z[:, Z_CKV:Z_POOL], KV_LORA)
        ckvn = (ckv * kvln_ref[...]).astype(BF16)
        kpe = z[:, Z_KPE:]
        cqn_ref[...] = cqn
        ckvn_ref[...] = ckvn
        q_raw = _dot(cqn, wuq_ref[...])
        k_raw = _dot(ckvn, wuk_ref[...])
        c, s1, s2 = rc_ref[...], rs1_ref[...], rs2_ref[...]
        lane = lax.broadcasted_iota(jnp.int32, (tm, HEAD_PAD), 1)
        rope_lanes = (lane >= QK_NOPE) & (lane < QK_HEAD)
        dkpe = jnp.zeros((tm, HEAD_PAD), F32)
        dqn = jnp.zeros((tm, HEAD_PAD), F32)
        dkn = jnp.zeros((tm, HEAD_PAD), F32)
        for hd in range(HEADS):
            lanes = slice(hd * HEAD_PAD, (hd + 1) * HEAD_PAD)
            qh, q_rstd = _norm_fwd(q_raw[:, lanes], QK_HEAD)
            dqh = jnp.transpose(dqt_ref[lanes, :]) * scale
            dqr, dg = _norm_bwd(_rope_bwd(dqh, c, s1, s2), qn_ref[...], qh, q_rstd, QK_HEAD)
            dqn += dg
            dqr_ref[:, lanes] = dqr.astype(BF16)
            kh, k_rstd = _norm_fwd(k_raw[:, lanes] + kpe, QK_HEAD)
            dkr, dg = _norm_bwd(_rope_bwd(dk_ref[:, lanes], c, s1, s2), kn_ref[...], kh, k_rstd, QK_HEAD)
            dkn += dg
            dkr_ref[:, lanes] = dkr.astype(BF16)
            dkpe += jnp.where(rope_lanes, dkr, 0.0)
        dvb = dv_ref[...].astype(BF16)
        dvb_ref[...] = dvb
        dcqn = _dot_nt(dqr_ref[...], wuq_ref[...])
        dckvn = _dot_nt(dkr_ref[...], wuk_ref[...]) + _dot_nt(dvb, wuv_ref[...])
        dcq, dqln = _norm_bwd(dcqn, qln_ref[...], cq, cq_rstd, Q_LORA)
        dckv, dkvln = _norm_bwd(dckvn, kvln_ref[...], ckv, ckv_rstd, KV_LORA)
        dz_ref[:, :Q_LORA] = dcq.astype(BF16)
        dz_ref[:, Z_CKV:Z_POOL] = dckv.astype(BF16)
        dz_ref[:, Z_POOL:Z_KPE] = dp_ref[...].astype(BF16)
        dz_ref[:, Z_KPE:] = dkpe.astype(BF16)
        parts = [(dqln_ref, dqln), (dkvln_ref, dkvln), (dqn_ref, dqn), (dkn_ref, dkn)]

        @pl.when(i == 0)
        def _():
            for ref, val in parts:
                ref[...] = jnp.sum(val, axis=0, keepdims=True)

        @pl.when(i > 0)
        def _():
            for ref, val in parts:
                ref[...] += jnp.sum(val, axis=0, keepdims=True)

    full = lambda shape: pl.BlockSpec(shape, lambda i: (0,) * len(shape))
    row = lambda w: pl.BlockSpec((tm, w), lambda i: (i, 0))
    return pl.pallas_call(
        body, name="mla_bwd",
        grid=(s // tm,),
        in_specs=[row(Z_W), pl.BlockSpec((ATT_W, tm), lambda i: (0, i)), row(ATT_W), row(ATT_W), row(POOL_W),
                  full((1, Q_LORA)), full((1, KV_LORA)), full((Q_LORA, ATT_W)), full((KV_LORA, ATT_W)),
                  full((KV_LORA, ATT_W)), full((1, HEAD_PAD)), full((1, HEAD_PAD)),
                  row(HEAD_PAD), row(HEAD_PAD), row(HEAD_PAD)],
        out_specs=[row(Z_W), row(Q_LORA), row(KV_LORA), row(ATT_W), row(ATT_W), row(ATT_W),
                   full((1, Q_LORA)), full((1, KV_LORA)), full((1, HEAD_PAD)), full((1, HEAD_PAD))],
        out_shape=[jax.ShapeDtypeStruct((s, Z_W), BF16), jax.ShapeDtypeStruct((s, Q_LORA), BF16),
                   jax.ShapeDtypeStruct((s, KV_LORA), BF16)] + [jax.ShapeDtypeStruct((s, ATT_W), BF16)] * 3 + [
                   jax.ShapeDtypeStruct((1, Q_LORA), F32), jax.ShapeDtypeStruct((1, KV_LORA), F32),
                   jax.ShapeDtypeStruct((1, HEAD_PAD), F32), jax.ShapeDtypeStruct((1, HEAD_PAD), F32)],
        compiler_params=_params("arbitrary"),
    )(z, dq_t, dk, dv, dp, qln, kvln, wuq, wuk, wuv, qn, kn, rc, rs1, rs2)


def mix_in_bwd(x, dx_in, dz, gm, win, tm=512):
    s = x.shape[0]
    tm = _row_tile(s, tm)

    def body(x_ref, dxin_ref, dz_ref, gm_ref, win_ref, dx_ref, h_ref, dgm_ref):
        i = pl.program_id(0)
        xhat, rstd = _norm_fwd(x_ref[...], D_MODEL)
        h_ref[...] = (xhat * gm_ref[...]).astype(BF16)
        dh = _dot_nt(dz_ref[...], win_ref[...])
        dxn, dgrow = _norm_bwd(dh, gm_ref[...], xhat, rstd, D_MODEL)
        dx_ref[...] = dxin_ref[...] + dxn
        part = jnp.sum(dgrow, axis=0, keepdims=True)

        @pl.when(i == 0)
        def _():
            dgm_ref[...] = part

        @pl.when(i > 0)
        def _():
            dgm_ref[...] += part

    full = lambda shape: pl.BlockSpec(shape, lambda i: (0,) * len(shape))
    row = lambda w: pl.BlockSpec((tm, w), lambda i: (i, 0))
    return pl.pallas_call(
        body, name="mix_in_bwd",
        grid=(s // tm,),
        in_specs=[row(D_MODEL), row(D_MODEL), row(Z_W), full((1, D_MODEL)), full((D_MODEL, Z_W))],
        out_specs=[row(D_MODEL), row(D_MODEL), full((1, D_MODEL))],
        out_shape=[jax.ShapeDtypeStruct((s, D_MODEL), F32), jax.ShapeDtypeStruct((s, D_MODEL), BF16),
                   jax.ShapeDtypeStruct((1, D_MODEL), F32)],
        compiler_params=_params("arbitrary"),
    )(x, dx_in, dz, gm, win)


def flash_fwd(q, k, v, tq=512, tk=512):
    s = q.shape[0]
    tq, tk = _row_tile(s, tq), _row_tile(s, tk)
    nk = s // tk

    def body(q_ref, k_ref, v_ref, o_ref, lse_ref, m_s, l_s, acc_s):
        kk = pl.program_id(2)

        @pl.when(kk == 0)
        def _():
            m_s[...] = jnp.full_like(m_s, -jnp.inf)
            l_s[...] = jnp.zeros_like(l_s)
            acc_s[...] = jnp.zeros_like(acc_s)

        sc = _dot_nt(q_ref[...], k_ref[...])
        m_prev = m_s[...]
        m_new = jnp.maximum(m_prev, jnp.max(sc, axis=-1, keepdims=True))
        alpha = jnp.exp(m_prev - m_new)
        p = jnp.exp(sc - m_new)
        l_s[...] = alpha * l_s[...] + jnp.sum(p, axis=-1, keepdims=True)
        acc_s[...] = alpha * acc_s[...] + _dot(p.astype(BF16), v_ref[...])
        m_s[...] = m_new

        @pl.when(kk == nk - 1)
        def _():
            o_ref[...] = (acc_s[...] / l_s[...]).astype(BF16)
            lse_ref[...] = m_s[...] + jnp.log(l_s[...])

    return pl.pallas_call(
        body, name="flash_fwd",
        grid=(HEADS, s // tq, nk),
        in_specs=[pl.BlockSpec((tq, HEAD_PAD), lambda h, i, kk: (i, h)),
                  pl.BlockSpec((tk, HEAD_PAD), lambda h, i, kk: (kk, h)),
                  pl.BlockSpec((tk, HEAD_PAD), lambda h, i, kk: (kk, h))],
        out_specs=[pl.BlockSpec((tq, HEAD_PAD), lambda h, i, kk: (i, h)),
                   pl.BlockSpec((None, tq, 1), lambda h, i, kk: (h, i, 0))],
        out_shape=[jax.ShapeDtypeStruct((s, ATT_W), BF16), jax.ShapeDtypeStruct((HEADS, s, 1), F32)],
        scratch_shapes=[pltpu.VMEM((tq, 1), F32), pltpu.VMEM((tq, 1), F32), pltpu.VMEM((tq, HEAD_PAD), F32)],
        compiler_params=_params("parallel", "parallel", "arbitrary"),
    )(q, k, v)


def flash_bwd(q, k, v, do, lse_row, delta_row, tq=512, tk=512):
    s = q.shape[0]
    tq, tk = _row_tile(s, tq), _row_tile(s, tk)
    nq = s // tq

    def body(q_ref, k_ref, v_ref, do_ref, lse_ref, dl_ref, dqt_ref, dk_ref, dv_ref, kt_s, dk_s, dv_s):
        kb, qb = pl.program_id(1), pl.program_id(2)

        @pl.when(qb == 0)
        def _():
            kt_s[...] = jnp.transpose(k_ref[...].astype(F32)).astype(BF16)
            dk_s[...] = jnp.zeros_like(dk_s)
            dv_s[...] = jnp.zeros_like(dv_s)

        qt, dot_ = q_ref[...], do_ref[...]
        p_t = jnp.exp(_dot_nt(k_ref[...], qt) - lse_ref[...])
        ds_t = (p_t * (_dot_nt(v_ref[...], dot_) - dl_ref[...])).astype(BF16)
        dv_s[...] += _dot(p_t.astype(BF16), dot_)
        dk_s[...] += _dot(ds_t, qt)
        contrib = _dot(kt_s[...], ds_t)
        cols = pl.ds(pl.multiple_of(qb * tq, tq), tq)

        @pl.when(kb == 0)
        def _():
            dqt_ref[:, cols] = contrib

        @pl.when(kb > 0)
        def _():
            dqt_ref[:, cols] += contrib

        @pl.when(qb == nq - 1)
        def _():
            dk_ref[...] = dk_s[...]
            dv_ref[...] = dv_s[...]

    qspec = pl.BlockSpec((tq, HEAD_PAD), lambda h, kb, qb: (qb, h))
    kspec = pl.BlockSpec((tk, HEAD_PAD), lambda h, kb, qb: (kb, h))
    rowspec = pl.BlockSpec((None, 1, tq), lambda h, kb, qb: (h, 0, qb))
    return pl.pallas_call(
        body, name="flash_bwd",
        grid=(HEADS, s // tk, nq),
        in_specs=[qspec, kspec, kspec, qspec, rowspec, rowspec],
        out_specs=[pl.BlockSpec((HEAD_PAD, s), lambda h, kb, qb: (h, 0)), kspec, kspec],
        out_shape=[jax.ShapeDtypeStruct((ATT_W, s), F32), jax.ShapeDtypeStruct((s, ATT_W), F32),
                   jax.ShapeDtypeStruct((s, ATT_W), F32)],
        scratch_shapes=[pltpu.VMEM((HEAD_PAD, tk), BF16), pltpu.VMEM((tk, HEAD_PAD), F32),
                        pltpu.VMEM((tk, HEAD_PAD), F32)],
        compiler_params=_params("parallel", "arbitrary", "arbitrary"),
    )(q, k, v, do, lse_row, delta_row)


def _halo_specs(tm, w, n_tiles):
    per = tm // POOL_HALO
    last = n_tiles * per - 1
    return [pl.BlockSpec((POOL_HALO, w), lambda i: (jnp.maximum(i * per - 1, 0), 0)),
            pl.BlockSpec((tm, w), lambda i: (i, 0)),
            pl.BlockSpec((POOL_HALO, w), lambda i: (jnp.minimum((i + 1) * per, last), 0))]


def _fill_ext(ext_ref, prev, cur, nxt, i, n_tiles, tm):
    ext_ref[pl.ds(0, POOL_HALO), :] = jnp.where(i > 0, prev, 0.0)
    ext_ref[pl.ds(POOL_HALO, tm), :] = cur
    ext_ref[pl.ds(POOL_HALO + tm, POOL_HALO), :] = jnp.where(i < n_tiles - 1, nxt, 0.0)


def _window_sum(ext_ref, g, lo, hi, tm):
    lanes = pl.ds(g * GROUP, GROUP)
    acc = ext_ref[pl.ds(POOL_HALO + lo, tm), lanes]
    for d in range(lo + 1, hi + 1):
        acc = acc + ext_ref[pl.ds(POOL_HALO + d, tm), lanes]
    return acc


def _pool_mixed(ext_ref, cur, i, s, tm):
    row = i * tm + lax.broadcasted_iota(jnp.int32, (tm, 1), 0)
    out = []
    for g, w in enumerate(POOL_WINDOWS):
        left = w // 2
        right = w - 1 - left
        cnt = (jnp.minimum(row + right + 1, s) - jnp.maximum(row - left, 0)).astype(F32)
        out.append(_window_sum(ext_ref, g, -left, right, tm) / cnt - cur[:, g * GROUP:(g + 1) * GROUP])
    return out


def pool_out_fwd(x, z, o, wpool, pscale, wout, tm=512):
    s = x.shape[0]
    tm = _row_tile(s, tm)
    n_tiles = s // tm

    def body(x_ref, zp_ref, z_ref, zn_ref, o_ref, wp_ref, ps_ref, wo_ref, y_ref, cat_ref, ext_s):
        i = pl.program_id(0)
        cur = z_ref[:, Z_POOL:Z_KPE]
        _fill_ext(ext_s, zp_ref[:, Z_POOL:Z_KPE], cur, zn_ref[:, Z_POOL:Z_KPE], i, n_tiles, tm)
        mixed = _pool_mixed(ext_s, cur, i, s, tm)
        cat_ref[:, :ATT_W] = o_ref[...]
        for g in range(len(POOL_WINDOWS)):
            lanes = slice(g * GROUP, (g + 1) * GROUP)
            yg = _dot(mixed[g].astype(BF16), wp_ref[g])
            cat_ref[:, ATT_W + g * GROUP:ATT_W + (g + 1) * GROUP] = (yg * ps_ref[:, lanes]).astype(BF16)
        y_ref[...] = x_ref[...] + _dot(cat_ref[...], wo_ref[...])

    full = lambda shape: pl.BlockSpec(shape, lambda i: (0,) * len(shape))
    row = lambda w: pl.BlockSpec((tm, w), lambda i: (i, 0))
    return pl.pallas_call(
        body, name="pool_out_fwd",
        grid=(n_tiles,),
        in_specs=[row(D_MODEL)] + _halo_specs(tm, Z_W, n_tiles) + [
            row(ATT_W), full((len(POOL_WINDOWS), GROUP, GROUP)), full((1, POOL_W)), full((CAT_W, D_MODEL))],
        out_specs=[row(D_MODEL), row(CAT_W)],
        out_shape=[jax.ShapeDtypeStruct((s, D_MODEL), F32), jax.ShapeDtypeStruct((s, CAT_W), BF16)],
        scratch_shapes=[pltpu.VMEM((tm + 2 * POOL_HALO, POOL_W), F32)],
        compiler_params=_params("parallel"),
    )(x, z, z, z, o, wpool, pscale, wout)


def out_bwd(dx, wout, cat, tm=512):
    s = dx.shape[0]
    tm = _row_tile(s, tm)

    def body(dx_ref, wo_ref, cat_ref, do_ref, db_ref, dl_ref):
        dcat = _dot_nt(dx_ref[...].astype(BF16), wo_ref[...])
        do_ref[...] = dcat[:, :ATT_W].astype(BF16)
        db_ref[...] = dcat[:, ATT_W:]
        for hd in range(HEADS):
            lanes = slice(hd * HEAD_PAD, (hd + 1) * HEAD_PAD)
            dl_ref[hd] = jnp.sum(dcat[:, lanes] * cat_ref[:, lanes].astype(F32), axis=-1, keepdims=True)

    row = lambda w: pl.BlockSpec((tm, w), lambda i: (i, 0))
    return pl.pallas_call(
        body, name="out_bwd",
        grid=(s // tm,),
        in_specs=[row(D_MODEL), pl.BlockSpec((CAT_W, D_MODEL), lambda i: (0, 0)), row(CAT_W)],
        out_specs=[row(ATT_W), row(POOL_W), pl.BlockSpec((HEADS, tm, 1), lambda i: (0, i, 0))],
        out_shape=[jax.ShapeDtypeStruct((s, ATT_W), BF16), jax.ShapeDtypeStruct((s, POOL_W), F32),
                   jax.ShapeDtypeStruct((HEADS, s, 1), F32)],
        compiler_params=_params("parallel"),
    )(dx, wout, cat)


def pool_bwd(db, z, wpool, pscale, tm=512):
    s = db.shape[0]
    tm = _row_tile(s, tm)
    n_tiles = s // tm
    te = tm + 2 * POOL_HALO

    def body(dbp_ref, db_ref, dbn_ref, zp_ref, z_ref, zn_ref, wp_ref, ps_ref,
             dp_ref, mixed_ref, dys_ref, dps_ref, ext_s, text_s):
        i = pl.program_id(0)
        cur = z_ref[:, Z_POOL:Z_KPE]
        _fill_ext(ext_s, zp_ref[:, Z_POOL:Z_KPE], cur, zn_ref[:, Z_POOL:Z_KPE], i, n_tiles, tm)
        mixed = _pool_mixed(ext_s, cur, i, s, tm)
        _fill_ext(ext_s, dbp_ref[...], db_ref[...], dbn_ref[...], i, n_tiles, tm)
        erow = i * tm - POOL_HALO + lax.broadcasted_iota(jnp.int32, (te, 1), 0)
        dps = []
        for g, w in enumerate(POOL_WINDOWS):
            lanes = slice(g * GROUP, (g + 1) * GROUP)
            left = w // 2
            right = w - 1 - left
            mb = mixed[g].astype(BF16)
            mixed_ref[:, lanes] = mb
            dps.append(jnp.sum(db_ref[:, lanes] * _dot(mb, wp_ref[g]), axis=0, keepdims=True))
            dys_e = (ext_s[:, lanes] * ps_ref[:, lanes]).astype(BF16)
            dys_ref[:, lanes] = (db_ref[:, lanes] * ps_ref[:, lanes]).astype(BF16)
            dmix_e = _dot_nt(dys_e, wp_ref[g])
            cnt = jnp.minimum(erow + right + 1, s) - jnp.maximum(erow - left, 0)
            text_s[:, lanes] = dmix_e / jnp.maximum(cnt, 1).astype(F32)
            dp_ref[:, lanes] = _window_sum(text_s, g, -right, left, tm) - dmix_e[POOL_HALO:POOL_HALO + tm]
        part = jnp.concatenate(dps, axis=1)

        @pl.when(i == 0)
        def _():
            dps_ref[...] = part

        @pl.when(i > 0)
        def _():
            dps_ref[...] += part

    full = lambda shape: pl.BlockSpec(shape, lambda i: (0,) * len(shape))
    row = lambda w: pl.BlockSpec((tm, w), lambda i: (i, 0))
    return pl.pallas_call(
        body, name="pool_bwd",
        grid=(n_tiles,),
        in_specs=_halo_specs(tm, POOL_W, n_tiles) + _halo_specs(tm, Z_W, n_tiles) + [
            full((len(POOL_WINDOWS), GROUP, GROUP)), full((1, POOL_W))],
        out_specs=[row(POOL_W), row(POOL_W), row(POOL_W), full((1, POOL_W))],
        out_shape=[jax.ShapeDtypeStruct((s, POOL_W), F32), jax.ShapeDtypeStruct((s, POOL_W), BF16),
                   jax.ShapeDtypeStruct((s, POOL_W), BF16), jax.ShapeDtypeStruct((1, POOL_W), F32)],
        scratch_shapes=[pltpu.VMEM((te, POOL_W), F32), pltpu.VMEM((te, POOL_W), F32)],
        compiler_params=_params("arbitrary"),
    )(db, db, db, z, z, z, wpool, pscale)


def loss_head(y, target, tm=512):
    s = y.shape[0]
    tm = _row_tile(s, tm)

    def body(y_ref, t_ref, dy_ref, loss_ref):
        i = pl.program_id(0)
        err = y_ref[...] - t_ref[...]
        dy_ref[...] = err * (1.0 / D_MODEL)
        part = 0.5 * jnp.sum(jnp.sum(err * err, axis=-1, keepdims=True) * (1.0 / D_MODEL), axis=0, keepdims=True)
        part = jnp.broadcast_to(part, (1, 128))

        @pl.when(i == 0)
        def _():
            loss_ref[...] = part

        @pl.when(i > 0)
        def _():
            loss_ref[...] += part

    row = pl.BlockSpec((tm, D_MODEL), lambda i: (i, 0))
    return pl.pallas_call(
        body, name="loss_head",
        grid=(s // tm,),
        in_specs=[row, row],
        out_specs=[row, pl.BlockSpec((1, 128), lambda i: (0, 0))],
        out_shape=[jax.ShapeDtypeStruct((s, D_MODEL), F32), jax.ShapeDtypeStruct((1, 128), F32)],
        compiler_params=_params("arbitrary"),
    )(y, target)


def _pad_heads(w, real):
    lead = w.shape[:-1]
    w = w.reshape(lead + (HEADS, real))
    w = jnp.pad(w, [(0, 0)] * len(lead) + [(0, 0), (0, HEAD_PAD - real)])
    return w.reshape(lead + (ATT_W,))


def _unpad_heads(w, real):
    lead = w.shape[:-1]
    return w.reshape(lead + (HEADS, HEAD_PAD))[..., :real].reshape(lead + (HEADS * real,))


def layout_weights(full):
    w = {}
    depth = full["w_in"].shape[0]
    win = full["w_in"]
    zc = lambda n: jnp.zeros((depth, D_MODEL, n), win.dtype)
    o_pe = Q_LORA + KV_LORA
    w["win"] = jnp.concatenate([win[..., :o_pe], win[..., o_pe + QK_ROPE:], zc(QK_NOPE), win[..., o_pe:o_pe + QK_ROPE],
                                zc(HEAD_PAD - QK_HEAD)], axis=-1)
    w["wuq"] = _pad_heads(full["w_uq"], QK_HEAD)
    w["wuk"] = _pad_heads(full["w_uk"], QK_NOPE)
    w["wuv"] = _pad_heads(full["w_uv"], V_HEAD)
    wo = full["w_out"]
    wo_att = jnp.swapaxes(_pad_heads(jnp.swapaxes(wo[:, :HEADS * V_HEAD], 1, 2), V_HEAD), 1, 2)
    w["wout"] = jnp.concatenate([wo_att, wo[:, HEADS * V_HEAD:]], axis=1)
    pad_vec = lambda v: jnp.pad(v, ((0, 0), (0, HEAD_PAD - QK_HEAD)))
    w["qn"], w["kn"] = pad_vec(full["q_norm"]), pad_vec(full["k_norm"])
    for n in ("ffn1_norm", "mix_norm", "ffn2_norm", "q_lat_norm", "kv_lat_norm", "pool_scale", "w_pool"):
        w[n] = full[n]
    return w


def local_step(x, target, w, wgu1, wd1, wgu2, wd2):
    s = x.shape[0]
    depth = w["win"].shape[0]
    rc, rs1, rs2 = rope_tables(s)
    vec = lambda name, l: w[name][l][None, :]
    saved = []
    for l in range(depth):
        x1 = ffn_fwd(x, vec("ffn1_norm", l), wgu1, wd1, l)
        z, q, k, v = mix_in_fwd(x1, vec("mix_norm", l), w["win"][l], vec("q_lat_norm", l), vec("kv_lat_norm", l),
                                w["wuq"][l], w["wuk"][l], w["wuv"][l], vec("qn", l), vec("kn", l), rc, rs1, rs2)
        o, lse = flash_fwd(q, k, v)
        x2, cat = pool_out_fwd(x1, z, o, w["w_pool"][l], vec("pool_scale", l), w["wout"][l])
        x3 = ffn_fwd(x2, vec("ffn2_norm", l), wgu2, wd2, l)
        saved.append((x, x1, z, q, k, v, lse, cat, x2))
        x = x3
    dy, loss = loss_head(x, target)

    names = ["ffn1_norm", "ffn1_w_gu", "ffn1_w_down", "mix_norm", "w_in", "q_lat_norm", "kv_lat_norm", "w_uq", "w_uk",
             "w_uv", "q_norm", "k_norm", "w_pool", "pool_scale", "w_out", "ffn2_norm", "ffn2_w_gu", "ffn2_w_down"]
    grads = {n: [None] * depth for n in names}
    for l in reversed(range(depth)):
        x0, x1, z, q, k, v, lse, cat, x2 = saved[l]
        dx2, h, act, dg, du, dyh, dgn = ffn_bwd(x2, dy, vec("ffn2_norm", l), wgu2, wd2, l)
        grads["ffn2_norm"][l] = dgn[0]
        grads["ffn2_w_gu"][l] = jnp.concatenate([matmul_tn(h, dg), matmul_tn(h, du)], axis=1)
        grads["ffn2_w_down"][l] = matmul_tn(act, dyh)

        do, db, delta = out_bwd(dx2, w["wout"][l], cat)
        dwout = matmul_tn(cat, dx2)
        grads["w_out"][l] = jnp.concatenate(
            [jnp.swapaxes(_unpad_heads(jnp.swapaxes(dwout[:ATT_W], 0, 1), V_HEAD), 0, 1), dwout[ATT_W:]], axis=0)
        dp, mixed, dys, dps = pool_bwd(db, z, w["w_pool"][l], vec("pool_scale", l))
        grads["pool_scale"][l] = dps[0]
        dwp = matmul_tn(mixed, dys)
        grads["w_pool"][l] = jnp.stack([dwp[g * GROUP:(g + 1) * GROUP, g * GROUP:(g + 1) * GROUP]
                                        for g in range(len(POOL_WINDOWS))])
        dq_t, dk, dv = flash_bwd(q, k, v, do, lse.reshape(HEADS, 1, s), delta.reshape(HEADS, 1, s))
        dz, cqn, ckvn, dqr, dkr, dvb, dqln, dkvln, dqn, dkn = mla_bwd(
            z, dq_t, dk, dv, dp, vec("q_lat_norm", l), vec("kv_lat_norm", l), w["wuq"][l], w["wuk"][l], w["wuv"][l],
            vec("qn", l), vec("kn", l), rc, rs1, rs2)
        grads["q_lat_norm"][l], grads["kv_lat_norm"][l] = dqln[0], dkvln[0]
        grads["q_norm"][l], grads["k_norm"][l] = dqn[0, :QK_HEAD], dkn[0, :QK_HEAD]
        grads["w_uq"][l] = _unpad_heads(matmul_tn(cqn, dqr), QK_HEAD)
        grads["w_uk"][l] = _unpad_heads(matmul_tn(ckvn, dkr), QK_NOPE)
        grads["w_uv"][l] = _unpad_heads(matmul_tn(ckvn, dvb), V_HEAD)
        dx1, h2, dgm = mix_in_bwd(x1, dx2, dz, vec("mix_norm", l), w["win"][l])
        grads["mix_norm"][l] = dgm[0]
        dwin = matmul_tn(h2, dz)
        grads["w_in"][l] = jnp.concatenate(
            [dwin[:, :Z_POOL], dwin[:, Z_KPE + QK_NOPE:Z_KPE + QK_HEAD], dwin[:, Z_POOL:Z_KPE]], axis=1)

        dy, h, act, dg, du, dyh, dgn = ffn_bwd(x0, dx1, vec("ffn1_norm", l), wgu1, wd1, l)
        grads["ffn1_norm"][l] = dgn[0]
        grads["ffn1_w_gu"][l] = jnp.concatenate([matmul_tn(h, dg), matmul_tn(h, du)], axis=1)
        grads["ffn1_w_down"][l] = matmul_tn(act, dyh)
    return loss, dy, {n: jnp.stack(g) for n, g in grads.items()}


HBM_SPEC = pl.BlockSpec(memory_space=pltpu.HBM)


def _place():
    x, y, c = lax.axis_index("x"), lax.axis_index("y"), lax.axis_index("c")
    other_chips = [(1 - x, y), (x, 1 - y), (1 - x, 1 - y)]
    return x, y, c, other_chips


def _exchange(name, arrays, out_shapes, plan, n_local, n_remote):
    n = len(arrays)

    def body(*refs):
        ins, outs = refs[:n], refs[n:n + len(out_shapes)]
        send_sems, recv_sems, local_sems = refs[n + len(out_shapes):]
        local, sends, recvs = plan(ins, outs, _place())
        started = []
        for k, (src, dst) in enumerate(local):
            cp = pltpu.make_async_copy(src, dst, local_sems.at[k])
            cp.start()
            started.append(cp)
        out_going = []
        for k, (src, dst, dev) in enumerate(sends):
            cp = pltpu.make_async_remote_copy(src_ref=src, dst_ref=dst, send_sem=send_sems.at[k], recv_sem=recv_sems.at[k],
                                              device_id=dev, device_id_type=MESH)
            cp.start()
            out_going.append(cp)
        for k, (dst, dev) in enumerate(recvs):
            pltpu.make_async_remote_copy(src_ref=dst, dst_ref=dst, send_sem=send_sems.at[k], recv_sem=recv_sems.at[k],
                                         device_id=dev, device_id_type=MESH).wait_recv()
        for cp in out_going:
            cp.wait_send()
        for cp in started:
            cp.wait()

    return pl.pallas_call(
        body, name=name,
        in_specs=[HBM_SPEC] * n,
        out_specs=[HBM_SPEC] * len(out_shapes),
        out_shape=out_shapes,
        scratch_shapes=[pltpu.SemaphoreType.DMA((n_remote,)), pltpu.SemaphoreType.DMA((n_remote,)),
                        pltpu.SemaphoreType.DMA((max(n_local, 1),))],
    )(*arrays)


def gather_chips(shards):
    n = len(shards)

    def plan(ins, outs, place):
        x, y, c, chips = place
        me = 2 * x + y
        local = [(ins[p], outs[p].at[me]) for p in range(n)]
        sends = [(ins[p], outs[p].at[me], (px, py, c)) for p in range(n) for (px, py) in chips]
        recvs = [(outs[p].at[2 * px + py], (px, py, c)) for p in range(n) for (px, py) in chips]
        return local, sends, recvs

    out_shapes = [jax.ShapeDtypeStruct((N_CHIPS,) + a.shape, a.dtype) for a in shards]
    return _exchange("gather_chips", shards, out_shapes, plan, n, 3 * n)


def scatter_chips(fulls):
    n = len(fulls)

    def plan(ins, outs, place):
        x, y, c, chips = place
        sends = [(ins[p].at[2 * px + py], outs[p].at[j], (px, py, c)) for p in range(n) for j, (px, py) in enumerate(chips)]
        recvs = [(outs[p].at[j], (px, py, c)) for p in range(n) for j, (px, py) in enumerate(chips)]
        return [], sends, recvs

    out_shapes = [jax.ShapeDtypeStruct((3,) + a.shape[1:], a.dtype) for a in fulls]
    return _exchange("scatter_chips", fulls, out_shapes, plan, 0, 3 * n)


def swap_sibling(arrays):
    n = len(arrays)

    def plan(ins, outs, place):
        x, y, c, _ = place
        sib = (x, y, 1 - c)
        return [], [(ins[p], outs[p], sib) for p in range(n)], [(outs[p], sib) for p in range(n)]

    return _exchange("swap_sibling", arrays, [jax.ShapeDtypeStruct(a.shape, a.dtype) for a in arrays], plan, 0, n)


def gather_all(vec):
    def plan(ins, outs, place):
        x, y, c, _ = place
        me = 4 * x + 2 * y + c
        flips = [(fx, fy, fc) for fx in (0, 1) for fy in (0, 1) for fc in (0, 1)][1:]
        peers = [(1 - x if fx else x, 1 - y if fy else y, 1 - c if fc else c) for fx, fy, fc in flips]
        local = [(ins[0], outs[0].at[me])]
        sends = [(ins[0], outs[0].at[me], dev) for dev in peers]
        recvs = [(outs[0].at[4 * dev[0] + 2 * dev[1] + dev[2]], dev) for dev in peers]
        return local, sends, recvs

    return _exchange("gather_all", [vec], [jax.ShapeDtypeStruct((8,) + vec.shape, vec.dtype)], plan, 1, 7)[0]


def _pick_rows(rows, cols, block_bytes=512 * 1024):
    best = None
    for t in range(8, rows + 1, 8):
        if rows % t == 0 and t * cols * 4 <= block_bytes:
            best = t
    return best or rows


def sum_rows_of(own, stacked):
    rows, cols = own.shape
    n = stacked.shape[0]
    tr = _pick_rows(rows, cols)

    def body(own_ref, st_ref, o_ref):
        acc = own_ref[...]
        for k in range(n):
            acc = acc + st_ref[k]
        o_ref[...] = acc

    return pl.pallas_call(
        body, name="sum_rows_of",
        grid=(rows // tr,),
        in_specs=[pl.BlockSpec((tr, cols), lambda i: (i, 0)), pl.BlockSpec((n, tr, cols), lambda i: (0, i, 0))],
        out_specs=pl.BlockSpec((tr, cols), lambda i: (i, 0)),
        out_shape=jax.ShapeDtypeStruct((rows, cols), F32),
        compiler_params=_params("parallel"),
    )(own, stacked)


def sum_stack(stacked):
    n, rows, cols = stacked.shape
    tr = _pick_rows(rows, cols)

    def body(st_ref, o_ref):
        acc = st_ref[0]
        for k in range(1, n):
            acc = acc + st_ref[k]
        o_ref[...] = acc

    return pl.pallas_call(
        body, name="sum_stack",
        grid=(rows // tr,),
        in_specs=[pl.BlockSpec((n, tr, cols), lambda i: (0, i, 0))],
        out_specs=pl.BlockSpec((tr, cols), lambda i: (i, 0)),
        out_shape=jax.ShapeDtypeStruct((rows, cols), F32),
        compiler_params=_params("parallel"),
    )(stacked)


def adamw(w, ga, gb, m, v):
    rows, cols = w.shape
    tr = _pick_rows(rows, cols, 256 * 1024)
    c1 = 1.0 - ADAM_B1 ** ADAM_STEP
    c2 = 1.0 - ADAM_B2 ** ADAM_STEP
    ins = [w, ga] + ([gb] if gb is not None else []) + [m, v]

    def body(*refs):
        w_ref, ga_ref = refs[0], refs[1]
        m_ref, v_ref, g_out, d_out, m_out, v_out = refs[-6:]
        g = ga_ref[...]
        if gb is not None:
            g = g + refs[2][...]
        mn = ADAM_B1 * m_ref[...] + (1.0 - ADAM_B1) * g
        vn = ADAM_B2 * v_ref[...] + (1.0 - ADAM_B2) * (g * g)
        m_hat = mn / c1
        v_hat = vn / c2
        g_out[...] = g
        d_out[...] = -ADAM_LR * (m_hat / (jnp.sqrt(v_hat) + ADAM_EPS) + ADAM_WD * w_ref[...])
        m_out[...] = mn
        v_out[...] = vn

    spec = pl.BlockSpec((tr, cols), lambda i: (i, 0))
    return pl.pallas_call(
        body, name="adamw",
        grid=(rows // tr,),
        in_specs=[spec] * len(ins),
        out_specs=[spec] * 4,
        out_shape=[jax.ShapeDtypeStruct((rows, cols), F32)] * 4,
        compiler_params=_params("parallel"),
    )(*ins)


WEIGHT_NAMES = ["ffn1_norm", "ffn1_w_gu", "ffn1_w_down", "mix_norm", "w_in", "q_lat_norm", "kv_lat_norm", "w_uq", "w_uk",
                "w_uv", "q_norm", "k_norm", "w_pool", "pool_scale", "w_out", "ffn2_norm", "ffn2_w_gu", "ffn2_w_down"]
COL_SHARDED = ["ffn1_w_gu", "w_in", "w_uq", "w_uk", "w_uv", "ffn2_w_gu"]
ROW_SHARDED = ["ffn1_w_down", "w_out", "ffn2_w_down"]
SHARDED = [n for n in WEIGHT_NAMES if n in COL_SHARDED or n in ROW_SHARDED]
REPLICATED = [n for n in WEIGHT_NAMES if n not in SHARDED]


def _join_chips(g, name):
    _, depth, r, c = g.shape
    if name in COL_SHARDED:
        return jnp.transpose(g, (1, 2, 0, 3)).reshape(depth, r, N_CHIPS * c)
    return jnp.transpose(g, (1, 0, 2, 3)).reshape(depth, N_CHIPS * r, c)


def _split_chips(full, name):
    depth, r, c = full.shape
    if name in COL_SHARDED:
        return jnp.transpose(full.reshape(depth, r, N_CHIPS, c // N_CHIPS), (2, 0, 1, 3))
    return jnp.transpose(full.reshape(depth, N_CHIPS, r // N_CHIPS, c), (1, 0, 2, 3))


def _pack(vs):
    flat = jnp.concatenate([v.reshape(-1) for v in vs])
    pad = (-flat.shape[0]) % (8 * 128)
    return jnp.pad(flat, (0, pad)).reshape(-1, 128)


def _unpack(packed, like):
    flat = packed.reshape(-1)
    out, at = [], 0
    for v in like:
        out.append(flat[at:at + v.size].reshape(v.shape))
        at += v.size
    return out


def kernel(x, ffn1_norm, ffn1_w_gu, ffn1_w_down, mix_norm, w_in, q_lat_norm, kv_lat_norm, w_uq, w_uk, w_uv, q_norm, k_norm, w_pool, pool_scale, w_out, ffn2_norm, ffn2_w_gu, ffn2_w_down, loss_target, m_ffn1_norm, m_ffn1_w_gu, m_ffn1_w_down, m_mix_norm, m_w_in, m_q_lat_norm, m_kv_lat_norm, m_w_uq, m_w_uk, m_w_uv, m_q_norm, m_k_norm, m_w_pool, m_pool_scale, m_w_out, m_ffn2_norm, m_ffn2_w_gu, m_ffn2_w_down, v_ffn1_norm, v_ffn1_w_gu, v_ffn1_w_down, v_mix_norm, v_w_in, v_q_lat_norm, v_kv_lat_norm, v_w_uq, v_w_uk, v_w_uv, v_q_norm, v_k_norm, v_w_pool, v_pool_scale, v_w_out, v_ffn2_norm, v_ffn2_w_gu, v_ffn2_w_down):
    args = locals()
    weights = {n: args[n] for n in WEIGHT_NAMES}
    moments_m = {n: args["m_" + n] for n in WEIGHT_NAMES}
    moments_v = {n: args["v_" + n] for n in WEIGHT_NAMES}

    gathered = dict(zip(SHARDED, gather_chips([weights[n].astype(BF16) for n in SHARDED])))
    full = {n: _join_chips(gathered[n], n) for n in SHARDED if "ffn" not in n}
    for n in REPLICATED:
        full[n] = weights[n].astype(BF16) if n == "w_pool" else weights[n]
    w = layout_weights(full)
    wd1 = _join_chips(gathered["ffn1_w_down"], "ffn1_w_down")
    wd2 = _join_chips(gathered["ffn2_w_down"], "ffn2_w_down")

    loss_part, grad_x, grads = local_step(x[0], loss_target[0], w, gathered["ffn1_w_gu"], wd1, gathered["ffn2_w_gu"], wd2)
    loss = lax.psum(loss_part[0, 0], ("x", "y", "c"))

    chip = 2 * lax.axis_index("x") + lax.axis_index("y")
    split = [_split_chips(grads[n], n) for n in SHARDED]
    landed = scatter_chips(split)
    partial = []
    for g4, land in zip(split, landed):
        own = lax.dynamic_index_in_dim(g4, chip, 0, keepdims=False)
        rows = own.shape[0] * own.shape[1]
        partial.append(sum_rows_of(own.reshape(rows, -1), land.reshape(3, rows, -1)))
    from_sibling = swap_sibling(partial)
    out = {}
    for n, mine, theirs in zip(SHARDED, partial, from_sibling):
        shape = weights[n].shape
        flat = lambda a: a.reshape(mine.shape)
        res = adamw(flat(weights[n]), mine, theirs, flat(moments_m[n]), flat(moments_v[n]))
        out[n] = [r.reshape(shape) for r in res]

    rep_g = [grads[n] for n in REPLICATED]
    total = sum_stack(gather_all(_pack(rep_g)))
    res = adamw(_pack([weights[n] for n in REPLICATED]), total, None,
                _pack([moments_m[n] for n in REPLICATED]), _pack([moments_v[n] for n in REPLICATED]))
    unpacked = [_unpack(r, rep_g) for r in res]
    for k, n in enumerate(REPLICATED):
        out[n] = [u[k] for u in unpacked]

    return (loss, grad_x[None], *[out[n][0] for n in WEIGHT_NAMES], *[out[n][1] for n in WEIGHT_NAMES],
            *[out[n][2] for n in WEIGHT_NAMES], *[out[n][3] for n in WEIGHT_NAMES])
```

```python
import functools

import jax
import jax.numpy as jnp
from jax import lax
from jax.experimental import pallas as pl
from jax.experimental.pallas import tpu as pltpu

F32 = jnp.float32
BF16 = jnp.bfloat16
MESH = pl.DeviceIdType.MESH

DEPTH = 4
D_MODEL = 1024
D_FF = 2816
FF_BLOCK = D_FF // 2
N_CHIPS = 4
HEADS = 8
HEAD_PAD = 128
QK_NOPE = 64
QK_ROPE = 32
QK_HEAD = QK_NOPE + QK_ROPE
V_HEAD = 64
Q_LORA = 384
KV_LORA = 256
POOL_W = 512
POOL_WINDOWS = (2, 4, 8, 16)
POOL_HALO = 8
GROUP = 128
D_IN = Q_LORA + KV_LORA + QK_ROPE + POOL_W
Z_W = 1280
Z_CKV = Q_LORA
Z_POOL = Q_LORA + KV_LORA
Z_KPE = Z_POOL + POOL_W
ATT_W = HEADS * HEAD_PAD
CAT_W = ATT_W + POOL_W
EPS = 1e-6
ROPE_THETA = 10000.0
ADAM_LR, ADAM_B1, ADAM_B2, ADAM_EPS, ADAM_WD, ADAM_STEP = 0.001, 0.9, 0.999, 1e-08, 0.01, 10

VMEM_LIMIT = 56 * 1024 * 1024


def _params(*sem):
    return pltpu.CompilerParams(dimension_semantics=sem, vmem_limit_bytes=VMEM_LIMIT)


def _dot(a, b):
    return jnp.dot(a, b, preferred_element_type=F32)


def _dot_nt(a, b):
    return lax.dot_general(a, b, (((1,), (1,)), ((), ())), preferred_element_type=F32)


def _dot_tn(a, b):
    return lax.dot_general(a, b, (((0,), (0,)), ((), ())), preferred_element_type=F32)


def _norm_fwd(x, n):
    rstd = lax.rsqrt(jnp.sum(x * x, axis=-1, keepdims=True) * (1.0 / n) + EPS)
    return x * rstd, rstd


def _norm_bwd(dy, g, xhat, rstd, n):
    dxhat = dy * g
    dx = rstd * (dxhat - xhat * (jnp.sum(dxhat * xhat, axis=-1, keepdims=True) * (1.0 / n)))
    return dx, dy * xhat


def _row_tile(s, want):
    t = min(s, want)
    assert s % t == 0
    return t


def ffn_fwd(x, gn, wgu4, wd, layer, tm=512):
    s = x.shape[0]
    tm = _row_tile(s, tm)

    def body(x_ref, gn_ref, wg_ref, wu_ref, wd_ref, o_ref, h_s, acc_s):
        j = pl.program_id(1)

        @pl.when(j == 0)
        def _():
            xhat, _ = _norm_fwd(x_ref[...], D_MODEL)
            h_s[...] = (xhat * gn_ref[...]).astype(BF16)
            acc_s[...] = jnp.zeros_like(acc_s)

        h = h_s[...]
        g = _dot(h, wg_ref[...])
        u = _dot(h, wu_ref[...])
        a = (g * jax.nn.sigmoid(g) * u).astype(BF16)
        acc_s[...] += _dot(a, wd_ref[...])

        @pl.when(j == 1)
        def _():
            o_ref[...] = x_ref[...] + 0.5 * acc_s[...]

    return pl.pallas_call(
        body, name="ffn_fwd",
        grid=(s // tm, 2),
        in_specs=[
            pl.BlockSpec((tm, D_MODEL), lambda i, j: (i, 0)),
            pl.BlockSpec((1, D_MODEL), lambda i, j: (0, 0)),
            pl.BlockSpec((None, None, D_MODEL, FF_BLOCK), lambda i, j: (j, layer, 0, 0)),
            pl.BlockSpec((None, None, D_MODEL, FF_BLOCK), lambda i, j: (j + 2, layer, 0, 0)),
            pl.BlockSpec((None, FF_BLOCK, D_MODEL), lambda i, j: (layer, j, 0)),
        ],
        out_specs=pl.BlockSpec((tm, D_MODEL), lambda i, j: (i, 0)),
        out_shape=jax.ShapeDtypeStruct((s, D_MODEL), F32),
        scratch_shapes=[pltpu.VMEM((tm, D_MODEL), BF16), pltpu.VMEM((tm, D_MODEL), F32)],
        compiler_params=_params("parallel", "arbitrary"),
    )(x, gn, wgu4, wgu4, wd)


def ffn_bwd(x, dy, gn, wgu4, wd, layer, tm=256):
    s = x.shape[0]
    tm = _row_tile(s, tm)

    def body(x_ref, dy_ref, gn_ref, wg_ref, wu_ref, wd_ref,
             dx_ref, h_ref, a_ref, dg_ref, du_ref, dyh_ref, dgn_ref, dh_s):
        i, j = pl.program_id(0), pl.program_id(1)

        @pl.when(j == 0)
        def _():
            xhat, _ = _norm_fwd(x_ref[...], D_MODEL)
            h_ref[...] = (xhat * gn_ref[...]).astype(BF16)
            dyh_ref[...] = (0.5 * dy_ref[...]).astype(BF16)
            dh_s[...] = jnp.zeros_like(dh_s)

        h = h_ref[...]
        g = _dot(h, wg_ref[...])
        u = _dot(h, wu_ref[...])
        sg = jax.nn.sigmoid(g)
        silu = g * sg
        a_ref[...] = (silu * u).astype(BF16)
        da = _dot_nt(dyh_ref[...], wd_ref[...])
        dg = (da * u * (sg * (1.0 + g * (1.0 - sg)))).astype(BF16)
        du = (da * silu).astype(BF16)
        dg_ref[...] = dg
        du_ref[...] = du
        dh_s[...] += _dot_nt(dg, wg_ref[...]) + _dot_nt(du, wu_ref[...])

        @pl.when(j == 1)
        def _():
            xhat, rstd = _norm_fwd(x_ref[...], D_MODEL)
            dxn, dgrow = _norm_bwd(dh_s[...], gn_ref[...], xhat, rstd, D_MODEL)
            dx_ref[...] = dy_ref[...] + dxn
            part = jnp.sum(dgrow, axis=0, keepdims=True)

            @pl.when(i == 0)
            def _():
                dgn_ref[...] = part

            @pl.when(i > 0)
            def _():
                dgn_ref[...] += part

    row = lambda i, j: (i, 0)
    return pl.pallas_call(
        body, name="ffn_bwd",
        grid=(s // tm, 2),
        in_specs=[
            pl.BlockSpec((tm, D_MODEL), row),
            pl.BlockSpec((tm, D_MODEL), row),
            pl.BlockSpec((1, D_MODEL), lambda i, j: (0, 0)),
            pl.BlockSpec((None, None, D_MODEL, FF_BLOCK), lambda i, j: (j, layer, 0, 0)),
            pl.BlockSpec((None, None, D_MODEL, FF_BLOCK), lambda i, j: (j + 2, layer, 0, 0)),
            pl.BlockSpec((None, FF_BLOCK, D_MODEL), lambda i, j: (layer, j, 0)),
        ],
        out_specs=[
            pl.BlockSpec((tm, D_MODEL), row),
            pl.BlockSpec((tm, D_MODEL), row),
            pl.BlockSpec((tm, FF_BLOCK), lambda i, j: (i, j)),
            pl.BlockSpec((tm, FF_BLOCK), lambda i, j: (i, j)),
            pl.BlockSpec((tm, FF_BLOCK), lambda i, j: (i, j)),
            pl.BlockSpec((tm, D_MODEL), row),
            pl.BlockSpec((1, D_MODEL), lambda i, j: (0, 0)),
        ],
        out_shape=[
            jax.ShapeDtypeStruct((s, D_MODEL), F32),
            jax.ShapeDtypeStruct((s, D_MODEL), BF16),
            jax.ShapeDtypeStruct((s, D_FF), BF16),
            jax.ShapeDtypeStruct((s, D_FF), BF16),
            jax.ShapeDtypeStruct((s, D_FF), BF16),
            jax.ShapeDtypeStruct((s, D_MODEL), BF16),
            jax.ShapeDtypeStruct((1, D_MODEL), F32),
        ],
        scratch_shapes=[pltpu.VMEM((tm, D_MODEL), F32)],
        compiler_params=_params("arbitrary", "arbitrary"),
    )(x, dy, gn, wgu4, wgu4, wd)


def matmul_tn(a, b, tk=512, tn_max=1536):
    s, m = a.shape
    n = b.shape[1]
    tk = _row_tile(s, tk)
    tn = n
    if n > tn_max:
        tn = n // 2
    assert n % tn == 0 and tn % 128 == 0
    nk = s // tk

    def body(a_ref, b_ref, o_ref, acc_s):
        k = pl.program_id(1)

        @pl.when(k == 0)
        def _():
            acc_s[...] = jnp.zeros_like(acc_s)

        acc_s[...] += _dot_tn(a_ref[...].astype(BF16), b_ref[...].astype(BF16))

        @pl.when(k == nk - 1)
        def _():
            o_ref[...] = acc_s[...]

    return pl.pallas_call(
        body, name="matmul_tn",
        grid=(n // tn, nk),
        in_specs=[pl.BlockSpec((tk, m), lambda j, k: (k, 0)), pl.BlockSpec((tk, tn), lambda j, k: (k, j))],
        out_specs=pl.BlockSpec((m, tn), lambda j, k: (0, j)),
        out_shape=jax.ShapeDtypeStruct((m, n), F32),
        scratch_shapes=[pltpu.VMEM((m, tn), F32)],
        compiler_params=_params("parallel", "arbitrary"),
    )(a, b)


def _rope_fwd(y, c, s1, s2):
    return y * c + pltpu.roll(y, HEAD_PAD - 16, 1) * s1 + pltpu.roll(y, 16, 1) * s2


def _rope_bwd(dy, c, s1, s2):
    return dy * c + pltpu.roll(dy * s1, 16, 1) + pltpu.roll(dy * s2, HEAD_PAD - 16, 1)


def rope_tables(s):
    pos = jnp.arange(s, dtype=F32)
    inv = ROPE_THETA ** (-jnp.arange(0, QK_ROPE, 2, dtype=F32) / QK_ROPE)
    ang = pos[:, None] * inv[None, :]
    cos, sin = jnp.cos(ang), jnp.sin(ang)
    z16 = jnp.zeros((s, 16), F32)
    c = jnp.concatenate([jnp.ones((s, QK_NOPE), F32), cos, cos, z16, z16], axis=1)
    s1 = jnp.concatenate([jnp.zeros((s, QK_NOPE), F32), -sin, z16, z16, z16], axis=1)
    s2 = jnp.concatenate([jnp.zeros((s, QK_NOPE), F32), z16, sin, z16, z16], axis=1)
    return c, s1, s2


def mix_in_fwd(x, gm, win, qln, kvln, wuq, wuk, wuv, qn, kn, rc, rs1, rs2, tm=512):
    s = x.shape[0]
    tm = _row_tile(s, tm)
    scale = QK_HEAD ** -0.5

    def body(x_ref, gm_ref, win_ref, qln_ref, kvln_ref, wuq_ref, wuk_ref, wuv_ref, qn_ref, kn_ref,
             rc_ref, rs1_ref, rs2_ref, z_ref, q_ref, k_ref, v_ref, vt_ref):
        xhat, _ = _norm_fwd(x_ref[...], D_MODEL)
        h = (xhat * gm_ref[...]).astype(BF16)
        z = _dot(h, win_ref[...])
        z_ref[...] = z
        cq, _ = _norm_fwd(z[:, :Q_LORA], Q_LORA)
        cqn = (cq * qln_ref[...]).astype(BF16)
        ckv, _ = _norm_fwd(z[:, Z_CKV:Z_POOL], KV_LORA)
        ckvn = (ckv * kvln_ref[...]).astype(BF16)
        kpe = z[:, Z_KPE:]
        q_raw = _dot(cqn, wuq_ref[...])
        k_raw = _dot(ckvn, wuk_ref[...])
        v = _dot(ckvn, wuv_ref[...])
        v_ref[...] = v.astype(BF16)
        vt_ref[...] = jnp.transpose(v).astype(BF16)
        c, s1, s2 = rc_ref[...], rs1_ref[...], rs2_ref[...]
        for hd in range(HEADS):
            lanes = slice(hd * HEAD_PAD, (hd + 1) * HEAD_PAD)
            qh, _ = _norm_fwd(q_raw[:, lanes], QK_HEAD)
            q_ref[:, lanes] = (_rope_fwd(qh * qn_ref[...], c, s1, s2) * scale).astype(BF16)
            kh, _ = _norm_fwd(k_raw[:, lanes] + kpe, QK_HEAD)
            k_ref[:, lanes] = _rope_fwd(kh * kn_ref[...], c, s1, s2).astype(BF16)

    full = lambda shape: pl.BlockSpec(shape, lambda i: (0,) * len(shape))
    row = lambda w: pl.BlockSpec((tm, w), lambda i: (i, 0))
    return pl.pallas_call(
        body, name="mix_in_fwd",
        grid=(s // tm,),
        in_specs=[row(D_MODEL), full((1, D_MODEL)), full((D_MODEL, Z_W)), full((1, Q_LORA)), full((1, KV_LORA)),
                  full((Q_LORA, ATT_W)), full((KV_LORA, ATT_W)), full((KV_LORA, ATT_W)),
                  full((1, HEAD_PAD)), full((1, HEAD_PAD)), row(HEAD_PAD), row(HEAD_PAD), row(HEAD_PAD)],
        out_specs=[row(Z_W), row(ATT_W), row(ATT_W), row(ATT_W), pl.BlockSpec((ATT_W, tm), lambda i: (0, i))],
        out_shape=[jax.ShapeDtypeStruct((s, Z_W), F32)] + [jax.ShapeDtypeStruct((s, ATT_W), BF16)] * 3 + [
            jax.ShapeDtypeStruct((ATT_W, s), BF16)],
        compiler_params=_params("parallel"),
    )(x, gm, win, qln, kvln, wuq, wuk, wuv, qn, kn, rc, rs1, rs2)


def mla_bwd(z, dq_t, dk, dv, dp, qln, kvln, wuq, wuk, wuv, qn, kn, rc, rs1, rs2, tm=512):
    s = z.shape[0]
    tm = _row_tile(s, tm)
    scale = QK_HEAD ** -0.5

    def body(z_ref, dqt_ref, dk_ref, dv_ref, dp_ref, qln_ref, kvln_ref, wuq_ref, wuk_ref, wuv_ref, qn_ref, kn_ref,
             rc_ref, rs1_ref, rs2_ref,
             dz_ref, cqn_ref, ckvn_ref, dqr_ref, dkr_ref, dvb_ref, dqln_ref, dkvln_ref, dqn_ref, dkn_ref):
        i = pl.program_id(0)
        z = z_ref[...]
        cq, cq_rstd = _norm_fwd(z[:, :Q_LORA], Q_LORA)
        cqn = (cq * qln_ref[...]).astype(BF16)
        ckv, ckv_rstd = _norm_fwd(z[:, Z_CKV:Z_POOL], KV_LORA)
        ckvn = (ckv * kvln_ref[...]).astype(BF16)
        kpe = z[:, Z_KPE:]
        cqn_ref[...] = cqn
        ckvn_ref[...] = ckvn
        q_raw = _dot(cqn, wuq_ref[...])
        k_raw = _dot(ckvn, wuk_ref[...])
        c, s1, s2 = rc_ref[...], rs1_ref[...], rs2_ref[...]
        lane = lax.broadcasted_iota(jnp.int32, (tm, HEAD_PAD), 1)
        rope_lanes = (lane >= QK_NOPE) & (lane < QK_HEAD)
        dkpe = jnp.zeros((tm, HEAD_PAD), F32)
        dqn = jnp.zeros((tm, HEAD_PAD), F32)
        dkn = jnp.zeros((tm, HEAD_PAD), F32)
        for hd in range(HEADS):
            lanes = slice(hd * HEAD_PAD, (hd + 1) * HEAD_PAD)
            qh, q_rstd = _norm_fwd(q_raw[:, lanes], QK_HEAD)
            dqh = jnp.transpose(dqt_ref[lanes, :]) * scale
            dqr, dg = _norm_bwd(_rope_bwd(dqh, c, s1, s2), qn_ref[...], qh, q_rstd, QK_HEAD)
            dqn += dg
            dqr_ref[:, lanes] = dqr.astype(BF16)
            kh, k_rstd = _norm_fwd(k_raw[:, lanes] + kpe, QK_HEAD)
            dkr, dg = _norm_bwd(_rope_bwd(dk_ref[:, lanes], c, s1, s2), kn_ref[...], kh, k_rstd, QK_HEAD)
            dkn += dg
            dkr_ref[:, lanes] = dkr.astype(BF16)
            dkpe += jnp.where(rope_lanes, dkr, 0.0)
        dvb = dv_ref[...].astype(BF16)
        dvb_ref[...] = dvb
        dcqn = _dot_nt(dqr_ref[...], wuq_ref[...])
        dckvn = _dot_nt(dkr_ref[...], wuk_ref[...]) + _dot_nt(dvb, wuv_ref[...])
        dcq, dqln = _norm_bwd(dcqn, qln_ref[...], cq, cq_rstd, Q_LORA)
        dckv, dkvln = _norm_bwd(dckvn, kvln_ref[...], ckv, ckv_rstd, KV_LORA)
        dz_ref[:, :Q_LORA] = dcq.astype(BF16)
        dz_ref[:, Z_CKV:Z_POOL] = dckv.astype(BF16)
        dz_ref[:, Z_POOL:Z_KPE] = dp_ref[...].astype(BF16)
        dz_ref[:, Z_KPE:] = dkpe.astype(BF16)
        parts = [(dqln_ref, dqln), (dkvln_ref, dkvln), (dqn_ref, dqn), (dkn_ref, dkn)]

        @pl.when(i == 0)
        def _():
            for ref, val in parts:
                ref[...] = jnp.sum(val, axis=0, keepdims=True)

        @pl.when(i > 0)
        def _():
            for ref, val in parts:
                ref[...] += jnp.sum(val, axis=0, keepdims=True)

    full = lambda shape: pl.BlockSpec(shape, lambda i: (0,) * len(shape))
    row = lambda w: pl.BlockSpec((tm, w), lambda i: (i, 0))
    return pl.pallas_call(
        body, name="mla_bwd",
        grid=(s // tm,),
        in_specs=[row(Z_W), pl.BlockSpec((ATT_W, tm), lambda i: (0, i)), row(ATT_W), row(ATT_W), row(POOL_W),
                  full((1, Q_LORA)), full((1, KV_LORA)), full((Q_LORA, ATT_W)), full((KV_LORA, ATT_W)),
                  full((KV_LORA, ATT_W)), full((1, HEAD_PAD)), full((1, HEAD_PAD)),
                  row(HEAD_PAD), row(HEAD_PAD), row(HEAD_PAD)],
        out_specs=[row(Z_W), row(Q_LORA), row(KV_LORA), row(ATT_W), row(ATT_W), row(ATT_W),
                   full((1, Q_LORA)), full((1, KV_LORA)), full((1, HEAD_PAD)), full((1, HEAD_PAD))],
        out_shape=[jax.ShapeDtypeStruct((s, Z_W), BF16), jax.ShapeDtypeStruct((s, Q_LORA), BF16),
                   jax.ShapeDtypeStruct((s, KV_LORA), BF16)] + [jax.ShapeDtypeStruct((s, ATT_W), BF16)] * 3 + [
                   jax.ShapeDtypeStruct((1, Q_LORA), F32), jax.ShapeDtypeStruct((1, KV_LORA), F32),
                   jax.ShapeDtypeStruct((1, HEAD_PAD), F32), jax.ShapeDtypeStruct((1, HEAD_PAD), F32)],
        compiler_params=_params("arbitrary"),
    )(z, dq_t, dk, dv, dp, qln, kvln, wuq, wuk, wuv, qn, kn, rc, rs1, rs2)


def mix_in_bwd(x, dx_in, dz, gm, win, tm=512):
    s = x.shape[0]
    tm = _row_tile(s, tm)

    def body(x_ref, dxin_ref, dz_ref, gm_ref, win_ref, dx_ref, h_ref, dgm_ref):
        i = pl.program_id(0)
        xhat, rstd = _norm_fwd(x_ref[...], D_MODEL)
        h_ref[...] = (xhat * gm_ref[...]).astype(BF16)
        dh = _dot_nt(dz_ref[...], win_ref[...])
        dxn, dgrow = _norm_bwd(dh, gm_ref[...], xhat, rstd, D_MODEL)
        dx_ref[...] = dxin_ref[...] + dxn
        part = jnp.sum(dgrow, axis=0, keepdims=True)

        @pl.when(i == 0)
        def _():
            dgm_ref[...] = part

        @pl.when(i > 0)
        def _():
            dgm_ref[...] += part

    full = lambda shape: pl.BlockSpec(shape, lambda i: (0,) * len(shape))
    row = lambda w: pl.BlockSpec((tm, w), lambda i: (i, 0))
    return pl.pallas_call(
        body, name="mix_in_bwd",
        grid=(s // tm,),
        in_specs=[row(D_MODEL), row(D_MODEL), row(Z_W), full((1, D_MODEL)), full((D_MODEL, Z_W))],
        out_specs=[row(D_MODEL), row(D_MODEL), full((1, D_MODEL))],
        out_shape=[jax.ShapeDtypeStruct((s, D_MODEL), F32), jax.ShapeDtypeStruct((s, D_MODEL), BF16),
                   jax.ShapeDtypeStruct((1, D_MODEL), F32)],
        compiler_params=_params("arbitrary"),
    )(x, dx_in, dz, gm, win)


ATT_CHUNK = 512


def flash_fwd(q, k, vt, tq=1024, tk=4096, chunk=ATT_CHUNK):
    s = q.shape[0]
    tq, tk = _row_tile(s, tq), _row_tile(s, tk)
    nk = s // tk
    tc = _row_tile(tk, chunk)
    sub = tk // tc

    def body(q_ref, k_ref, vt_ref, o_ref, lse_ref, m_s, l_s, acc_s):
        kk = pl.program_id(2)

        @pl.when(kk == 0)
        def _():
            m_s[...] = jnp.full_like(m_s, -jnp.inf)
            l_s[...] = jnp.zeros_like(l_s)
            acc_s[...] = jnp.zeros_like(acc_s)

        q_t = q_ref[...]

        def scores(c):
            return _dot_nt(k_ref[c * tc:(c + 1) * tc, :], q_t)

        m, l, acc = m_s[...], l_s[...], acc_s[...]
        s_next = scores(0)
        for c in range(sub):
            s_t = s_next
            if c + 1 < sub:
                s_next = scores(c + 1)
            m_new = jnp.maximum(m, jnp.max(s_t, axis=0, keepdims=True))
            alpha = jnp.exp(m - m_new)
            p_t = jnp.exp(s_t - m_new)
            l = alpha * l + jnp.sum(p_t, axis=0, keepdims=True)
            acc = alpha * acc + _dot(vt_ref[:, c * tc:(c + 1) * tc], p_t.astype(BF16))
            m = m_new
        m_s[...], l_s[...], acc_s[...] = m, l, acc

        @pl.when(kk == nk - 1)
        def _():
            o_ref[...] = jnp.transpose(acc_s[...] / l_s[...]).astype(BF16)
            lse_ref[...] = m_s[...] + jnp.log(l_s[...])

    return pl.pallas_call(
        body, name="flash_fwd",
        grid=(HEADS, s // tq, nk),
        in_specs=[pl.BlockSpec((tq, HEAD_PAD), lambda h, i, kk: (i, h)),
                  pl.BlockSpec((tk, HEAD_PAD), lambda h, i, kk: (kk, h)),
                  pl.BlockSpec((HEAD_PAD, tk), lambda h, i, kk: (h, kk))],
        out_specs=[pl.BlockSpec((tq, HEAD_PAD), lambda h, i, kk: (i, h)),
                   pl.BlockSpec((None, 1, tq), lambda h, i, kk: (h, 0, i))],
        out_shape=[jax.ShapeDtypeStruct((s, ATT_W), BF16), jax.ShapeDtypeStruct((HEADS, 1, s), F32)],
        scratch_shapes=[pltpu.VMEM((1, tq), F32), pltpu.VMEM((1, tq), F32), pltpu.VMEM((HEAD_PAD, tq), F32)],
        compiler_params=_params("parallel", "parallel", "arbitrary"),
    )(q, k, vt)


def flash_bwd(q, k, v, do, lse_row, delta_row, tq=2048, tk=1024, chunk=ATT_CHUNK):
    s = q.shape[0]
    tq, tk = _row_tile(s, tq), _row_tile(s, tk)
    nq = s // tq
    tc = _row_tile(tq, chunk)
    sub = tq // tc

    def body(q_ref, k_ref, v_ref, do_ref, lse_ref, dl_ref, dqt_ref, dk_ref, dv_ref, kt_s, dk_s, dv_s):
        kb, qb = pl.program_id(1), pl.program_id(2)

        @pl.when(qb == 0)
        def _():
            kt_s[...] = jnp.transpose(k_ref[...].astype(F32)).astype(BF16)
            dk_s[...] = jnp.zeros_like(dk_s)
            dv_s[...] = jnp.zeros_like(dv_s)

        k_t, v_t = k_ref[...], v_ref[...]

        def products(c):
            rows = slice(c * tc, (c + 1) * tc)
            return _dot_nt(k_t, q_ref[rows, :]), _dot_nt(v_t, do_ref[rows, :])

        dk, dv = dk_s[...], dv_s[...]
        nxt = products(0)
        contribs = []
        for c in range(sub):
            rows = slice(c * tc, (c + 1) * tc)
            s_t, dp_t = nxt
            if c + 1 < sub:
                nxt = products(c + 1)
            p_t = jnp.exp(s_t - lse_ref[:, rows])
            ds_t = (p_t * (dp_t - dl_ref[:, rows])).astype(BF16)
            dv = dv + _dot(p_t.astype(BF16), do_ref[rows, :])
            dk = dk + _dot(ds_t, q_ref[rows, :])
            contribs.append(_dot(kt_s[...], ds_t))
        dk_s[...], dv_s[...] = dk, dv
        contrib = jnp.concatenate(contribs, axis=1) if sub > 1 else contribs[0]
        cols = pl.ds(pl.multiple_of(qb * tq, tq), tq)

        @pl.when(kb == 0)
        def _():
            dqt_ref[:, cols] = contrib

        @pl.when(kb > 0)
        def _():
            dqt_ref[:, cols] += contrib

        @pl.when(qb == nq - 1)
        def _():
            dk_ref[...] = dk_s[...]
            dv_ref[...] = dv_s[...]

    qspec = pl.BlockSpec((tq, HEAD_PAD), lambda h, kb, qb: (qb, h))
    kspec = pl.BlockSpec((tk, HEAD_PAD), lambda h, kb, qb: (kb, h))
    rowspec = pl.BlockSpec((None, 1, tq), lambda h, kb, qb: (h, 0, qb))
    return pl.pallas_call(
        body, name="flash_bwd",
        grid=(HEADS, s // tk, nq),
        in_specs=[qspec, kspec, kspec, qspec, rowspec, rowspec],
        out_specs=[pl.BlockSpec((HEAD_PAD, s), lambda h, kb, qb: (h, 0)), kspec, kspec],
        out_shape=[jax.ShapeDtypeStruct((ATT_W, s), F32), jax.ShapeDtypeStruct((s, ATT_W), F32),
                   jax.ShapeDtypeStruct((s, ATT_W), F32)],
        scratch_shapes=[pltpu.VMEM((HEAD_PAD, tk), BF16), pltpu.VMEM((tk, HEAD_PAD), F32),
                        pltpu.VMEM((tk, HEAD_PAD), F32)],
        compiler_params=_params("parallel", "arbitrary", "arbitrary"),
    )(q, k, v, do, lse_row, delta_row)


def _halo_specs(tm, w, n_tiles):
    per = tm // POOL_HALO
    last = n_tiles * per - 1
    return [pl.BlockSpec((POOL_HALO, w), lambda i: (jnp.maximum(i * per - 1, 0), 0)),
            pl.BlockSpec((tm, w), lambda i: (i, 0)),
            pl.BlockSpec((POOL_HALO, w), lambda i: (jnp.minimum((i + 1) * per, last), 0))]


def _fill_ext(ext_ref, prev, cur, nxt, i, n_tiles, tm):
    ext_ref[pl.ds(0, POOL_HALO), :] = jnp.where(i > 0, prev, 0.0)
    ext_ref[pl.ds(POOL_HALO, tm), :] = cur
    ext_ref[pl.ds(POOL_HALO + tm, POOL_HALO), :] = jnp.where(i < n_tiles - 1, nxt, 0.0)


def _window_sum(ext_ref, g, lo, hi, tm):
    lanes = pl.ds(g * GROUP, GROUP)
    acc = ext_ref[pl.ds(POOL_HALO + lo, tm), lanes]
    for d in range(lo + 1, hi + 1):
        acc = acc + ext_ref[pl.ds(POOL_HALO + d, tm), lanes]
    return acc


def _pool_mixed(ext_ref, cur, i, s, tm):
    row = i * tm + lax.broadcasted_iota(jnp.int32, (tm, 1), 0)
    out = []
    for g, w in enumerate(POOL_WINDOWS):
        left = w // 2
        right = w - 1 - left
        cnt = (jnp.minimum(row + right + 1, s) - jnp.maximum(row - left, 0)).astype(F32)
        out.append(_window_sum(ext_ref, g, -left, right, tm) / cnt - cur[:, g * GROUP:(g + 1) * GROUP])
    return out


def pool_out_fwd(x, z, o, wpool, pscale, wout, tm=512):
    s = x.shape[0]
    tm = _row_tile(s, tm)
    n_tiles = s // tm

    def body(x_ref, zp_ref, z_ref, zn_ref, o_ref, wp_ref, ps_ref, wo_ref, y_ref, cat_ref, ext_s):
        i = pl.program_id(0)
        cur = z_ref[:, Z_POOL:Z_KPE]
        _fill_ext(ext_s, zp_ref[:, Z_POOL:Z_KPE], cur, zn_ref[:, Z_POOL:Z_KPE], i, n_tiles, tm)
        mixed = _pool_mixed(ext_s, cur, i, s, tm)
        cat_ref[:, :ATT_W] = o_ref[...]
        for g in range(len(POOL_WINDOWS)):
            lanes = slice(g * GROUP, (g + 1) * GROUP)
            yg = _dot(mixed[g].astype(BF16), wp_ref[g])
            cat_ref[:, ATT_W + g * GROUP:ATT_W + (g + 1) * GROUP] = (yg * ps_ref[:, lanes]).astype(BF16)
        y_ref[...] = x_ref[...] + _dot(cat_ref[...], wo_ref[...])

    full = lambda shape: pl.BlockSpec(shape, lambda i: (0,) * len(shape))
    row = lambda w: pl.BlockSpec((tm, w), lambda i: (i, 0))
    return pl.pallas_call(
        body, name="pool_out_fwd",
        grid=(n_tiles,),
        in_specs=[row(D_MODEL)] + _halo_specs(tm, Z_W, n_tiles) + [
            row(ATT_W), full((len(POOL_WINDOWS), GROUP, GROUP)), full((1, POOL_W)), full((CAT_W, D_MODEL))],
        out_specs=[row(D_MODEL), row(CAT_W)],
        out_shape=[jax.ShapeDtypeStruct((s, D_MODEL), F32), jax.ShapeDtypeStruct((s, CAT_W), BF16)],
        scratch_shapes=[pltpu.VMEM((tm + 2 * POOL_HALO, POOL_W), F32)],
        compiler_params=_params("parallel"),
    )(x, z, z, z, o, wpool, pscale, wout)


def out_bwd(dx, wout, cat, tm=512):
    s = dx.shape[0]
    tm = _row_tile(s, tm)

    def body(dx_ref, wo_ref, cat_ref, do_ref, db_ref, dl_ref):
        dcat = _dot_nt(dx_ref[...].astype(BF16), wo_ref[...])
        do_ref[...] = dcat[:, :ATT_W].astype(BF16)
        db_ref[...] = dcat[:, ATT_W:]
        for hd in range(HEADS):
            lanes = slice(hd * HEAD_PAD, (hd + 1) * HEAD_PAD)
            dl_ref[hd] = jnp.sum(dcat[:, lanes] * cat_ref[:, lanes].astype(F32), axis=-1, keepdims=True)

    row = lambda w: pl.BlockSpec((tm, w), lambda i: (i, 0))
    return pl.pallas_call(
        body, name="out_bwd",
        grid=(s // tm,),
        in_specs=[row(D_MODEL), pl.BlockSpec((CAT_W, D_MODEL), lambda i: (0, 0)), row(CAT_W)],
        out_specs=[row(ATT_W), row(POOL_W), pl.BlockSpec((HEADS, tm, 1), lambda i: (0, i, 0))],
        out_shape=[jax.ShapeDtypeStruct((s, ATT_W), BF16), jax.ShapeDtypeStruct((s, POOL_W), F32),
                   jax.ShapeDtypeStruct((HEADS, s, 1), F32)],
        compiler_params=_params("parallel"),
    )(dx, wout, cat)


def pool_bwd(db, z, wpool, pscale, tm=512):
    s = db.shape[0]
    tm = _row_tile(s, tm)
    n_tiles = s // tm
    te = tm + 2 * POOL_HALO

    def body(dbp_ref, db_ref, dbn_ref, zp_ref, z_ref, zn_ref, wp_ref, ps_ref,
             dp_ref, mixed_ref, dys_ref, dps_ref, ext_s, text_s):
        i = pl.program_id(0)
        cur = z_ref[:, Z_POOL:Z_KPE]
        _fill_ext(ext_s, zp_ref[:, Z_POOL:Z_KPE], cur, zn_ref[:, Z_POOL:Z_KPE], i, n_tiles, tm)
        mixed = _pool_mixed(ext_s, cur, i, s, tm)
        _fill_ext(ext_s, dbp_ref[...], db_ref[...], dbn_ref[...], i, n_tiles, tm)
        erow = i * tm - POOL_HALO + lax.broadcasted_iota(jnp.int32, (te, 1), 0)
        dps = []
        for g, w in enumerate(POOL_WINDOWS):
            lanes = slice(g * GROUP, (g + 1) * GROUP)
            left = w // 2
            right = w - 1 - left
            mb = mixed[g].astype(BF16)
            mixed_ref[:, lanes] = mb
            dps.append(jnp.sum(db_ref[:, lanes] * _dot(mb, wp_ref[g]), axis=0, keepdims=True))
            dys_e = (ext_s[:, lanes] * ps_ref[:, lanes]).astype(BF16)
            dys_ref[:, lanes] = (db_ref[:, lanes] * ps_ref[:, lanes]).astype(BF16)
            dmix_e = _dot_nt(dys_e, wp_ref[g])
            cnt = jnp.minimum(erow + right + 1, s) - jnp.maximum(erow - left, 0)
            text_s[:, lanes] = dmix_e / jnp.maximum(cnt, 1).astype(F32)
            dp_ref[:, lanes] = _window_sum(text_s, g, -right, left, tm) - dmix_e[POOL_HALO:POOL_HALO + tm]
        part = jnp.concatenate(dps, axis=1)

        @pl.when(i == 0)
        def _():
            dps_ref[...] = part

        @pl.when(i > 0)
        def _():
            dps_ref[...] += part

    full = lambda shape: pl.BlockSpec(shape, lambda i: (0,) * len(shape))
    row = lambda w: pl.BlockSpec((tm, w), lambda i: (i, 0))
    return pl.pallas_call(
        body, name="pool_bwd",
        grid=(n_tiles,),
        in_specs=_halo_specs(tm, POOL_W, n_tiles) + _halo_specs(tm, Z_W, n_tiles) + [
            full((len(POOL_WINDOWS), GROUP, GROUP)), full((1, POOL_W))],
        out_specs=[row(POOL_W), row(POOL_W), row(POOL_W), full((1, POOL_W))],
        out_shape=[jax.ShapeDtypeStruct((s, POOL_W), F32), jax.ShapeDtypeStruct((s, POOL_W), BF16),
                   jax.ShapeDtypeStruct((s, POOL_W), BF16), jax.ShapeDtypeStruct((1, POOL_W), F32)],
        scratch_shapes=[pltpu.VMEM((te, POOL_W), F32), pltpu.VMEM((te, POOL_W), F32)],
        compiler_params=_params("arbitrary"),
    )(db, db, db, z, z, z, wpool, pscale)


def loss_head(y, target, tm=512):
    s = y.shape[0]
    tm = _row_tile(s, tm)

    def body(y_ref, t_ref, dy_ref, loss_ref):
        i = pl.program_id(0)
        err = y_ref[...] - t_ref[...]
        dy_ref[...] = err * (1.0 / D_MODEL)
        part = 0.5 * jnp.sum(jnp.sum(err * err, axis=-1, keepdims=True) * (1.0 / D_MODEL), axis=0, keepdims=True)
        part = jnp.broadcast_to(part, (1, 128))

        @pl.when(i == 0)
        def _():
            loss_ref[...] = part

        @pl.when(i > 0)
        def _():
            loss_ref[...] += part

    row = pl.BlockSpec((tm, D_MODEL), lambda i: (i, 0))
    return pl.pallas_call(
        body, name="loss_head",
        grid=(s // tm,),
        in_specs=[row, row],
        out_specs=[row, pl.BlockSpec((1, 128), lambda i: (0, 0))],
        out_shape=[jax.ShapeDtypeStruct((s, D_MODEL), F32), jax.ShapeDtypeStruct((1, 128), F32)],
        compiler_params=_params("arbitrary"),
    )(y, target)


def _pad_heads(w, real):
    lead = w.shape[:-1]
    w = w.reshape(lead + (HEADS, real))
    w = jnp.pad(w, [(0, 0)] * len(lead) + [(0, 0), (0, HEAD_PAD - real)])
    return w.reshape(lead + (ATT_W,))


def _unpad_heads(w, real):
    lead = w.shape[:-1]
    return w.reshape(lead + (HEADS, HEAD_PAD))[..., :real].reshape(lead + (HEADS * real,))


def layout_weights(full):
    w = {}
    depth = full["w_in"].shape[0]
    win = full["w_in"]
    zc = lambda n: jnp.zeros((depth, D_MODEL, n), win.dtype)
    o_pe = Q_LORA + KV_LORA
    w["win"] = jnp.concatenate([win[..., :o_pe], win[..., o_pe + QK_ROPE:], zc(QK_NOPE), win[..., o_pe:o_pe + QK_ROPE],
                                zc(HEAD_PAD - QK_HEAD)], axis=-1)
    w["wuq"] = _pad_heads(full["w_uq"], QK_HEAD)
    w["wuk"] = _pad_heads(full["w_uk"], QK_NOPE)
    w["wuv"] = _pad_heads(full["w_uv"], V_HEAD)
    wo = full["w_out"]
    wo_att = jnp.swapaxes(_pad_heads(jnp.swapaxes(wo[:, :HEADS * V_HEAD], 1, 2), V_HEAD), 1, 2)
    w["wout"] = jnp.concatenate([wo_att, wo[:, HEADS * V_HEAD:]], axis=1)
    pad_vec = lambda v: jnp.pad(v, ((0, 0), (0, HEAD_PAD - QK_HEAD)))
    w["qn"], w["kn"] = pad_vec(full["q_norm"]), pad_vec(full["k_norm"])
    for n in ("ffn1_norm", "mix_norm", "ffn2_norm", "q_lat_norm", "kv_lat_norm", "pool_scale", "w_pool"):
        w[n] = full[n]
    return w


def local_step(x, target, w, wgu1, wd1, wgu2, wd2):
    s = x.shape[0]
    depth = w["win"].shape[0]
    rc, rs1, rs2 = rope_tables(s)
    vec = lambda name, l: w[name][l][None, :]
    saved = []
    for l in range(depth):
        x1 = ffn_fwd(x, vec("ffn1_norm", l), wgu1, wd1, l)
        z, q, k, v, vt = mix_in_fwd(x1, vec("mix_norm", l), w["win"][l], vec("q_lat_norm", l), vec("kv_lat_norm", l),
                                    w["wuq"][l], w["wuk"][l], w["wuv"][l], vec("qn", l), vec("kn", l), rc, rs1, rs2)
        o, lse = flash_fwd(q, k, vt)
        x2, cat = pool_out_fwd(x1, z, o, w["w_pool"][l], vec("pool_scale", l), w["wout"][l])
        x3 = ffn_fwd(x2, vec("ffn2_norm", l), wgu2, wd2, l)
        saved.append((x, x1, z, q, k, v, lse, cat, x2))
        x = x3
    dy, loss = loss_head(x, target)

    names = ["ffn1_norm", "ffn1_w_gu", "ffn1_w_down", "mix_norm", "w_in", "q_lat_norm", "kv_lat_norm", "w_uq", "w_uk",
             "w_uv", "q_norm", "k_norm", "w_pool", "pool_scale", "w_out", "ffn2_norm", "ffn2_w_gu", "ffn2_w_down"]
    grads = {n: [None] * depth for n in names}
    for l in reversed(range(depth)):
        x0, x1, z, q, k, v, lse, cat, x2 = saved[l]
        dx2, h, act, dg, du, dyh, dgn = ffn_bwd(x2, dy, vec("ffn2_norm", l), wgu2, wd2, l)
        grads["ffn2_norm"][l] = dgn[0]
        grads["ffn2_w_gu"][l] = jnp.concatenate([matmul_tn(h, dg), matmul_tn(h, du)], axis=1)
        grads["ffn2_w_down"][l] = matmul_tn(act, dyh)

        do, db, delta = out_bwd(dx2, w["wout"][l], cat)
        dwout = matmul_tn(cat, dx2)
        grads["w_out"][l] = jnp.concatenate(
            [jnp.swapaxes(_unpad_heads(jnp.swapaxes(dwout[:ATT_W], 0, 1), V_HEAD), 0, 1), dwout[ATT_W:]], axis=0)
        dp, mixed, dys, dps = pool_bwd(db, z, w["w_pool"][l], vec("pool_scale", l))
        grads["pool_scale"][l] = dps[0]
        dwp = matmul_tn(mixed, dys)
        grads["w_pool"][l] = jnp.stack([dwp[g * GROUP:(g + 1) * GROUP, g * GROUP:(g + 1) * GROUP]
                                        for g in range(len(POOL_WINDOWS))])
        dq_t, dk, dv = flash_bwd(q, k, v, do, lse, delta.reshape(HEADS, 1, s))
        dz, cqn, ckvn, dqr, dkr, dvb, dqln, dkvln, dqn, dkn = mla_bwd(
            z, dq_t, dk, dv, dp, vec("q_lat_norm", l), vec("kv_lat_norm", l), w["wuq"][l], w["wuk"][l], w["wuv"][l],
            vec("qn", l), vec("kn", l), rc, rs1, rs2)
        grads["q_lat_norm"][l], grads["kv_lat_norm"][l] = dqln[0], dkvln[0]
        grads["q_norm"][l], grads["k_norm"][l] = dqn[0, :QK_HEAD], dkn[0, :QK_HEAD]
        grads["w_uq"][l] = _unpad_heads(matmul_tn(cqn, dqr), QK_HEAD)
        grads["w_uk"][l] = _unpad_heads(matmul_tn(ckvn, dkr), QK_NOPE)
        grads["w_uv"][l] = _unpad_heads(matmul_tn(ckvn, dvb), V_HEAD)
        dx1, h2, dgm = mix_in_bwd(x1, dx2, dz, vec("mix_norm", l), w["win"][l])
        grads["mix_norm"][l] = dgm[0]
        dwin = matmul_tn(h2, dz)
        grads["w_in"][l] = jnp.concatenate(
            [dwin[:, :Z_POOL], dwin[:, Z_KPE + QK_NOPE:Z_KPE + QK_HEAD], dwin[:, Z_POOL:Z_KPE]], axis=1)

        dy, h, act, dg, du, dyh, dgn = ffn_bwd(x0, dx1, vec("ffn1_norm", l), wgu1, wd1, l)
        grads["ffn1_norm"][l] = dgn[0]
        grads["ffn1_w_gu"][l] = jnp.concatenate([matmul_tn(h, dg), matmul_tn(h, du)], axis=1)
        grads["ffn1_w_down"][l] = matmul_tn(act, dyh)
    return loss, dy, {n: jnp.stack(g) for n, g in grads.items()}


HBM_SPEC = pl.BlockSpec(memory_space=pltpu.HBM)


def _place():
    x, y, c = lax.axis_index("x"), lax.axis_index("y"), lax.axis_index("c")
    other_chips = [(1 - x, y), (x, 1 - y), (1 - x, 1 - y)]
    return x, y, c, other_chips


def _exchange(name, arrays, out_shapes, plan, n_local, n_remote):
    n = len(arrays)

    def body(*refs):
        ins, outs = refs[:n], refs[n:n + len(out_shapes)]
        send_sems, recv_sems, local_sems = refs[n + len(out_shapes):]
        local, sends, recvs = plan(ins, outs, _place())
        started = []
        for k, (src, dst) in enumerate(local):
            cp = pltpu.make_async_copy(src, dst, local_sems.at[k])
            cp.start()
            started.append(cp)
        out_going = []
        for k, (src, dst, dev) in enumerate(sends):
            cp = pltpu.make_async_remote_copy(src_ref=src, dst_ref=dst, send_sem=send_sems.at[k], recv_sem=recv_sems.at[k],
                                              device_id=dev, device_id_type=MESH)
            cp.start()
            out_going.append(cp)
        for k, (dst, dev) in enumerate(recvs):
            pltpu.make_async_remote_copy(src_ref=dst, dst_ref=dst, send_sem=send_sems.at[k], recv_sem=recv_sems.at[k],
                                         device_id=dev, device_id_type=MESH).wait_recv()
        for cp in out_going:
            cp.wait_send()
        for cp in started:
            cp.wait()

    return pl.pallas_call(
        body, name=name,
        in_specs=[HBM_SPEC] * n,
        out_specs=[HBM_SPEC] * len(out_shapes),
        out_shape=out_shapes,
        scratch_shapes=[pltpu.SemaphoreType.DMA((n_remote,)), pltpu.SemaphoreType.DMA((n_remote,)),
                        pltpu.SemaphoreType.DMA((max(n_local, 1),))],
    )(*arrays)


def gather_chips(shards):
    n = len(shards)

    def plan(ins, outs, place):
        x, y, c, chips = place
        me = 2 * x + y
        local = [(ins[p], outs[p].at[me]) for p in range(n)]
        sends = [(ins[p], outs[p].at[me], (px, py, c)) for p in range(n) for (px, py) in chips]
        recvs = [(outs[p].at[2 * px + py], (px, py, c)) for p in range(n) for (px, py) in chips]
        return local, sends, recvs

    out_shapes = [jax.ShapeDtypeStruct((N_CHIPS,) + a.shape, a.dtype) for a in shards]
    return _exchange("gather_chips", shards, out_shapes, plan, n, 3 * n)


def scatter_chips(fulls):
    n = len(fulls)

    def plan(ins, outs, place):
        x, y, c, chips = place
        sends = [(ins[p].at[2 * px + py], outs[p].at[j], (px, py, c)) for p in range(n) for j, (px, py) in enumerate(chips)]
        recvs = [(outs[p].at[j], (px, py, c)) for p in range(n) for j, (px, py) in enumerate(chips)]
        return [], sends, recvs

    out_shapes = [jax.ShapeDtypeStruct((3,) + a.shape[1:], a.dtype) for a in fulls]
    return _exchange("scatter_chips", fulls, out_shapes, plan, 0, 3 * n)


def swap_sibling(arrays):
    n = len(arrays)

    def plan(ins, outs, place):
        x, y, c, _ = place
        sib = (x, y, 1 - c)
        return [], [(ins[p], outs[p], sib) for p in range(n)], [(outs[p], sib) for p in range(n)]

    return _exchange("swap_sibling", arrays, [jax.ShapeDtypeStruct(a.shape, a.dtype) for a in arrays], plan, 0, n)


def gather_all(vec):
    def plan(ins, outs, place):
        x, y, c, _ = place
        me = 4 * x + 2 * y + c
        flips = [(fx, fy, fc) for fx in (0, 1) for fy in (0, 1) for fc in (0, 1)][1:]
        peers = [(1 - x if fx else x, 1 - y if fy else y, 1 - c if fc else c) for fx, fy, fc in flips]
        local = [(ins[0], outs[0].at[me])]
        sends = [(ins[0], outs[0].at[me], dev) for dev in peers]
        recvs = [(outs[0].at[4 * dev[0] + 2 * dev[1] + dev[2]], dev) for dev in peers]
        return local, sends, recvs

    return _exchange("gather_all", [vec], [jax.ShapeDtypeStruct((8,) + vec.shape, vec.dtype)], plan, 1, 7)[0]


def _pick_rows(rows, cols, block_bytes=512 * 1024):
    best = None
    for t in range(8, rows + 1, 8):
        if rows % t == 0 and t * cols * 4 <= block_bytes:
            best = t
    return best or rows


def sum_rows_of(own, stacked):
    rows, cols = own.shape
    n = stacked.shape[0]
    tr = _pick_rows(rows, cols)

    def body(own_ref, st_ref, o_ref):
        acc = own_ref[...]
        for k in range(n):
            acc = acc + st_ref[k]
        o_ref[...] = acc

    return pl.pallas_call(
        body, name="sum_rows_of",
        grid=(rows // tr,),
        in_specs=[pl.BlockSpec((tr, cols), lambda i: (i, 0)), pl.BlockSpec((n, tr, cols), lambda i: (0, i, 0))],
        out_specs=pl.BlockSpec((tr, cols), lambda i: (i, 0)),
        out_shape=jax.ShapeDtypeStruct((rows, cols), F32),
        compiler_params=_params("parallel"),
    )(own, stacked)


def sum_stack(stacked):
    n, rows, cols = stacked.shape
    tr = _pick_rows(rows, cols)

    def body(st_ref, o_ref):
        acc = st_ref[0]
        for k in range(1, n):
            acc = acc + st_ref[k]
        o_ref[...] = acc

    return pl.pallas_call(
        body, name="sum_stack",
        grid=(rows // tr,),
        in_specs=[pl.BlockSpec((n, tr, cols), lambda i: (0, i, 0))],
        out_specs=pl.BlockSpec((tr, cols), lambda i: (i, 0)),
        out_shape=jax.ShapeDtypeStruct((rows, cols), F32),
        compiler_params=_params("parallel"),
    )(stacked)


def adamw(w, ga, gb, m, v):
    rows, cols = w.shape
    tr = _pick_rows(rows, cols, 256 * 1024)
    c1 = 1.0 - ADAM_B1 ** ADAM_STEP
    c2 = 1.0 - ADAM_B2 ** ADAM_STEP
    ins = [w, ga] + ([gb] if gb is not None else []) + [m, v]

    def body(*refs):
        w_ref, ga_ref = refs[0], refs[1]
        m_ref, v_ref, g_out, d_out, m_out, v_out = refs[-6:]
        g = ga_ref[...]
        if gb is not None:
            g = g + refs[2][...]
        mn = ADAM_B1 * m_ref[...] + (1.0 - ADAM_B1) * g
        vn = ADAM_B2 * v_ref[...] + (1.0 - ADAM_B2) * (g * g)
        m_hat = mn / c1
        v_hat = vn / c2
        g_out[...] = g
        d_out[...] = -ADAM_LR * (m_hat / (jnp.sqrt(v_hat) + ADAM_EPS) + ADAM_WD * w_ref[...])
        m_out[...] = mn
        v_out[...] = vn

    spec = pl.BlockSpec((tr, cols), lambda i: (i, 0))
    return pl.pallas_call(
        body, name="adamw",
        grid=(rows // tr,),
        in_specs=[spec] * len(ins),
        out_specs=[spec] * 4,
        out_shape=[jax.ShapeDtypeStruct((rows, cols), F32)] * 4,
        compiler_params=_params("parallel"),
    )(*ins)


WEIGHT_NAMES = ["ffn1_norm", "ffn1_w_gu", "ffn1_w_down", "mix_norm", "w_in", "q_lat_norm", "kv_lat_norm", "w_uq", "w_uk",
                "w_uv", "q_norm", "k_norm", "w_pool", "pool_scale", "w_out", "ffn2_norm", "ffn2_w_gu", "ffn2_w_down"]
COL_SHARDED = ["ffn1_w_gu", "w_in", "w_uq", "w_uk", "w_uv", "ffn2_w_gu"]
ROW_SHARDED = ["ffn1_w_down", "w_out", "ffn2_w_down"]
SHARDED = [n for n in WEIGHT_NAMES if n in COL_SHARDED or n in ROW_SHARDED]
REPLICATED = [n for n in WEIGHT_NAMES if n not in SHARDED]


def _join_chips(g, name):
    _, depth, r, c = g.shape
    if name in COL_SHARDED:
        return jnp.transpose(g, (1, 2, 0, 3)).reshape(depth, r, N_CHIPS * c)
    return jnp.transpose(g, (1, 0, 2, 3)).reshape(depth, N_CHIPS * r, c)


def _split_chips(full, name):
    depth, r, c = full.shape
    if name in COL_SHARDED:
        return jnp.transpose(full.reshape(depth, r, N_CHIPS, c // N_CHIPS), (2, 0, 1, 3))
    return jnp.transpose(full.reshape(depth, N_CHIPS, r // N_CHIPS, c), (1, 0, 2, 3))


def _pack(vs):
    flat = jnp.concatenate([v.reshape(-1) for v in vs])
    pad = (-flat.shape[0]) % (8 * 128)
    return jnp.pad(flat, (0, pad)).reshape(-1, 128)


def _unpack(packed, like):
    flat = packed.reshape(-1)
    out, at = [], 0
    for v in like:
        out.append(flat[at:at + v.size].reshape(v.shape))
        at += v.size
    return out


def kernel(x, ffn1_norm, ffn1_w_gu, ffn1_w_down, mix_norm, w_in, q_lat_norm, kv_lat_norm, w_uq, w_uk, w_uv, q_norm, k_norm, w_pool, pool_scale, w_out, ffn2_norm, ffn2_w_gu, ffn2_w_down, loss_target, m_ffn1_norm, m_ffn1_w_gu, m_ffn1_w_down, m_mix_norm, m_w_in, m_q_lat_norm, m_kv_lat_norm, m_w_uq, m_w_uk, m_w_uv, m_q_norm, m_k_norm, m_w_pool, m_pool_scale, m_w_out, m_ffn2_norm, m_ffn2_w_gu, m_ffn2_w_down, v_ffn1_norm, v_ffn1_w_gu, v_ffn1_w_down, v_mix_norm, v_w_in, v_q_lat_norm, v_kv_lat_norm, v_w_uq, v_w_uk, v_w_uv, v_q_norm, v_k_norm, v_w_pool, v_pool_scale, v_w_out, v_ffn2_norm, v_ffn2_w_gu, v_ffn2_w_down):
    args = locals()
    weights = {n: args[n] for n in WEIGHT_NAMES}
    moments_m = {n: args["m_" + n] for n in WEIGHT_NAMES}
    moments_v = {n: args["v_" + n] for n in WEIGHT_NAMES}

    gathered = dict(zip(SHARDED, gather_chips([weights[n].astype(BF16) for n in SHARDED])))
    full = {n: _join_chips(gathered[n], n) for n in SHARDED if "ffn" not in n}
    for n in REPLICATED:
        full[n] = weights[n].astype(BF16) if n == "w_pool" else weights[n]
    w = layout_weights(full)
    wd1 = _join_chips(gathered["ffn1_w_down"], "ffn1_w_down")
    wd2 = _join_chips(gathered["ffn2_w_down"], "ffn2_w_down")

    loss_part, grad_x, grads = local_step(x[0], loss_target[0], w, gathered["ffn1_w_gu"], wd1, gathered["ffn2_w_gu"], wd2)
    loss = lax.psum(loss_part[0, 0], ("x", "y", "c"))

    chip = 2 * lax.axis_index("x") + lax.axis_index("y")
    split = [_split_chips(grads[n], n) for n in SHARDED]
    landed = scatter_chips(split)
    partial = []
    for g4, land in zip(split, landed):
        own = lax.dynamic_index_in_dim(g4, chip, 0, keepdims=False)
        rows = own.shape[0] * own.shape[1]
        partial.append(sum_rows_of(own.reshape(rows, -1), land.reshape(3, rows, -1)))
    from_sibling = swap_sibling(partial)
    out = {}
    for n, mine, theirs in zip(SHARDED, partial, from_sibling):
        shape = weights[n].shape
        flat = lambda a: a.reshape(mine.shape)
        res = adamw(flat(weights[n]), mine, theirs, flat(moments_m[n]), flat(moments_v[n]))
        out[n] = [r.reshape(shape) for r in res]

    rep_g = [grads[n] for n in REPLICATED]
    total = sum_stack(gather_all(_pack(rep_g)))
    res = adamw(_pack([weights[n] for n in REPLICATED]), total, None,
                _pack([moments_m[n] for n in REPLICATED]), _pack([moments_v[n] for n in REPLICATED]))
    unpacked = [_unpack(r, rep_g) for r in res]
    for k, n in enumerate(REPLICATED):
        out[n] = [u[k] for u in unpacked]

    return (loss, grad_x[None], *[out[n][0] for n in WEIGHT_NAMES], *[out[n][1] for n in WEIGHT_NAMES],
            *[out[n][2] for n in WEIGHT_NAMES], *[out[n][3] for n in WEIGHT_NAMES])
```

```python
import functools

import jax
import jax.numpy as jnp
from jax import lax
from jax.experimental import pallas as pl
from jax.experimental.pallas import tpu as pltpu

F32 = jnp.float32
BF16 = jnp.bfloat16
MESH = pl.DeviceIdType.MESH

DEPTH = 4
D_MODEL = 1024
D_FF = 2816
FF_BLOCK = D_FF // 2
N_CHIPS = 4
HEADS = 8
HEAD_PAD = 128
QK_NOPE = 64
QK_ROPE = 32
QK_HEAD = QK_NOPE + QK_ROPE
V_HEAD = 64
Q_LORA = 384
KV_LORA = 256
POOL_W = 512
POOL_WINDOWS = (2, 4, 8, 16)
POOL_HALO = 8
GROUP = 128
D_IN = Q_LORA + KV_LORA + QK_ROPE + POOL_W
Z_W = 1280
Z_CKV = Q_LORA
Z_POOL = Q_LORA + KV_LORA
Z_KPE = Z_POOL + POOL_W
ATT_W = HEADS * HEAD_PAD
CAT_W = ATT_W + POOL_W
EPS = 1e-6
ROPE_THETA = 10000.0
ADAM_LR, ADAM_B1, ADAM_B2, ADAM_EPS, ADAM_WD, ADAM_STEP = 0.001, 0.9, 0.999, 1e-08, 0.01, 10

VMEM_LIMIT = 56 * 1024 * 1024


def _params(*sem):
    return pltpu.CompilerParams(dimension_semantics=sem, vmem_limit_bytes=VMEM_LIMIT)


def _dot(a, b):
    return jnp.dot(a, b, preferred_element_type=F32)


def _dot_nt(a, b):
    return lax.dot_general(a, b, (((1,), (1,)), ((), ())), preferred_element_type=F32)


def _dot_tn(a, b):
    return lax.dot_general(a, b, (((0,), (0,)), ((), ())), preferred_element_type=F32)


def _norm_fwd(x, n):
    rstd = lax.rsqrt(jnp.sum(x * x, axis=-1, keepdims=True) * (1.0 / n) + EPS)
    return x * rstd, rstd


def _norm_bwd(dy, g, xhat, rstd, n):
    dxhat = dy * g
    dx = rstd * (dxhat - xhat * (jnp.sum(dxhat * xhat, axis=-1, keepdims=True) * (1.0 / n)))
    return dx, dy * xhat


def _row_tile(s, want):
    t = min(s, want)
    assert s % t == 0
    return t


def ffn_fwd(x, gn, wgu4, wd, layer, tm=512):
    s = x.shape[0]
    tm = _row_tile(s, tm)

    def body(x_ref, gn_ref, wg_ref, wu_ref, wd_ref, o_ref, h_s, acc_s):
        j = pl.program_id(1)

        @pl.when(j == 0)
        def _():
            xhat, _ = _norm_fwd(x_ref[...], D_MODEL)
            h_s[...] = (xhat * gn_ref[...]).astype(BF16)
            acc_s[...] = jnp.zeros_like(acc_s)

        h = h_s[...]
        g = _dot(h, wg_ref[...])
        u = _dot(h, wu_ref[...])
        a = (g * jax.nn.sigmoid(g) * u).astype(BF16)
        acc_s[...] += _dot(a, wd_ref[...])

        @pl.when(j == 1)
        def _():
            o_ref[...] = x_ref[...] + 0.5 * acc_s[...]

    return pl.pallas_call(
        body, name="ffn_fwd",
        grid=(s // tm, 2),
        in_specs=[
            pl.BlockSpec((tm, D_MODEL), lambda i, j: (i, 0)),
            pl.BlockSpec((1, D_MODEL), lambda i, j: (0, 0)),
            pl.BlockSpec((None, None, D_MODEL, FF_BLOCK), lambda i, j: (j, layer, 0, 0)),
            pl.BlockSpec((None, None, D_MODEL, FF_BLOCK), lambda i, j: (j + 2, layer, 0, 0)),
            pl.BlockSpec((None, FF_BLOCK, D_MODEL), lambda i, j: (layer, j, 0)),
        ],
        out_specs=pl.BlockSpec((tm, D_MODEL), lambda i, j: (i, 0)),
        out_shape=jax.ShapeDtypeStruct((s, D_MODEL), F32),
        scratch_shapes=[pltpu.VMEM((tm, D_MODEL), BF16), pltpu.VMEM((tm, D_MODEL), F32)],
        compiler_params=_params("parallel", "arbitrary"),
    )(x, gn, wgu4, wgu4, wd)


def ffn_bwd(x, dy, gn, wgu4, wd, layer, tm=256):
    s = x.shape[0]
    tm = _row_tile(s, tm)

    def body(x_ref, dy_ref, gn_ref, wg_ref, wu_ref, wd_ref,
             dx_ref, h_ref, a_ref, dg_ref, du_ref, dyh_ref, dgn_ref, dh_s):
        i, j = pl.program_id(0), pl.program_id(1)

        @pl.when(j == 0)
        def _():
            xhat, _ = _norm_fwd(x_ref[...], D_MODEL)
            h_ref[...] = (xhat * gn_ref[...]).astype(BF16)
            dyh_ref[...] = (0.5 * dy_ref[...]).astype(BF16)
            dh_s[...] = jnp.zeros_like(dh_s)

        h = h_ref[...]
        g = _dot(h, wg_ref[...])
        u = _dot(h, wu_ref[...])
        sg = jax.nn.sigmoid(g)
        silu = g * sg
        a_ref[...] = (silu * u).astype(BF16)
        da = _dot_nt(dyh_ref[...], wd_ref[...])
        dg = (da * u * (sg * (1.0 + g * (1.0 - sg)))).astype(BF16)
        du = (da * silu).astype(BF16)
        dg_ref[...] = dg
        du_ref[...] = du
        dh_s[...] += _dot_nt(dg, wg_ref[...]) + _dot_nt(du, wu_ref[...])

        @pl.when(j == 1)
        def _():
            xhat, rstd = _norm_fwd(x_ref[...], D_MODEL)
            dxn, dgrow = _norm_bwd(dh_s[...], gn_ref[...], xhat, rstd, D_MODEL)
            dx_ref[...] = dy_ref[...] + dxn
            part = jnp.sum(dgrow, axis=0, keepdims=True)

            @pl.when(i == 0)
            def _():
                dgn_ref[...] = part

            @pl.when(i > 0)
            def _():
                dgn_ref[...] += part

    row = lambda i, j: (i, 0)
    return pl.pallas_call(
        body, name="ffn_bwd",
        grid=(s // tm, 2),
        in_specs=[
            pl.BlockSpec((tm, D_MODEL), row),
            pl.BlockSpec((tm, D_MODEL), row),
            pl.BlockSpec((1, D_MODEL), lambda i, j: (0, 0)),
            pl.BlockSpec((None, None, D_MODEL, FF_BLOCK), lambda i, j: (j, layer, 0, 0)),
            pl.BlockSpec((None, None, D_MODEL, FF_BLOCK), lambda i, j: (j + 2, layer, 0, 0)),
            pl.BlockSpec((None, FF_BLOCK, D_MODEL), lambda i, j: (layer, j, 0)),
        ],
        out_specs=[
            pl.BlockSpec((tm, D_MODEL), row),
            pl.BlockSpec((tm, D_MODEL), row),
            pl.BlockSpec((tm, FF_BLOCK), lambda i, j: (i, j)),
            pl.BlockSpec((tm, FF_BLOCK), lambda i, j: (i, j)),
            pl.BlockSpec((tm, FF_BLOCK), lambda i, j: (i, j)),
            pl.BlockSpec((tm, D_MODEL), row),
            pl.BlockSpec((1, D_MODEL), lambda i, j: (0, 0)),
        ],
        out_shape=[
            jax.ShapeDtypeStruct((s, D_MODEL), F32),
            jax.ShapeDtypeStruct((s, D_MODEL), BF16),
            jax.ShapeDtypeStruct((s, D_FF), BF16),
            jax.ShapeDtypeStruct((s, D_FF), BF16),
            jax.ShapeDtypeStruct((s, D_FF), BF16),
            jax.ShapeDtypeStruct((s, D_MODEL), BF16),
            jax.ShapeDtypeStruct((1, D_MODEL), F32),
        ],
        scratch_shapes=[pltpu.VMEM((tm, D_MODEL), F32)],
        compiler_params=_params("arbitrary", "arbitrary"),
    )(x, dy, gn, wgu4, wgu4, wd)


def matmul_tn(a, b, tk=1024, tn_max=1536):
    s, m = a.shape
    n = b.shape[1]
    tk = _row_tile(s, tk)
    tn = n
    if n > tn_max:
        tn = n // 2
    assert n % tn == 0 and tn % 128 == 0
    nk = s // tk

    def body(a_ref, b_ref, o_ref, acc_s):
        k = pl.program_id(1)

        @pl.when(k == 0)
        def _():
            acc_s[...] = jnp.zeros_like(acc_s)

        acc_s[...] += _dot_tn(a_ref[...].astype(BF16), b_ref[...].astype(BF16))

        @pl.when(k == nk - 1)
        def _():
            o_ref[...] = acc_s[...]

    return pl.pallas_call(
        body, name="matmul_tn",
        grid=(n // tn, nk),
        in_specs=[pl.BlockSpec((tk, m), lambda j, k: (k, 0)), pl.BlockSpec((tk, tn), lambda j, k: (k, j))],
        out_specs=pl.BlockSpec((m, tn), lambda j, k: (0, j)),
        out_shape=jax.ShapeDtypeStruct((m, n), F32),
        scratch_shapes=[pltpu.VMEM((m, tn), F32)],
        compiler_params=_params("parallel", "arbitrary"),
    )(a, b)


def _rope_fwd(y, c, s1, s2):
    return y * c + pltpu.roll(y, HEAD_PAD - 16, 1) * s1 + pltpu.roll(y, 16, 1) * s2


def _rope_bwd(dy, c, s1, s2):
    return dy * c + pltpu.roll(dy * s1, 16, 1) + pltpu.roll(dy * s2, HEAD_PAD - 16, 1)


def rope_tables(s):
    pos = jnp.arange(s, dtype=F32)
    inv = ROPE_THETA ** (-jnp.arange(0, QK_ROPE, 2, dtype=F32) / QK_ROPE)
    ang = pos[:, None] * inv[None, :]
    cos, sin = jnp.cos(ang), jnp.sin(ang)
    z16 = jnp.zeros((s, 16), F32)
    c = jnp.concatenate([jnp.ones((s, QK_NOPE), F32), cos, cos, z16, z16], axis=1)
    s1 = jnp.concatenate([jnp.zeros((s, QK_NOPE), F32), -sin, z16, z16, z16], axis=1)
    s2 = jnp.concatenate([jnp.zeros((s, QK_NOPE), F32), z16, sin, z16, z16], axis=1)
    return c, s1, s2


def mix_in_fwd(x, gm, win, qln, kvln, wuq, wuk, wuv, qn, kn, rc, rs1, rs2, tm=512):
    s = x.shape[0]
    tm = _row_tile(s, tm)
    scale = QK_HEAD ** -0.5

    def body(x_ref, gm_ref, win_ref, qln_ref, kvln_ref, wuq_ref, wuk_ref, wuv_ref, qn_ref, kn_ref,
             rc_ref, rs1_ref, rs2_ref, z_ref, q_ref, k_ref, v_ref, vt_ref):
        xhat, _ = _norm_fwd(x_ref[...], D_MODEL)
        h = (xhat * gm_ref[...]).astype(BF16)
        z = _dot(h, win_ref[...])
        z_ref[...] = z
        cq, _ = _norm_fwd(z[:, :Q_LORA], Q_LORA)
        cqn = (cq * qln_ref[...]).astype(BF16)
        ckv, _ = _norm_fwd(z[:, Z_CKV:Z_POOL], KV_LORA)
        ckvn = (ckv * kvln_ref[...]).astype(BF16)
        kpe = z[:, Z_KPE:]
        q_raw = _dot(cqn, wuq_ref[...])
        k_raw = _dot(ckvn, wuk_ref[...])
        v = _dot(ckvn, wuv_ref[...])
        v_ref[...] = v.astype(BF16)
        vt_ref[...] = jnp.transpose(v).astype(BF16)
        c, s1, s2 = rc_ref[...], rs1_ref[...], rs2_ref[...]
        for hd in range(HEADS):
            lanes = slice(hd * HEAD_PAD, (hd + 1) * HEAD_PAD)
            qh, _ = _norm_fwd(q_raw[:, lanes], QK_HEAD)
            q_ref[:, lanes] = (_rope_fwd(qh * qn_ref[...], c, s1, s2) * scale).astype(BF16)
            kh, _ = _norm_fwd(k_raw[:, lanes] + kpe, QK_HEAD)
            k_ref[:, lanes] = _rope_fwd(kh * kn_ref[...], c, s1, s2).astype(BF16)

    full = lambda shape: pl.BlockSpec(shape, lambda i: (0,) * len(shape))
    row = lambda w: pl.BlockSpec((tm, w), lambda i: (i, 0))
    return pl.pallas_call(
        body, name="mix_in_fwd",
        grid=(s // tm,),
        in_specs=[row(D_MODEL), full((1, D_MODEL)), full((D_MODEL, Z_W)), full((1, Q_LORA)), full((1, KV_LORA)),
                  full((Q_LORA, ATT_W)), full((KV_LORA, ATT_W)), full((KV_LORA, ATT_W)),
                  full((1, HEAD_PAD)), full((1, HEAD_PAD)), row(HEAD_PAD), row(HEAD_PAD), row(HEAD_PAD)],
        out_specs=[row(Z_W), row(ATT_W), row(ATT_W), row(ATT_W), pl.BlockSpec((ATT_W, tm), lambda i: (0, i))],
        out_shape=[jax.ShapeDtypeStruct((s, Z_W), F32)] + [jax.ShapeDtypeStruct((s, ATT_W), BF16)] * 3 + [
            jax.ShapeDtypeStruct((ATT_W, s), BF16)],
        compiler_params=_params("parallel"),
    )(x, gm, win, qln, kvln, wuq, wuk, wuv, qn, kn, rc, rs1, rs2)


def mla_bwd(z, dq_t, dk, dv, dp, qln, kvln, wuq, wuk, wuv, qn, kn, rc, rs1, rs2, tm=512):
    s = z.shape[0]
    tm = _row_tile(s, tm)
    scale = QK_HEAD ** -0.5

    def body(z_ref, dqt_ref, dk_ref, dv_ref, dp_ref, qln_ref, kvln_ref, wuq_ref, wuk_ref, wuv_ref, qn_ref, kn_ref,
             rc_ref, rs1_ref, rs2_ref,
             dz_ref, cqn_ref, ckvn_ref, dqr_ref, dkr_ref, dvb_ref, dqln_ref, dkvln_ref, dqn_ref, dkn_ref):
        i = pl.program_id(0)
        z = z_ref[...]
        cq, cq_rstd = _norm_fwd(z[:, :Q_LORA], Q_LORA)
        cqn = (cq * qln_ref[...]).astype(BF16)
        ckv, ckv_rstd = _norm_fwd(z[:, Z_CKV:Z_POOL], KV_LORA)
        ckvn = (ckv * kvln_ref[...]).astype(BF16)
        kpe = z[:, Z_KPE:]
        cqn_ref[...] = cqn
        ckvn_ref[...] = ckvn
        q_raw = _dot(cqn, wuq_ref[...])
        k_raw = _dot(ckvn, wuk_ref[...])
        c, s1, s2 = rc_ref[...], rs1_ref[...], rs2_ref[...]
        lane = lax.broadcasted_iota(jnp.int32, (tm, HEAD_PAD), 1)
        rope_lanes = (lane >= QK_NOPE) & (lane < QK_HEAD)
        dkpe = jnp.zeros((tm, HEAD_PAD), F32)
        dqn = jnp.zeros((tm, HEAD_PAD), F32)
        dkn = jnp.zeros((tm, HEAD_PAD), F32)
        for hd in range(HEADS):
            lanes = slice(hd * HEAD_PAD, (hd + 1) * HEAD_PAD)
            qh, q_rstd = _norm_fwd(q_raw[:, lanes], QK_HEAD)
            dqh = jnp.transpose(dqt_ref[lanes, :]) * scale
            dqr, dg = _norm_bwd(_rope_bwd(dqh, c, s1, s2), qn_ref[...], qh, q_rstd, QK_HEAD)
            dqn += dg
            dqr_ref[:, lanes] = dqr.astype(BF16)
            kh, k_rstd = _norm_fwd(k_raw[:, lanes] + kpe, QK_HEAD)
            dkr, dg = _norm_bwd(_rope_bwd(dk_ref[:, lanes], c, s1, s2), kn_ref[...], kh, k_rstd, QK_HEAD)
            dkn += dg
            dkr_ref[:, lanes] = dkr.astype(BF16)
            dkpe += jnp.where(rope_lanes, dkr, 0.0)
        dvb = dv_ref[...].astype(BF16)
        dvb_ref[...] = dvb
        dcqn = _dot_nt(dqr_ref[...], wuq_ref[...])
        dckvn = _dot_nt(dkr_ref[...], wuk_ref[...]) + _dot_nt(dvb, wuv_ref[...])
        dcq, dqln = _norm_bwd(dcqn, qln_ref[...], cq, cq_rstd, Q_LORA)
        dckv, dkvln = _norm_bwd(dckvn, kvln_ref[...], ckv, ckv_rstd, KV_LORA)
        dz_ref[:, :Q_LORA] = dcq.astype(BF16)
        dz_ref[:, Z_CKV:Z_POOL] = dckv.astype(BF16)
        dz_ref[:, Z_POOL:Z_KPE] = dp_ref[...].astype(BF16)
        dz_ref[:, Z_KPE:] = dkpe.astype(BF16)
        parts = [(dqln_ref, dqln), (dkvln_ref, dkvln), (dqn_ref, dqn), (dkn_ref, dkn)]

        @pl.when(i == 0)
        def _():
            for ref, val in parts:
                ref[...] = jnp.sum(val, axis=0, keepdims=True)

        @pl.when(i > 0)
        def _():
            for ref, val in parts:
                ref[...] += jnp.sum(val, axis=0, keepdims=True)

    full = lambda shape: pl.BlockSpec(shape, lambda i: (0,) * len(shape))
    row = lambda w: pl.BlockSpec((tm, w), lambda i: (i, 0))
    return pl.pallas_call(
        body, name="mla_bwd",
        grid=(s // tm,),
        in_specs=[row(Z_W), pl.BlockSpec((ATT_W, tm), lambda i: (0, i)), row(ATT_W), row(ATT_W), row(POOL_W),
                  full((1, Q_LORA)), full((1, KV_LORA)), full((Q_LORA, ATT_W)), full((KV_LORA, ATT_W)),
                  full((KV_LORA, ATT_W)), full((1, HEAD_PAD)), full((1, HEAD_PAD)),
                  row(HEAD_PAD), row(HEAD_PAD), row(HEAD_PAD)],
        out_specs=[row(Z_W), row(Q_LORA), row(KV_LORA), row(ATT_W), row(ATT_W), row(ATT_W),
                   full((1, Q_LORA)), full((1, KV_LORA)), full((1, HEAD_PAD)), full((1, HEAD_PAD))],
        out_shape=[jax.ShapeDtypeStruct((s, Z_W), BF16), jax.ShapeDtypeStruct((s, Q_LORA), BF16),
                   jax.ShapeDtypeStruct((s, KV_LORA), BF16)] + [jax.ShapeDtypeStruct((s, ATT_W), BF16)] * 3 + [
                   jax.ShapeDtypeStruct((1, Q_LORA), F32), jax.ShapeDtypeStruct((1, KV_LORA), F32),
                   jax.ShapeDtypeStruct((1, HEAD_PAD), F32), jax.ShapeDtypeStruct((1, HEAD_PAD), F32)],
        compiler_params=_params("arbitrary"),
    )(z, dq_t, dk, dv, dp, qln, kvln, wuq, wuk, wuv, qn, kn, rc, rs1, rs2)


def mix_in_bwd(x, dx_in, dz, gm, win, tm=512):
    s = x.shape[0]
    tm = _row_tile(s, tm)

    def body(x_ref, dxin_ref, dz_ref, gm_ref, win_ref, dx_ref, h_ref, dgm_ref):
        i = pl.program_id(0)
        xhat, rstd = _norm_fwd(x_ref[...], D_MODEL)
        h_ref[...] = (xhat * gm_ref[...]).astype(BF16)
        dh = _dot_nt(dz_ref[...], win_ref[...])
        dxn, dgrow = _norm_bwd(dh, gm_ref[...], xhat, rstd, D_MODEL)
        dx_ref[...] = dxin_ref[...] + dxn
        part = jnp.sum(dgrow, axis=0, keepdims=True)

        @pl.when(i == 0)
        def _():
            dgm_ref[...] = part

        @pl.when(i > 0)
        def _():
            dgm_ref[...] += part

    full = lambda shape: pl.BlockSpec(shape, lambda i: (0,) * len(shape))
    row = lambda w: pl.BlockSpec((tm, w), lambda i: (i, 0))
    return pl.pallas_call(
        body, name="mix_in_bwd",
        grid=(s // tm,),
        in_specs=[row(D_MODEL), row(D_MODEL), row(Z_W), full((1, D_MODEL)), full((D_MODEL, Z_W))],
        out_specs=[row(D_MODEL), row(D_MODEL), full((1, D_MODEL))],
        out_shape=[jax.ShapeDtypeStruct((s, D_MODEL), F32), jax.ShapeDtypeStruct((s, D_MODEL), BF16),
                   jax.ShapeDtypeStruct((1, D_MODEL), F32)],
        compiler_params=_params("arbitrary"),
    )(x, dx_in, dz, gm, win)


ATT_CHUNK = 512


def flash_fwd(q, k, vt, tq=1024, tk=4096, chunk=ATT_CHUNK):
    s = q.shape[0]
    tq, tk = _row_tile(s, tq), _row_tile(s, tk)
    nk = s // tk
    tc = _row_tile(tk, chunk)
    sub = tk // tc

    def body(q_ref, k_ref, vt_ref, o_ref, lse_ref, m_s, l_s, acc_s):
        kk = pl.program_id(2)

        @pl.when(kk == 0)
        def _():
            m_s[...] = jnp.full_like(m_s, -jnp.inf)
            l_s[...] = jnp.zeros_like(l_s)
            acc_s[...] = jnp.zeros_like(acc_s)

        q_t = q_ref[...]

        def scores(c):
            return _dot_nt(k_ref[c * tc:(c + 1) * tc, :], q_t)

        m, l, acc = m_s[...], l_s[...], acc_s[...]
        s_next = scores(0)
        for c in range(sub):
            s_t = s_next
            if c + 1 < sub:
                s_next = scores(c + 1)
            m_new = jnp.maximum(m, jnp.max(s_t, axis=0, keepdims=True))
            alpha = jnp.exp(m - m_new)
            p_t = jnp.exp(s_t - m_new)
            l = alpha * l + jnp.sum(p_t, axis=0, keepdims=True)
            acc = alpha * acc + _dot(vt_ref[:, c * tc:(c + 1) * tc], p_t.astype(BF16))
            m = m_new
        m_s[...], l_s[...], acc_s[...] = m, l, acc

        @pl.when(kk == nk - 1)
        def _():
            o_ref[...] = jnp.transpose(acc_s[...] / l_s[...]).astype(BF16)
            lse_ref[...] = m_s[...] + jnp.log(l_s[...])

    return pl.pallas_call(
        body, name="flash_fwd",
        grid=(HEADS, s // tq, nk),
        in_specs=[pl.BlockSpec((tq, HEAD_PAD), lambda h, i, kk: (i, h)),
                  pl.BlockSpec((tk, HEAD_PAD), lambda h, i, kk: (kk, h)),
                  pl.BlockSpec((HEAD_PAD, tk), lambda h, i, kk: (h, kk))],
        out_specs=[pl.BlockSpec((tq, HEAD_PAD), lambda h, i, kk: (i, h)),
                   pl.BlockSpec((None, 1, tq), lambda h, i, kk: (h, 0, i))],
        out_shape=[jax.ShapeDtypeStruct((s, ATT_W), BF16), jax.ShapeDtypeStruct((HEADS, 1, s), F32)],
        scratch_shapes=[pltpu.VMEM((1, tq), F32), pltpu.VMEM((1, tq), F32), pltpu.VMEM((HEAD_PAD, tq), F32)],
        compiler_params=_params("parallel", "parallel", "arbitrary"),
    )(q, k, vt)


def flash_bwd(q, k, v, do, lse_row, delta_row, tq=2048, tk=1024, chunk=ATT_CHUNK):
    s = q.shape[0]
    tq, tk = _row_tile(s, tq), _row_tile(s, tk)
    nq = s // tq
    tc = _row_tile(tq, chunk)
    sub = tq // tc

    def body(q_ref, k_ref, v_ref, do_ref, lse_ref, dl_ref, dqt_ref, dk_ref, dv_ref, kt_s, dk_s, dv_s):
        kb, qb = pl.program_id(1), pl.program_id(2)

        @pl.when(qb == 0)
        def _():
            kt_s[...] = jnp.transpose(k_ref[...].astype(F32)).astype(BF16)
            dk_s[...] = jnp.zeros_like(dk_s)
            dv_s[...] = jnp.zeros_like(dv_s)

        k_t, v_t = k_ref[...], v_ref[...]

        def products(c):
            rows = slice(c * tc, (c + 1) * tc)
            return _dot_nt(k_t, q_ref[rows, :]), _dot_nt(v_t, do_ref[rows, :])

        dk, dv = dk_s[...], dv_s[...]
        nxt = products(0)
        contribs = []
        for c in range(sub):
            rows = slice(c * tc, (c + 1) * tc)
            s_t, dp_t = nxt
            if c + 1 < sub:
                nxt = products(c + 1)
            p_t = jnp.exp(s_t - lse_ref[:, rows])
            ds_t = (p_t * (dp_t - dl_ref[:, rows])).astype(BF16)
            dv = dv + _dot(p_t.astype(BF16), do_ref[rows, :])
            dk = dk + _dot(ds_t, q_ref[rows, :])
            contribs.append(_dot(kt_s[...], ds_t))
        dk_s[...], dv_s[...] = dk, dv
        contrib = jnp.concatenate(contribs, axis=1) if sub > 1 else contribs[0]
        cols = pl.ds(pl.multiple_of(qb * tq, tq), tq)

        @pl.when(kb == 0)
        def _():
            dqt_ref[:, cols] = contrib

        @pl.when(kb > 0)
        def _():
            dqt_ref[:, cols] += contrib

        @pl.when(qb == nq - 1)
        def _():
            dk_ref[...] = dk_s[...]
            dv_ref[...] = dv_s[...]

    qspec = pl.BlockSpec((tq, HEAD_PAD), lambda h, kb, qb: (qb, h))
    kspec = pl.BlockSpec((tk, HEAD_PAD), lambda h, kb, qb: (kb, h))
    rowspec = pl.BlockSpec((None, 1, tq), lambda h, kb, qb: (h, 0, qb))
    return pl.pallas_call(
        body, name="flash_bwd",
        grid=(HEADS, s // tk, nq),
        in_specs=[qspec, kspec, kspec, qspec, rowspec, rowspec],
        out_specs=[pl.BlockSpec((HEAD_PAD, s), lambda h, kb, qb: (h, 0)), kspec, kspec],
        out_shape=[jax.ShapeDtypeStruct((ATT_W, s), F32), jax.ShapeDtypeStruct((s, ATT_W), F32),
                   jax.ShapeDtypeStruct((s, ATT_W), F32)],
        scratch_shapes=[pltpu.VMEM((HEAD_PAD, tk), BF16), pltpu.VMEM((tk, HEAD_PAD), F32),
                        pltpu.VMEM((tk, HEAD_PAD), F32)],
        compiler_params=_params("parallel", "arbitrary", "arbitrary"),
    )(q, k, v, do, lse_row, delta_row)


def _halo_specs(tm, w, n_tiles):
    per = tm // POOL_HALO
    last = n_tiles * per - 1
    return [pl.BlockSpec((POOL_HALO, w), lambda i: (jnp.maximum(i * per - 1, 0), 0)),
            pl.BlockSpec((tm, w), lambda i: (i, 0)),
            pl.BlockSpec((POOL_HALO, w), lambda i: (jnp.minimum((i + 1) * per, last), 0))]


def _fill_ext(ext_ref, prev, cur, nxt, i, n_tiles, tm):
    ext_ref[pl.ds(0, POOL_HALO), :] = jnp.where(i > 0, prev, 0.0)
    ext_ref[pl.ds(POOL_HALO, tm), :] = cur
    ext_ref[pl.ds(POOL_HALO + tm, POOL_HALO), :] = jnp.where(i < n_tiles - 1, nxt, 0.0)


def _window_sum(ext_ref, g, lo, hi, tm):
    lanes = pl.ds(g * GROUP, GROUP)
    acc = ext_ref[pl.ds(POOL_HALO + lo, tm), lanes]
    for d in range(lo + 1, hi + 1):
        acc = acc + ext_ref[pl.ds(POOL_HALO + d, tm), lanes]
    return acc


def _pool_mixed(ext_ref, cur, i, s, tm):
    row = i * tm + lax.broadcasted_iota(jnp.int32, (tm, 1), 0)
    out = []
    for g, w in enumerate(POOL_WINDOWS):
        left = w // 2
        right = w - 1 - left
        cnt = (jnp.minimum(row + right + 1, s) - jnp.maximum(row - left, 0)).astype(F32)
        out.append(_window_sum(ext_ref, g, -left, right, tm) / cnt - cur[:, g * GROUP:(g + 1) * GROUP])
    return out


def pool_out_fwd(x, z, o, wpool, pscale, wout, tm=512):
    s = x.shape[0]
    tm = _row_tile(s, tm)
    n_tiles = s // tm

    def body(x_ref, zp_ref, z_ref, zn_ref, o_ref, wp_ref, ps_ref, wo_ref, y_ref, cat_ref, ext_s):
        i = pl.program_id(0)
        cur = z_ref[:, Z_POOL:Z_KPE]
        _fill_ext(ext_s, zp_ref[:, Z_POOL:Z_KPE], cur, zn_ref[:, Z_POOL:Z_KPE], i, n_tiles, tm)
        mixed = _pool_mixed(ext_s, cur, i, s, tm)
        cat_ref[:, :ATT_W] = o_ref[...]
        for g in range(len(POOL_WINDOWS)):
            lanes = slice(g * GROUP, (g + 1) * GROUP)
            yg = _dot(mixed[g].astype(BF16), wp_ref[g])
            cat_ref[:, ATT_W + g * GROUP:ATT_W + (g + 1) * GROUP] = (yg * ps_ref[:, lanes]).astype(BF16)
        y_ref[...] = x_ref[...] + _dot(cat_ref[...], wo_ref[...])

    full = lambda shape: pl.BlockSpec(shape, lambda i: (0,) * len(shape))
    row = lambda w: pl.BlockSpec((tm, w), lambda i: (i, 0))
    return pl.pallas_call(
        body, name="pool_out_fwd",
        grid=(n_tiles,),
        in_specs=[row(D_MODEL)] + _halo_specs(tm, Z_W, n_tiles) + [
            row(ATT_W), full((len(POOL_WINDOWS), GROUP, GROUP)), full((1, POOL_W)), full((CAT_W, D_MODEL))],
        out_specs=[row(D_MODEL), row(CAT_W)],
        out_shape=[jax.ShapeDtypeStruct((s, D_MODEL), F32), jax.ShapeDtypeStruct((s, CAT_W), BF16)],
        scratch_shapes=[pltpu.VMEM((tm + 2 * POOL_HALO, POOL_W), F32)],
        compiler_params=_params("parallel"),
    )(x, z, z, z, o, wpool, pscale, wout)


def out_bwd(dx, wout, cat, tm=512):
    s = dx.shape[0]
    tm = _row_tile(s, tm)

    def body(dx_ref, wo_ref, cat_ref, do_ref, db_ref, dl_ref):
        dcat = _dot_nt(dx_ref[...].astype(BF16), wo_ref[...])
        do_ref[...] = dcat[:, :ATT_W].astype(BF16)
        db_ref[...] = dcat[:, ATT_W:]
        for hd in range(HEADS):
            lanes = slice(hd * HEAD_PAD, (hd + 1) * HEAD_PAD)
            dl_ref[hd] = jnp.sum(dcat[:, lanes] * cat_ref[:, lanes].astype(F32), axis=-1, keepdims=True)

    row = lambda w: pl.BlockSpec((tm, w), lambda i: (i, 0))
    return pl.pallas_call(
        body, name="out_bwd",
        grid=(s // tm,),
        in_specs=[row(D_MODEL), pl.BlockSpec((CAT_W, D_MODEL), lambda i: (0, 0)), row(CAT_W)],
        out_specs=[row(ATT_W), row(POOL_W), pl.BlockSpec((HEADS, tm, 1), lambda i: (0, i, 0))],
        out_shape=[jax.ShapeDtypeStruct((s, ATT_W), BF16), jax.ShapeDtypeStruct((s, POOL_W), F32),
                   jax.ShapeDtypeStruct((HEADS, s, 1), F32)],
        compiler_params=_params("parallel"),
    )(dx, wout, cat)


def pool_bwd(db, z, wpool, pscale, tm=512):
    s = db.shape[0]
    tm = _row_tile(s, tm)
    n_tiles = s // tm
    te = tm + 2 * POOL_HALO

    def body(dbp_ref, db_ref, dbn_ref, zp_ref, z_ref, zn_ref, wp_ref, ps_ref,
             dp_ref, mixed_ref, dys_ref, dps_ref, ext_s, text_s):
        i = pl.program_id(0)
        cur = z_ref[:, Z_POOL:Z_KPE]
        _fill_ext(ext_s, zp_ref[:, Z_POOL:Z_KPE], cur, zn_ref[:, Z_POOL:Z_KPE], i, n_tiles, tm)
        mixed = _pool_mixed(ext_s, cur, i, s, tm)
        _fill_ext(ext_s, dbp_ref[...], db_ref[...], dbn_ref[...], i, n_tiles, tm)
        erow = i * tm - POOL_HALO + lax.broadcasted_iota(jnp.int32, (te, 1), 0)
        dps = []
        for g, w in enumerate(POOL_WINDOWS):
            lanes = slice(g * GROUP, (g + 1) * GROUP)
            left = w // 2
            right = w - 1 - left
            mb = mixed[g].astype(BF16)
            mixed_ref[:, lanes] = mb
            dps.append(jnp.sum(db_ref[:, lanes] * _dot(mb, wp_ref[g]), axis=0, keepdims=True))
            dys_e = (ext_s[:, lanes] * ps_ref[:, lanes]).astype(BF16)
            dys_ref[:, lanes] = (db_ref[:, lanes] * ps_ref[:, lanes]).astype(BF16)
            dmix_e = _dot_nt(dys_e, wp_ref[g])
            cnt = jnp.minimum(erow + right + 1, s) - jnp.maximum(erow - left, 0)
            text_s[:, lanes] = dmix_e / jnp.maximum(cnt, 1).astype(F32)
            dp_ref[:, lanes] = _window_sum(text_s, g, -right, left, tm) - dmix_e[POOL_HALO:POOL_HALO + tm]
        part = jnp.concatenate(dps, axis=1)

        @pl.when(i == 0)
        def _():
            dps_ref[...] = part

        @pl.when(i > 0)
        def _():
            dps_ref[...] += part

    full = lambda shape: pl.BlockSpec(shape, lambda i: (0,) * len(shape))
    row = lambda w: pl.BlockSpec((tm, w), lambda i: (i, 0))
    return pl.pallas_call(
        body, name="pool_bwd",
        grid=(n_tiles,),
        in_specs=_halo_specs(tm, POOL_W, n_tiles) + _halo_specs(tm, Z_W, n_tiles) + [
            full((len(POOL_WINDOWS), GROUP, GROUP)), full((1, POOL_W))],
        out_specs=[row(POOL_W), row(POOL_W), row(POOL_W), full((1, POOL_W))],
        out_shape=[jax.ShapeDtypeStruct((s, POOL_W), F32), jax.ShapeDtypeStruct((s, POOL_W), BF16),
                   jax.ShapeDtypeStruct((s, POOL_W), BF16), jax.ShapeDtypeStruct((1, POOL_W), F32)],
        scratch_shapes=[pltpu.VMEM((te, POOL_W), F32), pltpu.VMEM((te, POOL_W), F32)],
        compiler_params=_params("arbitrary"),
    )(db, db, db, z, z, z, wpool, pscale)


def loss_head(y, target, tm=512):
    s = y.shape[0]
    tm = _row_tile(s, tm)

    def body(y_ref, t_ref, dy_ref, loss_ref):
        i = pl.program_id(0)
        err = y_ref[...] - t_ref[...]
        dy_ref[...] = err * (1.0 / D_MODEL)
        part = 0.5 * jnp.sum(jnp.sum(err * err, axis=-1, keepdims=True) * (1.0 / D_MODEL), axis=0, keepdims=True)
        part = jnp.broadcast_to(part, (1, 128))

        @pl.when(i == 0)
        def _():
            loss_ref[...] = part

        @pl.when(i > 0)
        def _():
            loss_ref[...] += part

    row = pl.BlockSpec((tm, D_MODEL), lambda i: (i, 0))
    return pl.pallas_call(
        body, name="loss_head",
        grid=(s // tm,),
        in_specs=[row, row],
        out_specs=[row, pl.BlockSpec((1, 128), lambda i: (0, 0))],
        out_shape=[jax.ShapeDtypeStruct((s, D_MODEL), F32), jax.ShapeDtypeStruct((1, 128), F32)],
        compiler_params=_params("arbitrary"),
    )(y, target)


def _pad_heads(w, real):
    lead = w.shape[:-1]
    w = w.reshape(lead + (HEADS, real))
    w = jnp.pad(w, [(0, 0)] * len(lead) + [(0, 0), (0, HEAD_PAD - real)])
    return w.reshape(lead + (ATT_W,))


def _unpad_heads(w, real):
    lead = w.shape[:-1]
    return w.reshape(lead + (HEADS, HEAD_PAD))[..., :real].reshape(lead + (HEADS * real,))


def layout_weights(full):
    w = {}
    depth = full["w_in"].shape[0]
    win = full["w_in"]
    zc = lambda n: jnp.zeros((depth, D_MODEL, n), win.dtype)
    o_pe = Q_LORA + KV_LORA
    w["win"] = jnp.concatenate([win[..., :o_pe], win[..., o_pe + QK_ROPE:], zc(QK_NOPE), win[..., o_pe:o_pe + QK_ROPE],
                                zc(HEAD_PAD - QK_HEAD)], axis=-1)
    w["wuq"] = _pad_heads(full["w_uq"], QK_HEAD)
    w["wuk"] = _pad_heads(full["w_uk"], QK_NOPE)
    w["wuv"] = _pad_heads(full["w_uv"], V_HEAD)
    wo = full["w_out"]
    wo_att = jnp.swapaxes(_pad_heads(jnp.swapaxes(wo[:, :HEADS * V_HEAD], 1, 2), V_HEAD), 1, 2)
    w["wout"] = jnp.concatenate([wo_att, wo[:, HEADS * V_HEAD:]], axis=1)
    pad_vec = lambda v: jnp.pad(v, ((0, 0), (0, HEAD_PAD - QK_HEAD)))
    w["qn"], w["kn"] = pad_vec(full["q_norm"]), pad_vec(full["k_norm"])
    for n in ("ffn1_norm", "mix_norm", "ffn2_norm", "q_lat_norm", "kv_lat_norm", "pool_scale", "w_pool"):
        w[n] = full[n]
    return w


def local_step(x, target, w, wgu1, wd1, wgu2, wd2):
    s = x.shape[0]
    depth = w["win"].shape[0]
    rc, rs1, rs2 = rope_tables(s)
    vec = lambda name, l: w[name][l][None, :]
    saved = []
    for l in range(depth):
        x1 = ffn_fwd(x, vec("ffn1_norm", l), wgu1, wd1, l)
        z, q, k, v, vt = mix_in_fwd(x1, vec("mix_norm", l), w["win"][l], vec("q_lat_norm", l), vec("kv_lat_norm", l),
                                    w["wuq"][l], w["wuk"][l], w["wuv"][l], vec("qn", l), vec("kn", l), rc, rs1, rs2)
        o, lse = flash_fwd(q, k, vt)
        x2, cat = pool_out_fwd(x1, z, o, w["w_pool"][l], vec("pool_scale", l), w["wout"][l])
        x3 = ffn_fwd(x2, vec("ffn2_norm", l), wgu2, wd2, l)
        saved.append((x, x1, z, q, k, v, lse, cat, x2))
        x = x3
    dy, loss = loss_head(x, target)

    names = ["ffn1_norm", "ffn1_w_gu", "ffn1_w_down", "mix_norm", "w_in", "q_lat_norm", "kv_lat_norm", "w_uq", "w_uk",
             "w_uv", "q_norm", "k_norm", "w_pool", "pool_scale", "w_out", "ffn2_norm", "ffn2_w_gu", "ffn2_w_down"]
    grads = {n: [None] * depth for n in names}
    for l in reversed(range(depth)):
        x0, x1, z, q, k, v, lse, cat, x2 = saved[l]
        dx2, h, act, dg, du, dyh, dgn = ffn_bwd(x2, dy, vec("ffn2_norm", l), wgu2, wd2, l)
        grads["ffn2_norm"][l] = dgn[0]
        grads["ffn2_w_gu"][l] = jnp.concatenate([matmul_tn(h, dg, tk=2048), matmul_tn(h, du, tk=2048)], axis=1)
        grads["ffn2_w_down"][l] = matmul_tn(act, dyh)

        do, db, delta = out_bwd(dx2, w["wout"][l], cat)
        dwout = matmul_tn(cat, dx2)
        grads["w_out"][l] = jnp.concatenate(
            [jnp.swapaxes(_unpad_heads(jnp.swapaxes(dwout[:ATT_W], 0, 1), V_HEAD), 0, 1), dwout[ATT_W:]], axis=0)
        dp, mixed, dys, dps = pool_bwd(db, z, w["w_pool"][l], vec("pool_scale", l))
        grads["pool_scale"][l] = dps[0]
        dwp = matmul_tn(mixed, dys)
        grads["w_pool"][l] = jnp.stack([dwp[g * GROUP:(g + 1) * GROUP, g * GROUP:(g + 1) * GROUP]
                                        for g in range(len(POOL_WINDOWS))])
        dq_t, dk, dv = flash_bwd(q, k, v, do, lse, delta.reshape(HEADS, 1, s))
        dz, cqn, ckvn, dqr, dkr, dvb, dqln, dkvln, dqn, dkn = mla_bwd(
            z, dq_t, dk, dv, dp, vec("q_lat_norm", l), vec("kv_lat_norm", l), w["wuq"][l], w["wuk"][l], w["wuv"][l],
            vec("qn", l), vec("kn", l), rc, rs1, rs2)
        grads["q_lat_norm"][l], grads["kv_lat_norm"][l] = dqln[0], dkvln[0]
        grads["q_norm"][l], grads["k_norm"][l] = dqn[0, :QK_HEAD], dkn[0, :QK_HEAD]
        grads["w_uq"][l] = _unpad_heads(matmul_tn(cqn, dqr), QK_HEAD)
        grads["w_uk"][l] = _unpad_heads(matmul_tn(ckvn, dkr), QK_NOPE)
        grads["w_uv"][l] = _unpad_heads(matmul_tn(ckvn, dvb), V_HEAD)
        dx1, h2, dgm = mix_in_bwd(x1, dx2, dz, vec("mix_norm", l), w["win"][l])
        grads["mix_norm"][l] = dgm[0]
        dwin = matmul_tn(h2, dz)
        grads["w_in"][l] = jnp.concatenate(
            [dwin[:, :Z_POOL], dwin[:, Z_KPE + QK_NOPE:Z_KPE + QK_HEAD], dwin[:, Z_POOL:Z_KPE]], axis=1)

        dy, h, act, dg, du, dyh, dgn = ffn_bwd(x0, dx1, vec("ffn1_norm", l), wgu1, wd1, l)
        grads["ffn1_norm"][l] = dgn[0]
        grads["ffn1_w_gu"][l] = jnp.concatenate([matmul_tn(h, dg, tk=2048), matmul_tn(h, du, tk=2048)], axis=1)
        grads["ffn1_w_down"][l] = matmul_tn(act, dyh)
    return loss, dy, {n: jnp.stack(g) for n, g in grads.items()}


HBM_SPEC = pl.BlockSpec(memory_space=pltpu.HBM)


def _place():
    x, y, c = lax.axis_index("x"), lax.axis_index("y"), lax.axis_index("c")
    other_chips = [(1 - x, y), (x, 1 - y), (1 - x, 1 - y)]
    return x, y, c, other_chips


def _exchange(name, arrays, out_shapes, plan, n_local, n_remote):
    n = len(arrays)

    def body(*refs):
        ins, outs = refs[:n], refs[n:n + len(out_shapes)]
        send_sems, recv_sems, local_sems = refs[n + len(out_shapes):]
        local, sends, recvs, forwards = plan(ins, outs, _place())

        def remote(k, src, dst, dev):
            return pltpu.make_async_remote_copy(src_ref=src, dst_ref=dst, send_sem=send_sems.at[k],
                                                recv_sem=recv_sems.at[k], device_id=dev, device_id_type=MESH)

        started = []
        for k, (src, dst) in enumerate(local):
            cp = pltpu.make_async_copy(src, dst, local_sems.at[k])
            cp.start()
            started.append(cp)
        out_going = []
        for k, (src, dst, dev) in enumerate(sends):
            cp = remote(k, src, dst, dev)
            cp.start()
            out_going.append(cp)
        for k, (dst, dev) in enumerate(recvs):
            remote(k, dst, dst, dev).wait_recv()
            if forwards:
                src, fdst, fdev, _ = forwards[k]
                cp = remote(len(sends) + k, src, fdst, fdev)
                cp.start()
                out_going.append(cp)
        for k, (_, _, fdev, landing) in enumerate(forwards):
            remote(len(sends) + k, landing, landing, fdev).wait_recv()
        for cp in out_going:
            cp.wait_send()
        for cp in started:
            cp.wait()

    return pl.pallas_call(
        body, name=name,
        in_specs=[HBM_SPEC] * n,
        out_specs=[HBM_SPEC] * len(out_shapes),
        out_shape=out_shapes,
        scratch_shapes=[pltpu.SemaphoreType.DMA((n_remote,)), pltpu.SemaphoreType.DMA((n_remote,)),
                        pltpu.SemaphoreType.DMA((max(n_local, 1),))],
    )(*arrays)


def _layer_half(depth, which):
    assert depth % 2 == 0
    return pl.ds(which * (depth // 2), depth // 2)


def gather_chips(shards):
    n = len(shards)
    depth = shards[0].shape[0]

    def plan(ins, outs, place):
        x, y, c, chips = place
        me, sib = 2 * x + y, (x, y, 1 - c)
        mine, theirs = _layer_half(depth, c), _layer_half(depth, 1 - c)
        local = [(ins[p], outs[p].at[me]) for p in range(n)]
        sends = [(ins[p].at[mine], outs[p].at[me, mine], (px, py, c)) for p in range(n) for (px, py) in chips]
        recvs = [(outs[p].at[2 * px + py, mine], (px, py, c)) for p in range(n) for (px, py) in chips]
        forwards = [(outs[p].at[2 * px + py, mine], outs[p].at[2 * px + py, mine], sib, outs[p].at[2 * px + py, theirs])
                    for p in range(n) for (px, py) in chips]
        return local, sends, recvs, forwards

    out_shapes = [jax.ShapeDtypeStruct((N_CHIPS,) + a.shape, a.dtype) for a in shards]
    return _exchange("gather_chips", shards, out_shapes, plan, n, 6 * n)


def swap_halves(fulls):
    n = len(fulls)
    depth = fulls[0].shape[1]

    def plan(ins, outs, place):
        x, y, c, _ = place
        sib = (x, y, 1 - c)
        sends = [(ins[p].at[:, _layer_half(depth, 1 - c)], outs[p], sib) for p in range(n)]
        return [], sends, [(outs[p], sib) for p in range(n)], []

    out_shapes = [jax.ShapeDtypeStruct((N_CHIPS, depth // 2) + a.shape[2:], a.dtype) for a in fulls]
    return _exchange("swap_halves", fulls, out_shapes, plan, 0, n)


def scatter_chips(fulls):
    n = len(fulls)

    def plan(ins, outs, place):
        x, y, c, chips = place
        sends = [(ins[p].at[2 * px + py], outs[p].at[j], (px, py, c)) for p in range(n) for j, (px, py) in enumerate(chips)]
        recvs = [(outs[p].at[j], (px, py, c)) for p in range(n) for j, (px, py) in enumerate(chips)]
        return [], sends, recvs, []

    out_shapes = [jax.ShapeDtypeStruct((3,) + a.shape[1:], a.dtype) for a in fulls]
    return _exchange("scatter_chips", fulls, out_shapes, plan, 0, 3 * n)


def join_halves(halves):
    n = len(halves)
    depth = 2 * halves[0].shape[0]

    def plan(ins, outs, place):
        x, y, c, _ = place
        sib = (x, y, 1 - c)
        mine, theirs = _layer_half(depth, c), _layer_half(depth, 1 - c)
        local = [(ins[p], outs[p].at[mine]) for p in range(n)]
        sends = [(ins[p], outs[p].at[mine], sib) for p in range(n)]
        return local, sends, [(outs[p].at[theirs], sib) for p in range(n)], []

    out_shapes = [jax.ShapeDtypeStruct((depth,) + a.shape[1:], a.dtype) for a in halves]
    return _exchange("join_halves", halves, out_shapes, plan, n, n)


def gather_all(vec):
    def plan(ins, outs, place):
        x, y, c, _ = place
        me = 4 * x + 2 * y + c
        flips = [(fx, fy, fc) for fx in (0, 1) for fy in (0, 1) for fc in (0, 1)][1:]
        peers = [(1 - x if fx else x, 1 - y if fy else y, 1 - c if fc else c) for fx, fy, fc in flips]
        local = [(ins[0], outs[0].at[me])]
        sends = [(ins[0], outs[0].at[me], dev) for dev in peers]
        recvs = [(outs[0].at[4 * dev[0] + 2 * dev[1] + dev[2]], dev) for dev in peers]
        return local, sends, recvs, []

    return _exchange("gather_all", [vec], [jax.ShapeDtypeStruct((8,) + vec.shape, vec.dtype)], plan, 1, 7)[0]


def _pick_rows(rows, cols, block_bytes=512 * 1024):
    best = None
    for t in range(8, rows + 1, 8):
        if rows % t == 0 and t * cols * 4 <= block_bytes:
            best = t
    return best or rows


def sum_rows_of(own, stacked):
    rows, cols = own.shape
    n = stacked.shape[0]
    tr = _pick_rows(rows, cols)

    def body(own_ref, st_ref, o_ref):
        acc = own_ref[...]
        for k in range(n):
            acc = acc + st_ref[k]
        o_ref[...] = acc

    return pl.pallas_call(
        body, name="sum_rows_of",
        grid=(rows // tr,),
        in_specs=[pl.BlockSpec((tr, cols), lambda i: (i, 0)), pl.BlockSpec((n, tr, cols), lambda i: (0, i, 0))],
        out_specs=pl.BlockSpec((tr, cols), lambda i: (i, 0)),
        out_shape=jax.ShapeDtypeStruct((rows, cols), F32),
        compiler_params=_params("parallel"),
    )(own, stacked)


def sum_stack(stacked):
    n, rows, cols = stacked.shape
    tr = _pick_rows(rows, cols)

    def body(st_ref, o_ref):
        acc = st_ref[0]
        for k in range(1, n):
            acc = acc + st_ref[k]
        o_ref[...] = acc

    return pl.pallas_call(
        body, name="sum_stack",
        grid=(rows // tr,),
        in_specs=[pl.BlockSpec((n, tr, cols), lambda i: (0, i, 0))],
        out_specs=pl.BlockSpec((tr, cols), lambda i: (i, 0)),
        out_shape=jax.ShapeDtypeStruct((rows, cols), F32),
        compiler_params=_params("parallel"),
    )(stacked)


def adamw(w, ga, gb, m, v):
    rows, cols = w.shape
    tr = _pick_rows(rows, cols, 256 * 1024)
    c1 = 1.0 - ADAM_B1 ** ADAM_STEP
    c2 = 1.0 - ADAM_B2 ** ADAM_STEP
    ins = [w, ga] + ([gb] if gb is not None else []) + [m, v]

    def body(*refs):
        w_ref, ga_ref = refs[0], refs[1]
        m_ref, v_ref, g_out, d_out, m_out, v_out = refs[-6:]
        g = ga_ref[...]
        if gb is not None:
            g = g + refs[2][...]
        mn = ADAM_B1 * m_ref[...] + (1.0 - ADAM_B1) * g
        vn = ADAM_B2 * v_ref[...] + (1.0 - ADAM_B2) * (g * g)
        m_hat = mn / c1
        v_hat = vn / c2
        g_out[...] = g
        d_out[...] = -ADAM_LR * (m_hat / (jnp.sqrt(v_hat) + ADAM_EPS) + ADAM_WD * w_ref[...])
        m_out[...] = mn
        v_out[...] = vn

    spec = pl.BlockSpec((tr, cols), lambda i: (i, 0))
    return pl.pallas_call(
        body, name="adamw",
        grid=(rows // tr,),
        in_specs=[spec] * len(ins),
        out_specs=[spec] * 4,
        out_shape=[jax.ShapeDtypeStruct((rows, cols), F32)] * 4,
        compiler_params=_params("parallel"),
    )(*ins)


WEIGHT_NAMES = ["ffn1_norm", "ffn1_w_gu", "ffn1_w_down", "mix_norm", "w_in", "q_lat_norm", "kv_lat_norm", "w_uq", "w_uk",
                "w_uv", "q_norm", "k_norm", "w_pool", "pool_scale", "w_out", "ffn2_norm", "ffn2_w_gu", "ffn2_w_down"]
COL_SHARDED = ["ffn1_w_gu", "w_in", "w_uq", "w_uk", "w_uv", "ffn2_w_gu"]
ROW_SHARDED = ["ffn1_w_down", "w_out", "ffn2_w_down"]
SHARDED = [n for n in WEIGHT_NAMES if n in COL_SHARDED or n in ROW_SHARDED]
REPLICATED = [n for n in WEIGHT_NAMES if n not in SHARDED]


def _join_chips(g, name):
    _, depth, r, c = g.shape
    if name in COL_SHARDED:
        return jnp.transpose(g, (1, 2, 0, 3)).reshape(depth, r, N_CHIPS * c)
    return jnp.transpose(g, (1, 0, 2, 3)).reshape(depth, N_CHIPS * r, c)


def _split_chips(full, name):
    depth, r, c = full.shape
    if name in COL_SHARDED:
        return jnp.transpose(full.reshape(depth, r, N_CHIPS, c // N_CHIPS), (2, 0, 1, 3))
    return jnp.transpose(full.reshape(depth, N_CHIPS, r // N_CHIPS, c), (1, 0, 2, 3))


def _pack(vs):
    flat = jnp.concatenate([v.reshape(-1) for v in vs])
    pad = (-flat.shape[0]) % (8 * 128)
    return jnp.pad(flat, (0, pad)).reshape(-1, 128)


def _unpack(packed, like):
    flat = packed.reshape(-1)
    out, at = [], 0
    for v in like:
        out.append(flat[at:at + v.size].reshape(v.shape))
        at += v.size
    return out


def kernel(x, ffn1_norm, ffn1_w_gu, ffn1_w_down, mix_norm, w_in, q_lat_norm, kv_lat_norm, w_uq, w_uk, w_uv, q_norm, k_norm, w_pool, pool_scale, w_out, ffn2_norm, ffn2_w_gu, ffn2_w_down, loss_target, m_ffn1_norm, m_ffn1_w_gu, m_ffn1_w_down, m_mix_norm, m_w_in, m_q_lat_norm, m_kv_lat_norm, m_w_uq, m_w_uk, m_w_uv, m_q_norm, m_k_norm, m_w_pool, m_pool_scale, m_w_out, m_ffn2_norm, m_ffn2_w_gu, m_ffn2_w_down, v_ffn1_norm, v_ffn1_w_gu, v_ffn1_w_down, v_mix_norm, v_w_in, v_q_lat_norm, v_kv_lat_norm, v_w_uq, v_w_uk, v_w_uv, v_q_norm, v_k_norm, v_w_pool, v_pool_scale, v_w_out, v_ffn2_norm, v_ffn2_w_gu, v_ffn2_w_down):
    args = locals()
    weights = {n: args[n] for n in WEIGHT_NAMES}
    moments_m = {n: args["m_" + n] for n in WEIGHT_NAMES}
    moments_v = {n: args["v_" + n] for n in WEIGHT_NAMES}

    gathered = dict(zip(SHARDED, gather_chips([weights[n].astype(BF16) for n in SHARDED])))
    full = {n: _join_chips(gathered[n], n) for n in SHARDED if "ffn" not in n}
    for n in REPLICATED:
        full[n] = weights[n].astype(BF16) if n == "w_pool" else weights[n]
    w = layout_weights(full)
    wd1 = _join_chips(gathered["ffn1_w_down"], "ffn1_w_down")
    wd2 = _join_chips(gathered["ffn2_w_down"], "ffn2_w_down")

    loss_part, grad_x, grads = local_step(x[0], loss_target[0], w, gathered["ffn1_w_gu"], wd1, gathered["ffn2_w_gu"], wd2)
    loss = lax.psum(loss_part[0, 0], ("x", "y", "c"))

    chip = 2 * lax.axis_index("x") + lax.axis_index("y")
    core = lax.axis_index("c")
    split = [_split_chips(grads[n], n) for n in SHARDED]
    half = split[0].shape[1] // 2

    def add(own, stacked):
        rows = own.size // own.shape[-1]
        total = sum_rows_of(own.reshape(rows, -1), stacked.reshape(stacked.shape[0], rows, -1))
        return total.reshape(own.shape)

    from_sibling = swap_halves(split)
    chip_sums = [add(lax.dynamic_slice_in_dim(g4, core * half, half, axis=1), other[None])
                 for g4, other in zip(split, from_sibling)]
    landed = scatter_chips(chip_sums)
    totals = [add(lax.dynamic_index_in_dim(h4, chip, 0, keepdims=False), land) for h4, land in zip(chip_sums, landed)]
    out = {}
    for n, g in zip(SHARDED, join_halves(totals)):
        shape = weights[n].shape
        flat = lambda a: a.reshape(shape[0] * shape[1], shape[2])
        res = adamw(flat(weights[n]), flat(g), None, flat(moments_m[n]), flat(moments_v[n]))
        out[n] = [r.reshape(shape) for r in res]

    rep_g = [grads[n] for n in REPLICATED]
    total = sum_stack(gather_all(_pack(rep_g)))
    res = adamw(_pack([weights[n] for n in REPLICATED]), total, None,
                _pack([moments_m[n] for n in REPLICATED]), _pack([moments_v[n] for n in REPLICATED]))
    unpacked = [_unpack(r, rep_g) for r in res]
    for k, n in enumerate(REPLICATED):
        out[n] = [u[k] for u in unpacked]

    return (loss, grad_x[None], *[out[n][0] for n in WEIGHT_NAMES], *[out[n][1] for n in WEIGHT_NAMES],
            *[out[n][2] for n in WEIGHT_NAMES], *[out[n][3] for n in WEIGHT_NAMES])
```

```python
import functools

import jax
import jax.numpy as jnp
from jax import lax
from jax.experimental import pallas as pl
from jax.experimental.pallas import tpu as pltpu

F32 = jnp.float32
BF16 = jnp.bfloat16
MESH = pl.DeviceIdType.MESH

DEPTH = 4
D_MODEL = 1024
D_FF = 2816
FF_BLOCK = D_FF // 2
N_CHIPS = 4
HEADS = 8
HEAD_PAD = 128
QK_NOPE = 64
QK_ROPE = 32
QK_HEAD = QK_NOPE + QK_ROPE
V_HEAD = 64
Q_LORA = 384
KV_LORA = 256
POOL_W = 512
POOL_WINDOWS = (2, 4, 8, 16)
POOL_HALO = 8
GROUP = 128
D_IN = Q_LORA + KV_LORA + QK_ROPE + POOL_W
Z_W = 1280
Z_CKV = Q_LORA
Z_POOL = Q_LORA + KV_LORA
Z_KPE = Z_POOL + POOL_W
ATT_W = HEADS * HEAD_PAD
CAT_W = ATT_W + POOL_W
EPS = 1e-6
ROPE_THETA = 10000.0
ADAM_LR, ADAM_B1, ADAM_B2, ADAM_EPS, ADAM_WD, ADAM_STEP = 0.001, 0.9, 0.999, 1e-08, 0.01, 10

VMEM_LIMIT = 56 * 1024 * 1024


def _params(*sem):
    return pltpu.CompilerParams(dimension_semantics=sem, vmem_limit_bytes=VMEM_LIMIT)


def _dot(a, b):
    return jnp.dot(a, b, preferred_element_type=F32)


def _dot_nt(a, b):
    return lax.dot_general(a, b, (((1,), (1,)), ((), ())), preferred_element_type=F32)


def _dot_tn(a, b):
    return lax.dot_general(a, b, (((0,), (0,)), ((), ())), preferred_element_type=F32)


def _norm_fwd(x, n):
    rstd = lax.rsqrt(jnp.sum(x * x, axis=-1, keepdims=True) * (1.0 / n) + EPS)
    return x * rstd, rstd


def _norm_bwd(dy, g, xhat, rstd, n):
    dxhat = dy * g
    dx = rstd * (dxhat - xhat * (jnp.sum(dxhat * xhat, axis=-1, keepdims=True) * (1.0 / n)))
    return dx, dy * xhat


def _row_tile(s, want):
    t = min(s, want)
    assert s % t == 0
    return t


def ffn_fwd(x, gn, wgu4, wd, layer, tm=512):
    s = x.shape[0]
    tm = _row_tile(s, tm)

    def body(x_ref, gn_ref, wg_ref, wu_ref, wd_ref, o_ref, h_s, acc_s):
        j = pl.program_id(1)

        @pl.when(j == 0)
        def _():
            xhat, _ = _norm_fwd(x_ref[...], D_MODEL)
            h_s[...] = (xhat * gn_ref[...]).astype(BF16)
            acc_s[...] = jnp.zeros_like(acc_s)

        h = h_s[...]
        g = _dot(h, wg_ref[...])
        u = _dot(h, wu_ref[...])
        a = (g * jax.nn.sigmoid(g) * u).astype(BF16)
        acc_s[...] += _dot(a, wd_ref[...])

        @pl.when(j == 1)
        def _():
            o_ref[...] = x_ref[...] + 0.5 * acc_s[...]

    return pl.pallas_call(
        body, name="ffn_fwd",
        grid=(s // tm, 2),
        in_specs=[
            pl.BlockSpec((tm, D_MODEL), lambda i, j: (i, 0)),
            pl.BlockSpec((1, D_MODEL), lambda i, j: (0, 0)),
            pl.BlockSpec((None, None, D_MODEL, FF_BLOCK), lambda i, j: (j, layer, 0, 0)),
            pl.BlockSpec((None, None, D_MODEL, FF_BLOCK), lambda i, j: (j + 2, layer, 0, 0)),
            pl.BlockSpec((None, FF_BLOCK, D_MODEL), lambda i, j: (layer, j, 0)),
        ],
        out_specs=pl.BlockSpec((tm, D_MODEL), lambda i, j: (i, 0)),
        out_shape=jax.ShapeDtypeStruct((s, D_MODEL), F32),
        scratch_shapes=[pltpu.VMEM((tm, D_MODEL), BF16), pltpu.VMEM((tm, D_MODEL), F32)],
        compiler_params=_params("parallel", "arbitrary"),
    )(x, gn, wgu4, wgu4, wd)


def ffn_bwd(x, dy, gn, wgu4, wd, layer, tm=256):
    s = x.shape[0]
    tm = _row_tile(s, tm)

    def body(x_ref, dy_ref, gn_ref, wg_ref, wu_ref, wd_ref,
             dx_ref, h_ref, a_ref, dg_ref, du_ref, dyh_ref, dgn_ref, dh_s):
        i, j = pl.program_id(0), pl.program_id(1)

        @pl.when(j == 0)
        def _():
            xhat, _ = _norm_fwd(x_ref[...], D_MODEL)
            h_ref[...] = (xhat * gn_ref[...]).astype(BF16)
            dyh_ref[...] = (0.5 * dy_ref[...]).astype(BF16)
            dh_s[...] = jnp.zeros_like(dh_s)

        h = h_ref[...]
        g = _dot(h, wg_ref[...])
        u = _dot(h, wu_ref[...])
        sg = jax.nn.sigmoid(g)
        silu = g * sg
        a_ref[...] = (silu * u).astype(BF16)
        da = _dot_nt(dyh_ref[...], wd_ref[...])
        dg = (da * u * (sg * (1.0 + g * (1.0 - sg)))).astype(BF16)
        du = (da * silu).astype(BF16)
        dg_ref[...] = dg
        du_ref[...] = du
        dh_s[...] += _dot_nt(dg, wg_ref[...]) + _dot_nt(du, wu_ref[...])

        @pl.when(j == 1)
        def _():
            xhat, rstd = _norm_fwd(x_ref[...], D_MODEL)
            dxn, dgrow = _norm_bwd(dh_s[...], gn_ref[...], xhat, rstd, D_MODEL)
            dx_ref[...] = dy_ref[...] + dxn
            part = jnp.sum(dgrow, axis=0, keepdims=True)

            @pl.when(i == 0)
            def _():
                dgn_ref[...] = part

            @pl.when(i > 0)
            def _():
                dgn_ref[...] += part

    row = lambda i, j: (i, 0)
    return pl.pallas_call(
        body, name="ffn_bwd",
        grid=(s // tm, 2),
        in_specs=[
            pl.BlockSpec((tm, D_MODEL), row),
            pl.BlockSpec((tm, D_MODEL), row),
            pl.BlockSpec((1, D_MODEL), lambda i, j: (0, 0)),
            pl.BlockSpec((None, None, D_MODEL, FF_BLOCK), lambda i, j: (j, layer, 0, 0)),
            pl.BlockSpec((None, None, D_MODEL, FF_BLOCK), lambda i, j: (j + 2, layer, 0, 0)),
            pl.BlockSpec((None, FF_BLOCK, D_MODEL), lambda i, j: (layer, j, 0)),
        ],
        out_specs=[
            pl.BlockSpec((tm, D_MODEL), row),
            pl.BlockSpec((tm, D_MODEL), row),
            pl.BlockSpec((tm, FF_BLOCK), lambda i, j: (i, j)),
            pl.BlockSpec((tm, FF_BLOCK), lambda i, j: (i, j)),
            pl.BlockSpec((tm, FF_BLOCK), lambda i, j: (i, j)),
            pl.BlockSpec((tm, D_MODEL), row),
            pl.BlockSpec((1, D_MODEL), lambda i, j: (0, 0)),
        ],
        out_shape=[
            jax.ShapeDtypeStruct((s, D_MODEL), F32),
            jax.ShapeDtypeStruct((s, D_MODEL), BF16),
            jax.ShapeDtypeStruct((s, D_FF), BF16),
            jax.ShapeDtypeStruct((s, D_FF), BF16),
            jax.ShapeDtypeStruct((s, D_FF), BF16),
            jax.ShapeDtypeStruct((s, D_MODEL), BF16),
            jax.ShapeDtypeStruct((1, D_MODEL), F32),
        ],
        scratch_shapes=[pltpu.VMEM((tm, D_MODEL), F32)],
        compiler_params=_params("arbitrary", "arbitrary"),
    )(x, dy, gn, wgu4, wgu4, wd)


def matmul_tn(a, b, tk=1024, tn_max=1536, into=None, at=None):
    s, m = a.shape
    n = b.shape[1]
    tk = _row_tile(s, tk)
    tn = n
    if n > tn_max:
        tn = n // 2
    assert n % tn == 0 and tn % 128 == 0
    nk = s // tk

    def body(a_ref, b_ref, *rest):
        o_ref, acc_s = rest[-2:]
        k = pl.program_id(1)

        @pl.when(k == 0)
        def _():
            acc_s[...] = jnp.zeros_like(acc_s)

        acc_s[...] += _dot_tn(a_ref[...].astype(BF16), b_ref[...].astype(BF16))

        @pl.when(k == nk - 1)
        def _():
            o_ref[...] = acc_s[...]

    in_specs = [pl.BlockSpec((tk, m), lambda j, k: (k, 0)), pl.BlockSpec((tk, tn), lambda j, k: (k, j))]
    if into is None:
        operands, aliases = (a, b), {}
        out_specs = pl.BlockSpec((m, tn), lambda j, k: (0, j))
        out_shape = jax.ShapeDtypeStruct((m, n), F32)
    else:
        assert into.shape[-2:] == (m, tn) and into.dtype == F32
        lead = into.ndim - 2
        operands, aliases = (a, b, into), {2: 0}
        in_specs.append(pl.BlockSpec(memory_space=pl.ANY))
        out_specs = pl.BlockSpec((None,) * lead + (m, tn), lambda j, k: tuple(at(j)) + (0, 0))
        out_shape = jax.ShapeDtypeStruct(into.shape, F32)
    return pl.pallas_call(
        body, name="matmul_tn",
        grid=(n // tn, nk),
        in_specs=in_specs,
        out_specs=out_specs,
        out_shape=out_shape,
        scratch_shapes=[pltpu.VMEM((m, tn), F32)],
        input_output_aliases=aliases,
        compiler_params=_params("parallel", "arbitrary"),
    )(*operands)


def _rope_fwd(y, c, s1, s2):
    return y * c + pltpu.roll(y, HEAD_PAD - 16, 1) * s1 + pltpu.roll(y, 16, 1) * s2


def _rope_bwd(dy, c, s1, s2):
    return dy * c + pltpu.roll(dy * s1, 16, 1) + pltpu.roll(dy * s2, HEAD_PAD - 16, 1)


def rope_tables(s):
    pos = jnp.arange(s, dtype=F32)
    inv = ROPE_THETA ** (-jnp.arange(0, QK_ROPE, 2, dtype=F32) / QK_ROPE)
    ang = pos[:, None] * inv[None, :]
    cos, sin = jnp.cos(ang), jnp.sin(ang)
    z16 = jnp.zeros((s, 16), F32)
    c = jnp.concatenate([jnp.ones((s, QK_NOPE), F32), cos, cos, z16, z16], axis=1)
    s1 = jnp.concatenate([jnp.zeros((s, QK_NOPE), F32), -sin, z16, z16, z16], axis=1)
    s2 = jnp.concatenate([jnp.zeros((s, QK_NOPE), F32), z16, sin, z16, z16], axis=1)
    return c, s1, s2


def mix_in_fwd(x, gm, win, qln, kvln, wuq, wuk, wuv, qn, kn, rc, rs1, rs2, tm=512):
    s = x.shape[0]
    tm = _row_tile(s, tm)
    scale = QK_HEAD ** -0.5

    def body(x_ref, gm_ref, win_ref, qln_ref, kvln_ref, wuq_ref, wuk_ref, wuv_ref, qn_ref, kn_ref,
             rc_ref, rs1_ref, rs2_ref, z_ref, q_ref, k_ref, v_ref, vt_ref):
        xhat, _ = _norm_fwd(x_ref[...], D_MODEL)
        h = (xhat * gm_ref[...]).astype(BF16)
        z = _dot(h, win_ref[...])
        z_ref[...] = z
        cq, _ = _norm_fwd(z[:, :Q_LORA], Q_LORA)
        cqn = (cq * qln_ref[...]).astype(BF16)
        ckv, _ = _norm_fwd(z[:, Z_CKV:Z_POOL], KV_LORA)
        ckvn = (ckv * kvln_ref[...]).astype(BF16)
        kpe = z[:, Z_KPE:]
        q_raw = _dot(cqn, wuq_ref[...])
        k_raw = _dot(ckvn, wuk_ref[...])
        v = _dot(ckvn, wuv_ref[...])
        v_ref[...] = v.astype(BF16)
        vt_ref[...] = jnp.transpose(v).astype(BF16)
        c, s1, s2 = rc_ref[...], rs1_ref[...], rs2_ref[...]
        for hd in range(HEADS):
            lanes = slice(hd * HEAD_PAD, (hd + 1) * HEAD_PAD)
            qh, _ = _norm_fwd(q_raw[:, lanes], QK_HEAD)
            q_ref[:, lanes] = (_rope_fwd(qh * qn_ref[...], c, s1, s2) * scale).astype(BF16)
            kh, _ = _norm_fwd(k_raw[:, lanes] + kpe, QK_HEAD)
            k_ref[:, lanes] = _rope_fwd(kh * kn_ref[...], c, s1, s2).astype(BF16)

    full = lambda shape: pl.BlockSpec(shape, lambda i: (0,) * len(shape))
    row = lambda w: pl.BlockSpec((tm, w), lambda i: (i, 0))
    return pl.pallas_call(
        body, name="mix_in_fwd",
        grid=(s // tm,),
        in_specs=[row(D_MODEL), full((1, D_MODEL)), full((D_MODEL, Z_W)), full((1, Q_LORA)), full((1, KV_LORA)),
                  full((Q_LORA, ATT_W)), full((KV_LORA, ATT_W)), full((KV_LORA, ATT_W)),
                  full((1, HEAD_PAD)), full((1, HEAD_PAD)), row(HEAD_PAD), row(HEAD_PAD), row(HEAD_PAD)],
        out_specs=[row(Z_W), row(ATT_W), row(ATT_W), row(ATT_W), pl.BlockSpec((ATT_W, tm), lambda i: (0, i))],
        out_shape=[jax.ShapeDtypeStruct((s, Z_W), F32)] + [jax.ShapeDtypeStruct((s, ATT_W), BF16)] * 3 + [
            jax.ShapeDtypeStruct((ATT_W, s), BF16)],
        compiler_params=_params("parallel"),
    )(x, gm, win, qln, kvln, wuq, wuk, wuv, qn, kn, rc, rs1, rs2)


def mla_bwd(z, dq_t, dk, dv, dp, qln, kvln, wuq, wuk, wuv, qn, kn, rc, rs1, rs2, tm=512):
    s = z.shape[0]
    tm = _row_tile(s, tm)
    scale = QK_HEAD ** -0.5

    def body(z_ref, dqt_ref, dk_ref, dv_ref, dp_ref, qln_ref, kvln_ref, wuq_ref, wuk_ref, wuv_ref, qn_ref, kn_ref,
             rc_ref, rs1_ref, rs2_ref,
             dz_ref, cqn_ref, ckvn_ref, dqr_ref, dkr_ref, dvb_ref, dqln_ref, dkvln_ref, dqn_ref, dkn_ref):
        i = pl.program_id(0)
        z = z_ref[...]
        cq, cq_rstd = _norm_fwd(z[:, :Q_LORA], Q_LORA)
        cqn = (cq * qln_ref[...]).astype(BF16)
        ckv, ckv_rstd = _norm_fwd(z[:, Z_CKV:Z_POOL], KV_LORA)
        ckvn = (ckv * kvln_ref[...]).astype(BF16)
        kpe = z[:, Z_KPE:]
        cqn_ref[...] = cqn
        ckvn_ref[...] = ckvn
        q_raw = _dot(cqn, wuq_ref[...])
        k_raw = _dot(ckvn, wuk_ref[...])
        c, s1, s2 = rc_ref[...], rs1_ref[...], rs2_ref[...]
        lane = lax.broadcasted_iota(jnp.int32, (tm, HEAD_PAD), 1)
        rope_lanes = (lane >= QK_NOPE) & (lane < QK_HEAD)
        dkpe = jnp.zeros((tm, HEAD_PAD), F32)
        dqn = jnp.zeros((tm, HEAD_PAD), F32)
        dkn = jnp.zeros((tm, HEAD_PAD), F32)
        for hd in range(HEADS):
            lanes = slice(hd * HEAD_PAD, (hd + 1) * HEAD_PAD)
            qh, q_rstd = _norm_fwd(q_raw[:, lanes], QK_HEAD)
            dqh = jnp.transpose(dqt_ref[lanes, :]) * scale
            dqr, dg = _norm_bwd(_rope_bwd(dqh, c, s1, s2), qn_ref[...], qh, q_rstd, QK_HEAD)
            dqn += dg
            dqr_ref[:, lanes] = dqr.astype(BF16)
            kh, k_rstd = _norm_fwd(k_raw[:, lanes] + kpe, QK_HEAD)
            dkr, dg = _norm_bwd(_rope_bwd(dk_ref[:, lanes], c, s1, s2), kn_ref[...], kh, k_rstd, QK_HEAD)
            dkn += dg
            dkr_ref[:, lanes] = dkr.astype(BF16)
            dkpe += jnp.where(rope_lanes, dkr, 0.0)
        dvb = dv_ref[...].astype(BF16)
        dvb_ref[...] = dvb
        dcqn = _dot_nt(dqr_ref[...], wuq_ref[...])
        dckvn = _dot_nt(dkr_ref[...], wuk_ref[...]) + _dot_nt(dvb, wuv_ref[...])
        dcq, dqln = _norm_bwd(dcqn, qln_ref[...], cq, cq_rstd, Q_LORA)
        dckv, dkvln = _norm_bwd(dckvn, kvln_ref[...], ckv, ckv_rstd, KV_LORA)
        dz_ref[:, :Q_LORA] = dcq.astype(BF16)
        dz_ref[:, Z_CKV:Z_POOL] = dckv.astype(BF16)
        dz_ref[:, Z_POOL:Z_KPE] = dp_ref[...].astype(BF16)
        dz_ref[:, Z_KPE:] = dkpe.astype(BF16)
        parts = [(dqln_ref, dqln), (dkvln_ref, dkvln), (dqn_ref, dqn), (dkn_ref, dkn)]

        @pl.when(i == 0)
        def _():
            for ref, val in parts:
                ref[...] = jnp.sum(val, axis=0, keepdims=True)

        @pl.when(i > 0)
        def _():
            for ref, val in parts:
                ref[...] += jnp.sum(val, axis=0, keepdims=True)

    full = lambda shape: pl.BlockSpec(shape, lambda i: (0,) * len(shape))
    row = lambda w: pl.BlockSpec((tm, w), lambda i: (i, 0))
    return pl.pallas_call(
        body, name="mla_bwd",
        grid=(s // tm,),
        in_specs=[row(Z_W), pl.BlockSpec((ATT_W, tm), lambda i: (0, i)), row(ATT_W), row(ATT_W), row(POOL_W),
                  full((1, Q_LORA)), full((1, KV_LORA)), full((Q_LORA, ATT_W)), full((KV_LORA, ATT_W)),
                  full((KV_LORA, ATT_W)), full((1, HEAD_PAD)), full((1, HEAD_PAD)),
                  row(HEAD_PAD), row(HEAD_PAD), row(HEAD_PAD)],
        out_specs=[row(Z_W), row(Q_LORA), row(KV_LORA), row(ATT_W), row(ATT_W), row(ATT_W),
                   full((1, Q_LORA)), full((1, KV_LORA)), full((1, HEAD_PAD)), full((1, HEAD_PAD))],
        out_shape=[jax.ShapeDtypeStruct((s, Z_W), BF16), jax.ShapeDtypeStruct((s, Q_LORA), BF16),
                   jax.ShapeDtypeStruct((s, KV_LORA), BF16)] + [jax.ShapeDtypeStruct((s, ATT_W), BF16)] * 3 + [
                   jax.ShapeDtypeStruct((1, Q_LORA), F32), jax.ShapeDtypeStruct((1, KV_LORA), F32),
                   jax.ShapeDtypeStruct((1, HEAD_PAD), F32), jax.ShapeDtypeStruct((1, HEAD_PAD), F32)],
        compiler_params=_params("arbitrary"),
    )(z, dq_t, dk, dv, dp, qln, kvln, wuq, wuk, wuv, qn, kn, rc, rs1, rs2)


def mix_in_bwd(x, dx_in, dz, gm, win, tm=512):
    s = x.shape[0]
    tm = _row_tile(s, tm)

    def body(x_ref, dxin_ref, dz_ref, gm_ref, win_ref, dx_ref, h_ref, dgm_ref):
        i = pl.program_id(0)
        xhat, rstd = _norm_fwd(x_ref[...], D_MODEL)
        h_ref[...] = (xhat * gm_ref[...]).astype(BF16)
        dh = _dot_nt(dz_ref[...], win_ref[...])
        dxn, dgrow = _norm_bwd(dh, gm_ref[...], xhat, rstd, D_MODEL)
        dx_ref[...] = dxin_ref[...] + dxn
        part = jnp.sum(dgrow, axis=0, keepdims=True)

        @pl.when(i == 0)
        def _():
            dgm_ref[...] = part

        @pl.when(i > 0)
        def _():
            dgm_ref[...] += part

    full = lambda shape: pl.BlockSpec(shape, lambda i: (0,) * len(shape))
    row = lambda w: pl.BlockSpec((tm, w), lambda i: (i, 0))
    return pl.pallas_call(
        body, name="mix_in_bwd",
        grid=(s // tm,),
        in_specs=[row(D_MODEL), row(D_MODEL), row(Z_W), full((1, D_MODEL)), full((D_MODEL, Z_W))],
        out_specs=[row(D_MODEL), row(D_MODEL), full((1, D_MODEL))],
        out_shape=[jax.ShapeDtypeStruct((s, D_MODEL), F32), jax.ShapeDtypeStruct((s, D_MODEL), BF16),
                   jax.ShapeDtypeStruct((1, D_MODEL), F32)],
        compiler_params=_params("arbitrary"),
    )(x, dx_in, dz, gm, win)


ATT_CHUNK = 512


def flash_fwd(q, k, vt, tq=1024, tk=4096, chunk=ATT_CHUNK):
    s = q.shape[0]
    tq, tk = _row_tile(s, tq), _row_tile(s, tk)
    nk = s // tk
    tc = _row_tile(tk, chunk)
    sub = tk // tc

    def body(q_ref, k_ref, vt_ref, o_ref, lse_ref, m_s, l_s, acc_s):
        kk = pl.program_id(2)

        @pl.when(kk == 0)
        def _():
            m_s[...] = jnp.full_like(m_s, -jnp.inf)
            l_s[...] = jnp.zeros_like(l_s)
            acc_s[...] = jnp.zeros_like(acc_s)

        q_t = q_ref[...]

        def scores(c):
            return _dot_nt(k_ref[c * tc:(c + 1) * tc, :], q_t)

        m, l, acc = m_s[...], l_s[...], acc_s[...]
        s_next = scores(0)
        for c in range(sub):
            s_t = s_next
            if c + 1 < sub:
                s_next = scores(c + 1)
            m_new = jnp.maximum(m, jnp.max(s_t, axis=0, keepdims=True))
            alpha = jnp.exp(m - m_new)
            p_t = jnp.exp(s_t - m_new)
            l = alpha * l + jnp.sum(p_t, axis=0, keepdims=True)
            acc = alpha * acc + _dot(vt_ref[:, c * tc:(c + 1) * tc], p_t.astype(BF16))
            m = m_new
        m_s[...], l_s[...], acc_s[...] = m, l, acc

        @pl.when(kk == nk - 1)
        def _():
            o_ref[...] = jnp.transpose(acc_s[...] / l_s[...]).astype(BF16)
            lse_ref[...] = m_s[...] + jnp.log(l_s[...])

    return pl.pallas_call(
        body, name="flash_fwd",
        grid=(HEADS, s // tq, nk),
        in_specs=[pl.BlockSpec((tq, HEAD_PAD), lambda h, i, kk: (i, h)),
                  pl.BlockSpec((tk, HEAD_PAD), lambda h, i, kk: (kk, h)),
                  pl.BlockSpec((HEAD_PAD, tk), lambda h, i, kk: (h, kk))],
        out_specs=[pl.BlockSpec((tq, HEAD_PAD), lambda h, i, kk: (i, h)),
                   pl.BlockSpec((None, 1, tq), lambda h, i, kk: (h, 0, i))],
        out_shape=[jax.ShapeDtypeStruct((s, ATT_W), BF16), jax.ShapeDtypeStruct((HEADS, 1, s), F32)],
        scratch_shapes=[pltpu.VMEM((1, tq), F32), pltpu.VMEM((1, tq), F32), pltpu.VMEM((HEAD_PAD, tq), F32)],
        compiler_params=_params("parallel", "parallel", "arbitrary"),
    )(q, k, vt)


def flash_bwd(q, k, v, do, lse_row, delta_row, tq=2048, tk=1024, chunk=ATT_CHUNK):
    s = q.shape[0]
    tq, tk = _row_tile(s, tq), _row_tile(s, tk)
    nq = s // tq
    tc = _row_tile(tq, chunk)
    sub = tq // tc

    def body(q_ref, k_ref, v_ref, do_ref, lse_ref, dl_ref, dqt_ref, dk_ref, dv_ref, kt_s, dk_s, dv_s):
        kb, qb = pl.program_id(1), pl.program_id(2)

        @pl.when(qb == 0)
        def _():
            kt_s[...] = jnp.transpose(k_ref[...].astype(F32)).astype(BF16)
            dk_s[...] = jnp.zeros_like(dk_s)
            dv_s[...] = jnp.zeros_like(dv_s)

        k_t, v_t = k_ref[...], v_ref[...]

        def products(c):
            rows = slice(c * tc, (c + 1) * tc)
            return _dot_nt(k_t, q_ref[rows, :]), _dot_nt(v_t, do_ref[rows, :])

        dk, dv = dk_s[...], dv_s[...]
        nxt = products(0)
        contribs = []
        for c in range(sub):
            rows = slice(c * tc, (c + 1) * tc)
            s_t, dp_t = nxt
            if c + 1 < sub:
                nxt = products(c + 1)
            p_t = jnp.exp(s_t - lse_ref[:, rows])
            ds_t = (p_t * (dp_t - dl_ref[:, rows])).astype(BF16)
            dv = dv + _dot(p_t.astype(BF16), do_ref[rows, :])
            dk = dk + _dot(ds_t, q_ref[rows, :])
            contribs.append(_dot(kt_s[...], ds_t))
        dk_s[...], dv_s[...] = dk, dv
        contrib = jnp.concatenate(contribs, axis=1) if sub > 1 else contribs[0]
        cols = pl.ds(pl.multiple_of(qb * tq, tq), tq)

        @pl.when(kb == 0)
        def _():
            dqt_ref[:, cols] = contrib

        @pl.when(kb > 0)
        def _():
            dqt_ref[:, cols] += contrib

        @pl.when(qb == nq - 1)
        def _():
            dk_ref[...] = dk_s[...]
            dv_ref[...] = dv_s[...]

    qspec = pl.BlockSpec((tq, HEAD_PAD), lambda h, kb, qb: (qb, h))
    kspec = pl.BlockSpec((tk, HEAD_PAD), lambda h, kb, qb: (kb, h))
    rowspec = pl.BlockSpec((None, 1, tq), lambda h, kb, qb: (h, 0, qb))
    return pl.pallas_call(
        body, name="flash_bwd",
        grid=(HEADS, s // tk, nq),
        in_specs=[qspec, kspec, kspec, qspec, rowspec, rowspec],
        out_specs=[pl.BlockSpec((HEAD_PAD, s), lambda h, kb, qb: (h, 0)), kspec, kspec],
        out_shape=[jax.ShapeDtypeStruct((ATT_W, s), F32), jax.ShapeDtypeStruct((s, ATT_W), F32),
                   jax.ShapeDtypeStruct((s, ATT_W), F32)],
        scratch_shapes=[pltpu.VMEM((HEAD_PAD, tk), BF16), pltpu.VMEM((tk, HEAD_PAD), F32),
                        pltpu.VMEM((tk, HEAD_PAD), F32)],
        compiler_params=_params("parallel", "arbitrary", "arbitrary"),
    )(q, k, v, do, lse_row, delta_row)


def _halo_specs(tm, w, n_tiles):
    per = tm // POOL_HALO
    last = n_tiles * per - 1
    return [pl.BlockSpec((POOL_HALO, w), lambda i: (jnp.maximum(i * per - 1, 0), 0)),
            pl.BlockSpec((tm, w), lambda i: (i, 0)),
            pl.BlockSpec((POOL_HALO, w), lambda i: (jnp.minimum((i + 1) * per, last), 0))]


def _fill_ext(ext_ref, prev, cur, nxt, i, n_tiles, tm):
    ext_ref[pl.ds(0, POOL_HALO), :] = jnp.where(i > 0, prev, 0.0)
    ext_ref[pl.ds(POOL_HALO, tm), :] = cur
    ext_ref[pl.ds(POOL_HALO + tm, POOL_HALO), :] = jnp.where(i < n_tiles - 1, nxt, 0.0)


def _window_sum(ext_ref, g, lo, hi, tm):
    lanes = pl.ds(g * GROUP, GROUP)
    acc = ext_ref[pl.ds(POOL_HALO + lo, tm), lanes]
    for d in range(lo + 1, hi + 1):
        acc = acc + ext_ref[pl.ds(POOL_HALO + d, tm), lanes]
    return acc


def _pool_mixed(ext_ref, cur, i, s, tm):
    row = i * tm + lax.broadcasted_iota(jnp.int32, (tm, 1), 0)
    out = []
    for g, w in enumerate(POOL_WINDOWS):
        left = w // 2
        right = w - 1 - left
        cnt = (jnp.minimum(row + right + 1, s) - jnp.maximum(row - left, 0)).astype(F32)
        out.append(_window_sum(ext_ref, g, -left, right, tm) / cnt - cur[:, g * GROUP:(g + 1) * GROUP])
    return out


def pool_out_fwd(x, z, o, wpool, pscale, wout, tm=512):
    s = x.shape[0]
    tm = _row_tile(s, tm)
    n_tiles = s // tm

    def body(x_ref, zp_ref, z_ref, zn_ref, o_ref, wp_ref, ps_ref, wo_ref, y_ref, cat_ref, ext_s):
        i = pl.program_id(0)
        cur = z_ref[:, Z_POOL:Z_KPE]
        _fill_ext(ext_s, zp_ref[:, Z_POOL:Z_KPE], cur, zn_ref[:, Z_POOL:Z_KPE], i, n_tiles, tm)
        mixed = _pool_mixed(ext_s, cur, i, s, tm)
        cat_ref[:, :ATT_W] = o_ref[...]
        for g in range(len(POOL_WINDOWS)):
            lanes = slice(g * GROUP, (g + 1) * GROUP)
            yg = _dot(mixed[g].astype(BF16), wp_ref[g])
            cat_ref[:, ATT_W + g * GROUP:ATT_W + (g + 1) * GROUP] = (yg * ps_ref[:, lanes]).astype(BF16)
        y_ref[...] = x_ref[...] + _dot(cat_ref[...], wo_ref[...])

    full = lambda shape: pl.BlockSpec(shape, lambda i: (0,) * len(shape))
    row = lambda w: pl.BlockSpec((tm, w), lambda i: (i, 0))
    return pl.pallas_call(
        body, name="pool_out_fwd",
        grid=(n_tiles,),
        in_specs=[row(D_MODEL)] + _halo_specs(tm, Z_W, n_tiles) + [
            row(ATT_W), full((len(POOL_WINDOWS), GROUP, GROUP)), full((1, POOL_W)), full((CAT_W, D_MODEL))],
        out_specs=[row(D_MODEL), row(CAT_W)],
        out_shape=[jax.ShapeDtypeStruct((s, D_MODEL), F32), jax.ShapeDtypeStruct((s, CAT_W), BF16)],
        scratch_shapes=[pltpu.VMEM((tm + 2 * POOL_HALO, POOL_W), F32)],
        compiler_params=_params("parallel"),
    )(x, z, z, z, o, wpool, pscale, wout)


def out_bwd(dx, wout, cat, tm=512):
    s = dx.shape[0]
    tm = _row_tile(s, tm)

    def body(dx_ref, wo_ref, cat_ref, do_ref, db_ref, dl_ref):
        dcat = _dot_nt(dx_ref[...].astype(BF16), wo_ref[...])
        do_ref[...] = dcat[:, :ATT_W].astype(BF16)
        db_ref[...] = dcat[:, ATT_W:]
        for hd in range(HEADS):
            lanes = slice(hd * HEAD_PAD, (hd + 1) * HEAD_PAD)
            dl_ref[hd] = jnp.sum(dcat[:, lanes] * cat_ref[:, lanes].astype(F32), axis=-1, keepdims=True)

    row = lambda w: pl.BlockSpec((tm, w), lambda i: (i, 0))
    return pl.pallas_call(
        body, name="out_bwd",
        grid=(s // tm,),
        in_specs=[row(D_MODEL), pl.BlockSpec((CAT_W, D_MODEL), lambda i: (0, 0)), row(CAT_W)],
        out_specs=[row(ATT_W), row(POOL_W), pl.BlockSpec((HEADS, tm, 1), lambda i: (0, i, 0))],
        out_shape=[jax.ShapeDtypeStruct((s, ATT_W), BF16), jax.ShapeDtypeStruct((s, POOL_W), F32),
                   jax.ShapeDtypeStruct((HEADS, s, 1), F32)],
        compiler_params=_params("parallel"),
    )(dx, wout, cat)


def pool_bwd(db, z, wpool, pscale, tm=512):
    s = db.shape[0]
    tm = _row_tile(s, tm)
    n_tiles = s // tm
    te = tm + 2 * POOL_HALO

    def body(dbp_ref, db_ref, dbn_ref, zp_ref, z_ref, zn_ref, wp_ref, ps_ref,
             dp_ref, mixed_ref, dys_ref, dps_ref, ext_s, text_s):
        i = pl.program_id(0)
        cur = z_ref[:, Z_POOL:Z_KPE]
        _fill_ext(ext_s, zp_ref[:, Z_POOL:Z_KPE], cur, zn_ref[:, Z_POOL:Z_KPE], i, n_tiles, tm)
        mixed = _pool_mixed(ext_s, cur, i, s, tm)
        _fill_ext(ext_s, dbp_ref[...], db_ref[...], dbn_ref[...], i, n_tiles, tm)
        erow = i * tm - POOL_HALO + lax.broadcasted_iota(jnp.int32, (te, 1), 0)
        dps = []
        for g, w in enumerate(POOL_WINDOWS):
            lanes = slice(g * GROUP, (g + 1) * GROUP)
            left = w // 2
            right = w - 1 - left
            mb = mixed[g].astype(BF16)
            mixed_ref[:, lanes] = mb
            dps.append(jnp.sum(db_ref[:, lanes] * _dot(mb, wp_ref[g]), axis=0, keepdims=True))
            dys_e = (ext_s[:, lanes] * ps_ref[:, lanes]).astype(BF16)
            dys_ref[:, lanes] = (db_ref[:, lanes] * ps_ref[:, lanes]).astype(BF16)
            dmix_e = _dot_nt(dys_e, wp_ref[g])
            cnt = jnp.minimum(erow + right + 1, s) - jnp.maximum(erow - left, 0)
            text_s[:, lanes] = dmix_e / jnp.maximum(cnt, 1).astype(F32)
            dp_ref[:, lanes] = _window_sum(text_s, g, -right, left, tm) - dmix_e[POOL_HALO:POOL_HALO + tm]
        part = jnp.concatenate(dps, axis=1)

        @pl.when(i == 0)
        def _():
            dps_ref[...] = part

        @pl.when(i > 0)
        def _():
            dps_ref[...] += part

    full = lambda shape: pl.BlockSpec(shape, lambda i: (0,) * len(shape))
    row = lambda w: pl.BlockSpec((tm, w), lambda i: (i, 0))
    return pl.pallas_call(
        body, name="pool_bwd",
        grid=(n_tiles,),
        in_specs=_halo_specs(tm, POOL_W, n_tiles) + _halo_specs(tm, Z_W, n_tiles) + [
            full((len(POOL_WINDOWS), GROUP, GROUP)), full((1, POOL_W))],
        out_specs=[row(POOL_W), row(POOL_W), row(POOL_W), full((1, POOL_W))],
        out_shape=[jax.ShapeDtypeStruct((s, POOL_W), F32), jax.ShapeDtypeStruct((s, POOL_W), BF16),
                   jax.ShapeDtypeStruct((s, POOL_W), BF16), jax.ShapeDtypeStruct((1, POOL_W), F32)],
        scratch_shapes=[pltpu.VMEM((te, POOL_W), F32), pltpu.VMEM((te, POOL_W), F32)],
        compiler_params=_params("arbitrary"),
    )(db, db, db, z, z, z, wpool, pscale)


def loss_head(y, target, tm=512):
    s = y.shape[0]
    tm = _row_tile(s, tm)

    def body(y_ref, t_ref, dy_ref, loss_ref):
        i = pl.program_id(0)
        err = y_ref[...] - t_ref[...]
        dy_ref[...] = err * (1.0 / D_MODEL)
        part = 0.5 * jnp.sum(jnp.sum(err * err, axis=-1, keepdims=True) * (1.0 / D_MODEL), axis=0, keepdims=True)
        part = jnp.broadcast_to(part, (1, 128))

        @pl.when(i == 0)
        def _():
            loss_ref[...] = part

        @pl.when(i > 0)
        def _():
            loss_ref[...] += part

    row = pl.BlockSpec((tm, D_MODEL), lambda i: (i, 0))
    return pl.pallas_call(
        body, name="loss_head",
        grid=(s // tm,),
        in_specs=[row, row],
        out_specs=[row, pl.BlockSpec((1, 128), lambda i: (0, 0))],
        out_shape=[jax.ShapeDtypeStruct((s, D_MODEL), F32), jax.ShapeDtypeStruct((1, 128), F32)],
        compiler_params=_params("arbitrary"),
    )(y, target)


def _pad_heads(w, real):
    lead = w.shape[:-1]
    w = w.reshape(lead + (HEADS, real))
    w = jnp.pad(w, [(0, 0)] * len(lead) + [(0, 0), (0, HEAD_PAD - real)])
    return w.reshape(lead + (ATT_W,))


def _unpad_heads(w, real):
    lead = w.shape[:-1]
    return w.reshape(lead + (HEADS, HEAD_PAD))[..., :real].reshape(lead + (HEADS * real,))


def layout_weights(full):
    w = {}
    depth = full["w_in"].shape[0]
    win = full["w_in"]
    zc = lambda n: jnp.zeros((depth, D_MODEL, n), win.dtype)
    o_pe = Q_LORA + KV_LORA
    w["win"] = jnp.concatenate([win[..., :o_pe], win[..., o_pe + QK_ROPE:], zc(QK_NOPE), win[..., o_pe:o_pe + QK_ROPE],
                                zc(HEAD_PAD - QK_HEAD)], axis=-1)
    w["wuq"] = _pad_heads(full["w_uq"], QK_HEAD)
    w["wuk"] = _pad_heads(full["w_uk"], QK_NOPE)
    w["wuv"] = _pad_heads(full["w_uv"], V_HEAD)
    wo = full["w_out"]
    wo_att = jnp.swapaxes(_pad_heads(jnp.swapaxes(wo[:, :HEADS * V_HEAD], 1, 2), V_HEAD), 1, 2)
    w["wout"] = jnp.concatenate([wo_att, wo[:, HEADS * V_HEAD:]], axis=1)
    pad_vec = lambda v: jnp.pad(v, ((0, 0), (0, HEAD_PAD - QK_HEAD)))
    w["qn"], w["kn"] = pad_vec(full["q_norm"]), pad_vec(full["k_norm"])
    for n in ("ffn1_norm", "mix_norm", "ffn2_norm", "q_lat_norm", "kv_lat_norm", "pool_scale", "w_pool"):
        w[n] = full[n]
    return w


def local_step(x, target, w, wgu1, wd1, wgu2, wd2):
    s = x.shape[0]
    depth = w["win"].shape[0]
    rc, rs1, rs2 = rope_tables(s)
    vec = lambda name, l: w[name][l][None, :]
    saved = []
    for l in range(depth):
        x1 = ffn_fwd(x, vec("ffn1_norm", l), wgu1, wd1, l)
        z, q, k, v, vt = mix_in_fwd(x1, vec("mix_norm", l), w["win"][l], vec("q_lat_norm", l), vec("kv_lat_norm", l),
                                    w["wuq"][l], w["wuk"][l], w["wuv"][l], vec("qn", l), vec("kn", l), rc, rs1, rs2)
        o, lse = flash_fwd(q, k, vt)
        x2, cat = pool_out_fwd(x1, z, o, w["w_pool"][l], vec("pool_scale", l), w["wout"][l])
        x3 = ffn_fwd(x2, vec("ffn2_norm", l), wgu2, wd2, l)
        saved.append((x, x1, z, q, k, v, lse, cat, x2))
        x = x3
    dy, loss = loss_head(x, target)

    names = ["ffn1_norm", "ffn1_w_gu", "ffn1_w_down", "mix_norm", "w_in", "q_lat_norm", "kv_lat_norm", "w_uq", "w_uk",
             "w_uv", "q_norm", "k_norm", "w_pool", "pool_scale", "w_out", "ffn2_norm", "ffn2_w_gu", "ffn2_w_down"]
    grads = {n: [None] * depth for n in names if "ffn" not in n or "norm" in n}
    for n in ("ffn1", "ffn2"):
        grads[n + "_w_gu"] = lax.empty((depth, N_CHIPS, D_MODEL, FF_BLOCK), F32)
        grads[n + "_w_down"] = lax.empty((depth, D_FF, D_MODEL), F32)

    def ffn_weight_grads(n, l, h, act, dg, du, dyh):
        gu = matmul_tn(h, dg, tk=2048, into=grads[n + "_w_gu"], at=lambda j: (l, j))
        grads[n + "_w_gu"] = matmul_tn(h, du, tk=2048, into=gu, at=lambda j: (l, j + 2))
        grads[n + "_w_down"] = matmul_tn(act, dyh, into=grads[n + "_w_down"], at=lambda j: (l,))

    for l in reversed(range(depth)):
        x0, x1, z, q, k, v, lse, cat, x2 = saved[l]
        dx2, h, act, dg, du, dyh, dgn = ffn_bwd(x2, dy, vec("ffn2_norm", l), wgu2, wd2, l)
        grads["ffn2_norm"][l] = dgn[0]
        ffn_weight_grads("ffn2", l, h, act, dg, du, dyh)

        do, db, delta = out_bwd(dx2, w["wout"][l], cat)
        dwout = matmul_tn(cat, dx2)
        grads["w_out"][l] = jnp.concatenate(
            [jnp.swapaxes(_unpad_heads(jnp.swapaxes(dwout[:ATT_W], 0, 1), V_HEAD), 0, 1), dwout[ATT_W:]], axis=0)
        dp, mixed, dys, dps = pool_bwd(db, z, w["w_pool"][l], vec("pool_scale", l))
        grads["pool_scale"][l] = dps[0]
        dwp = matmul_tn(mixed, dys)
        grads["w_pool"][l] = jnp.stack([dwp[g * GROUP:(g + 1) * GROUP, g * GROUP:(g + 1) * GROUP]
                                        for g in range(len(POOL_WINDOWS))])
        dq_t, dk, dv = flash_bwd(q, k, v, do, lse, delta.reshape(HEADS, 1, s))
        dz, cqn, ckvn, dqr, dkr, dvb, dqln, dkvln, dqn, dkn = mla_bwd(
            z, dq_t, dk, dv, dp, vec("q_lat_norm", l), vec("kv_lat_norm", l), w["wuq"][l], w["wuk"][l], w["wuv"][l],
            vec("qn", l), vec("kn", l), rc, rs1, rs2)
        grads["q_lat_norm"][l], grads["kv_lat_norm"][l] = dqln[0], dkvln[0]
        grads["q_norm"][l], grads["k_norm"][l] = dqn[0, :QK_HEAD], dkn[0, :QK_HEAD]
        grads["w_uq"][l] = _unpad_heads(matmul_tn(cqn, dqr), QK_HEAD)
        grads["w_uk"][l] = _unpad_heads(matmul_tn(ckvn, dkr), QK_NOPE)
        grads["w_uv"][l] = _unpad_heads(matmul_tn(ckvn, dvb), V_HEAD)
        dx1, h2, dgm = mix_in_bwd(x1, dx2, dz, vec("mix_norm", l), w["win"][l])
        grads["mix_norm"][l] = dgm[0]
        dwin = matmul_tn(h2, dz)
        grads["w_in"][l] = jnp.concatenate(
            [dwin[:, :Z_POOL], dwin[:, Z_KPE + QK_NOPE:Z_KPE + QK_HEAD], dwin[:, Z_POOL:Z_KPE]], axis=1)

        dy, h, act, dg, du, dyh, dgn = ffn_bwd(x0, dx1, vec("ffn1_norm", l), wgu1, wd1, l)
        grads["ffn1_norm"][l] = dgn[0]
        ffn_weight_grads("ffn1", l, h, act, dg, du, dyh)
    return loss, dy, {n: (jnp.stack(g) if isinstance(g, list) else g) for n, g in grads.items()}


HBM_SPEC = pl.BlockSpec(memory_space=pltpu.HBM)


def _place():
    x, y, c = lax.axis_index("x"), lax.axis_index("y"), lax.axis_index("c")
    other_chips = [(1 - x, y), (x, 1 - y), (1 - x, 1 - y)]
    return x, y, c, other_chips


def _exchange(name, arrays, out_shapes, plan, n_local, n_remote):
    n = len(arrays)

    def body(*refs):
        ins, outs = refs[:n], refs[n:n + len(out_shapes)]
        send_sems, recv_sems, local_sems = refs[n + len(out_shapes):]
        local, sends, recvs, forwards = plan(ins, outs, _place())

        def remote(k, src, dst, dev):
            return pltpu.make_async_remote_copy(src_ref=src, dst_ref=dst, send_sem=send_sems.at[k],
                                                recv_sem=recv_sems.at[k], device_id=dev, device_id_type=MESH)

        started = []
        for k, (src, dst) in enumerate(local):
            cp = pltpu.make_async_copy(src, dst, local_sems.at[k])
            cp.start()
            started.append(cp)
        out_going = []
        for k, (src, dst, dev) in enumerate(sends):
            cp = remote(k, src, dst, dev)
            cp.start()
            out_going.append(cp)
        for k, (dst, dev) in enumerate(recvs):
            remote(k, dst, dst, dev).wait_recv()
            if forwards:
                src, fdst, fdev, _ = forwards[k]
                cp = remote(len(sends) + k, src, fdst, fdev)
                cp.start()
                out_going.append(cp)
        for k, (_, _, fdev, landing) in enumerate(forwards):
            remote(len(sends) + k, landing, landing, fdev).wait_recv()
        for cp in out_going:
            cp.wait_send()
        for cp in started:
            cp.wait()

    return pl.pallas_call(
        body, name=name,
        in_specs=[HBM_SPEC] * n,
        out_specs=[HBM_SPEC] * len(out_shapes),
        out_shape=out_shapes,
        scratch_shapes=[pltpu.SemaphoreType.DMA((n_remote,)), pltpu.SemaphoreType.DMA((n_remote,)),
                        pltpu.SemaphoreType.DMA((max(n_local, 1),))],
    )(*arrays)


def _layer_half(depth, which):
    assert depth % 2 == 0
    return pl.ds(which * (depth // 2), depth // 2)


def gather_chips(shards):
    n = len(shards)
    depth = shards[0].shape[0]

    def plan(ins, outs, place):
        x, y, c, chips = place
        me, sib = 2 * x + y, (x, y, 1 - c)
        mine, theirs = _layer_half(depth, c), _layer_half(depth, 1 - c)
        sends = [(ins[p].at[mine], outs[p].at[me, mine], (px, py, c)) for p in range(n) for (px, py) in chips]
        recvs = [(outs[p].at[2 * px + py, mine], (px, py, c)) for p in range(n) for (px, py) in chips]
        forwards = [(outs[p].at[2 * px + py, mine], outs[p].at[2 * px + py, mine], sib, outs[p].at[2 * px + py, theirs])
                    for p in range(n) for (px, py) in chips]
        return [], sends, recvs, forwards

    out_shapes = [jax.ShapeDtypeStruct((N_CHIPS,) + a.shape, a.dtype) for a in shards]
    return _exchange("gather_chips", shards, out_shapes, plan, 0, 6 * n)


def swap_halves(fulls):
    n = len(fulls)
    depth = fulls[0].shape[0]

    def plan(ins, outs, place):
        x, y, c, _ = place
        sib = (x, y, 1 - c)
        sends = [(ins[p].at[_layer_half(depth, 1 - c)], outs[p], sib) for p in range(n)]
        return [], sends, [(outs[p], sib) for p in range(n)], []

    out_shapes = [jax.ShapeDtypeStruct((depth // 2,) + a.shape[1:], a.dtype) for a in fulls]
    return _exchange("swap_halves", fulls, out_shapes, plan, 0, n)


def scatter_chips(fulls):
    n = len(fulls)

    def plan(ins, outs, place):
        x, y, c, chips = place
        sends = [(ins[p].at[:, 2 * px + py], outs[p].at[j], (px, py, c))
                 for p in range(n) for j, (px, py) in enumerate(chips)]
        recvs = [(outs[p].at[j], (px, py, c)) for p in range(n) for j, (px, py) in enumerate(chips)]
        return [], sends, recvs, []

    out_shapes = [jax.ShapeDtypeStruct((3, a.shape[0]) + a.shape[2:], a.dtype) for a in fulls]
    return _exchange("scatter_chips", fulls, out_shapes, plan, 0, 3 * n)


def swap_sibling(arrays):
    n = len(arrays)

    def plan(ins, outs, place):
        x, y, c, _ = place
        sib = (x, y, 1 - c)
        return [], [(ins[p], outs[p], sib) for p in range(n)], [(outs[p], sib) for p in range(n)], []

    return _exchange("swap_sibling", arrays, [jax.ShapeDtypeStruct(a.shape, a.dtype) for a in arrays], plan, 0, n)


def gather_all(vec):
    def plan(ins, outs, place):
        x, y, c, _ = place
        me = 4 * x + 2 * y + c
        flips = [(fx, fy, fc) for fx in (0, 1) for fy in (0, 1) for fc in (0, 1)][1:]
        peers = [(1 - x if fx else x, 1 - y if fy else y, 1 - c if fc else c) for fx, fy, fc in flips]
        local = [(ins[0], outs[0].at[me])]
        sends = [(ins[0], outs[0].at[me], dev) for dev in peers]
        recvs = [(outs[0].at[4 * dev[0] + 2 * dev[1] + dev[2]], dev) for dev in peers]
        return local, sends, recvs, []

    return _exchange("gather_all", [vec], [jax.ShapeDtypeStruct((8,) + vec.shape, vec.dtype)], plan, 1, 7)[0]


def _pick_rows(rows, cols, block_bytes=512 * 1024):
    best = None
    for t in range(8, rows + 1, 8):
        if rows % t == 0 and t * cols * 4 <= block_bytes:
            best = t
    return best or rows


def sum_rows_of(own, stacked):
    rows, cols = own.shape
    n = stacked.shape[0]
    tr = _pick_rows(rows, cols)

    def body(own_ref, st_ref, o_ref):
        acc = own_ref[...]
        for k in range(n):
            acc = acc + st_ref[k]
        o_ref[...] = acc

    return pl.pallas_call(
        body, name="sum_rows_of",
        grid=(rows // tr,),
        in_specs=[pl.BlockSpec((tr, cols), lambda i: (i, 0)), pl.BlockSpec((n, tr, cols), lambda i: (0, i, 0))],
        out_specs=pl.BlockSpec((tr, cols), lambda i: (i, 0)),
        out_shape=jax.ShapeDtypeStruct((rows, cols), F32),
        compiler_params=_params("parallel"),
    )(own, stacked)


def sum_stack(stacked):
    n, rows, cols = stacked.shape
    tr = _pick_rows(rows, cols)

    def body(st_ref, o_ref):
        acc = st_ref[0]
        for k in range(1, n):
            acc = acc + st_ref[k]
        o_ref[...] = acc

    return pl.pallas_call(
        body, name="sum_stack",
        grid=(rows // tr,),
        in_specs=[pl.BlockSpec((n, tr, cols), lambda i: (0, i, 0))],
        out_specs=pl.BlockSpec((tr, cols), lambda i: (i, 0)),
        out_shape=jax.ShapeDtypeStruct((rows, cols), F32),
        compiler_params=_params("parallel"),
    )(stacked)


def adamw(w, ga, gb, m, v):
    rows, cols = w.shape
    tr = _pick_rows(rows, cols, 256 * 1024)
    c1 = 1.0 - ADAM_B1 ** ADAM_STEP
    c2 = 1.0 - ADAM_B2 ** ADAM_STEP
    ins = [w, ga] + ([gb] if gb is not None else []) + [m, v]

    def body(*refs):
        w_ref, ga_ref = refs[0], refs[1]
        m_ref, v_ref, g_out, d_out, m_out, v_out = refs[-6:]
        g = ga_ref[...]
        if gb is not None:
            g = g + refs[2][...]
        mn = ADAM_B1 * m_ref[...] + (1.0 - ADAM_B1) * g
        vn = ADAM_B2 * v_ref[...] + (1.0 - ADAM_B2) * (g * g)
        m_hat = mn / c1
        v_hat = vn / c2
        g_out[...] = g
        d_out[...] = -ADAM_LR * (m_hat / (jnp.sqrt(v_hat) + ADAM_EPS) + ADAM_WD * w_ref[...])
        m_out[...] = mn
        v_out[...] = vn

    spec = pl.BlockSpec((tr, cols), lambda i: (i, 0))
    return pl.pallas_call(
        body, name="adamw",
        grid=(rows // tr,),
        in_specs=[spec] * len(ins),
        out_specs=[spec] * 4,
        out_shape=[jax.ShapeDtypeStruct((rows, cols), F32)] * 4,
        compiler_params=_params("parallel"),
    )(*ins)


WEIGHT_NAMES = ["ffn1_norm", "ffn1_w_gu", "ffn1_w_down", "mix_norm", "w_in", "q_lat_norm", "kv_lat_norm", "w_uq", "w_uk",
                "w_uv", "q_norm", "k_norm", "w_pool", "pool_scale", "w_out", "ffn2_norm", "ffn2_w_gu", "ffn2_w_down"]
COL_SHARDED = ["ffn1_w_gu", "w_in", "w_uq", "w_uk", "w_uv", "ffn2_w_gu"]
ROW_SHARDED = ["ffn1_w_down", "w_out", "ffn2_w_down"]
SHARDED = [n for n in WEIGHT_NAMES if n in COL_SHARDED or n in ROW_SHARDED]
REPLICATED = [n for n in WEIGHT_NAMES if n not in SHARDED]


def _join_chips(g, name):
    _, depth, r, c = g.shape
    if name in COL_SHARDED:
        return jnp.transpose(g, (1, 2, 0, 3)).reshape(depth, r, N_CHIPS * c)
    return jnp.transpose(g, (1, 0, 2, 3)).reshape(depth, N_CHIPS * r, c)


def _split_chips(full, name):
    depth, r, c = full.shape
    if name in COL_SHARDED:
        return jnp.transpose(full.reshape(depth, r, N_CHIPS, c // N_CHIPS), (0, 2, 1, 3))
    return full.reshape(depth, N_CHIPS, r // N_CHIPS, c)


def _pack(vs):
    flat = jnp.concatenate([v.reshape(-1) for v in vs])
    pad = (-flat.shape[0]) % (8 * 128)
    return jnp.pad(flat, (0, pad)).reshape(-1, 128)


def _unpack(packed, like):
    flat = packed.reshape(-1)
    out, at = [], 0
    for v in like:
        out.append(flat[at:at + v.size].reshape(v.shape))
        at += v.size
    return out


def kernel(x, ffn1_norm, ffn1_w_gu, ffn1_w_down, mix_norm, w_in, q_lat_norm, kv_lat_norm, w_uq, w_uk, w_uv, q_norm, k_norm, w_pool, pool_scale, w_out, ffn2_norm, ffn2_w_gu, ffn2_w_down, loss_target, m_ffn1_norm, m_ffn1_w_gu, m_ffn1_w_down, m_mix_norm, m_w_in, m_q_lat_norm, m_kv_lat_norm, m_w_uq, m_w_uk, m_w_uv, m_q_norm, m_k_norm, m_w_pool, m_pool_scale, m_w_out, m_ffn2_norm, m_ffn2_w_gu, m_ffn2_w_down, v_ffn1_norm, v_ffn1_w_gu, v_ffn1_w_down, v_mix_norm, v_w_in, v_q_lat_norm, v_kv_lat_norm, v_w_uq, v_w_uk, v_w_uv, v_q_norm, v_k_norm, v_w_pool, v_pool_scale, v_w_out, v_ffn2_norm, v_ffn2_w_gu, v_ffn2_w_down):
    args = locals()
    weights = {n: args[n] for n in WEIGHT_NAMES}
    moments_m = {n: args["m_" + n] for n in WEIGHT_NAMES}
    moments_v = {n: args["v_" + n] for n in WEIGHT_NAMES}

    chip = 2 * lax.axis_index("x") + lax.axis_index("y")
    core = lax.axis_index("c")
    shards = [weights[n].astype(BF16) for n in SHARDED]
    gathered = {n: lax.dynamic_update_index_in_dim(g, own, chip, 0)
                for n, own, g in zip(SHARDED, shards, gather_chips(shards))}
    full = {n: _join_chips(gathered[n], n) for n in SHARDED if "ffn" not in n}
    for n in REPLICATED:
        full[n] = weights[n].astype(BF16) if n == "w_pool" else weights[n]
    w = layout_weights(full)
    wd1 = _join_chips(gathered["ffn1_w_down"], "ffn1_w_down")
    wd2 = _join_chips(gathered["ffn2_w_down"], "ffn2_w_down")

    loss_part, grad_x, grads = local_step(x[0], loss_target[0], w, gathered["ffn1_w_gu"], wd1, gathered["ffn2_w_gu"], wd2)
    loss = lax.psum(loss_part[0, 0], ("x", "y", "c"))

    split = [grads[n] if n.endswith("w_gu") else _split_chips(grads[n], n) for n in SHARDED]
    half = split[0].shape[0] // 2

    def add(own, stacked):
        rows = own.size // own.shape[-1]
        total = sum_rows_of(own.reshape(rows, -1), stacked.reshape(stacked.shape[0], rows, -1))
        return total.reshape(own.shape)

    from_sibling = swap_halves(split)
    chip_sums = [add(lax.dynamic_slice_in_dim(g4, core * half, half, axis=0), other[None])
                 for g4, other in zip(split, from_sibling)]
    landed = scatter_chips(chip_sums)
    totals = [add(lax.dynamic_index_in_dim(h4, chip, 1, keepdims=False), land) for h4, land in zip(chip_sums, landed)]
    out = {}
    for n, mine, theirs in zip(SHARDED, totals, swap_sibling(totals)):
        g = jnp.concatenate([jnp.where(core == 0, mine, theirs), jnp.where(core == 0, theirs, mine)], axis=0)
        shape = weights[n].shape
        flat = lambda a: a.reshape(shape[0] * shape[1], shape[2])
        res = adamw(flat(weights[n]), flat(g), None, flat(moments_m[n]), flat(moments_v[n]))
        out[n] = [r.reshape(shape) for r in res]

    rep_g = [grads[n] for n in REPLICATED]
    total = sum_stack(gather_all(_pack(rep_g)))
    res = adamw(_pack([weights[n] for n in REPLICATED]), total, None,
                _pack([moments_m[n] for n in REPLICATED]), _pack([moments_v[n] for n in REPLICATED]))
    unpacked = [_unpack(r, rep_g) for r in res]
    for k, n in enumerate(REPLICATED):
        out[n] = [u[k] for u in unpacked]

    return (loss, grad_x[None], *[out[n][0] for n in WEIGHT_NAMES], *[out[n][1] for n in WEIGHT_NAMES],
            *[out[n][2] for n in WEIGHT_NAMES], *[out[n][3] for n in WEIGHT_NAMES])
```

```python
import functools

import jax
import jax.numpy as jnp
from jax import lax
from jax.experimental import pallas as pl
from jax.experimental.pallas import tpu as pltpu

F32 = jnp.float32
BF16 = jnp.bfloat16
MESH = pl.DeviceIdType.MESH

DEPTH = 4
D_MODEL = 1024
D_FF = 2816
FF_BLOCK = D_FF // 2
N_CHIPS = 4
HEADS = 8
HEAD_PAD = 128
QK_NOPE = 64
QK_ROPE = 32
QK_HEAD = QK_NOPE + QK_ROPE
V_HEAD = 64
Q_LORA = 384
KV_LORA = 256
POOL_W = 512
POOL_WINDOWS = (2, 4, 8, 16)
POOL_HALO = 8
GROUP = 128
D_IN = Q_LORA + KV_LORA + QK_ROPE + POOL_W
Z_W = 1280
Z_CKV = Q_LORA
Z_POOL = Q_LORA + KV_LORA
Z_KPE = Z_POOL + POOL_W
ATT_W = HEADS * HEAD_PAD
CAT_W = ATT_W + POOL_W
EPS = 1e-6
ROPE_THETA = 10000.0
ADAM_LR, ADAM_B1, ADAM_B2, ADAM_EPS, ADAM_WD, ADAM_STEP = 0.001, 0.9, 0.999, 1e-08, 0.01, 10

VMEM_LIMIT = 56 * 1024 * 1024


def _params(*sem):
    return pltpu.CompilerParams(dimension_semantics=sem, vmem_limit_bytes=VMEM_LIMIT)


def _dot(a, b):
    return jnp.dot(a, b, preferred_element_type=F32)


def _dot_nt(a, b):
    return lax.dot_general(a, b, (((1,), (1,)), ((), ())), preferred_element_type=F32)


def _dot_tn(a, b):
    return lax.dot_general(a, b, (((0,), (0,)), ((), ())), preferred_element_type=F32)


def _norm_fwd(x, n):
    rstd = lax.rsqrt(jnp.sum(x * x, axis=-1, keepdims=True) * (1.0 / n) + EPS)
    return x * rstd, rstd


def _norm_bwd(dy, g, xhat, rstd, n):
    dxhat = dy * g
    dx = rstd * (dxhat - xhat * (jnp.sum(dxhat * xhat, axis=-1, keepdims=True) * (1.0 / n)))
    return dx, dy * xhat


def _row_tile(s, want):
    t = min(s, want)
    assert s % t == 0
    return t


def ffn_fwd(x, gn, wgu4, wd, layer, tm=512):
    s = x.shape[0]
    tm = _row_tile(s, tm)

    def body(x_ref, gn_ref, wg_ref, wu_ref, wd_ref, o_ref, g_ref, u_ref, h_s, acc_s):
        j = pl.program_id(1)

        @pl.when(j == 0)
        def _():
            xhat, _ = _norm_fwd(x_ref[...], D_MODEL)
            h_s[...] = (xhat * gn_ref[...]).astype(BF16)
            acc_s[...] = jnp.zeros_like(acc_s)

        h = h_s[...]
        g = _dot(h, wg_ref[...])
        u = _dot(h, wu_ref[...])
        g_ref[...] = g.astype(BF16)
        u_ref[...] = u.astype(BF16)
        a = (g * jax.nn.sigmoid(g) * u).astype(BF16)
        acc_s[...] += _dot(a, wd_ref[...])

        @pl.when(j == 1)
        def _():
            o_ref[...] = x_ref[...] + 0.5 * acc_s[...]

    return pl.pallas_call(
        body, name="ffn_fwd",
        grid=(s // tm, 2),
        in_specs=[
            pl.BlockSpec((tm, D_MODEL), lambda i, j: (i, 0)),
            pl.BlockSpec((1, D_MODEL), lambda i, j: (0, 0)),
            pl.BlockSpec((None, None, D_MODEL, FF_BLOCK), lambda i, j: (j, layer, 0, 0)),
            pl.BlockSpec((None, None, D_MODEL, FF_BLOCK), lambda i, j: (j + 2, layer, 0, 0)),
            pl.BlockSpec((None, FF_BLOCK, D_MODEL), lambda i, j: (layer, j, 0)),
        ],
        out_specs=[pl.BlockSpec((tm, D_MODEL), lambda i, j: (i, 0)),
                   pl.BlockSpec((tm, FF_BLOCK), lambda i, j: (i, j)), pl.BlockSpec((tm, FF_BLOCK), lambda i, j: (i, j))],
        out_shape=[jax.ShapeDtypeStruct((s, D_MODEL), F32), jax.ShapeDtypeStruct((s, D_FF), BF16),
                   jax.ShapeDtypeStruct((s, D_FF), BF16)],
        scratch_shapes=[pltpu.VMEM((tm, D_MODEL), BF16), pltpu.VMEM((tm, D_MODEL), F32)],
        compiler_params=_params("parallel", "arbitrary"),
    )(x, gn, wgu4, wgu4, wd)


def ffn_bwd(x, dy, gn, g_pre, u_pre, wgu4, wd, layer, tm=256):
    s = x.shape[0]
    tm = _row_tile(s, tm)

    def body(x_ref, dy_ref, gn_ref, g_ref, u_ref, wg_ref, wu_ref, wd_ref,
             dx_ref, h_ref, a_ref, dg_ref, du_ref, dyh_ref, dgn_ref, dh_s):
        i, j = pl.program_id(0), pl.program_id(1)

        @pl.when(j == 0)
        def _():
            xhat, _ = _norm_fwd(x_ref[...], D_MODEL)
            h_ref[...] = (xhat * gn_ref[...]).astype(BF16)
            dyh_ref[...] = (0.5 * dy_ref[...]).astype(BF16)
            dh_s[...] = jnp.zeros_like(dh_s)

        g = g_ref[...].astype(F32)
        u = u_ref[...].astype(F32)
        sg = jax.nn.sigmoid(g)
        silu = g * sg
        a_ref[...] = (silu * u).astype(BF16)
        da = _dot_nt(dyh_ref[...], wd_ref[...])
        dg = (da * u * (sg * (1.0 + g * (1.0 - sg)))).astype(BF16)
        du = (da * silu).astype(BF16)
        dg_ref[...] = dg
        du_ref[...] = du
        dh_s[...] += _dot_nt(dg, wg_ref[...]) + _dot_nt(du, wu_ref[...])

        @pl.when(j == 1)
        def _():
            xhat, rstd = _norm_fwd(x_ref[...], D_MODEL)
            dxn, dgrow = _norm_bwd(dh_s[...], gn_ref[...], xhat, rstd, D_MODEL)
            dx_ref[...] = dy_ref[...] + dxn
            part = jnp.sum(dgrow, axis=0, keepdims=True)

            @pl.when(i == 0)
            def _():
                dgn_ref[...] = part

            @pl.when(i > 0)
            def _():
                dgn_ref[...] += part

    row = lambda i, j: (i, 0)
    return pl.pallas_call(
        body, name="ffn_bwd",
        grid=(s // tm, 2),
        in_specs=[
            pl.BlockSpec((tm, D_MODEL), row),
            pl.BlockSpec((tm, D_MODEL), row),
            pl.BlockSpec((1, D_MODEL), lambda i, j: (0, 0)),
            pl.BlockSpec((tm, FF_BLOCK), lambda i, j: (i, j)),
            pl.BlockSpec((tm, FF_BLOCK), lambda i, j: (i, j)),
            pl.BlockSpec((None, None, D_MODEL, FF_BLOCK), lambda i, j: (j, layer, 0, 0)),
            pl.BlockSpec((None, None, D_MODEL, FF_BLOCK), lambda i, j: (j + 2, layer, 0, 0)),
            pl.BlockSpec((None, FF_BLOCK, D_MODEL), lambda i, j: (layer, j, 0)),
        ],
        out_specs=[
            pl.BlockSpec((tm, D_MODEL), row),
            pl.BlockSpec((tm, D_MODEL), row),
            pl.BlockSpec((tm, FF_BLOCK), lambda i, j: (i, j)),
            pl.BlockSpec((tm, FF_BLOCK), lambda i, j: (i, j)),
            pl.BlockSpec((tm, FF_BLOCK), lambda i, j: (i, j)),
            pl.BlockSpec((tm, D_MODEL), row),
            pl.BlockSpec((1, D_MODEL), lambda i, j: (0, 0)),
        ],
        out_shape=[
            jax.ShapeDtypeStruct((s, D_MODEL), F32),
            jax.ShapeDtypeStruct((s, D_MODEL), BF16),
            jax.ShapeDtypeStruct((s, D_FF), BF16),
            jax.ShapeDtypeStruct((s, D_FF), BF16),
            jax.ShapeDtypeStruct((s, D_FF), BF16),
            jax.ShapeDtypeStruct((s, D_MODEL), BF16),
            jax.ShapeDtypeStruct((1, D_MODEL), F32),
        ],
        scratch_shapes=[pltpu.VMEM((tm, D_MODEL), F32)],
        compiler_params=_params("arbitrary", "arbitrary"),
    )(x, dy, gn, g_pre, u_pre, wgu4, wgu4, wd)


def matmul_tn(a, b, tk=1024, tn_max=1536, into=None, at=None):
    s, m = a.shape
    n = b.shape[1]
    tk = _row_tile(s, tk)
    tn = n
    if n > tn_max:
        tn = n // 2
    assert n % tn == 0 and tn % 128 == 0
    nk = s // tk

    def body(a_ref, b_ref, *rest):
        o_ref, acc_s = rest[-2:]
        k = pl.program_id(1)

        @pl.when(k == 0)
        def _():
            acc_s[...] = jnp.zeros_like(acc_s)

        acc_s[...] += _dot_tn(a_ref[...].astype(BF16), b_ref[...].astype(BF16))

        @pl.when(k == nk - 1)
        def _():
            o_ref[...] = acc_s[...]

    in_specs = [pl.BlockSpec((tk, m), lambda j, k: (k, 0)), pl.BlockSpec((tk, tn), lambda j, k: (k, j))]
    if into is None:
        operands, aliases = (a, b), {}
        out_specs = pl.BlockSpec((m, tn), lambda j, k: (0, j))
        out_shape = jax.ShapeDtypeStruct((m, n), F32)
    else:
        assert into.shape[-2:] == (m, tn) and into.dtype == F32
        lead = into.ndim - 2
        operands, aliases = (a, b, into), {2: 0}
        in_specs.append(pl.BlockSpec(memory_space=pl.ANY))
        out_specs = pl.BlockSpec((None,) * lead + (m, tn), lambda j, k: tuple(at(j)) + (0, 0))
        out_shape = jax.ShapeDtypeStruct(into.shape, F32)
    return pl.pallas_call(
        body, name="matmul_tn",
        grid=(n // tn, nk),
        in_specs=in_specs,
        out_specs=out_specs,
        out_shape=out_shape,
        scratch_shapes=[pltpu.VMEM((m, tn), F32)],
        input_output_aliases=aliases,
        compiler_params=_params("parallel", "arbitrary"),
    )(*operands)


def _rope_fwd(y, c, s1, s2):
    return y * c + pltpu.roll(y, HEAD_PAD - 16, 1) * s1 + pltpu.roll(y, 16, 1) * s2


def _rope_bwd(dy, c, s1, s2):
    return dy * c + pltpu.roll(dy * s1, 16, 1) + pltpu.roll(dy * s2, HEAD_PAD - 16, 1)


def rope_tables(s):
    pos = jnp.arange(s, dtype=F32)
    inv = ROPE_THETA ** (-jnp.arange(0, QK_ROPE, 2, dtype=F32) / QK_ROPE)
    ang = pos[:, None] * inv[None, :]
    cos, sin = jnp.cos(ang), jnp.sin(ang)
    z16 = jnp.zeros((s, 16), F32)
    c = jnp.concatenate([jnp.ones((s, QK_NOPE), F32), cos, cos, z16, z16], axis=1)
    s1 = jnp.concatenate([jnp.zeros((s, QK_NOPE), F32), -sin, z16, z16, z16], axis=1)
    s2 = jnp.concatenate([jnp.zeros((s, QK_NOPE), F32), z16, sin, z16, z16], axis=1)
    return c, s1, s2


def mix_in_fwd(x, gm, win, qln, kvln, wuq, wuk, wuv, qn, kn, rc, rs1, rs2, tm=512):
    s = x.shape[0]
    tm = _row_tile(s, tm)
    scale = QK_HEAD ** -0.5

    def body(x_ref, gm_ref, win_ref, qln_ref, kvln_ref, wuq_ref, wuk_ref, wuv_ref, qn_ref, kn_ref,
             rc_ref, rs1_ref, rs2_ref, z_ref, q_ref, k_ref, v_ref, vt_ref):
        xhat, _ = _norm_fwd(x_ref[...], D_MODEL)
        h = (xhat * gm_ref[...]).astype(BF16)
        z = _dot(h, win_ref[...])
        z_ref[...] = z
        cq, _ = _norm_fwd(z[:, :Q_LORA], Q_LORA)
        cqn = (cq * qln_ref[...]).astype(BF16)
        ckv, _ = _norm_fwd(z[:, Z_CKV:Z_POOL], KV_LORA)
        ckvn = (ckv * kvln_ref[...]).astype(BF16)
        kpe = z[:, Z_KPE:]
        q_raw = _dot(cqn, wuq_ref[...])
        k_raw = _dot(ckvn, wuk_ref[...])
        v = _dot(ckvn, wuv_ref[...])
        v_ref[...] = v.astype(BF16)
        vt_ref[...] = jnp.transpose(v).astype(BF16)
        c, s1, s2 = rc_ref[...], rs1_ref[...], rs2_ref[...]
        for hd in range(HEADS):
            lanes = slice(hd * HEAD_PAD, (hd + 1) * HEAD_PAD)
            qh, _ = _norm_fwd(q_raw[:, lanes], QK_HEAD)
            q_ref[:, lanes] = (_rope_fwd(qh * qn_ref[...], c, s1, s2) * scale).astype(BF16)
            kh, _ = _norm_fwd(k_raw[:, lanes] + kpe, QK_HEAD)
            k_ref[:, lanes] = _rope_fwd(kh * kn_ref[...], c, s1, s2).astype(BF16)

    full = lambda shape: pl.BlockSpec(shape, lambda i: (0,) * len(shape))
    row = lambda w: pl.BlockSpec((tm, w), lambda i: (i, 0))
    return pl.pallas_call(
        body, name="mix_in_fwd",
        grid=(s // tm,),
        in_specs=[row(D_MODEL), full((1, D_MODEL)), full((D_MODEL, Z_W)), full((1, Q_LORA)), full((1, KV_LORA)),
                  full((Q_LORA, ATT_W)), full((KV_LORA, ATT_W)), full((KV_LORA, ATT_W)),
                  full((1, HEAD_PAD)), full((1, HEAD_PAD)), row(HEAD_PAD), row(HEAD_PAD), row(HEAD_PAD)],
        out_specs=[row(Z_W), row(ATT_W), row(ATT_W), row(ATT_W), pl.BlockSpec((ATT_W, tm), lambda i: (0, i))],
        out_shape=[jax.ShapeDtypeStruct((s, Z_W), F32)] + [jax.ShapeDtypeStruct((s, ATT_W), BF16)] * 3 + [
            jax.ShapeDtypeStruct((ATT_W, s), BF16)],
        compiler_params=_params("parallel"),
    )(x, gm, win, qln, kvln, wuq, wuk, wuv, qn, kn, rc, rs1, rs2)


def mla_bwd(z, dq_t, dk, dv, dp, qln, kvln, wuq, wuk, wuv, qn, kn, rc, rs1, rs2, tm=512):
    s = z.shape[0]
    tm = _row_tile(s, tm)
    scale = QK_HEAD ** -0.5

    def body(z_ref, dqt_ref, dk_ref, dv_ref, dp_ref, qln_ref, kvln_ref, wuq_ref, wuk_ref, wuv_ref, qn_ref, kn_ref,
             rc_ref, rs1_ref, rs2_ref,
             dz_ref, cqn_ref, ckvn_ref, dqr_ref, dkr_ref, dvb_ref, dqln_ref, dkvln_ref, dqn_ref, dkn_ref):
        i = pl.program_id(0)
        z = z_ref[...]
        cq, cq_rstd = _norm_fwd(z[:, :Q_LORA], Q_LORA)
        cqn = (cq * qln_ref[...]).astype(BF16)
        ckv, ckv_rstd = _norm_fwd(z[:, Z_CKV:Z_POOL], KV_LORA)
        ckvn = (ckv * kvln_ref[...]).astype(BF16)
        kpe = z[:, Z_KPE:]
        cqn_ref[...] = cqn
        ckvn_ref[...] = ckvn
        q_raw = _dot(cqn, wuq_ref[...])
        k_raw = _dot(ckvn, wuk_ref[...])
        c, s1, s2 = rc_ref[...], rs1_ref[...], rs2_ref[...]
        lane = lax.broadcasted_iota(jnp.int32, (tm, HEAD_PAD), 1)
        rope_lanes = (lane >= QK_NOPE) & (lane < QK_HEAD)
        dkpe = jnp.zeros((tm, HEAD_PAD), F32)
        dqn = jnp.zeros((tm, HEAD_PAD), F32)
        dkn = jnp.zeros((tm, HEAD_PAD), F32)
        for hd in range(HEADS):
            lanes = slice(hd * HEAD_PAD, (hd + 1) * HEAD_PAD)
            qh, q_rstd = _norm_fwd(q_raw[:, lanes], QK_HEAD)
            dqh = jnp.transpose(dqt_ref[lanes, :]) * scale
            dqr, dg = _norm_bwd(_rope_bwd(dqh, c, s1, s2), qn_ref[...], qh, q_rstd, QK_HEAD)
            dqn += dg
            dqr_ref[:, lanes] = dqr.astype(BF16)
            kh, k_rstd = _norm_fwd(k_raw[:, lanes] + kpe, QK_HEAD)
            dkr, dg = _norm_bwd(_rope_bwd(dk_ref[:, lanes], c, s1, s2), kn_ref[...], kh, k_rstd, QK_HEAD)
            dkn += dg
            dkr_ref[:, lanes] = dkr.astype(BF16)
            dkpe += jnp.where(rope_lanes, dkr, 0.0)
        dvb = dv_ref[...].astype(BF16)
        dvb_ref[...] = dvb
        dcqn = _dot_nt(dqr_ref[...], wuq_ref[...])
        dckvn = _dot_nt(dkr_ref[...], wuk_ref[...]) + _dot_nt(dvb, wuv_ref[...])
        dcq, dqln = _norm_bwd(dcqn, qln_ref[...], cq, cq_rstd, Q_LORA)
        dckv, dkvln = _norm_bwd(dckvn, kvln_ref[...], ckv, ckv_rstd, KV_LORA)
        dz_ref[:, :Q_LORA] = dcq.astype(BF16)
        dz_ref[:, Z_CKV:Z_POOL] = dckv.astype(BF16)
        dz_ref[:, Z_POOL:Z_KPE] = dp_ref[...].astype(BF16)
        dz_ref[:, Z_KPE:] = dkpe.astype(BF16)
        parts = [(dqln_ref, dqln), (dkvln_ref, dkvln), (dqn_ref, dqn), (dkn_ref, dkn)]

        @pl.when(i == 0)
        def _():
            for ref, val in parts:
                ref[...] = jnp.sum(val, axis=0, keepdims=True)

        @pl.when(i > 0)
        def _():
            for ref, val in parts:
                ref[...] += jnp.sum(val, axis=0, keepdims=True)

    full = lambda shape: pl.BlockSpec(shape, lambda i: (0,) * len(shape))
    row = lambda w: pl.BlockSpec((tm, w), lambda i: (i, 0))
    return pl.pallas_call(
        body, name="mla_bwd",
        grid=(s // tm,),
        in_specs=[row(Z_W), pl.BlockSpec((ATT_W, tm), lambda i: (0, i)), row(ATT_W), row(ATT_W), row(POOL_W),
                  full((1, Q_LORA)), full((1, KV_LORA)), full((Q_LORA, ATT_W)), full((KV_LORA, ATT_W)),
                  full((KV_LORA, ATT_W)), full((1, HEAD_PAD)), full((1, HEAD_PAD)),
                  row(HEAD_PAD), row(HEAD_PAD), row(HEAD_PAD)],
        out_specs=[row(Z_W), row(Q_LORA), row(KV_LORA), row(ATT_W), row(ATT_W), row(ATT_W),
                   full((1, Q_LORA)), full((1, KV_LORA)), full((1, HEAD_PAD)), full((1, HEAD_PAD))],
        out_shape=[jax.ShapeDtypeStruct((s, Z_W), BF16), jax.ShapeDtypeStruct((s, Q_LORA), BF16),
                   jax.ShapeDtypeStruct((s, KV_LORA), BF16)] + [jax.ShapeDtypeStruct((s, ATT_W), BF16)] * 3 + [
                   jax.ShapeDtypeStruct((1, Q_LORA), F32), jax.ShapeDtypeStruct((1, KV_LORA), F32),
                   jax.ShapeDtypeStruct((1, HEAD_PAD), F32), jax.ShapeDtypeStruct((1, HEAD_PAD), F32)],
        compiler_params=_params("arbitrary"),
    )(z, dq_t, dk, dv, dp, qln, kvln, wuq, wuk, wuv, qn, kn, rc, rs1, rs2)


def mix_in_bwd(x, dx_in, dz, gm, win, tm=512):
    s = x.shape[0]
    tm = _row_tile(s, tm)

    def body(x_ref, dxin_ref, dz_ref, gm_ref, win_ref, dx_ref, h_ref, dgm_ref):
        i = pl.program_id(0)
        xhat, rstd = _norm_fwd(x_ref[...], D_MODEL)
        h_ref[...] = (xhat * gm_ref[...]).astype(BF16)
        dh = _dot_nt(dz_ref[...], win_ref[...])
        dxn, dgrow = _norm_bwd(dh, gm_ref[...], xhat, rstd, D_MODEL)
        dx_ref[...] = dxin_ref[...] + dxn
        part = jnp.sum(dgrow, axis=0, keepdims=True)

        @pl.when(i == 0)
        def _():
            dgm_ref[...] = part

        @pl.when(i > 0)
        def _():
            dgm_ref[...] += part

    full = lambda shape: pl.BlockSpec(shape, lambda i: (0,) * len(shape))
    row = lambda w: pl.BlockSpec((tm, w), lambda i: (i, 0))
    return pl.pallas_call(
        body, name="mix_in_bwd",
        grid=(s // tm,),
        in_specs=[row(D_MODEL), row(D_MODEL), row(Z_W), full((1, D_MODEL)), full((D_MODEL, Z_W))],
        out_specs=[row(D_MODEL), row(D_MODEL), full((1, D_MODEL))],
        out_shape=[jax.ShapeDtypeStruct((s, D_MODEL), F32), jax.ShapeDtypeStruct((s, D_MODEL), BF16),
                   jax.ShapeDtypeStruct((1, D_MODEL), F32)],
        compiler_params=_params("arbitrary"),
    )(x, dx_in, dz, gm, win)


ATT_CHUNK = 512


def flash_fwd(q, k, vt, tq=1024, tk=4096, chunk=ATT_CHUNK):
    s = q.shape[0]
    tq, tk = _row_tile(s, tq), _row_tile(s, tk)
    nk = s // tk
    tc = _row_tile(tk, chunk)
    sub = tk // tc

    def body(q_ref, k_ref, vt_ref, o_ref, lse_ref, m_s, l_s, acc_s):
        kk = pl.program_id(2)

        @pl.when(kk == 0)
        def _():
            m_s[...] = jnp.full_like(m_s, -jnp.inf)
            l_s[...] = jnp.zeros_like(l_s)
            acc_s[...] = jnp.zeros_like(acc_s)

        q_t = q_ref[...]

        def scores(c):
            return _dot_nt(k_ref[c * tc:(c + 1) * tc, :], q_t)

        m, l, acc = m_s[...], l_s[...], acc_s[...]
        s_next = scores(0)
        for c in range(sub):
            s_t = s_next
            if c + 1 < sub:
                s_next = scores(c + 1)
            m_new = jnp.maximum(m, jnp.max(s_t, axis=0, keepdims=True))
            alpha = jnp.exp(m - m_new)
            p_t = jnp.exp(s_t - m_new)
            l = alpha * l + jnp.sum(p_t, axis=0, keepdims=True)
            acc = alpha * acc + _dot(vt_ref[:, c * tc:(c + 1) * tc], p_t.astype(BF16))
            m = m_new
        m_s[...], l_s[...], acc_s[...] = m, l, acc

        @pl.when(kk == nk - 1)
        def _():
            o_ref[...] = jnp.transpose(acc_s[...] / l_s[...]).astype(BF16)
            lse_ref[...] = m_s[...] + jnp.log(l_s[...])

    return pl.pallas_call(
        body, name="flash_fwd",
        grid=(HEADS, s // tq, nk),
        in_specs=[pl.BlockSpec((tq, HEAD_PAD), lambda h, i, kk: (i, h)),
                  pl.BlockSpec((tk, HEAD_PAD), lambda h, i, kk: (kk, h)),
                  pl.BlockSpec((HEAD_PAD, tk), lambda h, i, kk: (h, kk))],
        out_specs=[pl.BlockSpec((tq, HEAD_PAD), lambda h, i, kk: (i, h)),
                   pl.BlockSpec((None, 1, tq), lambda h, i, kk: (h, 0, i))],
        out_shape=[jax.ShapeDtypeStruct((s, ATT_W), BF16), jax.ShapeDtypeStruct((HEADS, 1, s), F32)],
        scratch_shapes=[pltpu.VMEM((1, tq), F32), pltpu.VMEM((1, tq), F32), pltpu.VMEM((HEAD_PAD, tq), F32)],
        compiler_params=_params("parallel", "parallel", "arbitrary"),
    )(q, k, vt)


def flash_bwd(q, k, v, do, lse_row, delta_row, tq=2048, tk=1024, chunk=ATT_CHUNK):
    s = q.shape[0]
    tq, tk = _row_tile(s, tq), _row_tile(s, tk)
    nq = s // tq
    tc = _row_tile(tq, chunk)
    sub = tq // tc

    def body(q_ref, k_ref, v_ref, do_ref, lse_ref, dl_ref, dqt_ref, dk_ref, dv_ref, kt_s, dk_s, dv_s):
        kb, qb = pl.program_id(1), pl.program_id(2)

        @pl.when(qb == 0)
        def _():
            kt_s[...] = jnp.transpose(k_ref[...].astype(F32)).astype(BF16)
            dk_s[...] = jnp.zeros_like(dk_s)
            dv_s[...] = jnp.zeros_like(dv_s)

        k_t, v_t = k_ref[...], v_ref[...]

        def products(c):
            rows = slice(c * tc, (c + 1) * tc)
            return _dot_nt(k_t, q_ref[rows, :]), _dot_nt(v_t, do_ref[rows, :])

        dk, dv = dk_s[...], dv_s[...]
        nxt = products(0)
        contribs = []
        for c in range(sub):
            rows = slice(c * tc, (c + 1) * tc)
            s_t, dp_t = nxt
            if c + 1 < sub:
                nxt = products(c + 1)
            p_t = jnp.exp(s_t - lse_ref[:, rows])
            ds_t = (p_t * (dp_t - dl_ref[:, rows])).astype(BF16)
            dv = dv + _dot(p_t.astype(BF16), do_ref[rows, :])
            dk = dk + _dot(ds_t, q_ref[rows, :])
            contribs.append(_dot(kt_s[...], ds_t))
        dk_s[...], dv_s[...] = dk, dv
        contrib = jnp.concatenate(contribs, axis=1) if sub > 1 else contribs[0]
        cols = pl.ds(pl.multiple_of(qb * tq, tq), tq)

        @pl.when(kb == 0)
        def _():
            dqt_ref[:, cols] = contrib

        @pl.when(kb > 0)
        def _():
            dqt_ref[:, cols] += contrib

        @pl.when(qb == nq - 1)
        def _():
            dk_ref[...] = dk_s[...]
            dv_ref[...] = dv_s[...]

    qspec = pl.BlockSpec((tq, HEAD_PAD), lambda h, kb, qb: (qb, h))
    kspec = pl.BlockSpec((tk, HEAD_PAD), lambda h, kb, qb: (kb, h))
    rowspec = pl.BlockSpec((None, 1, tq), lambda h, kb, qb: (h, 0, qb))
    return pl.pallas_call(
        body, name="flash_bwd",
        grid=(HEADS, s // tk, nq),
        in_specs=[qspec, kspec, kspec, qspec, rowspec, rowspec],
        out_specs=[pl.BlockSpec((HEAD_PAD, s), lambda h, kb, qb: (h, 0)), kspec, kspec],
        out_shape=[jax.ShapeDtypeStruct((ATT_W, s), F32), jax.ShapeDtypeStruct((s, ATT_W), F32),
                   jax.ShapeDtypeStruct((s, ATT_W), F32)],
        scratch_shapes=[pltpu.VMEM((HEAD_PAD, tk), BF16), pltpu.VMEM((tk, HEAD_PAD), F32),
                        pltpu.VMEM((tk, HEAD_PAD), F32)],
        compiler_params=_params("parallel", "arbitrary", "arbitrary"),
    )(q, k, v, do, lse_row, delta_row)


def _halo_specs(tm, w, n_tiles):
    per = tm // POOL_HALO
    last = n_tiles * per - 1
    return [pl.BlockSpec((POOL_HALO, w), lambda i: (jnp.maximum(i * per - 1, 0), 0)),
            pl.BlockSpec((tm, w), lambda i: (i, 0)),
            pl.BlockSpec((POOL_HALO, w), lambda i: (jnp.minimum((i + 1) * per, last), 0))]


def _fill_ext(ext_ref, prev, cur, nxt, i, n_tiles, tm):
    ext_ref[pl.ds(0, POOL_HALO), :] = jnp.where(i > 0, prev, 0.0)
    ext_ref[pl.ds(POOL_HALO, tm), :] = cur
    ext_ref[pl.ds(POOL_HALO + tm, POOL_HALO), :] = jnp.where(i < n_tiles - 1, nxt, 0.0)


def _window_sum(ext_ref, g, lo, hi, tm):
    lanes = pl.ds(g * GROUP, GROUP)
    acc = ext_ref[pl.ds(POOL_HALO + lo, tm), lanes]
    for d in range(lo + 1, hi + 1):
        acc = acc + ext_ref[pl.ds(POOL_HALO + d, tm), lanes]
    return acc


def _pool_mixed(ext_ref, cur, i, s, tm):
    row = i * tm + lax.broadcasted_iota(jnp.int32, (tm, 1), 0)
    out = []
    for g, w in enumerate(POOL_WINDOWS):
        left = w // 2
        right = w - 1 - left
        cnt = (jnp.minimum(row + right + 1, s) - jnp.maximum(row - left, 0)).astype(F32)
        out.append(_window_sum(ext_ref, g, -left, right, tm) / cnt - cur[:, g * GROUP:(g + 1) * GROUP])
    return out


def pool_out_fwd(x, z, o, wpool, pscale, wout, tm=512):
    s = x.shape[0]
    tm = _row_tile(s, tm)
    n_tiles = s // tm

    def body(x_ref, zp_ref, z_ref, zn_ref, o_ref, wp_ref, ps_ref, wo_ref, y_ref, cat_ref, ext_s):
        i = pl.program_id(0)
        cur = z_ref[:, Z_POOL:Z_KPE]
        _fill_ext(ext_s, zp_ref[:, Z_POOL:Z_KPE], cur, zn_ref[:, Z_POOL:Z_KPE], i, n_tiles, tm)
        mixed = _pool_mixed(ext_s, cur, i, s, tm)
        cat_ref[:, :ATT_W] = o_ref[...]
        for g in range(len(POOL_WINDOWS)):
            lanes = slice(g * GROUP, (g + 1) * GROUP)
            yg = _dot(mixed[g].astype(BF16), wp_ref[g])
            cat_ref[:, ATT_W + g * GROUP:ATT_W + (g + 1) * GROUP] = (yg * ps_ref[:, lanes]).astype(BF16)
        y_ref[...] = x_ref[...] + _dot(cat_ref[...], wo_ref[...])

    full = lambda shape: pl.BlockSpec(shape, lambda i: (0,) * len(shape))
    row = lambda w: pl.BlockSpec((tm, w), lambda i: (i, 0))
    return pl.pallas_call(
        body, name="pool_out_fwd",
        grid=(n_tiles,),
        in_specs=[row(D_MODEL)] + _halo_specs(tm, Z_W, n_tiles) + [
            row(ATT_W), full((len(POOL_WINDOWS), GROUP, GROUP)), full((1, POOL_W)), full((CAT_W, D_MODEL))],
        out_specs=[row(D_MODEL), row(CAT_W)],
        out_shape=[jax.ShapeDtypeStruct((s, D_MODEL), F32), jax.ShapeDtypeStruct((s, CAT_W), BF16)],
        scratch_shapes=[pltpu.VMEM((tm + 2 * POOL_HALO, POOL_W), F32)],
        compiler_params=_params("parallel"),
    )(x, z, z, z, o, wpool, pscale, wout)


def out_bwd(dx, wout, cat, tm=512):
    s = dx.shape[0]
    tm = _row_tile(s, tm)

    def body(dx_ref, wo_ref, cat_ref, do_ref, db_ref, dl_ref):
        dcat = _dot_nt(dx_ref[...].astype(BF16), wo_ref[...])
        do_ref[...] = dcat[:, :ATT_W].astype(BF16)
        db_ref[...] = dcat[:, ATT_W:]
        for hd in range(HEADS):
            lanes = slice(hd * HEAD_PAD, (hd + 1) * HEAD_PAD)
            dl_ref[hd] = jnp.sum(dcat[:, lanes] * cat_ref[:, lanes].astype(F32), axis=-1, keepdims=True)

    row = lambda w: pl.BlockSpec((tm, w), lambda i: (i, 0))
    return pl.pallas_call(
        body, name="out_bwd",
        grid=(s // tm,),
        in_specs=[row(D_MODEL), pl.BlockSpec((CAT_W, D_MODEL), lambda i: (0, 0)), row(CAT_W)],
        out_specs=[row(ATT_W), row(POOL_W), pl.BlockSpec((HEADS, tm, 1), lambda i: (0, i, 0))],
        out_shape=[jax.ShapeDtypeStruct((s, ATT_W), BF16), jax.ShapeDtypeStruct((s, POOL_W), F32),
                   jax.ShapeDtypeStruct((HEADS, s, 1), F32)],
        compiler_params=_params("parallel"),
    )(dx, wout, cat)


def pool_bwd(db, z, wpool, pscale, tm=512):
    s = db.shape[0]
    tm = _row_tile(s, tm)
    n_tiles = s // tm
    te = tm + 2 * POOL_HALO

    def body(dbp_ref, db_ref, dbn_ref, zp_ref, z_ref, zn_ref, wp_ref, ps_ref,
             dp_ref, mixed_ref, dys_ref, dps_ref, ext_s, text_s):
        i = pl.program_id(0)
        cur = z_ref[:, Z_POOL:Z_KPE]
        _fill_ext(ext_s, zp_ref[:, Z_POOL:Z_KPE], cur, zn_ref[:, Z_POOL:Z_KPE], i, n_tiles, tm)
        mixed = _pool_mixed(ext_s, cur, i, s, tm)
        _fill_ext(ext_s, dbp_ref[...], db_ref[...], dbn_ref[...], i, n_tiles, tm)
        erow = i * tm - POOL_HALO + lax.broadcasted_iota(jnp.int32, (te, 1), 0)
        dps = []
        for g, w in enumerate(POOL_WINDOWS):
            lanes = slice(g * GROUP, (g + 1) * GROUP)
            left = w // 2
            right = w - 1 - left
            mb = mixed[g].astype(BF16)
            mixed_ref[:, lanes] = mb
            dps.append(jnp.sum(db_ref[:, lanes] * _dot(mb, wp_ref[g]), axis=0, keepdims=True))
            dys_e = (ext_s[:, lanes] * ps_ref[:, lanes]).astype(BF16)
            dys_ref[:, lanes] = (db_ref[:, lanes] * ps_ref[:, lanes]).astype(BF16)
            dmix_e = _dot_nt(dys_e, wp_ref[g])
            cnt = jnp.minimum(erow + right + 1, s) - jnp.maximum(erow - left, 0)
            text_s[:, lanes] = dmix_e / jnp.maximum(cnt, 1).astype(F32)
            dp_ref[:, lanes] = _window_sum(text_s, g, -right, left, tm) - dmix_e[POOL_HALO:POOL_HALO + tm]
        part = jnp.concatenate(dps, axis=1)

        @pl.when(i == 0)
        def _():
            dps_ref[...] = part

        @pl.when(i > 0)
        def _():
            dps_ref[...] += part

    full = lambda shape: pl.BlockSpec(shape, lambda i: (0,) * len(shape))
    row = lambda w: pl.BlockSpec((tm, w), lambda i: (i, 0))
    return pl.pallas_call(
        body, name="pool_bwd",
        grid=(n_tiles,),
        in_specs=_halo_specs(tm, POOL_W, n_tiles) + _halo_specs(tm, Z_W, n_tiles) + [
            full((len(POOL_WINDOWS), GROUP, GROUP)), full((1, POOL_W))],
        out_specs=[row(POOL_W), row(POOL_W), row(POOL_W), full((1, POOL_W))],
        out_shape=[jax.ShapeDtypeStruct((s, POOL_W), F32), jax.ShapeDtypeStruct((s, POOL_W), BF16),
                   jax.ShapeDtypeStruct((s, POOL_W), BF16), jax.ShapeDtypeStruct((1, POOL_W), F32)],
        scratch_shapes=[pltpu.VMEM((te, POOL_W), F32), pltpu.VMEM((te, POOL_W), F32)],
        compiler_params=_params("arbitrary"),
    )(db, db, db, z, z, z, wpool, pscale)


def loss_head(y, target, tm=512):
    s = y.shape[0]
    tm = _row_tile(s, tm)

    def body(y_ref, t_ref, dy_ref, loss_ref):
        i = pl.program_id(0)
        err = y_ref[...] - t_ref[...]
        dy_ref[...] = err * (1.0 / D_MODEL)
        part = 0.5 * jnp.sum(jnp.sum(err * err, axis=-1, keepdims=True) * (1.0 / D_MODEL), axis=0, keepdims=True)
        part = jnp.broadcast_to(part, (1, 128))

        @pl.when(i == 0)
        def _():
            loss_ref[...] = part

        @pl.when(i > 0)
        def _():
            loss_ref[...] += part

    row = pl.BlockSpec((tm, D_MODEL), lambda i: (i, 0))
    return pl.pallas_call(
        body, name="loss_head",
        grid=(s // tm,),
        in_specs=[row, row],
        out_specs=[row, pl.BlockSpec((1, 128), lambda i: (0, 0))],
        out_shape=[jax.ShapeDtypeStruct((s, D_MODEL), F32), jax.ShapeDtypeStruct((1, 128), F32)],
        compiler_params=_params("arbitrary"),
    )(y, target)


def _pad_heads(w, real):
    lead = w.shape[:-1]
    w = w.reshape(lead + (HEADS, real))
    w = jnp.pad(w, [(0, 0)] * len(lead) + [(0, 0), (0, HEAD_PAD - real)])
    return w.reshape(lead + (ATT_W,))


def _unpad_heads(w, real):
    lead = w.shape[:-1]
    return w.reshape(lead + (HEADS, HEAD_PAD))[..., :real].reshape(lead + (HEADS * real,))


def layout_weights(full):
    w = {}
    depth = full["w_in"].shape[0]
    win = full["w_in"]
    zc = lambda n: jnp.zeros((depth, D_MODEL, n), win.dtype)
    o_pe = Q_LORA + KV_LORA
    w["win"] = jnp.concatenate([win[..., :o_pe], win[..., o_pe + QK_ROPE:], zc(QK_NOPE), win[..., o_pe:o_pe + QK_ROPE],
                                zc(HEAD_PAD - QK_HEAD)], axis=-1)
    w["wuq"] = _pad_heads(full["w_uq"], QK_HEAD)
    w["wuk"] = _pad_heads(full["w_uk"], QK_NOPE)
    w["wuv"] = _pad_heads(full["w_uv"], V_HEAD)
    wo = full["w_out"]
    wo_att = jnp.swapaxes(_pad_heads(jnp.swapaxes(wo[:, :HEADS * V_HEAD], 1, 2), V_HEAD), 1, 2)
    w["wout"] = jnp.concatenate([wo_att, wo[:, HEADS * V_HEAD:]], axis=1)
    pad_vec = lambda v: jnp.pad(v, ((0, 0), (0, HEAD_PAD - QK_HEAD)))
    w["qn"], w["kn"] = pad_vec(full["q_norm"]), pad_vec(full["k_norm"])
    for n in ("ffn1_norm", "mix_norm", "ffn2_norm", "q_lat_norm", "kv_lat_norm", "pool_scale", "w_pool"):
        w[n] = full[n]
    return w


def local_step(x, target, w, wgu1, wd1, wgu2, wd2):
    s = x.shape[0]
    depth = w["win"].shape[0]
    rc, rs1, rs2 = rope_tables(s)
    vec = lambda name, l: w[name][l][None, :]
    saved = []
    for l in range(depth):
        x1, g1, u1 = ffn_fwd(x, vec("ffn1_norm", l), wgu1, wd1, l)
        z, q, k, v, vt = mix_in_fwd(x1, vec("mix_norm", l), w["win"][l], vec("q_lat_norm", l), vec("kv_lat_norm", l),
                                    w["wuq"][l], w["wuk"][l], w["wuv"][l], vec("qn", l), vec("kn", l), rc, rs1, rs2)
        o, lse = flash_fwd(q, k, vt)
        x2, cat = pool_out_fwd(x1, z, o, w["w_pool"][l], vec("pool_scale", l), w["wout"][l])
        x3, g2, u2 = ffn_fwd(x2, vec("ffn2_norm", l), wgu2, wd2, l)
        saved.append((x, x1, z, q, k, v, lse, cat, x2, g1, u1, g2, u2))
        x = x3
    dy, loss = loss_head(x, target)

    names = ["ffn1_norm", "ffn1_w_gu", "ffn1_w_down", "mix_norm", "w_in", "q_lat_norm", "kv_lat_norm", "w_uq", "w_uk",
             "w_uv", "q_norm", "k_norm", "w_pool", "pool_scale", "w_out", "ffn2_norm", "ffn2_w_gu", "ffn2_w_down"]
    grads = {n: [None] * depth for n in names if "ffn" not in n or "norm" in n}
    for n in ("ffn1", "ffn2"):
        grads[n + "_w_gu"] = lax.empty((depth, N_CHIPS, D_MODEL, FF_BLOCK), F32)
        grads[n + "_w_down"] = lax.empty((depth, D_FF, D_MODEL), F32)

    def ffn_weight_grads(n, l, h, act, dg, du, dyh):
        gu = matmul_tn(h, dg, tk=2048, into=grads[n + "_w_gu"], at=lambda j: (l, j))
        grads[n + "_w_gu"] = matmul_tn(h, du, tk=2048, into=gu, at=lambda j: (l, j + 2))
        grads[n + "_w_down"] = matmul_tn(act, dyh, into=grads[n + "_w_down"], at=lambda j: (l,))

    for l in reversed(range(depth)):
        x0, x1, z, q, k, v, lse, cat, x2, g1, u1, g2, u2 = saved[l]
        dx2, h, act, dg, du, dyh, dgn = ffn_bwd(x2, dy, vec("ffn2_norm", l), g2, u2, wgu2, wd2, l)
        grads["ffn2_norm"][l] = dgn[0]
        ffn_weight_grads("ffn2", l, h, act, dg, du, dyh)

        do, db, delta = out_bwd(dx2, w["wout"][l], cat)
        dwout = matmul_tn(cat, dx2)
        grads["w_out"][l] = jnp.concatenate(
            [jnp.swapaxes(_unpad_heads(jnp.swapaxes(dwout[:ATT_W], 0, 1), V_HEAD), 0, 1), dwout[ATT_W:]], axis=0)
        dp, mixed, dys, dps = pool_bwd(db, z, w["w_pool"][l], vec("pool_scale", l))
        grads["pool_scale"][l] = dps[0]
        dwp = matmul_tn(mixed, dys)
        grads["w_pool"][l] = jnp.stack([dwp[g * GROUP:(g + 1) * GROUP, g * GROUP:(g + 1) * GROUP]
                                        for g in range(len(POOL_WINDOWS))])
        dq_t, dk, dv = flash_bwd(q, k, v, do, lse, delta.reshape(HEADS, 1, s))
        dz, cqn, ckvn, dqr, dkr, dvb, dqln, dkvln, dqn, dkn = mla_bwd(
            z, dq_t, dk, dv, dp, vec("q_lat_norm", l), vec("kv_lat_norm", l), w["wuq"][l], w["wuk"][l], w["wuv"][l],
            vec("qn", l), vec("kn", l), rc, rs1, rs2)
        grads["q_lat_norm"][l], grads["kv_lat_norm"][l] = dqln[0], dkvln[0]
        grads["q_norm"][l], grads["k_norm"][l] = dqn[0, :QK_HEAD], dkn[0, :QK_HEAD]
        grads["w_uq"][l] = _unpad_heads(matmul_tn(cqn, dqr), QK_HEAD)
        grads["w_uk"][l] = _unpad_heads(matmul_tn(ckvn, dkr), QK_NOPE)
        grads["w_uv"][l] = _unpad_heads(matmul_tn(ckvn, dvb), V_HEAD)
        dx1, h2, dgm = mix_in_bwd(x1, dx2, dz, vec("mix_norm", l), w["win"][l])
        grads["mix_norm"][l] = dgm[0]
        dwin = matmul_tn(h2, dz)
        grads["w_in"][l] = jnp.concatenate(
            [dwin[:, :Z_POOL], dwin[:, Z_KPE + QK_NOPE:Z_KPE + QK_HEAD], dwin[:, Z_POOL:Z_KPE]], axis=1)

        dy, h, act, dg, du, dyh, dgn = ffn_bwd(x0, dx1, vec("ffn1_norm", l), g1, u1, wgu1, wd1, l)
        grads["ffn1_norm"][l] = dgn[0]
        ffn_weight_grads("ffn1", l, h, act, dg, du, dyh)
    return loss, dy, {n: (jnp.stack(g) if isinstance(g, list) else g) for n, g in grads.items()}


HBM_SPEC = pl.BlockSpec(memory_space=pltpu.HBM)


def _place():
    x, y, c = lax.axis_index("x"), lax.axis_index("y"), lax.axis_index("c")
    other_chips = [(1 - x, y), (x, 1 - y), (1 - x, 1 - y)]
    return x, y, c, other_chips


def _exchange(name, arrays, out_shapes, plan, n_local, n_remote):
    n = len(arrays)

    def body(*refs):
        ins, outs = refs[:n], refs[n:n + len(out_shapes)]
        send_sems, recv_sems, local_sems = refs[n + len(out_shapes):]
        local, sends, recvs, forwards = plan(ins, outs, _place())

        def remote(k, src, dst, dev):
            return pltpu.make_async_remote_copy(src_ref=src, dst_ref=dst, send_sem=send_sems.at[k],
                                                recv_sem=recv_sems.at[k], device_id=dev, device_id_type=MESH)

        started = []
        for k, (src, dst) in enumerate(local):
            cp = pltpu.make_async_copy(src, dst, local_sems.at[k])
            cp.start()
            started.append(cp)
        out_going = []
        for k, (src, dst, dev) in enumerate(sends):
            cp = remote(k, src, dst, dev)
            cp.start()
            out_going.append(cp)
        for k, (dst, dev) in enumerate(recvs):
            remote(k, dst, dst, dev).wait_recv()
            if forwards:
                src, fdst, fdev, _ = forwards[k]
                cp = remote(len(sends) + k, src, fdst, fdev)
                cp.start()
                out_going.append(cp)
        for k, (_, _, fdev, landing) in enumerate(forwards):
            remote(len(sends) + k, landing, landing, fdev).wait_recv()
        for cp in out_going:
            cp.wait_send()
        for cp in started:
            cp.wait()

    return pl.pallas_call(
        body, name=name,
        in_specs=[HBM_SPEC] * n,
        out_specs=[HBM_SPEC] * len(out_shapes),
        out_shape=out_shapes,
        scratch_shapes=[pltpu.SemaphoreType.DMA((n_remote,)), pltpu.SemaphoreType.DMA((n_remote,)),
                        pltpu.SemaphoreType.DMA((max(n_local, 1),))],
    )(*arrays)


def _layer_half(depth, which):
    assert depth % 2 == 0
    return pl.ds(which * (depth // 2), depth // 2)


def gather_chips(shards):
    n = len(shards)
    depth = shards[0].shape[0]

    def plan(ins, outs, place):
        x, y, c, chips = place
        me, sib = 2 * x + y, (x, y, 1 - c)
        mine, theirs = _layer_half(depth, c), _layer_half(depth, 1 - c)
        sends = [(ins[p].at[mine], outs[p].at[me, mine], (px, py, c)) for p in range(n) for (px, py) in chips]
        recvs = [(outs[p].at[2 * px + py, mine], (px, py, c)) for p in range(n) for (px, py) in chips]
        forwards = [(outs[p].at[2 * px + py, mine], outs[p].at[2 * px + py, mine], sib, outs[p].at[2 * px + py, theirs])
                    for p in range(n) for (px, py) in chips]
        return [], sends, recvs, forwards

    out_shapes = [jax.ShapeDtypeStruct((N_CHIPS,) + a.shape, a.dtype) for a in shards]
    return _exchange("gather_chips", shards, out_shapes, plan, 0, 6 * n)


def swap_halves(fulls):
    n = len(fulls)
    depth = fulls[0].shape[0]

    def plan(ins, outs, place):
        x, y, c, _ = place
        sib = (x, y, 1 - c)
        sends = [(ins[p].at[_layer_half(depth, 1 - c)], outs[p], sib) for p in range(n)]
        return [], sends, [(outs[p], sib) for p in range(n)], []

    out_shapes = [jax.ShapeDtypeStruct((depth // 2,) + a.shape[1:], a.dtype) for a in fulls]
    return _exchange("swap_halves", fulls, out_shapes, plan, 0, n)


def scatter_chips(fulls):
    n = len(fulls)

    def plan(ins, outs, place):
        x, y, c, chips = place
        sends = [(ins[p].at[:, 2 * px + py], outs[p].at[j], (px, py, c))
                 for p in range(n) for j, (px, py) in enumerate(chips)]
        recvs = [(outs[p].at[j], (px, py, c)) for p in range(n) for j, (px, py) in enumerate(chips)]
        return [], sends, recvs, []

    out_shapes = [jax.ShapeDtypeStruct((3, a.shape[0]) + a.shape[2:], a.dtype) for a in fulls]
    return _exchange("scatter_chips", fulls, out_shapes, plan, 0, 3 * n)


def swap_sibling(arrays):
    n = len(arrays)

    def plan(ins, outs, place):
        x, y, c, _ = place
        sib = (x, y, 1 - c)
        return [], [(ins[p], outs[p], sib) for p in range(n)], [(outs[p], sib) for p in range(n)], []

    return _exchange("swap_sibling", arrays, [jax.ShapeDtypeStruct(a.shape, a.dtype) for a in arrays], plan, 0, n)


def gather_all(vec):
    def plan(ins, outs, place):
        x, y, c, _ = place
        me = 4 * x + 2 * y + c
        flips = [(fx, fy, fc) for fx in (0, 1) for fy in (0, 1) for fc in (0, 1)][1:]
        peers = [(1 - x if fx else x, 1 - y if fy else y, 1 - c if fc else c) for fx, fy, fc in flips]
        local = [(ins[0], outs[0].at[me])]
        sends = [(ins[0], outs[0].at[me], dev) for dev in peers]
        recvs = [(outs[0].at[4 * dev[0] + 2 * dev[1] + dev[2]], dev) for dev in peers]
        return local, sends, recvs, []

    return _exchange("gather_all", [vec], [jax.ShapeDtypeStruct((8,) + vec.shape, vec.dtype)], plan, 1, 7)[0]


def _pick_rows(rows, cols, block_bytes=512 * 1024):
    best = None
    for t in range(16, rows + 1, 16):
        if rows % t == 0 and t * cols * 4 <= block_bytes:
            best = t
    return best or rows


def sum_rows_of(own, stacked, with_bf16=False):
    rows, cols = own.shape
    n = stacked.shape[0]
    tr = _pick_rows(rows, cols)

    def body(own_ref, st_ref, o_ref, *narrow):
        acc = own_ref[...]
        for k in range(n):
            acc = acc + st_ref[k].astype(F32)
        o_ref[...] = acc
        if with_bf16:
            narrow[0][...] = acc.astype(BF16)

    spec = pl.BlockSpec((tr, cols), lambda i: (i, 0))
    return pl.pallas_call(
        body, name="sum_rows_of",
        grid=(rows // tr,),
        in_specs=[spec, pl.BlockSpec((n, tr, cols), lambda i: (0, i, 0))],
        out_specs=[spec, spec] if with_bf16 else spec,
        out_shape=([jax.ShapeDtypeStruct((rows, cols), F32), jax.ShapeDtypeStruct((rows, cols), BF16)]
                   if with_bf16 else jax.ShapeDtypeStruct((rows, cols), F32)),
        compiler_params=_params("parallel"),
    )(own, stacked)


def sum_stack(stacked):
    n, rows, cols = stacked.shape
    tr = _pick_rows(rows, cols)

    def body(st_ref, o_ref):
        acc = st_ref[0]
        for k in range(1, n):
            acc = acc + st_ref[k]
        o_ref[...] = acc

    return pl.pallas_call(
        body, name="sum_stack",
        grid=(rows // tr,),
        in_specs=[pl.BlockSpec((n, tr, cols), lambda i: (0, i, 0))],
        out_specs=pl.BlockSpec((tr, cols), lambda i: (i, 0)),
        out_shape=jax.ShapeDtypeStruct((rows, cols), F32),
        compiler_params=_params("parallel"),
    )(stacked)


def adamw(w, ga, gb, m, v):
    rows, cols = w.shape
    tr = _pick_rows(rows, cols, 256 * 1024)
    c1 = 1.0 - ADAM_B1 ** ADAM_STEP
    c2 = 1.0 - ADAM_B2 ** ADAM_STEP
    ins = [w, ga] + ([gb] if gb is not None else []) + [m, v]

    def body(*refs):
        w_ref, ga_ref = refs[0], refs[1]
        m_ref, v_ref, g_out, d_out, m_out, v_out = refs[-6:]
        g = ga_ref[...]
        if gb is not None:
            g = g + refs[2][...]
        mn = ADAM_B1 * m_ref[...] + (1.0 - ADAM_B1) * g
        vn = ADAM_B2 * v_ref[...] + (1.0 - ADAM_B2) * (g * g)
        m_hat = mn / c1
        v_hat = vn / c2
        g_out[...] = g
        d_out[...] = -ADAM_LR * (m_hat / (jnp.sqrt(v_hat) + ADAM_EPS) + ADAM_WD * w_ref[...])
        m_out[...] = mn
        v_out[...] = vn

    spec = pl.BlockSpec((tr, cols), lambda i: (i, 0))
    return pl.pallas_call(
        body, name="adamw",
        grid=(rows // tr,),
        in_specs=[spec] * len(ins),
        out_specs=[spec] * 4,
        out_shape=[jax.ShapeDtypeStruct((rows, cols), F32)] * 4,
        compiler_params=_params("parallel"),
    )(*ins)


WEIGHT_NAMES = ["ffn1_norm", "ffn1_w_gu", "ffn1_w_down", "mix_norm", "w_in", "q_lat_norm", "kv_lat_norm", "w_uq", "w_uk",
                "w_uv", "q_norm", "k_norm", "w_pool", "pool_scale", "w_out", "ffn2_norm", "ffn2_w_gu", "ffn2_w_down"]
COL_SHARDED = ["ffn1_w_gu", "w_in", "w_uq", "w_uk", "w_uv", "ffn2_w_gu"]
ROW_SHARDED = ["ffn1_w_down", "w_out", "ffn2_w_down"]
SHARDED = [n for n in WEIGHT_NAMES if n in COL_SHARDED or n in ROW_SHARDED]
REPLICATED = [n for n in WEIGHT_NAMES if n not in SHARDED]


def _join_chips(g, name):
    _, depth, r, c = g.shape
    if name in COL_SHARDED:
        return jnp.transpose(g, (1, 2, 0, 3)).reshape(depth, r, N_CHIPS * c)
    return jnp.transpose(g, (1, 0, 2, 3)).reshape(depth, N_CHIPS * r, c)


def _split_chips(full, name):
    depth, r, c = full.shape
    if name in COL_SHARDED:
        return jnp.transpose(full.reshape(depth, r, N_CHIPS, c // N_CHIPS), (0, 2, 1, 3))
    return full.reshape(depth, N_CHIPS, r // N_CHIPS, c)


def _pack(vs):
    flat = jnp.concatenate([v.reshape(-1) for v in vs])
    pad = (-flat.shape[0]) % (8 * 128)
    return jnp.pad(flat, (0, pad)).reshape(-1, 128)


def _unpack(packed, like):
    flat = packed.reshape(-1)
    out, at = [], 0
    for v in like:
        out.append(flat[at:at + v.size].reshape(v.shape))
        at += v.size
    return out


def kernel(x, ffn1_norm, ffn1_w_gu, ffn1_w_down, mix_norm, w_in, q_lat_norm, kv_lat_norm, w_uq, w_uk, w_uv, q_norm, k_norm, w_pool, pool_scale, w_out, ffn2_norm, ffn2_w_gu, ffn2_w_down, loss_target, m_ffn1_norm, m_ffn1_w_gu, m_ffn1_w_down, m_mix_norm, m_w_in, m_q_lat_norm, m_kv_lat_norm, m_w_uq, m_w_uk, m_w_uv, m_q_norm, m_k_norm, m_w_pool, m_pool_scale, m_w_out, m_ffn2_norm, m_ffn2_w_gu, m_ffn2_w_down, v_ffn1_norm, v_ffn1_w_gu, v_ffn1_w_down, v_mix_norm, v_w_in, v_q_lat_norm, v_kv_lat_norm, v_w_uq, v_w_uk, v_w_uv, v_q_norm, v_k_norm, v_w_pool, v_pool_scale, v_w_out, v_ffn2_norm, v_ffn2_w_gu, v_ffn2_w_down):
    args = locals()
    weights = {n: args[n] for n in WEIGHT_NAMES}
    moments_m = {n: args["m_" + n] for n in WEIGHT_NAMES}
    moments_v = {n: args["v_" + n] for n in WEIGHT_NAMES}

    chip = 2 * lax.axis_index("x") + lax.axis_index("y")
    core = lax.axis_index("c")
    shards = [weights[n].astype(BF16) for n in SHARDED]
    gathered = {n: lax.dynamic_update_index_in_dim(g, own, chip, 0)
                for n, own, g in zip(SHARDED, shards, gather_chips(shards))}
    full = {n: _join_chips(gathered[n], n) for n in SHARDED if "ffn" not in n}
    for n in REPLICATED:
        full[n] = weights[n].astype(BF16) if n == "w_pool" else weights[n]
    w = layout_weights(full)
    wd1 = _join_chips(gathered["ffn1_w_down"], "ffn1_w_down")
    wd2 = _join_chips(gathered["ffn2_w_down"], "ffn2_w_down")

    loss_part, grad_x, grads = local_step(x[0], loss_target[0], w, gathered["ffn1_w_gu"], wd1, gathered["ffn2_w_gu"], wd2)
    loss = lax.psum(loss_part[0, 0], ("x", "y", "c"))

    split = [grads[n] if n.endswith("w_gu") else _split_chips(grads[n], n) for n in SHARDED]
    half = split[0].shape[0] // 2

    def add(own, stacked, with_bf16=False):
        rows = own.size // own.shape[-1]
        total = sum_rows_of(own.reshape(rows, -1), stacked.reshape(stacked.shape[0], rows, -1), with_bf16)
        return [t.reshape(own.shape) for t in total] if with_bf16 else total.reshape(own.shape)

    from_sibling = swap_halves(split)
    chip_sums = [add(lax.dynamic_slice_in_dim(g4, core * half, half, axis=0), other[None], with_bf16=True)
                 for g4, other in zip(split, from_sibling)]
    landed = scatter_chips([narrow for _, narrow in chip_sums])
    totals = [add(lax.dynamic_index_in_dim(wide, chip, 1, keepdims=False), land)
              for (wide, _), land in zip(chip_sums, landed)]
    out = {}
    for n, mine, theirs in zip(SHARDED, totals, swap_sibling(totals)):
        g = jnp.concatenate([jnp.where(core == 0, mine, theirs), jnp.where(core == 0, theirs, mine)], axis=0)
        shape = weights[n].shape
        flat = lambda a: a.reshape(shape[0] * shape[1], shape[2])
        res = adamw(flat(weights[n]), flat(g), None, flat(moments_m[n]), flat(moments_v[n]))
        out[n] = [r.reshape(shape) for r in res]

    rep_g = [grads[n] for n in REPLICATED]
    total = sum_stack(gather_all(_pack(rep_g)))
    res = adamw(_pack([weights[n] for n in REPLICATED]), total, None,
                _pack([moments_m[n] for n in REPLICATED]), _pack([moments_v[n] for n in REPLICATED]))
    unpacked = [_unpack(r, rep_g) for r in res]
    for k, n in enumerate(REPLICATED):
        out[n] = [u[k] for u in unpacked]

    return (loss, grad_x[None], *[out[n][0] for n in WEIGHT_NAMES], *[out[n][1] for n in WEIGHT_NAMES],
            *[out[n][2] for n in WEIGHT_NAMES], *[out[n][3] for n in WEIGHT_NAMES])
```

```python
import functools

import jax
import jax.numpy as jnp
from jax import lax
from jax.experimental import pallas as pl
from jax.experimental.pallas import tpu as pltpu

F32 = jnp.float32
BF16 = jnp.bfloat16
MESH = pl.DeviceIdType.MESH

DEPTH = 4
D_MODEL = 1024
D_FF = 2816
FF_BLOCK = D_FF // 2
N_CHIPS = 4
HEADS = 8
HEAD_PAD = 128
QK_NOPE = 64
QK_ROPE = 32
QK_HEAD = QK_NOPE + QK_ROPE
V_HEAD = 64
Q_LORA = 384
KV_LORA = 256
POOL_W = 512
POOL_WINDOWS = (2, 4, 8, 16)
POOL_HALO = 8
GROUP = 128
D_IN = Q_LORA + KV_LORA + QK_ROPE + POOL_W
Z_W = 1280
Z_CKV = Q_LORA
Z_POOL = Q_LORA + KV_LORA
Z_KPE = Z_POOL + POOL_W
ATT_W = HEADS * HEAD_PAD
CAT_W = ATT_W + POOL_W
EPS = 1e-6
ROPE_THETA = 10000.0
ADAM_LR, ADAM_B1, ADAM_B2, ADAM_EPS, ADAM_WD, ADAM_STEP = 0.001, 0.9, 0.999, 1e-08, 0.01, 10
LOG2_E = 1.4426950408889634
LN_2 = 0.6931471805599453

VMEM_LIMIT = 56 * 1024 * 1024
VMEM_LIMIT_FFN_BWD = 62 * 1024 * 1024


def _params(*sem, vmem=VMEM_LIMIT):
    return pltpu.CompilerParams(dimension_semantics=sem, vmem_limit_bytes=vmem)


def _dot(a, b):
    return jnp.dot(a, b, preferred_element_type=F32)


def _dot_nt(a, b):
    return lax.dot_general(a, b, (((1,), (1,)), ((), ())), preferred_element_type=F32)


def _dot_tn(a, b):
    return lax.dot_general(a, b, (((0,), (0,)), ((), ())), preferred_element_type=F32)


def _norm_fwd(x, n):
    rstd = lax.rsqrt(jnp.sum(x * x, axis=-1, keepdims=True) * (1.0 / n) + EPS)
    return x * rstd, rstd


def _norm_bwd(dy, g, xhat, rstd, n):
    dxhat = dy * g
    dx = rstd * (dxhat - xhat * (jnp.sum(dxhat * xhat, axis=-1, keepdims=True) * (1.0 / n)))
    return dx, dy * xhat


def _row_tile(s, want):
    t = min(s, want)
    assert s % t == 0
    return t


def ffn_fwd(x, gn, wgu4, wd, layer, tm=512):
    s = x.shape[0]
    tm = _row_tile(s, tm)

    def body(x_ref, gn_ref, wg_ref, wu_ref, wd_ref, o_ref, g_ref, u_ref, h_s, acc_s):
        j = pl.program_id(1)

        @pl.when(j == 0)
        def _():
            xhat, _ = _norm_fwd(x_ref[...], D_MODEL)
            h_s[...] = (xhat * gn_ref[...]).astype(BF16)
            acc_s[...] = jnp.zeros_like(acc_s)

        h = h_s[...]
        g = _dot(h, wg_ref[...])
        u = _dot(h, wu_ref[...])
        g_ref[...] = g.astype(BF16)
        u_ref[...] = u.astype(BF16)
        a = (g * jax.nn.sigmoid(g) * u).astype(BF16)
        acc_s[...] += _dot(a, wd_ref[...])

        @pl.when(j == 1)
        def _():
            o_ref[...] = x_ref[...] + 0.5 * acc_s[...]

    return pl.pallas_call(
        body, name="ffn_fwd",
        grid=(s // tm, 2),
        in_specs=[
            pl.BlockSpec((tm, D_MODEL), lambda i, j: (i, 0)),
            pl.BlockSpec((1, D_MODEL), lambda i, j: (0, 0)),
            pl.BlockSpec((None, None, D_MODEL, FF_BLOCK), lambda i, j: (j, layer, 0, 0)),
            pl.BlockSpec((None, None, D_MODEL, FF_BLOCK), lambda i, j: (j + 2, layer, 0, 0)),
            pl.BlockSpec((None, FF_BLOCK, D_MODEL), lambda i, j: (layer, j, 0)),
        ],
        out_specs=[pl.BlockSpec((tm, D_MODEL), lambda i, j: (i, 0)),
                   pl.BlockSpec((tm, FF_BLOCK), lambda i, j: (i, j)), pl.BlockSpec((tm, FF_BLOCK), lambda i, j: (i, j))],
        out_shape=[jax.ShapeDtypeStruct((s, D_MODEL), F32), jax.ShapeDtypeStruct((s, D_FF), BF16),
                   jax.ShapeDtypeStruct((s, D_FF), BF16)],
        scratch_shapes=[pltpu.VMEM((tm, D_MODEL), BF16), pltpu.VMEM((tm, D_MODEL), F32)],
        compiler_params=_params("parallel", "arbitrary"),
    )(x, gn, wgu4, wgu4, wd)


def ffn_bwd(x, dy, gn, g_pre, u_pre, wgu4, wd, layer, tm=512):
    s = x.shape[0]
    tm = _row_tile(s, tm)
    halves = 2 if tm % 32 == 0 else 1

    def body(x_ref, dy_ref, gn_ref, g_ref, u_ref, wg_ref, wu_ref, wd_ref,
             dx_ref, h_ref, a_ref, dg_ref, du_ref, dyh_ref, dgn_ref, dh_s):
        i, j = pl.program_id(0), pl.program_id(1)

        @pl.when(j == 0)
        def _():
            xhat, _ = _norm_fwd(x_ref[...], D_MODEL)
            h_ref[...] = (xhat * gn_ref[...]).astype(BF16)
            dyh_ref[...] = (0.5 * dy_ref[...]).astype(BF16)
            dh_s[...] = jnp.zeros_like(dh_s)

        for r in range(halves):
            rows = slice(r * (tm // halves), (r + 1) * (tm // halves))
            g = g_ref[rows, :].astype(F32)
            u = u_ref[rows, :].astype(F32)
            sg = jax.nn.sigmoid(g)
            silu = g * sg
            a_ref[rows, :] = (silu * u).astype(BF16)
            da = _dot_nt(dyh_ref[rows, :], wd_ref[...])
            dg = (da * u * (sg * (1.0 + g * (1.0 - sg)))).astype(BF16)
            du = (da * silu).astype(BF16)
            dg_ref[rows, :] = dg
            du_ref[rows, :] = du
            dh_s[rows, :] += _dot_nt(dg, wg_ref[...]) + _dot_nt(du, wu_ref[...])

        @pl.when(j == 1)
        def _():
            xhat, rstd = _norm_fwd(x_ref[...], D_MODEL)
            dxn, dgrow = _norm_bwd(dh_s[...], gn_ref[...], xhat, rstd, D_MODEL)
            dx_ref[...] = dy_ref[...] + dxn
            part = jnp.sum(dgrow, axis=0, keepdims=True)

            @pl.when(i == 0)
            def _():
                dgn_ref[...] = part

            @pl.when(i > 0)
            def _():
                dgn_ref[...] += part

    row = lambda i, j: (i, 0)
    return pl.pallas_call(
        body, name="ffn_bwd",
        grid=(s // tm, 2),
        in_specs=[
            pl.BlockSpec((tm, D_MODEL), row),
            pl.BlockSpec((tm, D_MODEL), row),
            pl.BlockSpec((1, D_MODEL), lambda i, j: (0, 0)),
            pl.BlockSpec((tm, FF_BLOCK), lambda i, j: (i, j)),
            pl.BlockSpec((tm, FF_BLOCK), lambda i, j: (i, j)),
            pl.BlockSpec((None, None, D_MODEL, FF_BLOCK), lambda i, j: (j, layer, 0, 0)),
            pl.BlockSpec((None, None, D_MODEL, FF_BLOCK), lambda i, j: (j + 2, layer, 0, 0)),
            pl.BlockSpec((None, FF_BLOCK, D_MODEL), lambda i, j: (layer, j, 0)),
        ],
        out_specs=[
            pl.BlockSpec((tm, D_MODEL), row),
            pl.BlockSpec((tm, D_MODEL), row),
            pl.BlockSpec((tm, FF_BLOCK), lambda i, j: (i, j)),
            pl.BlockSpec((tm, FF_BLOCK), lambda i, j: (i, j)),
            pl.BlockSpec((tm, FF_BLOCK), lambda i, j: (i, j)),
            pl.BlockSpec((tm, D_MODEL), row),
            pl.BlockSpec((1, D_MODEL), lambda i, j: (0, 0)),
        ],
        out_shape=[
            jax.ShapeDtypeStruct((s, D_MODEL), F32),
            jax.ShapeDtypeStruct((s, D_MODEL), BF16),
            jax.ShapeDtypeStruct((s, D_FF), BF16),
            jax.ShapeDtypeStruct((s, D_FF), BF16),
            jax.ShapeDtypeStruct((s, D_FF), BF16),
            jax.ShapeDtypeStruct((s, D_MODEL), BF16),
            jax.ShapeDtypeStruct((1, D_MODEL), F32),
        ],
        scratch_shapes=[pltpu.VMEM((tm, D_MODEL), F32)],
        compiler_params=_params("arbitrary", "arbitrary", vmem=VMEM_LIMIT_FFN_BWD),
    )(x, dy, gn, g_pre, u_pre, wgu4, wgu4, wd)


def matmul_tn(a, b, tk=1024, tn_max=1536, into=None, at=None):
    s, m = a.shape
    n = b.shape[1]
    tk = _row_tile(s, tk)
    tn = n
    if n > tn_max:
        tn = n // 2
    assert n % tn == 0 and tn % 128 == 0
    nk = s // tk

    def body(a_ref, b_ref, *rest):
        o_ref, acc_s = rest[-2:]
        k = pl.program_id(1)

        @pl.when(k == 0)
        def _():
            acc_s[...] = jnp.zeros_like(acc_s)

        acc_s[...] += _dot_tn(a_ref[...].astype(BF16), b_ref[...].astype(BF16))

        @pl.when(k == nk - 1)
        def _():
            o_ref[...] = acc_s[...]

    in_specs = [pl.BlockSpec((tk, m), lambda j, k: (k, 0)), pl.BlockSpec((tk, tn), lambda j, k: (k, j))]
    if into is None:
        operands, aliases = (a, b), {}
        out_specs = pl.BlockSpec((m, tn), lambda j, k: (0, j))
        out_shape = jax.ShapeDtypeStruct((m, n), F32)
    else:
        assert into.shape[-2:] == (m, tn) and into.dtype == F32
        lead = into.ndim - 2
        operands, aliases = (a, b, into), {2: 0}
        in_specs.append(pl.BlockSpec(memory_space=pl.ANY))
        out_specs = pl.BlockSpec((None,) * lead + (m, tn), lambda j, k: tuple(at(j)) + (0, 0))
        out_shape = jax.ShapeDtypeStruct(into.shape, F32)
    return pl.pallas_call(
        body, name="matmul_tn",
        grid=(n // tn, nk),
        in_specs=in_specs,
        out_specs=out_specs,
        out_shape=out_shape,
        scratch_shapes=[pltpu.VMEM((m, tn), F32)],
        input_output_aliases=aliases,
        compiler_params=_params("parallel", "arbitrary"),
    )(*operands)


def _rope_fwd(y, c, s1, s2):
    return y * c + pltpu.roll(y, HEAD_PAD - 16, 1) * s1 + pltpu.roll(y, 16, 1) * s2


def _rope_bwd(dy, c, s1, s2):
    return dy * c + pltpu.roll(dy * s1, 16, 1) + pltpu.roll(dy * s2, HEAD_PAD - 16, 1)


def rope_tables(s):
    pos = jnp.arange(s, dtype=F32)
    inv = ROPE_THETA ** (-jnp.arange(0, QK_ROPE, 2, dtype=F32) / QK_ROPE)
    ang = pos[:, None] * inv[None, :]
    cos, sin = jnp.cos(ang), jnp.sin(ang)
    z16 = jnp.zeros((s, 16), F32)
    c = jnp.concatenate([jnp.ones((s, QK_NOPE), F32), cos, cos, z16, z16], axis=1)
    s1 = jnp.concatenate([jnp.zeros((s, QK_NOPE), F32), -sin, z16, z16, z16], axis=1)
    s2 = jnp.concatenate([jnp.zeros((s, QK_NOPE), F32), z16, sin, z16, z16], axis=1)
    return c, s1, s2


def mix_in_fwd(x, gm, win, qln, kvln, wuq, wuk, wuv, qn, kn, rc, rs1, rs2, tm=512):
    s = x.shape[0]
    tm = _row_tile(s, tm)
    scale = QK_HEAD ** -0.5 * LOG2_E

    def body(x_ref, gm_ref, win_ref, qln_ref, kvln_ref, wuq_ref, wuk_ref, wuv_ref, qn_ref, kn_ref,
             rc_ref, rs1_ref, rs2_ref, z_ref, q_ref, k_ref, v_ref, vt_ref):
        xhat, _ = _norm_fwd(x_ref[...], D_MODEL)
        h = (xhat * gm_ref[...]).astype(BF16)
        z = _dot(h, win_ref[...])
        z_ref[...] = z
        cq, _ = _norm_fwd(z[:, :Q_LORA], Q_LORA)
        cqn = (cq * qln_ref[...]).astype(BF16)
        ckv, _ = _norm_fwd(z[:, Z_CKV:Z_POOL], KV_LORA)
        ckvn = (ckv * kvln_ref[...]).astype(BF16)
        kpe = z[:, Z_KPE:]
        q_raw = _dot(cqn, wuq_ref[...])
        k_raw = _dot(ckvn, wuk_ref[...])
        v = _dot(ckvn, wuv_ref[...])
        v_ref[...] = v.astype(BF16)
        vt_ref[...] = jnp.transpose(v).astype(BF16)
        c, s1, s2 = rc_ref[...], rs1_ref[...], rs2_ref[...]
        for hd in range(HEADS):
            lanes = slice(hd * HEAD_PAD, (hd + 1) * HEAD_PAD)
            qh, _ = _norm_fwd(q_raw[:, lanes], QK_HEAD)
            q_ref[:, lanes] = (_rope_fwd(qh * qn_ref[...], c, s1, s2) * scale).astype(BF16)
            kh, _ = _norm_fwd(k_raw[:, lanes] + kpe, QK_HEAD)
            k_ref[:, lanes] = _rope_fwd(kh * kn_ref[...], c, s1, s2).astype(BF16)

    full = lambda shape: pl.BlockSpec(shape, lambda i: (0,) * len(shape))
    row = lambda w: pl.BlockSpec((tm, w), lambda i: (i, 0))
    return pl.pallas_call(
        body, name="mix_in_fwd",
        grid=(s // tm,),
        in_specs=[row(D_MODEL), full((1, D_MODEL)), full((D_MODEL, Z_W)), full((1, Q_LORA)), full((1, KV_LORA)),
                  full((Q_LORA, ATT_W)), full((KV_LORA, ATT_W)), full((KV_LORA, ATT_W)),
                  full((1, HEAD_PAD)), full((1, HEAD_PAD)), row(HEAD_PAD), row(HEAD_PAD), row(HEAD_PAD)],
        out_specs=[row(Z_W), row(ATT_W), row(ATT_W), row(ATT_W), pl.BlockSpec((ATT_W, tm), lambda i: (0, i))],
        out_shape=[jax.ShapeDtypeStruct((s, Z_W), F32)] + [jax.ShapeDtypeStruct((s, ATT_W), BF16)] * 3 + [
            jax.ShapeDtypeStruct((ATT_W, s), BF16)],
        compiler_params=_params("parallel"),
    )(x, gm, win, qln, kvln, wuq, wuk, wuv, qn, kn, rc, rs1, rs2)


def mla_bwd(z, dq_t, dk, dv, dp, qln, kvln, wuq, wuk, wuv, qn, kn, rc, rs1, rs2, tm=512):
    s = z.shape[0]
    tm = _row_tile(s, tm)
    scale = QK_HEAD ** -0.5

    def body(z_ref, dqt_ref, dk_ref, dv_ref, dp_ref, qln_ref, kvln_ref, wuq_ref, wuk_ref, wuv_ref, qn_ref, kn_ref,
             rc_ref, rs1_ref, rs2_ref,
             dz_ref, cqn_ref, ckvn_ref, dqr_ref, dkr_ref, dvb_ref, dqln_ref, dkvln_ref, dqn_ref, dkn_ref):
        i = pl.program_id(0)
        z = z_ref[...]
        cq, cq_rstd = _norm_fwd(z[:, :Q_LORA], Q_LORA)
        cqn = (cq * qln_ref[...]).astype(BF16)
        ckv, ckv_rstd = _norm_fwd(z[:, Z_CKV:Z_POOL], KV_LORA)
        ckvn = (ckv * kvln_ref[...]).astype(BF16)
        kpe = z[:, Z_KPE:]
        cqn_ref[...] = cqn
        ckvn_ref[...] = ckvn
        q_raw = _dot(cqn, wuq_ref[...])
        k_raw = _dot(ckvn, wuk_ref[...])
        c, s1, s2 = rc_ref[...], rs1_ref[...], rs2_ref[...]
        lane = lax.broadcasted_iota(jnp.int32, (tm, HEAD_PAD), 1)
        rope_lanes = (lane >= QK_NOPE) & (lane < QK_HEAD)
        dkpe = jnp.zeros((tm, HEAD_PAD), F32)
        dqn = jnp.zeros((tm, HEAD_PAD), F32)
        dkn = jnp.zeros((tm, HEAD_PAD), F32)
        for hd in range(HEADS):
            lanes = slice(hd * HEAD_PAD, (hd + 1) * HEAD_PAD)
            qh, q_rstd = _norm_fwd(q_raw[:, lanes], QK_HEAD)
            dqh = jnp.transpose(dqt_ref[lanes, :]) * scale
            dqr, dg = _norm_bwd(_rope_bwd(dqh, c, s1, s2), qn_ref[...], qh, q_rstd, QK_HEAD)
            dqn += dg
            dqr_ref[:, lanes] = dqr.astype(BF16)
            kh, k_rstd = _norm_fwd(k_raw[:, lanes] + kpe, QK_HEAD)
            dkr, dg = _norm_bwd(_rope_bwd(dk_ref[:, lanes], c, s1, s2), kn_ref[...], kh, k_rstd, QK_HEAD)
            dkn += dg
            dkr_ref[:, lanes] = dkr.astype(BF16)
            dkpe += jnp.where(rope_lanes, dkr, 0.0)
        dvb = dv_ref[...].astype(BF16)
        dvb_ref[...] = dvb
        dcqn = _dot_nt(dqr_ref[...], wuq_ref[...])
        dckvn = _dot_nt(dkr_ref[...], wuk_ref[...]) + _dot_nt(dvb, wuv_ref[...])
        dcq, dqln = _norm_bwd(dcqn, qln_ref[...], cq, cq_rstd, Q_LORA)
        dckv, dkvln = _norm_bwd(dckvn, kvln_ref[...], ckv, ckv_rstd, KV_LORA)
        dz_ref[:, :Q_LORA] = dcq.astype(BF16)
        dz_ref[:, Z_CKV:Z_POOL] = dckv.astype(BF16)
        dz_ref[:, Z_POOL:Z_KPE] = dp_ref[...].astype(BF16)
        dz_ref[:, Z_KPE:] = dkpe.astype(BF16)
        parts = [(dqln_ref, dqln), (dkvln_ref, dkvln), (dqn_ref, dqn), (dkn_ref, dkn)]

        @pl.when(i == 0)
        def _():
            for ref, val in parts:
                ref[...] = jnp.sum(val, axis=0, keepdims=True)

        @pl.when(i > 0)
        def _():
            for ref, val in parts:
                ref[...] += jnp.sum(val, axis=0, keepdims=True)

    full = lambda shape: pl.BlockSpec(shape, lambda i: (0,) * len(shape))
    row = lambda w: pl.BlockSpec((tm, w), lambda i: (i, 0))
    return pl.pallas_call(
        body, name="mla_bwd",
        grid=(s // tm,),
        in_specs=[row(Z_W), pl.BlockSpec((ATT_W, tm), lambda i: (0, i)), row(ATT_W), row(ATT_W), row(POOL_W),
                  full((1, Q_LORA)), full((1, KV_LORA)), full((Q_LORA, ATT_W)), full((KV_LORA, ATT_W)),
                  full((KV_LORA, ATT_W)), full((1, HEAD_PAD)), full((1, HEAD_PAD)),
                  row(HEAD_PAD), row(HEAD_PAD), row(HEAD_PAD)],
        out_specs=[row(Z_W), row(Q_LORA), row(KV_LORA), row(ATT_W), row(ATT_W), row(ATT_W),
                   full((1, Q_LORA)), full((1, KV_LORA)), full((1, HEAD_PAD)), full((1, HEAD_PAD))],
        out_shape=[jax.ShapeDtypeStruct((s, Z_W), BF16), jax.ShapeDtypeStruct((s, Q_LORA), BF16),
                   jax.ShapeDtypeStruct((s, KV_LORA), BF16)] + [jax.ShapeDtypeStruct((s, ATT_W), BF16)] * 3 + [
                   jax.ShapeDtypeStruct((1, Q_LORA), F32), jax.ShapeDtypeStruct((1, KV_LORA), F32),
                   jax.ShapeDtypeStruct((1, HEAD_PAD), F32), jax.ShapeDtypeStruct((1, HEAD_PAD), F32)],
        compiler_params=_params("arbitrary"),
    )(z, dq_t, dk, dv, dp, qln, kvln, wuq, wuk, wuv, qn, kn, rc, rs1, rs2)


def mix_in_bwd(x, dx_in, dz, gm, win, tm=512):
    s = x.shape[0]
    tm = _row_tile(s, tm)

    def body(x_ref, dxin_ref, dz_ref, gm_ref, win_ref, dx_ref, h_ref, dgm_ref):
        i = pl.program_id(0)
        xhat, rstd = _norm_fwd(x_ref[...], D_MODEL)
        h_ref[...] = (xhat * gm_ref[...]).astype(BF16)
        dh = _dot_nt(dz_ref[...], win_ref[...])
        dxn, dgrow = _norm_bwd(dh, gm_ref[...], xhat, rstd, D_MODEL)
        dx_ref[...] = dxin_ref[...] + dxn
        part = jnp.sum(dgrow, axis=0, keepdims=True)

        @pl.when(i == 0)
        def _():
            dgm_ref[...] = part

        @pl.when(i > 0)
        def _():
            dgm_ref[...] += part

    full = lambda shape: pl.BlockSpec(shape, lambda i: (0,) * len(shape))
    row = lambda w: pl.BlockSpec((tm, w), lambda i: (i, 0))
    return pl.pallas_call(
        body, name="mix_in_bwd",
        grid=(s // tm,),
        in_specs=[row(D_MODEL), row(D_MODEL), row(Z_W), full((1, D_MODEL)), full((D_MODEL, Z_W))],
        out_specs=[row(D_MODEL), row(D_MODEL), full((1, D_MODEL))],
        out_shape=[jax.ShapeDtypeStruct((s, D_MODEL), F32), jax.ShapeDtypeStruct((s, D_MODEL), BF16),
                   jax.ShapeDtypeStruct((1, D_MODEL), F32)],
        compiler_params=_params("arbitrary"),
    )(x, dx_in, dz, gm, win)


ATT_CHUNK = 512


def flash_fwd(q, k, vt, tq=1024, tk=4096, chunk=ATT_CHUNK):
    s = q.shape[0]
    tq, tk = _row_tile(s, tq), _row_tile(s, tk)
    nk = s // tk
    tc = _row_tile(tk, chunk)
    sub = tk // tc

    def body(q_ref, k_ref, vt_ref, o_ref, lse_ref, m_s, l_s, acc_s):
        kk = pl.program_id(2)

        @pl.when(kk == 0)
        def _():
            m_s[...] = jnp.full_like(m_s, -jnp.inf)
            l_s[...] = jnp.zeros_like(l_s)
            acc_s[...] = jnp.zeros_like(acc_s)

        q_t = q_ref[...]

        def scores(c):
            return _dot_nt(k_ref[c * tc:(c + 1) * tc, :], q_t)

        m, l, acc = m_s[...], l_s[...], acc_s[...]
        s_next = scores(0)
        for c in range(sub):
            s_t = s_next
            if c + 1 < sub:
                s_next = scores(c + 1)
            m_new = jnp.maximum(m, jnp.max(s_t, axis=0, keepdims=True))
            alpha = jnp.exp2(m - m_new)
            p_t = jnp.exp2(s_t - m_new)
            l = alpha * l + jnp.sum(p_t, axis=0, keepdims=True)
            acc = alpha * acc + _dot(vt_ref[:, c * tc:(c + 1) * tc], p_t.astype(BF16))
            m = m_new
        m_s[...], l_s[...], acc_s[...] = m, l, acc

        @pl.when(kk == nk - 1)
        def _():
            o_ref[...] = jnp.transpose(acc_s[...] / l_s[...]).astype(BF16)
            lse_ref[...] = m_s[...] + jnp.log2(l_s[...])

    return pl.pallas_call(
        body, name="flash_fwd",
        grid=(HEADS, s // tq, nk),
        in_specs=[pl.BlockSpec((tq, HEAD_PAD), lambda h, i, kk: (i, h)),
                  pl.BlockSpec((tk, HEAD_PAD), lambda h, i, kk: (kk, h)),
                  pl.BlockSpec((HEAD_PAD, tk), lambda h, i, kk: (h, kk))],
        out_specs=[pl.BlockSpec((tq, HEAD_PAD), lambda h, i, kk: (i, h)),
                   pl.BlockSpec((None, 1, tq), lambda h, i, kk: (h, 0, i))],
        out_shape=[jax.ShapeDtypeStruct((s, ATT_W), BF16), jax.ShapeDtypeStruct((HEADS, 1, s), F32)],
        scratch_shapes=[pltpu.VMEM((1, tq), F32), pltpu.VMEM((1, tq), F32), pltpu.VMEM((HEAD_PAD, tq), F32)],
        compiler_params=_params("parallel", "parallel", "arbitrary"),
    )(q, k, vt)


def flash_bwd(q, k, v, do, lse_row, delta_row, tq=2048, tk=1024, chunk=ATT_CHUNK):
    s = q.shape[0]
    tq, tk = _row_tile(s, tq), _row_tile(s, tk)
    nq = s // tq
    tc = _row_tile(tq, chunk)
    sub = tq // tc

    def body(q_ref, k_ref, v_ref, do_ref, lse_ref, dl_ref, dqt_ref, dk_ref, dv_ref, kt_s, dk_s, dv_s):
        kb, qb = pl.program_id(1), pl.program_id(2)

        @pl.when(qb == 0)
        def _():
            kt_s[...] = jnp.transpose(k_ref[...].astype(F32)).astype(BF16)
            dk_s[...] = jnp.zeros_like(dk_s)
            dv_s[...] = jnp.zeros_like(dv_s)

        k_t, v_t = k_ref[...], v_ref[...]

        def products(c):
            rows = slice(c * tc, (c + 1) * tc)
            return _dot_nt(k_t, q_ref[rows, :]), _dot_nt(v_t, do_ref[rows, :])

        dk, dv = dk_s[...], dv_s[...]
        nxt = products(0)
        contribs = []
        for c in range(sub):
            rows = slice(c * tc, (c + 1) * tc)
            s_t, dp_t = nxt
            if c + 1 < sub:
                nxt = products(c + 1)
            p_t = jnp.exp2(s_t - lse_ref[:, rows])
            ds_t = (p_t * (dp_t - dl_ref[:, rows])).astype(BF16)
            dv = dv + _dot(p_t.astype(BF16), do_ref[rows, :])
            dk = dk + _dot(ds_t, q_ref[rows, :])
            contribs.append(_dot(kt_s[...], ds_t))
        dk_s[...], dv_s[...] = dk, dv
        contrib = jnp.concatenate(contribs, axis=1) if sub > 1 else contribs[0]
        cols = pl.ds(pl.multiple_of(qb * tq, tq), tq)

        @pl.when(kb == 0)
        def _():
            dqt_ref[:, cols] = contrib

        @pl.when(kb > 0)
        def _():
            dqt_ref[:, cols] += contrib

        @pl.when(qb == nq - 1)
        def _():
            dk_ref[...] = dk_s[...] * LN_2
            dv_ref[...] = dv_s[...]

    qspec = pl.BlockSpec((tq, HEAD_PAD), lambda h, kb, qb: (qb, h))
    kspec = pl.BlockSpec((tk, HEAD_PAD), lambda h, kb, qb: (kb, h))
    rowspec = pl.BlockSpec((None, 1, tq), lambda h, kb, qb: (h, 0, qb))
    return pl.pallas_call(
        body, name="flash_bwd",
        grid=(HEADS, s // tk, nq),
        in_specs=[qspec, kspec, kspec, qspec, rowspec, rowspec],
        out_specs=[pl.BlockSpec((HEAD_PAD, s), lambda h, kb, qb: (h, 0)), kspec, kspec],
        out_shape=[jax.ShapeDtypeStruct((ATT_W, s), F32), jax.ShapeDtypeStruct((s, ATT_W), F32),
                   jax.ShapeDtypeStruct((s, ATT_W), F32)],
        scratch_shapes=[pltpu.VMEM((HEAD_PAD, tk), BF16), pltpu.VMEM((tk, HEAD_PAD), F32),
                        pltpu.VMEM((tk, HEAD_PAD), F32)],
        compiler_params=_params("parallel", "arbitrary", "arbitrary"),
    )(q, k, v, do, lse_row, delta_row)


def _halo_specs(tm, w, n_tiles):
    per = tm // POOL_HALO
    last = n_tiles * per - 1
    return [pl.BlockSpec((POOL_HALO, w), lambda i: (jnp.maximum(i * per - 1, 0), 0)),
            pl.BlockSpec((tm, w), lambda i: (i, 0)),
            pl.BlockSpec((POOL_HALO, w), lambda i: (jnp.minimum((i + 1) * per, last), 0))]


def _fill_ext(ext_ref, prev, cur, nxt, i, n_tiles, tm):
    ext_ref[pl.ds(0, POOL_HALO), :] = jnp.where(i > 0, prev, 0.0)
    ext_ref[pl.ds(POOL_HALO, tm), :] = cur
    ext_ref[pl.ds(POOL_HALO + tm, POOL_HALO), :] = jnp.where(i < n_tiles - 1, nxt, 0.0)


def _window_sum(ext_ref, g, lo, hi, tm):
    lanes = pl.ds(g * GROUP, GROUP)
    acc = ext_ref[pl.ds(POOL_HALO + lo, tm), lanes]
    for d in range(lo + 1, hi + 1):
        acc = acc + ext_ref[pl.ds(POOL_HALO + d, tm), lanes]
    return acc


def _pool_mixed(ext_ref, cur, i, s, tm):
    row = i * tm + lax.broadcasted_iota(jnp.int32, (tm, 1), 0)
    out = []
    for g, w in enumerate(POOL_WINDOWS):
        left = w // 2
        right = w - 1 - left
        cnt = (jnp.minimum(row + right + 1, s) - jnp.maximum(row - left, 0)).astype(F32)
        out.append(_window_sum(ext_ref, g, -left, right, tm) / cnt - cur[:, g * GROUP:(g + 1) * GROUP])
    return out


def pool_out_fwd(x, z, o, wpool, pscale, wout, tm=512):
    s = x.shape[0]
    tm = _row_tile(s, tm)
    n_tiles = s // tm

    def body(x_ref, zp_ref, z_ref, zn_ref, o_ref, wp_ref, ps_ref, wo_ref, y_ref, cat_ref, ext_s):
        i = pl.program_id(0)
        cur = z_ref[:, Z_POOL:Z_KPE]
        _fill_ext(ext_s, zp_ref[:, Z_POOL:Z_KPE], cur, zn_ref[:, Z_POOL:Z_KPE], i, n_tiles, tm)
        mixed = _pool_mixed(ext_s, cur, i, s, tm)
        cat_ref[:, :ATT_W] = o_ref[...]
        for g in range(len(POOL_WINDOWS)):
            lanes = slice(g * GROUP, (g + 1) * GROUP)
            yg = _dot(mixed[g].astype(BF16), wp_ref[g])
            cat_ref[:, ATT_W + g * GROUP:ATT_W + (g + 1) * GROUP] = (yg * ps_ref[:, lanes]).astype(BF16)
        y_ref[...] = x_ref[...] + _dot(cat_ref[...], wo_ref[...])

    full = lambda shape: pl.BlockSpec(shape, lambda i: (0,) * len(shape))
    row = lambda w: pl.BlockSpec((tm, w), lambda i: (i, 0))
    return pl.pallas_call(
        body, name="pool_out_fwd",
        grid=(n_tiles,),
        in_specs=[row(D_MODEL)] + _halo_specs(tm, Z_W, n_tiles) + [
            row(ATT_W), full((len(POOL_WINDOWS), GROUP, GROUP)), full((1, POOL_W)), full((CAT_W, D_MODEL))],
        out_specs=[row(D_MODEL), row(CAT_W)],
        out_shape=[jax.ShapeDtypeStruct((s, D_MODEL), F32), jax.ShapeDtypeStruct((s, CAT_W), BF16)],
        scratch_shapes=[pltpu.VMEM((tm + 2 * POOL_HALO, POOL_W), F32)],
        compiler_params=_params("parallel"),
    )(x, z, z, z, o, wpool, pscale, wout)


def out_bwd(dx, wout, cat, tm=512):
    s = dx.shape[0]
    tm = _row_tile(s, tm)

    def body(dx_ref, wo_ref, cat_ref, do_ref, db_ref, dl_ref):
        dcat = _dot_nt(dx_ref[...].astype(BF16), wo_ref[...])
        do_ref[...] = dcat[:, :ATT_W].astype(BF16)
        db_ref[...] = dcat[:, ATT_W:]
        for hd in range(HEADS):
            lanes = slice(hd * HEAD_PAD, (hd + 1) * HEAD_PAD)
            dl_ref[hd] = jnp.sum(dcat[:, lanes] * cat_ref[:, lanes].astype(F32), axis=-1, keepdims=True)

    row = lambda w: pl.BlockSpec((tm, w), lambda i: (i, 0))
    return pl.pallas_call(
        body, name="out_bwd",
        grid=(s // tm,),
        in_specs=[row(D_MODEL), pl.BlockSpec((CAT_W, D_MODEL), lambda i: (0, 0)), row(CAT_W)],
        out_specs=[row(ATT_W), row(POOL_W), pl.BlockSpec((HEADS, tm, 1), lambda i: (0, i, 0))],
        out_shape=[jax.ShapeDtypeStruct((s, ATT_W), BF16), jax.ShapeDtypeStruct((s, POOL_W), F32),
                   jax.ShapeDtypeStruct((HEADS, s, 1), F32)],
        compiler_params=_params("parallel"),
    )(dx, wout, cat)


def pool_bwd(db, z, wpool, pscale, tm=512):
    s = db.shape[0]
    tm = _row_tile(s, tm)
    n_tiles = s // tm
    te = tm + 2 * POOL_HALO

    def body(dbp_ref, db_ref, dbn_ref, zp_ref, z_ref, zn_ref, wp_ref, ps_ref,
             dp_ref, mixed_ref, dys_ref, dps_ref, ext_s, text_s):
        i = pl.program_id(0)
        cur = z_ref[:, Z_POOL:Z_KPE]
        _fill_ext(ext_s, zp_ref[:, Z_POOL:Z_KPE], cur, zn_ref[:, Z_POOL:Z_KPE], i, n_tiles, tm)
        mixed = _pool_mixed(ext_s, cur, i, s, tm)
        _fill_ext(ext_s, dbp_ref[...], db_ref[...], dbn_ref[...], i, n_tiles, tm)
        erow = i * tm - POOL_HALO + lax.broadcasted_iota(jnp.int32, (te, 1), 0)
        dps = []
        for g, w in enumerate(POOL_WINDOWS):
            lanes = slice(g * GROUP, (g + 1) * GROUP)
            left = w // 2
            right = w - 1 - left
            mb = mixed[g].astype(BF16)
            mixed_ref[:, lanes] = mb
            dps.append(jnp.sum(db_ref[:, lanes] * _dot(mb, wp_ref[g]), axis=0, keepdims=True))
            dys_e = (ext_s[:, lanes] * ps_ref[:, lanes]).astype(BF16)
            dys_ref[:, lanes] = (db_ref[:, lanes] * ps_ref[:, lanes]).astype(BF16)
            dmix_e = _dot_nt(dys_e, wp_ref[g])
            cnt = jnp.minimum(erow + right + 1, s) - jnp.maximum(erow - left, 0)
            text_s[:, lanes] = dmix_e / jnp.maximum(cnt, 1).astype(F32)
            dp_ref[:, lanes] = _window_sum(text_s, g, -right, left, tm) - dmix_e[POOL_HALO:POOL_HALO + tm]
        part = jnp.concatenate(dps, axis=1)

        @pl.when(i == 0)
        def _():
            dps_ref[...] = part

        @pl.when(i > 0)
        def _():
            dps_ref[...] += part

    full = lambda shape: pl.BlockSpec(shape, lambda i: (0,) * len(shape))
    row = lambda w: pl.BlockSpec((tm, w), lambda i: (i, 0))
    return pl.pallas_call(
        body, name="pool_bwd",
        grid=(n_tiles,),
        in_specs=_halo_specs(tm, POOL_W, n_tiles) + _halo_specs(tm, Z_W, n_tiles) + [
            full((len(POOL_WINDOWS), GROUP, GROUP)), full((1, POOL_W))],
        out_specs=[row(POOL_W), row(POOL_W), row(POOL_W), full((1, POOL_W))],
        out_shape=[jax.ShapeDtypeStruct((s, POOL_W), F32), jax.ShapeDtypeStruct((s, POOL_W), BF16),
                   jax.ShapeDtypeStruct((s, POOL_W), BF16), jax.ShapeDtypeStruct((1, POOL_W), F32)],
        scratch_shapes=[pltpu.VMEM((te, POOL_W), F32), pltpu.VMEM((te, POOL_W), F32)],
        compiler_params=_params("arbitrary"),
    )(db, db, db, z, z, z, wpool, pscale)


def loss_head(y, target, tm=512):
    s = y.shape[0]
    tm = _row_tile(s, tm)

    def body(y_ref, t_ref, dy_ref, loss_ref):
        i = pl.program_id(0)
        err = y_ref[...] - t_ref[...]
        dy_ref[...] = err * (1.0 / D_MODEL)
        part = 0.5 * jnp.sum(jnp.sum(err * err, axis=-1, keepdims=True) * (1.0 / D_MODEL), axis=0, keepdims=True)
        part = jnp.broadcast_to(part, (1, 128))

        @pl.when(i == 0)
        def _():
            loss_ref[...] = part

        @pl.when(i > 0)
        def _():
            loss_ref[...] += part

    row = pl.BlockSpec((tm, D_MODEL), lambda i: (i, 0))
    return pl.pallas_call(
        body, name="loss_head",
        grid=(s // tm,),
        in_specs=[row, row],
        out_specs=[row, pl.BlockSpec((1, 128), lambda i: (0, 0))],
        out_shape=[jax.ShapeDtypeStruct((s, D_MODEL), F32), jax.ShapeDtypeStruct((1, 128), F32)],
        compiler_params=_params("arbitrary"),
    )(y, target)


def _pad_heads(w, real):
    lead = w.shape[:-1]
    w = w.reshape(lead + (HEADS, real))
    w = jnp.pad(w, [(0, 0)] * len(lead) + [(0, 0), (0, HEAD_PAD - real)])
    return w.reshape(lead + (ATT_W,))


def _unpad_heads(w, real):
    lead = w.shape[:-1]
    return w.reshape(lead + (HEADS, HEAD_PAD))[..., :real].reshape(lead + (HEADS * real,))


def layout_weights(full):
    w = {}
    depth = full["w_in"].shape[0]
    win = full["w_in"]
    zc = lambda n: jnp.zeros((depth, D_MODEL, n), win.dtype)
    o_pe = Q_LORA + KV_LORA
    w["win"] = jnp.concatenate([win[..., :o_pe], win[..., o_pe + QK_ROPE:], zc(QK_NOPE), win[..., o_pe:o_pe + QK_ROPE],
                                zc(HEAD_PAD - QK_HEAD)], axis=-1)
    w["wuq"] = _pad_heads(full["w_uq"], QK_HEAD)
    w["wuk"] = _pad_heads(full["w_uk"], QK_NOPE)
    w["wuv"] = _pad_heads(full["w_uv"], V_HEAD)
    wo = full["w_out"]
    wo_att = jnp.swapaxes(_pad_heads(jnp.swapaxes(wo[:, :HEADS * V_HEAD], 1, 2), V_HEAD), 1, 2)
    w["wout"] = jnp.concatenate([wo_att, wo[:, HEADS * V_HEAD:]], axis=1)
    pad_vec = lambda v: jnp.pad(v, ((0, 0), (0, HEAD_PAD - QK_HEAD)))
    w["qn"], w["kn"] = pad_vec(full["q_norm"]), pad_vec(full["k_norm"])
    for n in ("ffn1_norm", "mix_norm", "ffn2_norm", "q_lat_norm", "kv_lat_norm", "pool_scale", "w_pool"):
        w[n] = full[n]
    return w


def local_step(x, target, w, wgu1, wd1, wgu2, wd2):
    s = x.shape[0]
    depth = w["win"].shape[0]
    rc, rs1, rs2 = rope_tables(s)
    vec = lambda name, l: w[name][l][None, :]
    saved = []
    for l in range(depth):
        x1, g1, u1 = ffn_fwd(x, vec("ffn1_norm", l), wgu1, wd1, l)
        z, q, k, v, vt = mix_in_fwd(x1, vec("mix_norm", l), w["win"][l], vec("q_lat_norm", l), vec("kv_lat_norm", l),
                                    w["wuq"][l], w["wuk"][l], w["wuv"][l], vec("qn", l), vec("kn", l), rc, rs1, rs2)
        o, lse = flash_fwd(q, k, vt)
        x2, cat = pool_out_fwd(x1, z, o, w["w_pool"][l], vec("pool_scale", l), w["wout"][l])
        x3, g2, u2 = ffn_fwd(x2, vec("ffn2_norm", l), wgu2, wd2, l)
        saved.append((x, x1, z, q, k, v, lse, cat, x2, g1, u1, g2, u2))
        x = x3
    dy, loss = loss_head(x, target)

    names = ["ffn1_norm", "ffn1_w_gu", "ffn1_w_down", "mix_norm", "w_in", "q_lat_norm", "kv_lat_norm", "w_uq", "w_uk",
             "w_uv", "q_norm", "k_norm", "w_pool", "pool_scale", "w_out", "ffn2_norm", "ffn2_w_gu", "ffn2_w_down"]
    grads = {n: [None] * depth for n in names if "ffn" not in n or "norm" in n}
    for n in ("ffn1", "ffn2"):
        grads[n + "_w_gu"] = lax.empty((depth, N_CHIPS, D_MODEL, FF_BLOCK), F32)
        grads[n + "_w_down"] = lax.empty((depth, D_FF, D_MODEL), F32)

    def ffn_weight_grads(n, l, h, act, dg, du, dyh):
        gu = matmul_tn(h, dg, tk=2048, into=grads[n + "_w_gu"], at=lambda j: (l, j))
        grads[n + "_w_gu"] = matmul_tn(h, du, tk=2048, into=gu, at=lambda j: (l, j + 2))
        grads[n + "_w_down"] = matmul_tn(act, dyh, into=grads[n + "_w_down"], at=lambda j: (l,))

    for l in reversed(range(depth)):
        x0, x1, z, q, k, v, lse, cat, x2, g1, u1, g2, u2 = saved[l]
        dx2, h, act, dg, du, dyh, dgn = ffn_bwd(x2, dy, vec("ffn2_norm", l), g2, u2, wgu2, wd2, l)
        grads["ffn2_norm"][l] = dgn[0]
        ffn_weight_grads("ffn2", l, h, act, dg, du, dyh)

        do, db, delta = out_bwd(dx2, w["wout"][l], cat)
        dwout = matmul_tn(cat, dx2)
        grads["w_out"][l] = jnp.concatenate(
            [jnp.swapaxes(_unpad_heads(jnp.swapaxes(dwout[:ATT_W], 0, 1), V_HEAD), 0, 1), dwout[ATT_W:]], axis=0)
        dp, mixed, dys, dps = pool_bwd(db, z, w["w_pool"][l], vec("pool_scale", l))
        grads["pool_scale"][l] = dps[0]
        dwp = matmul_tn(mixed, dys)
        grads["w_pool"][l] = jnp.stack([dwp[g * GROUP:(g + 1) * GROUP, g * GROUP:(g + 1) * GROUP]
                                        for g in range(len(POOL_WINDOWS))])
        dq_t, dk, dv = flash_bwd(q, k, v, do, lse, delta.reshape(HEADS, 1, s))
        dz, cqn, ckvn, dqr, dkr, dvb, dqln, dkvln, dqn, dkn = mla_bwd(
            z, dq_t, dk, dv, dp, vec("q_lat_norm", l), vec("kv_lat_norm", l), w["wuq"][l], w["wuk"][l], w["wuv"][l],
            vec("qn", l), vec("kn", l), rc, rs1, rs2)
        grads["q_lat_norm"][l], grads["kv_lat_norm"][l] = dqln[0], dkvln[0]
        grads["q_norm"][l], grads["k_norm"][l] = dqn[0, :QK_HEAD], dkn[0, :QK_HEAD]
        grads["w_uq"][l] = _unpad_heads(matmul_tn(cqn, dqr), QK_HEAD)
        grads["w_uk"][l] = _unpad_heads(matmul_tn(ckvn, dkr), QK_NOPE)
        grads["w_uv"][l] = _unpad_heads(matmul_tn(ckvn, dvb), V_HEAD)
        dx1, h2, dgm = mix_in_bwd(x1, dx2, dz, vec("mix_norm", l), w["win"][l])
        grads["mix_norm"][l] = dgm[0]
        dwin = matmul_tn(h2, dz)
        grads["w_in"][l] = jnp.concatenate(
            [dwin[:, :Z_POOL], dwin[:, Z_KPE + QK_NOPE:Z_KPE + QK_HEAD], dwin[:, Z_POOL:Z_KPE]], axis=1)

        dy, h, act, dg, du, dyh, dgn = ffn_bwd(x0, dx1, vec("ffn1_norm", l), g1, u1, wgu1, wd1, l)
        grads["ffn1_norm"][l] = dgn[0]
        ffn_weight_grads("ffn1", l, h, act, dg, du, dyh)
    return loss, dy, {n: (jnp.stack(g) if isinstance(g, list) else g) for n, g in grads.items()}


HBM_SPEC = pl.BlockSpec(memory_space=pltpu.HBM)


def _place():
    x, y, c = lax.axis_index("x"), lax.axis_index("y"), lax.axis_index("c")
    other_chips = [(1 - x, y), (x, 1 - y), (1 - x, 1 - y)]
    return x, y, c, other_chips


def _exchange(name, arrays, out_shapes, plan, n_local, n_remote):
    n = len(arrays)

    def body(*refs):
        ins, outs = refs[:n], refs[n:n + len(out_shapes)]
        send_sems, recv_sems, local_sems = refs[n + len(out_shapes):]
        local, sends, recvs, forwards = plan(ins, outs, _place())

        def remote(k, src, dst, dev):
            return pltpu.make_async_remote_copy(src_ref=src, dst_ref=dst, send_sem=send_sems.at[k],
                                                recv_sem=recv_sems.at[k], device_id=dev, device_id_type=MESH)

        started = []
        for k, (src, dst) in enumerate(local):
            cp = pltpu.make_async_copy(src, dst, local_sems.at[k])
            cp.start()
            started.append(cp)
        out_going = []
        for k, (src, dst, dev) in enumerate(sends):
            cp = remote(k, src, dst, dev)
            cp.start()
            out_going.append(cp)
        for k, (dst, dev) in enumerate(recvs):
            remote(k, dst, dst, dev).wait_recv()
            if forwards:
                src, fdst, fdev, _ = forwards[k]
                cp = remote(len(sends) + k, src, fdst, fdev)
                cp.start()
                out_going.append(cp)
        for k, (_, _, fdev, landing) in enumerate(forwards):
            remote(len(sends) + k, landing, landing, fdev).wait_recv()
        for cp in out_going:
            cp.wait_send()
        for cp in started:
            cp.wait()

    return pl.pallas_call(
        body, name=name,
        in_specs=[HBM_SPEC] * n,
        out_specs=[HBM_SPEC] * len(out_shapes),
        out_shape=out_shapes,
        scratch_shapes=[pltpu.SemaphoreType.DMA((n_remote,)), pltpu.SemaphoreType.DMA((n_remote,)),
                        pltpu.SemaphoreType.DMA((max(n_local, 1),))],
    )(*arrays)


def _layer_half(depth, which):
    assert depth % 2 == 0
    return pl.ds(which * (depth // 2), depth // 2)


def gather_chips(shards):
    n = len(shards)
    depth = shards[0].shape[0]

    def plan(ins, outs, place):
        x, y, c, chips = place
        me, sib = 2 * x + y, (x, y, 1 - c)
        mine, theirs = _layer_half(depth, c), _layer_half(depth, 1 - c)
        sends = [(ins[p].at[mine], outs[p].at[me, mine], (px, py, c)) for p in range(n) for (px, py) in chips]
        recvs = [(outs[p].at[2 * px + py, mine], (px, py, c)) for p in range(n) for (px, py) in chips]
        forwards = [(outs[p].at[2 * px + py, mine], outs[p].at[2 * px + py, mine], sib, outs[p].at[2 * px + py, theirs])
                    for p in range(n) for (px, py) in chips]
        return [], sends, recvs, forwards

    out_shapes = [jax.ShapeDtypeStruct((N_CHIPS,) + a.shape, a.dtype) for a in shards]
    return _exchange("gather_chips", shards, out_shapes, plan, 0, 6 * n)


def swap_halves(fulls):
    n = len(fulls)
    depth = fulls[0].shape[0]

    def plan(ins, outs, place):
        x, y, c, _ = place
        sib = (x, y, 1 - c)
        sends = [(ins[p].at[_layer_half(depth, 1 - c)], outs[p], sib) for p in range(n)]
        return [], sends, [(outs[p], sib) for p in range(n)], []

    out_shapes = [jax.ShapeDtypeStruct((depth // 2,) + a.shape[1:], a.dtype) for a in fulls]
    return _exchange("swap_halves", fulls, out_shapes, plan, 0, n)


def scatter_chips(fulls):
    n = len(fulls)

    def plan(ins, outs, place):
        x, y, c, chips = place
        sends = [(ins[p].at[:, 2 * px + py], outs[p].at[j], (px, py, c))
                 for p in range(n) for j, (px, py) in enumerate(chips)]
        recvs = [(outs[p].at[j], (px, py, c)) for p in range(n) for j, (px, py) in enumerate(chips)]
        return [], sends, recvs, []

    out_shapes = [jax.ShapeDtypeStruct((3, a.shape[0]) + a.shape[2:], a.dtype) for a in fulls]
    return _exchange("scatter_chips", fulls, out_shapes, plan, 0, 3 * n)


def swap_sibling(arrays):
    n = len(arrays)

    def plan(ins, outs, place):
        x, y, c, _ = place
        sib = (x, y, 1 - c)
        return [], [(ins[p], outs[p], sib) for p in range(n)], [(outs[p], sib) for p in range(n)], []

    return _exchange("swap_sibling", arrays, [jax.ShapeDtypeStruct(a.shape, a.dtype) for a in arrays], plan, 0, n)


def gather_all(vec):
    def plan(ins, outs, place):
        x, y, c, _ = place
        me = 4 * x + 2 * y + c
        flips = [(fx, fy, fc) for fx in (0, 1) for fy in (0, 1) for fc in (0, 1)][1:]
        peers = [(1 - x if fx else x, 1 - y if fy else y, 1 - c if fc else c) for fx, fy, fc in flips]
        local = [(ins[0], outs[0].at[me])]
        sends = [(ins[0], outs[0].at[me], dev) for dev in peers]
        recvs = [(outs[0].at[4 * dev[0] + 2 * dev[1] + dev[2]], dev) for dev in peers]
        return local, sends, recvs, []

    return _exchange("gather_all", [vec], [jax.ShapeDtypeStruct((8,) + vec.shape, vec.dtype)], plan, 1, 7)[0]


def _pick_rows(rows, cols, block_bytes=512 * 1024):
    best = None
    for t in range(16, rows + 1, 16):
        if rows % t == 0 and t * cols * 4 <= block_bytes:
            best = t
    return best or rows


def sum_rows_of(own, stacked, with_bf16=False):
    rows, cols = own.shape
    n = stacked.shape[0]
    tr = _pick_rows(rows, cols)

    def body(own_ref, st_ref, o_ref, *narrow):
        acc = own_ref[...]
        for k in range(n):
            acc = acc + st_ref[k].astype(F32)
        o_ref[...] = acc
        if with_bf16:
            narrow[0][...] = acc.astype(BF16)

    spec = pl.BlockSpec((tr, cols), lambda i: (i, 0))
    return pl.pallas_call(
        body, name="sum_rows_of",
        grid=(rows // tr,),
        in_specs=[spec, pl.BlockSpec((n, tr, cols), lambda i: (0, i, 0))],
        out_specs=[spec, spec] if with_bf16 else spec,
        out_shape=([jax.ShapeDtypeStruct((rows, cols), F32), jax.ShapeDtypeStruct((rows, cols), BF16)]
                   if with_bf16 else jax.ShapeDtypeStruct((rows, cols), F32)),
        compiler_params=_params("parallel"),
    )(own, stacked)


def sum_stack(stacked):
    n, rows, cols = stacked.shape
    tr = _pick_rows(rows, cols)

    def body(st_ref, o_ref):
        acc = st_ref[0]
        for k in range(1, n):
            acc = acc + st_ref[k]
        o_ref[...] = acc

    return pl.pallas_call(
        body, name="sum_stack",
        grid=(rows // tr,),
        in_specs=[pl.BlockSpec((n, tr, cols), lambda i: (0, i, 0))],
        out_specs=pl.BlockSpec((tr, cols), lambda i: (i, 0)),
        out_shape=jax.ShapeDtypeStruct((rows, cols), F32),
        compiler_params=_params("parallel"),
    )(stacked)


def adamw(w, ga, gb, m, v):
    rows, cols = w.shape
    tr = _pick_rows(rows, cols, 1024 * 1024)
    c1 = 1.0 - ADAM_B1 ** ADAM_STEP
    c2 = 1.0 - ADAM_B2 ** ADAM_STEP
    ins = [w, ga] + ([gb] if gb is not None else []) + [m, v]

    def body(*refs):
        w_ref, ga_ref = refs[0], refs[1]
        m_ref, v_ref, g_out, d_out, m_out, v_out = refs[-6:]
        g = ga_ref[...]
        if gb is not None:
            g = g + refs[2][...]
        mn = ADAM_B1 * m_ref[...] + (1.0 - ADAM_B1) * g
        vn = ADAM_B2 * v_ref[...] + (1.0 - ADAM_B2) * (g * g)
        m_hat = mn / c1
        v_hat = vn / c2
        g_out[...] = g
        d_out[...] = -ADAM_LR * (m_hat / (jnp.sqrt(v_hat) + ADAM_EPS) + ADAM_WD * w_ref[...])
        m_out[...] = mn
        v_out[...] = vn

    spec = pl.BlockSpec((tr, cols), lambda i: (i, 0))
    return pl.pallas_call(
        body, name="adamw",
        grid=(rows // tr,),
        in_specs=[spec] * len(ins),
        out_specs=[spec] * 4,
        out_shape=[jax.ShapeDtypeStruct((rows, cols), F32)] * 4,
        compiler_params=_params("parallel"),
    )(*ins)


WEIGHT_NAMES = ["ffn1_norm", "ffn1_w_gu", "ffn1_w_down", "mix_norm", "w_in", "q_lat_norm", "kv_lat_norm", "w_uq", "w_uk",
                "w_uv", "q_norm", "k_norm", "w_pool", "pool_scale", "w_out", "ffn2_norm", "ffn2_w_gu", "ffn2_w_down"]
COL_SHARDED = ["ffn1_w_gu", "w_in", "w_uq", "w_uk", "w_uv", "ffn2_w_gu"]
ROW_SHARDED = ["ffn1_w_down", "w_out", "ffn2_w_down"]
SHARDED = [n for n in WEIGHT_NAMES if n in COL_SHARDED or n in ROW_SHARDED]
REPLICATED = [n for n in WEIGHT_NAMES if n not in SHARDED]


def _join_chips(g, name):
    _, depth, r, c = g.shape
    if name in COL_SHARDED:
        return jnp.transpose(g, (1, 2, 0, 3)).reshape(depth, r, N_CHIPS * c)
    return jnp.transpose(g, (1, 0, 2, 3)).reshape(depth, N_CHIPS * r, c)


def _split_chips(full, name):
    depth, r, c = full.shape
    if name in COL_SHARDED:
        return jnp.transpose(full.reshape(depth, r, N_CHIPS, c // N_CHIPS), (0, 2, 1, 3))
    return full.reshape(depth, N_CHIPS, r // N_CHIPS, c)


def _pack(vs):
    flat = jnp.concatenate([v.reshape(-1) for v in vs])
    pad = (-flat.shape[0]) % (8 * 128)
    return jnp.pad(flat, (0, pad)).reshape(-1, 128)


def _unpack(packed, like):
    flat = packed.reshape(-1)
    out, at = [], 0
    for v in like:
        out.append(flat[at:at + v.size].reshape(v.shape))
        at += v.size
    return out


def kernel(x, ffn1_norm, ffn1_w_gu, ffn1_w_down, mix_norm, w_in, q_lat_norm, kv_lat_norm, w_uq, w_uk, w_uv, q_norm, k_norm, w_pool, pool_scale, w_out, ffn2_norm, ffn2_w_gu, ffn2_w_down, loss_target, m_ffn1_norm, m_ffn1_w_gu, m_ffn1_w_down, m_mix_norm, m_w_in, m_q_lat_norm, m_kv_lat_norm, m_w_uq, m_w_uk, m_w_uv, m_q_norm, m_k_norm, m_w_pool, m_pool_scale, m_w_out, m_ffn2_norm, m_ffn2_w_gu, m_ffn2_w_down, v_ffn1_norm, v_ffn1_w_gu, v_ffn1_w_down, v_mix_norm, v_w_in, v_q_lat_norm, v_kv_lat_norm, v_w_uq, v_w_uk, v_w_uv, v_q_norm, v_k_norm, v_w_pool, v_pool_scale, v_w_out, v_ffn2_norm, v_ffn2_w_gu, v_ffn2_w_down):
    args = locals()
    weights = {n: args[n] for n in WEIGHT_NAMES}
    moments_m = {n: args["m_" + n] for n in WEIGHT_NAMES}
    moments_v = {n: args["v_" + n] for n in WEIGHT_NAMES}

    chip = 2 * lax.axis_index("x") + lax.axis_index("y")
    core = lax.axis_index("c")
    shards = [weights[n].astype(BF16) for n in SHARDED]
    gathered = {n: lax.dynamic_update_index_in_dim(g, own, chip, 0)
                for n, own, g in zip(SHARDED, shards, gather_chips(shards))}
    full = {n: _join_chips(gathered[n], n) for n in SHARDED if "ffn" not in n}
    for n in REPLICATED:
        full[n] = weights[n].astype(BF16) if n == "w_pool" else weights[n]
    w = layout_weights(full)
    wd1 = _join_chips(gathered["ffn1_w_down"], "ffn1_w_down")
    wd2 = _join_chips(gathered["ffn2_w_down"], "ffn2_w_down")

    loss_part, grad_x, grads = local_step(x[0], loss_target[0], w, gathered["ffn1_w_gu"], wd1, gathered["ffn2_w_gu"], wd2)
    loss = lax.psum(loss_part[0, 0], ("x", "y", "c"))

    split = [grads[n] if n.endswith("w_gu") else _split_chips(grads[n], n) for n in SHARDED]
    half = split[0].shape[0] // 2

    def add(own, stacked, with_bf16=False):
        rows = own.size // own.shape[-1]
        total = sum_rows_of(own.reshape(rows, -1), stacked.reshape(stacked.shape[0], rows, -1), with_bf16)
        return [t.reshape(own.shape) for t in total] if with_bf16 else total.reshape(own.shape)

    from_sibling = swap_halves(split)
    chip_sums = [add(lax.dynamic_slice_in_dim(g4, core * half, half, axis=0), other[None], with_bf16=True)
                 for g4, other in zip(split, from_sibling)]
    landed = scatter_chips([narrow for _, narrow in chip_sums])
    totals = [add(lax.dynamic_index_in_dim(wide, chip, 1, keepdims=False), land)
              for (wide, _), land in zip(chip_sums, landed)]
    out = {}
    for n, mine, theirs in zip(SHARDED, totals, swap_sibling(totals)):
        g = jnp.concatenate([jnp.where(core == 0, mine, theirs), jnp.where(core == 0, theirs, mine)], axis=0)
        shape = weights[n].shape
        flat = lambda a: a.reshape(shape[0] * shape[1], shape[2])
        res = adamw(flat(weights[n]), flat(g), None, flat(moments_m[n]), flat(moments_v[n]))
        out[n] = [r.reshape(shape) for r in res]

    rep_g = [grads[n] for n in REPLICATED]
    total = sum_stack(gather_all(_pack(rep_g)))
    res = adamw(_pack([weights[n] for n in REPLICATED]), total, None,
                _pack([moments_m[n] for n in REPLICATED]), _pack([moments_v[n] for n in REPLICATED]))
    unpacked = [_unpack(r, rep_g) for r in res]
    for k, n in enumerate(REPLICATED):
        out[n] = [u[k] for u in unpacked]

    return (loss, grad_x[None], *[out[n][0] for n in WEIGHT_NAMES], *[out[n][1] for n in WEIGHT_NAMES],
            *[out[n][2] for n in WEIGHT_NAMES], *[out[n][3] for n in WEIGHT_NAMES])
```

```python
import jax
import jax.numpy as jnp
from jax import lax
from jax.experimental import pallas as pl
from jax.experimental.pallas import tpu as pltpu

F32 = jnp.float32
BF16 = jnp.bfloat16
MESH = pl.DeviceIdType.MESH

DEPTH = 4
D_MODEL = 1024
D_FF = 2816
FF_BLOCK = D_FF // 2
N_CHIPS = 4
HEADS = 8
HEAD_PAD = 128
QK_NOPE = 64
QK_ROPE = 32
QK_HEAD = QK_NOPE + QK_ROPE
V_HEAD = 64
Q_LORA = 384
KV_LORA = 256
POOL_W = 512
POOL_WINDOWS = (2, 4, 8, 16)
POOL_HALO = 8
GROUP = 128
D_IN = Q_LORA + KV_LORA + QK_ROPE + POOL_W
Z_W = 1280
Z_CKV = Q_LORA
Z_POOL = Q_LORA + KV_LORA
Z_KPE = Z_POOL + POOL_W
ATT_W = HEADS * HEAD_PAD
CAT_W = ATT_W + POOL_W
EPS = 1e-6
ROPE_THETA = 10000.0
ADAM_LR, ADAM_B1, ADAM_B2, ADAM_EPS, ADAM_WD, ADAM_STEP = 0.001, 0.9, 0.999, 1e-08, 0.01, 10
LOG2_E = 1.4426950408889634
LN_2 = 0.6931471805599453

VMEM_LIMIT = 56 * 1024 * 1024
VMEM_LIMIT_FFN_BWD = 62 * 1024 * 1024


def _params(*sem, vmem=VMEM_LIMIT):
    return pltpu.CompilerParams(dimension_semantics=sem, vmem_limit_bytes=vmem)


def _dot(a, b):
    return jnp.dot(a, b, preferred_element_type=F32)


def _dot_nt(a, b):
    return lax.dot_general(a, b, (((1,), (1,)), ((), ())), preferred_element_type=F32)


def _dot_tn(a, b):
    return lax.dot_general(a, b, (((0,), (0,)), ((), ())), preferred_element_type=F32)


def _norm_fwd(x, n):
    rstd = lax.rsqrt(jnp.sum(x * x, axis=-1, keepdims=True) * (1.0 / n) + EPS)
    return x * rstd, rstd


def _norm_bwd(dy, g, xhat, rstd, n):
    dxhat = dy * g
    dx = rstd * (dxhat - xhat * (jnp.sum(dxhat * xhat, axis=-1, keepdims=True) * (1.0 / n)))
    return dx, dy * xhat


def _row_tile(s, want):
    t = min(s, want)
    assert s % t == 0
    return t


def ffn_fwd(x, gn, wgu4, wd, layer, tm=512):
    s = x.shape[0]
    tm = _row_tile(s, tm)

    def body(x_ref, gn_ref, wg_ref, wu_ref, wd_ref, o_ref, g_ref, u_ref, h_s, acc_s):
        j = pl.program_id(1)

        @pl.when(j == 0)
        def _():
            xhat, _ = _norm_fwd(x_ref[...], D_MODEL)
            h_s[...] = (xhat * gn_ref[...]).astype(BF16)
            acc_s[...] = jnp.zeros_like(acc_s)

        h = h_s[...]
        g = _dot(h, wg_ref[...])
        u = _dot(h, wu_ref[...])
        g_ref[...] = g.astype(BF16)
        u_ref[...] = u.astype(BF16)
        a = (g * jax.nn.sigmoid(g) * u).astype(BF16)
        acc_s[...] += _dot(a, wd_ref[...])

        @pl.when(j == 1)
        def _():
            o_ref[...] = x_ref[...] + 0.5 * acc_s[...]

    return pl.pallas_call(
        body, name="ffn_fwd",
        grid=(s // tm, 2),
        in_specs=[
            pl.BlockSpec((tm, D_MODEL), lambda i, j: (i, 0)),
            pl.BlockSpec((1, D_MODEL), lambda i, j: (0, 0)),
            pl.BlockSpec((None, None, D_MODEL, FF_BLOCK), lambda i, j: (j, layer, 0, 0)),
            pl.BlockSpec((None, None, D_MODEL, FF_BLOCK), lambda i, j: (j + 2, layer, 0, 0)),
            pl.BlockSpec((None, FF_BLOCK, D_MODEL), lambda i, j: (layer, j, 0)),
        ],
        out_specs=[pl.BlockSpec((tm, D_MODEL), lambda i, j: (i, 0)),
                   pl.BlockSpec((tm, FF_BLOCK), lambda i, j: (i, j)), pl.BlockSpec((tm, FF_BLOCK), lambda i, j: (i, j))],
        out_shape=[jax.ShapeDtypeStruct((s, D_MODEL), F32), jax.ShapeDtypeStruct((s, D_FF), BF16),
                   jax.ShapeDtypeStruct((s, D_FF), BF16)],
        scratch_shapes=[pltpu.VMEM((tm, D_MODEL), BF16), pltpu.VMEM((tm, D_MODEL), F32)],
        compiler_params=_params("parallel", "arbitrary"),
    )(x, gn, wgu4, wgu4, wd)


def ffn_bwd(x, dy, gn, g_pre, u_pre, wgu4, wd, layer, tm=512):
    s = x.shape[0]
    tm = _row_tile(s, tm)
    halves = 2 if tm % 32 == 0 else 1

    def body(x_ref, dy_ref, gn_ref, g_ref, u_ref, wg_ref, wu_ref, wd_ref,
             dx_ref, h_ref, a_ref, dg_ref, du_ref, dyh_ref, dgn_ref, dh_s):
        i, j = pl.program_id(0), pl.program_id(1)

        @pl.when(j == 0)
        def _():
            xhat, _ = _norm_fwd(x_ref[...], D_MODEL)
            h_ref[...] = (xhat * gn_ref[...]).astype(BF16)
            dyh_ref[...] = (0.5 * dy_ref[...]).astype(BF16)
            dh_s[...] = jnp.zeros_like(dh_s)

        for r in range(halves):
            rows = slice(r * (tm // halves), (r + 1) * (tm // halves))
            g = g_ref[rows, :].astype(F32)
            u = u_ref[rows, :].astype(F32)
            sg = jax.nn.sigmoid(g)
            silu = g * sg
            a_ref[rows, :] = (silu * u).astype(BF16)
            da = _dot_nt(dyh_ref[rows, :], wd_ref[...])
            dg = (da * u * (sg * (1.0 + g * (1.0 - sg)))).astype(BF16)
            du = (da * silu).astype(BF16)
            dg_ref[rows, :] = dg
            du_ref[rows, :] = du
            dh_s[rows, :] += _dot_nt(dg, wg_ref[...]) + _dot_nt(du, wu_ref[...])

        @pl.when(j == 1)
        def _():
            xhat, rstd = _norm_fwd(x_ref[...], D_MODEL)
            dxn, dgrow = _norm_bwd(dh_s[...], gn_ref[...], xhat, rstd, D_MODEL)
            dx_ref[...] = dy_ref[...] + dxn
            part = jnp.sum(dgrow, axis=0, keepdims=True)

            @pl.when(i == 0)
            def _():
                dgn_ref[...] = part

            @pl.when(i > 0)
            def _():
                dgn_ref[...] += part

    row = lambda i, j: (i, 0)
    return pl.pallas_call(
        body, name="ffn_bwd",
        grid=(s // tm, 2),
        in_specs=[
            pl.BlockSpec((tm, D_MODEL), row),
            pl.BlockSpec((tm, D_MODEL), row),
            pl.BlockSpec((1, D_MODEL), lambda i, j: (0, 0)),
            pl.BlockSpec((tm, FF_BLOCK), lambda i, j: (i, j)),
            pl.BlockSpec((tm, FF_BLOCK), lambda i, j: (i, j)),
            pl.BlockSpec((None, None, D_MODEL, FF_BLOCK), lambda i, j: (j, layer, 0, 0)),
            pl.BlockSpec((None, None, D_MODEL, FF_BLOCK), lambda i, j: (j + 2, layer, 0, 0)),
            pl.BlockSpec((None, FF_BLOCK, D_MODEL), lambda i, j: (layer, j, 0)),
        ],
        out_specs=[
            pl.BlockSpec((tm, D_MODEL), row),
            pl.BlockSpec((tm, D_MODEL), row),
            pl.BlockSpec((tm, FF_BLOCK), lambda i, j: (i, j)),
            pl.BlockSpec((tm, FF_BLOCK), lambda i, j: (i, j)),
            pl.BlockSpec((tm, FF_BLOCK), lambda i, j: (i, j)),
            pl.BlockSpec((tm, D_MODEL), row),
            pl.BlockSpec((1, D_MODEL), lambda i, j: (0, 0)),
        ],
        out_shape=[
            jax.ShapeDtypeStruct((s, D_MODEL), F32),
            jax.ShapeDtypeStruct((s, D_MODEL), BF16),
            jax.ShapeDtypeStruct((s, D_FF), BF16),
            jax.ShapeDtypeStruct((s, D_FF), BF16),
            jax.ShapeDtypeStruct((s, D_FF), BF16),
            jax.ShapeDtypeStruct((s, D_MODEL), BF16),
            jax.ShapeDtypeStruct((1, D_MODEL), F32),
        ],
        scratch_shapes=[pltpu.VMEM((tm, D_MODEL), F32)],
        compiler_params=_params("arbitrary", "arbitrary", vmem=VMEM_LIMIT_FFN_BWD),
    )(x, dy, gn, g_pre, u_pre, wgu4, wgu4, wd)


def matmul_tn(a, b, tk=1024, tn_max=1536, into=None, at=None):
    s, m = a.shape
    n = b.shape[1]
    tk = _row_tile(s, tk)
    tn = n
    if n > tn_max:
        tn = n // 2
    assert n % tn == 0 and tn % 128 == 0
    nk = s // tk

    def body(a_ref, b_ref, *rest):
        o_ref, acc_s = rest[-2:]
        k = pl.program_id(1)

        @pl.when(k == 0)
        def _():
            acc_s[...] = jnp.zeros_like(acc_s)

        acc_s[...] += _dot_tn(a_ref[...].astype(BF16), b_ref[...].astype(BF16))

        @pl.when(k == nk - 1)
        def _():
            o_ref[...] = acc_s[...]

    in_specs = [pl.BlockSpec((tk, m), lambda j, k: (k, 0)), pl.BlockSpec((tk, tn), lambda j, k: (k, j))]
    if into is None:
        operands, aliases = (a, b), {}
        out_specs = pl.BlockSpec((m, tn), lambda j, k: (0, j))
        out_shape = jax.ShapeDtypeStruct((m, n), F32)
    else:
        assert into.shape[-2:] == (m, tn) and into.dtype == F32
        lead = into.ndim - 2
        operands, aliases = (a, b, into), {2: 0}
        in_specs.append(pl.BlockSpec(memory_space=pl.ANY))
        out_specs = pl.BlockSpec((None,) * lead + (m, tn), lambda j, k: tuple(at(j)) + (0, 0))
        out_shape = jax.ShapeDtypeStruct(into.shape, F32)
    return pl.pallas_call(
        body, name="matmul_tn",
        grid=(n // tn, nk),
        in_specs=in_specs,
        out_specs=out_specs,
        out_shape=out_shape,
        scratch_shapes=[pltpu.VMEM((m, tn), F32)],
        input_output_aliases=aliases,
        compiler_params=_params("parallel", "arbitrary"),
    )(*operands)


def _rope_fwd(y, c, s1, s2):
    return y * c + pltpu.roll(y, HEAD_PAD - 16, 1) * s1 + pltpu.roll(y, 16, 1) * s2


def _rope_bwd(dy, c, s1, s2):
    return dy * c + pltpu.roll(dy * s1, 16, 1) + pltpu.roll(dy * s2, HEAD_PAD - 16, 1)


def rope_tables(s):
    pos = jnp.arange(s, dtype=F32)
    inv = ROPE_THETA ** (-jnp.arange(0, QK_ROPE, 2, dtype=F32) / QK_ROPE)
    ang = pos[:, None] * inv[None, :]
    cos, sin = jnp.cos(ang), jnp.sin(ang)
    z16 = jnp.zeros((s, 16), F32)
    c = jnp.concatenate([jnp.ones((s, QK_NOPE), F32), cos, cos, z16, z16], axis=1)
    s1 = jnp.concatenate([jnp.zeros((s, QK_NOPE), F32), -sin, z16, z16, z16], axis=1)
    s2 = jnp.concatenate([jnp.zeros((s, QK_NOPE), F32), z16, sin, z16, z16], axis=1)
    return c, s1, s2


def mix_in_fwd(x, gm, win, qln, kvln, wuq, wuk, wuv, qn, kn, rc, rs1, rs2, tm=512):
    s = x.shape[0]
    tm = _row_tile(s, tm)
    scale = QK_HEAD ** -0.5 * LOG2_E

    def body(x_ref, gm_ref, win_ref, qln_ref, kvln_ref, wuq_ref, wuk_ref, wuv_ref, qn_ref, kn_ref,
             rc_ref, rs1_ref, rs2_ref, z_ref, q_ref, k_ref, v_ref, vt_ref):
        xhat, _ = _norm_fwd(x_ref[...], D_MODEL)
        h = (xhat * gm_ref[...]).astype(BF16)
        z = _dot(h, win_ref[...])
        z_ref[...] = z
        cq, _ = _norm_fwd(z[:, :Q_LORA], Q_LORA)
        cqn = (cq * qln_ref[...]).astype(BF16)
        ckv, _ = _norm_fwd(z[:, Z_CKV:Z_POOL], KV_LORA)
        ckvn = (ckv * kvln_ref[...]).astype(BF16)
        kpe = z[:, Z_KPE:]
        q_raw = _dot(cqn, wuq_ref[...])
        k_raw = _dot(ckvn, wuk_ref[...])
        v = _dot(ckvn, wuv_ref[...])
        v_ref[...] = v.astype(BF16)
        vt_ref[...] = jnp.transpose(v).astype(BF16)
        c, s1, s2 = rc_ref[...], rs1_ref[...], rs2_ref[...]
        for hd in range(HEADS):
            lanes = slice(hd * HEAD_PAD, (hd + 1) * HEAD_PAD)
            qh, _ = _norm_fwd(q_raw[:, lanes], QK_HEAD)
            q_ref[:, lanes] = (_rope_fwd(qh * qn_ref[...], c, s1, s2) * scale).astype(BF16)
            kh, _ = _norm_fwd(k_raw[:, lanes] + kpe, QK_HEAD)
            k_ref[:, lanes] = _rope_fwd(kh * kn_ref[...], c, s1, s2).astype(BF16)

    full = lambda shape: pl.BlockSpec(shape, lambda i: (0,) * len(shape))
    row = lambda w: pl.BlockSpec((tm, w), lambda i: (i, 0))
    return pl.pallas_call(
        body, name="mix_in_fwd",
        grid=(s // tm,),
        in_specs=[row(D_MODEL), full((1, D_MODEL)), full((D_MODEL, Z_W)), full((1, Q_LORA)), full((1, KV_LORA)),
                  full((Q_LORA, ATT_W)), full((KV_LORA, ATT_W)), full((KV_LORA, ATT_W)),
                  full((1, HEAD_PAD)), full((1, HEAD_PAD)), row(HEAD_PAD), row(HEAD_PAD), row(HEAD_PAD)],
        out_specs=[row(Z_W), row(ATT_W), row(ATT_W), row(ATT_W), pl.BlockSpec((ATT_W, tm), lambda i: (0, i))],
        out_shape=[jax.ShapeDtypeStruct((s, Z_W), F32)] + [jax.ShapeDtypeStruct((s, ATT_W), BF16)] * 3 + [
            jax.ShapeDtypeStruct((ATT_W, s), BF16)],
        compiler_params=_params("parallel"),
    )(x, gm, win, qln, kvln, wuq, wuk, wuv, qn, kn, rc, rs1, rs2)


def mla_bwd(z, dq_t, dk, dv, dp, qln, kvln, wuq, wuk, wuv, qn, kn, rc, rs1, rs2, tm=512):
    s = z.shape[0]
    tm = _row_tile(s, tm)
    scale = QK_HEAD ** -0.5

    def body(z_ref, dqt_ref, dk_ref, dv_ref, dp_ref, qln_ref, kvln_ref, wuq_ref, wuk_ref, wuv_ref, qn_ref, kn_ref,
             rc_ref, rs1_ref, rs2_ref,
             dz_ref, cqn_ref, ckvn_ref, dqr_ref, dkr_ref, dvb_ref, dqln_ref, dkvln_ref, dqn_ref, dkn_ref):
        i = pl.program_id(0)
        z = z_ref[...]
        cq, cq_rstd = _norm_fwd(z[:, :Q_LORA], Q_LORA)
        cqn = (cq * qln_ref[...]).astype(BF16)
        ckv, ckv_rstd = _norm_fwd(z[:, Z_CKV:Z_POOL], KV_LORA)
        ckvn = (ckv * kvln_ref[...]).astype(BF16)
        kpe = z[:, Z_KPE:]
        cqn_ref[...] = cqn
        ckvn_ref[...] = ckvn
        q_raw = _dot(cqn, wuq_ref[...])
        k_raw = _dot(ckvn, wuk_ref[...])
        c, s1, s2 = rc_ref[...], rs1_ref[...], rs2_ref[...]
        lane = lax.broadcasted_iota(jnp.int32, (tm, HEAD_PAD), 1)
        rope_lanes = (lane >= QK_NOPE) & (lane < QK_HEAD)
        dkpe = jnp.zeros((tm, HEAD_PAD), F32)
        dqn = jnp.zeros((tm, HEAD_PAD), F32)
        dkn = jnp.zeros((tm, HEAD_PAD), F32)
        for hd in range(HEADS):
            lanes = slice(hd * HEAD_PAD, (hd + 1) * HEAD_PAD)
            qh, q_rstd = _norm_fwd(q_raw[:, lanes], QK_HEAD)
            dqh = jnp.transpose(dqt_ref[lanes, :]) * scale
            dqr, dg = _norm_bwd(_rope_bwd(dqh, c, s1, s2), qn_ref[...], qh, q_rstd, QK_HEAD)
            dqn += dg
            dqr_ref[:, lanes] = dqr.astype(BF16)
            kh, k_rstd = _norm_fwd(k_raw[:, lanes] + kpe, QK_HEAD)
            dkr, dg = _norm_bwd(_rope_bwd(dk_ref[:, lanes], c, s1, s2), kn_ref[...], kh, k_rstd, QK_HEAD)
            dkn += dg
            dkr_ref[:, lanes] = dkr.astype(BF16)
            dkpe += jnp.where(rope_lanes, dkr, 0.0)
        dvb = dv_ref[...].astype(BF16)
        dvb_ref[...] = dvb
        dcqn = _dot_nt(dqr_ref[...], wuq_ref[...])
        dckvn = _dot_nt(dkr_ref[...], wuk_ref[...]) + _dot_nt(dvb, wuv_ref[...])
        dcq, dqln = _norm_bwd(dcqn, qln_ref[...], cq, cq_rstd, Q_LORA)
        dckv, dkvln = _norm_bwd(dckvn, kvln_ref[...], ckv, ckv_rstd, KV_LORA)
        dz_ref[:, :Q_LORA] = dcq.astype(BF16)
        dz_ref[:, Z_CKV:Z_POOL] = dckv.astype(BF16)
        dz_ref[:, Z_POOL:Z_KPE] = dp_ref[...].astype(BF16)
        dz_ref[:, Z_KPE:] = dkpe.astype(BF16)
        parts = [(dqln_ref, dqln), (dkvln_ref, dkvln), (dqn_ref, dqn), (dkn_ref, dkn)]

        @pl.when(i == 0)
        def _():
            for ref, val in parts:
                ref[...] = jnp.sum(val, axis=0, keepdims=True)

        @pl.when(i > 0)
        def _():
            for ref, val in parts:
                ref[...] += jnp.sum(val, axis=0, keepdims=True)

    full = lambda shape: pl.BlockSpec(shape, lambda i: (0,) * len(shape))
    row = lambda w: pl.BlockSpec((tm, w), lambda i: (i, 0))
    return pl.pallas_call(
        body, name="mla_bwd",
        grid=(s // tm,),
        in_specs=[row(Z_W), pl.BlockSpec((ATT_W, tm), lambda i: (0, i)), row(ATT_W), row(ATT_W), row(POOL_W),
                  full((1, Q_LORA)), full((1, KV_LORA)), full((Q_LORA, ATT_W)), full((KV_LORA, ATT_W)),
                  full((KV_LORA, ATT_W)), full((1, HEAD_PAD)), full((1, HEAD_PAD)),
                  row(HEAD_PAD), row(HEAD_PAD), row(HEAD_PAD)],
        out_specs=[row(Z_W), row(Q_LORA), row(KV_LORA), row(ATT_W), row(ATT_W), row(ATT_W),
                   full((1, Q_LORA)), full((1, KV_LORA)), full((1, HEAD_PAD)), full((1, HEAD_PAD))],
        out_shape=[jax.ShapeDtypeStruct((s, Z_W), BF16), jax.ShapeDtypeStruct((s, Q_LORA), BF16),
                   jax.ShapeDtypeStruct((s, KV_LORA), BF16)] + [jax.ShapeDtypeStruct((s, ATT_W), BF16)] * 3 + [
                   jax.ShapeDtypeStruct((1, Q_LORA), F32), jax.ShapeDtypeStruct((1, KV_LORA), F32),
                   jax.ShapeDtypeStruct((1, HEAD_PAD), F32), jax.ShapeDtypeStruct((1, HEAD_PAD), F32)],
        compiler_params=_params("arbitrary"),
    )(z, dq_t, dk, dv, dp, qln, kvln, wuq, wuk, wuv, qn, kn, rc, rs1, rs2)


def mix_in_bwd(x, dx_in, dz, gm, win, tm=512):
    s = x.shape[0]
    tm = _row_tile(s, tm)

    def body(x_ref, dxin_ref, dz_ref, gm_ref, win_ref, dx_ref, h_ref, dgm_ref):
        i = pl.program_id(0)
        xhat, rstd = _norm_fwd(x_ref[...], D_MODEL)
        h_ref[...] = (xhat * gm_ref[...]).astype(BF16)
        dh = _dot_nt(dz_ref[...], win_ref[...])
        dxn, dgrow = _norm_bwd(dh, gm_ref[...], xhat, rstd, D_MODEL)
        dx_ref[...] = dxin_ref[...] + dxn
        part = jnp.sum(dgrow, axis=0, keepdims=True)

        @pl.when(i == 0)
        def _():
            dgm_ref[...] = part

        @pl.when(i > 0)
        def _():
            dgm_ref[...] += part

    full = lambda shape: pl.BlockSpec(shape, lambda i: (0,) * len(shape))
    row = lambda w: pl.BlockSpec((tm, w), lambda i: (i, 0))
    return pl.pallas_call(
        body, name="mix_in_bwd",
        grid=(s // tm,),
        in_specs=[row(D_MODEL), row(D_MODEL), row(Z_W), full((1, D_MODEL)), full((D_MODEL, Z_W))],
        out_specs=[row(D_MODEL), row(D_MODEL), full((1, D_MODEL))],
        out_shape=[jax.ShapeDtypeStruct((s, D_MODEL), F32), jax.ShapeDtypeStruct((s, D_MODEL), BF16),
                   jax.ShapeDtypeStruct((1, D_MODEL), F32)],
        compiler_params=_params("arbitrary"),
    )(x, dx_in, dz, gm, win)


ATT_CHUNK = 512


def flash_fwd(q, k, vt, tq=1024, tk=8192, chunk=ATT_CHUNK):
    s = q.shape[0]
    tq, tk = _row_tile(s, tq), _row_tile(s, tk)
    nk = s // tk
    tc = _row_tile(tk, chunk)
    sub = tk // tc

    def body(q_ref, k_ref, vt_ref, o_ref, lse_ref, m_s, l_s, acc_s):
        kk = pl.program_id(2)

        @pl.when(kk == 0)
        def _():
            m_s[...] = jnp.full_like(m_s, -jnp.inf)
            l_s[...] = jnp.zeros_like(l_s)
            acc_s[...] = jnp.zeros_like(acc_s)

        q_t = q_ref[...]

        def scores(c):
            return _dot_nt(k_ref[c * tc:(c + 1) * tc, :], q_t)

        m, l, acc = m_s[...], l_s[...], acc_s[...]
        s_next = scores(0)
        for c in range(sub):
            s_t = s_next
            if c + 1 < sub:
                s_next = scores(c + 1)
            m_new = jnp.maximum(m, jnp.max(s_t, axis=0, keepdims=True))
            alpha = jnp.exp2(m - m_new)
            p_t = jnp.exp2(s_t - m_new)
            l = alpha * l + jnp.sum(p_t, axis=0, keepdims=True)
            acc = alpha * acc + _dot(vt_ref[:, c * tc:(c + 1) * tc], p_t.astype(BF16))
            m = m_new
        m_s[...], l_s[...], acc_s[...] = m, l, acc

        @pl.when(kk == nk - 1)
        def _():
            o_ref[...] = jnp.transpose(acc_s[...] / l_s[...]).astype(BF16)
            lse_ref[...] = m_s[...] + jnp.log2(l_s[...])

    return pl.pallas_call(
        body, name="flash_fwd",
        grid=(HEADS, s // tq, nk),
        in_specs=[pl.BlockSpec((tq, HEAD_PAD), lambda h, i, kk: (i, h)),
                  pl.BlockSpec((tk, HEAD_PAD), lambda h, i, kk: (kk, h)),
                  pl.BlockSpec((HEAD_PAD, tk), lambda h, i, kk: (h, kk))],
        out_specs=[pl.BlockSpec((tq, HEAD_PAD), lambda h, i, kk: (i, h)),
                   pl.BlockSpec((None, 1, tq), lambda h, i, kk: (h, 0, i))],
        out_shape=[jax.ShapeDtypeStruct((s, ATT_W), BF16), jax.ShapeDtypeStruct((HEADS, 1, s), F32)],
        scratch_shapes=[pltpu.VMEM((1, tq), F32), pltpu.VMEM((1, tq), F32), pltpu.VMEM((HEAD_PAD, tq), F32)],
        compiler_params=_params("parallel", "parallel", "arbitrary"),
    )(q, k, vt)


def flash_bwd(q, k, v, do, lse_row, delta_row, tq=8192, tk=1024, chunk=ATT_CHUNK):
    s = q.shape[0]
    tq, tk = _row_tile(s, tq), _row_tile(s, tk)
    nq = s // tq
    tc = _row_tile(tq, chunk)
    sub = tq // tc

    def body(q_ref, k_ref, v_ref, do_ref, lse_ref, dl_ref, dqt_ref, dk_ref, dv_ref, kt_s, dk_s, dv_s):
        kb, qb = pl.program_id(1), pl.program_id(2)

        @pl.when(qb == 0)
        def _():
            kt_s[...] = jnp.transpose(k_ref[...].astype(F32)).astype(BF16)
            dk_s[...] = jnp.zeros_like(dk_s)
            dv_s[...] = jnp.zeros_like(dv_s)

        k_t, v_t = k_ref[...], v_ref[...]

        def products(c):
            rows = slice(c * tc, (c + 1) * tc)
            return _dot_nt(k_t, q_ref[rows, :]), _dot_nt(v_t, do_ref[rows, :])

        dk, dv = dk_s[...], dv_s[...]
        nxt = products(0)
        contribs = []
        for c in range(sub):
            rows = slice(c * tc, (c + 1) * tc)
            s_t, dp_t = nxt
            if c + 1 < sub:
                nxt = products(c + 1)
            p_t = jnp.exp2(s_t - lse_ref[:, rows])
            ds_t = (p_t * (dp_t - dl_ref[:, rows])).astype(BF16)
            dv = dv + _dot(p_t.astype(BF16), do_ref[rows, :])
            dk = dk + _dot(ds_t, q_ref[rows, :])
            contribs.append(_dot(kt_s[...], ds_t))
        dk_s[...], dv_s[...] = dk, dv
        contrib = jnp.concatenate(contribs, axis=1) if sub > 1 else contribs[0]
        cols = pl.ds(pl.multiple_of(qb * tq, tq), tq)

        @pl.when(kb == 0)
        def _():
            dqt_ref[:, cols] = contrib

        @pl.when(kb > 0)
        def _():
            dqt_ref[:, cols] += contrib

        @pl.when(qb == nq - 1)
        def _():
            dk_ref[...] = dk_s[...] * LN_2
            dv_ref[...] = dv_s[...]

    qspec = pl.BlockSpec((tq, HEAD_PAD), lambda h, kb, qb: (qb, h))
    kspec = pl.BlockSpec((tk, HEAD_PAD), lambda h, kb, qb: (kb, h))
    rowspec = pl.BlockSpec((None, 1, tq), lambda h, kb, qb: (h, 0, qb))
    return pl.pallas_call(
        body, name="flash_bwd",
        grid=(HEADS, s // tk, nq),
        in_specs=[qspec, kspec, kspec, qspec, rowspec, rowspec],
        out_specs=[pl.BlockSpec((HEAD_PAD, s), lambda h, kb, qb: (h, 0)), kspec, kspec],
        out_shape=[jax.ShapeDtypeStruct((ATT_W, s), F32), jax.ShapeDtypeStruct((s, ATT_W), F32),
                   jax.ShapeDtypeStruct((s, ATT_W), F32)],
        scratch_shapes=[pltpu.VMEM((HEAD_PAD, tk), BF16), pltpu.VMEM((tk, HEAD_PAD), F32),
                        pltpu.VMEM((tk, HEAD_PAD), F32)],
        compiler_params=_params("parallel", "arbitrary", "arbitrary"),
    )(q, k, v, do, lse_row, delta_row)


def _halo_specs(tm, w, n_tiles):
    per = tm // POOL_HALO
    last = n_tiles * per - 1
    return [pl.BlockSpec((POOL_HALO, w), lambda i: (jnp.maximum(i * per - 1, 0), 0)),
            pl.BlockSpec((tm, w), lambda i: (i, 0)),
            pl.BlockSpec((POOL_HALO, w), lambda i: (jnp.minimum((i + 1) * per, last), 0))]


def _fill_ext(ext_ref, prev, cur, nxt, i, n_tiles, tm):
    ext_ref[pl.ds(0, POOL_HALO), :] = jnp.where(i > 0, prev, 0.0)
    ext_ref[pl.ds(POOL_HALO, tm), :] = cur
    ext_ref[pl.ds(POOL_HALO + tm, POOL_HALO), :] = jnp.where(i < n_tiles - 1, nxt, 0.0)


def _window_sum(ext_ref, g, lo, hi, tm):
    lanes = pl.ds(g * GROUP, GROUP)
    acc = ext_ref[pl.ds(POOL_HALO + lo, tm), lanes]
    for d in range(lo + 1, hi + 1):
        acc = acc + ext_ref[pl.ds(POOL_HALO + d, tm), lanes]
    return acc


def _pool_mixed(ext_ref, cur, i, s, tm):
    row = i * tm + lax.broadcasted_iota(jnp.int32, (tm, 1), 0)
    out = []
    for g, w in enumerate(POOL_WINDOWS):
        left = w // 2
        right = w - 1 - left
        cnt = (jnp.minimum(row + right + 1, s) - jnp.maximum(row - left, 0)).astype(F32)
        out.append(_window_sum(ext_ref, g, -left, right, tm) / cnt - cur[:, g * GROUP:(g + 1) * GROUP])
    return out


def pool_out_fwd(x, z, o, wpool, pscale, wout, tm=512):
    s = x.shape[0]
    tm = _row_tile(s, tm)
    n_tiles = s // tm

    def body(x_ref, zp_ref, z_ref, zn_ref, o_ref, wp_ref, ps_ref, wo_ref, y_ref, cat_ref, ext_s):
        i = pl.program_id(0)
        cur = z_ref[:, Z_POOL:Z_KPE]
        _fill_ext(ext_s, zp_ref[:, Z_POOL:Z_KPE], cur, zn_ref[:, Z_POOL:Z_KPE], i, n_tiles, tm)
        mixed = _pool_mixed(ext_s, cur, i, s, tm)
        cat_ref[:, :ATT_W] = o_ref[...]
        for g in range(len(POOL_WINDOWS)):
            lanes = slice(g * GROUP, (g + 1) * GROUP)
            yg = _dot(mixed[g].astype(BF16), wp_ref[g])
            cat_ref[:, ATT_W + g * GROUP:ATT_W + (g + 1) * GROUP] = (yg * ps_ref[:, lanes]).astype(BF16)
        y_ref[...] = x_ref[...] + _dot(cat_ref[...], wo_ref[...])

    full = lambda shape: pl.BlockSpec(shape, lambda i: (0,) * len(shape))
    row = lambda w: pl.BlockSpec((tm, w), lambda i: (i, 0))
    return pl.pallas_call(
        body, name="pool_out_fwd",
        grid=(n_tiles,),
        in_specs=[row(D_MODEL)] + _halo_specs(tm, Z_W, n_tiles) + [
            row(ATT_W), full((len(POOL_WINDOWS), GROUP, GROUP)), full((1, POOL_W)), full((CAT_W, D_MODEL))],
        out_specs=[row(D_MODEL), row(CAT_W)],
        out_shape=[jax.ShapeDtypeStruct((s, D_MODEL), F32), jax.ShapeDtypeStruct((s, CAT_W), BF16)],
        scratch_shapes=[pltpu.VMEM((tm + 2 * POOL_HALO, POOL_W), F32)],
        compiler_params=_params("parallel"),
    )(x, z, z, z, o, wpool, pscale, wout)


def out_bwd(dx, wout, cat, tm=512):
    s = dx.shape[0]
    tm = _row_tile(s, tm)

    def body(dx_ref, wo_ref, cat_ref, do_ref, db_ref, dl_ref):
        dcat = _dot_nt(dx_ref[...].astype(BF16), wo_ref[...])
        do_ref[...] = dcat[:, :ATT_W].astype(BF16)
        db_ref[...] = dcat[:, ATT_W:]
        for hd in range(HEADS):
            lanes = slice(hd * HEAD_PAD, (hd + 1) * HEAD_PAD)
            dl_ref[hd] = jnp.sum(dcat[:, lanes] * cat_ref[:, lanes].astype(F32), axis=-1, keepdims=True)

    row = lambda w: pl.BlockSpec((tm, w), lambda i: (i, 0))
    return pl.pallas_call(
        body, name="out_bwd",
        grid=(s // tm,),
        in_specs=[row(D_MODEL), pl.BlockSpec((CAT_W, D_MODEL), lambda i: (0, 0)), row(CAT_W)],
        out_specs=[row(ATT_W), row(POOL_W), pl.BlockSpec((HEADS, tm, 1), lambda i: (0, i, 0))],
        out_shape=[jax.ShapeDtypeStruct((s, ATT_W), BF16), jax.ShapeDtypeStruct((s, POOL_W), F32),
                   jax.ShapeDtypeStruct((HEADS, s, 1), F32)],
        compiler_params=_params("parallel"),
    )(dx, wout, cat)


def pool_bwd(db, z, wpool, pscale, tm=512):
    s = db.shape[0]
    tm = _row_tile(s, tm)
    n_tiles = s // tm
    te = tm + 2 * POOL_HALO

    def body(dbp_ref, db_ref, dbn_ref, zp_ref, z_ref, zn_ref, wp_ref, ps_ref,
             dp_ref, mixed_ref, dys_ref, dps_ref, ext_s, text_s):
        i = pl.program_id(0)
        cur = z_ref[:, Z_POOL:Z_KPE]
        _fill_ext(ext_s, zp_ref[:, Z_POOL:Z_KPE], cur, zn_ref[:, Z_POOL:Z_KPE], i, n_tiles, tm)
        mixed = _pool_mixed(ext_s, cur, i, s, tm)
        _fill_ext(ext_s, dbp_ref[...], db_ref[...], dbn_ref[...], i, n_tiles, tm)
        erow = i * tm - POOL_HALO + lax.broadcasted_iota(jnp.int32, (te, 1), 0)
        dps = []
        for g, w in enumerate(POOL_WINDOWS):
            lanes = slice(g * GROUP, (g + 1) * GROUP)
            left = w // 2
            right = w - 1 - left
            mb = mixed[g].astype(BF16)
            mixed_ref[:, lanes] = mb
            dps.append(jnp.sum(db_ref[:, lanes] * _dot(mb, wp_ref[g]), axis=0, keepdims=True))
            dys_e = (ext_s[:, lanes] * ps_ref[:, lanes]).astype(BF16)
            dys_ref[:, lanes] = (db_ref[:, lanes] * ps_ref[:, lanes]).astype(BF16)
            dmix_e = _dot_nt(dys_e, wp_ref[g])
            cnt = jnp.minimum(erow + right + 1, s) - jnp.maximum(erow - left, 0)
            text_s[:, lanes] = dmix_e / jnp.maximum(cnt, 1).astype(F32)
            dp_ref[:, lanes] = _window_sum(text_s, g, -right, left, tm) - dmix_e[POOL_HALO:POOL_HALO + tm]
        part = jnp.concatenate(dps, axis=1)

        @pl.when(i == 0)
        def _():
            dps_ref[...] = part

        @pl.when(i > 0)
        def _():
            dps_ref[...] += part

    full = lambda shape: pl.BlockSpec(shape, lambda i: (0,) * len(shape))
    row = lambda w: pl.BlockSpec((tm, w), lambda i: (i, 0))
    return pl.pallas_call(
        body, name="pool_bwd",
        grid=(n_tiles,),
        in_specs=_halo_specs(tm, POOL_W, n_tiles) + _halo_specs(tm, Z_W, n_tiles) + [
            full((len(POOL_WINDOWS), GROUP, GROUP)), full((1, POOL_W))],
        out_specs=[row(POOL_W), row(POOL_W), row(POOL_W), full((1, POOL_W))],
        out_shape=[jax.ShapeDtypeStruct((s, POOL_W), F32), jax.ShapeDtypeStruct((s, POOL_W), BF16),
                   jax.ShapeDtypeStruct((s, POOL_W), BF16), jax.ShapeDtypeStruct((1, POOL_W), F32)],
        scratch_shapes=[pltpu.VMEM((te, POOL_W), F32), pltpu.VMEM((te, POOL_W), F32)],
        compiler_params=_params("arbitrary"),
    )(db, db, db, z, z, z, wpool, pscale)


def loss_head(y, target, tm=512):
    s = y.shape[0]
    tm = _row_tile(s, tm)

    def body(y_ref, t_ref, dy_ref, loss_ref):
        i = pl.program_id(0)
        err = y_ref[...] - t_ref[...]
        dy_ref[...] = err * (1.0 / D_MODEL)
        part = 0.5 * jnp.sum(jnp.sum(err * err, axis=-1, keepdims=True) * (1.0 / D_MODEL), axis=0, keepdims=True)
        part = jnp.broadcast_to(part, (1, 128))

        @pl.when(i == 0)
        def _():
            loss_ref[...] = part

        @pl.when(i > 0)
        def _():
            loss_ref[...] += part

    row = pl.BlockSpec((tm, D_MODEL), lambda i: (i, 0))
    return pl.pallas_call(
        body, name="loss_head",
        grid=(s // tm,),
        in_specs=[row, row],
        out_specs=[row, pl.BlockSpec((1, 128), lambda i: (0, 0))],
        out_shape=[jax.ShapeDtypeStruct((s, D_MODEL), F32), jax.ShapeDtypeStruct((1, 128), F32)],
        compiler_params=_params("arbitrary"),
    )(y, target)


def _pad_heads(w, real):
    lead = w.shape[:-1]
    w = w.reshape(lead + (HEADS, real))
    w = jnp.pad(w, [(0, 0)] * len(lead) + [(0, 0), (0, HEAD_PAD - real)])
    return w.reshape(lead + (ATT_W,))


def _unpad_heads(w, real):
    lead = w.shape[:-1]
    return w.reshape(lead + (HEADS, HEAD_PAD))[..., :real].reshape(lead + (HEADS * real,))


def layout_weights(full):
    w = {}
    depth = full["w_in"].shape[0]
    win = full["w_in"]
    zc = lambda n: jnp.zeros((depth, D_MODEL, n), win.dtype)
    o_pe = Q_LORA + KV_LORA
    w["win"] = jnp.concatenate([win[..., :o_pe], win[..., o_pe + QK_ROPE:], zc(QK_NOPE), win[..., o_pe:o_pe + QK_ROPE],
                                zc(HEAD_PAD - QK_HEAD)], axis=-1)
    w["wuq"] = _pad_heads(full["w_uq"], QK_HEAD)
    w["wuk"] = _pad_heads(full["w_uk"], QK_NOPE)
    w["wuv"] = _pad_heads(full["w_uv"], V_HEAD)
    wo = full["w_out"]
    wo_att = jnp.swapaxes(_pad_heads(jnp.swapaxes(wo[:, :HEADS * V_HEAD], 1, 2), V_HEAD), 1, 2)
    w["wout"] = jnp.concatenate([wo_att, wo[:, HEADS * V_HEAD:]], axis=1)
    pad_vec = lambda v: jnp.pad(v, ((0, 0), (0, HEAD_PAD - QK_HEAD)))
    w["qn"], w["kn"] = pad_vec(full["q_norm"]), pad_vec(full["k_norm"])
    for n in ("ffn1_norm", "mix_norm", "ffn2_norm", "q_lat_norm", "kv_lat_norm", "pool_scale", "w_pool"):
        w[n] = full[n]
    return w


def local_step(x, target, w, wgu1, wd1, wgu2, wd2):
    s = x.shape[0]
    depth = w["win"].shape[0]
    rc, rs1, rs2 = rope_tables(s)
    vec = lambda name, l: w[name][l][None, :]
    saved = []
    for l in range(depth):
        x1, g1, u1 = ffn_fwd(x, vec("ffn1_norm", l), wgu1, wd1, l)
        z, q, k, v, vt = mix_in_fwd(x1, vec("mix_norm", l), w["win"][l], vec("q_lat_norm", l), vec("kv_lat_norm", l),
                                    w["wuq"][l], w["wuk"][l], w["wuv"][l], vec("qn", l), vec("kn", l), rc, rs1, rs2)
        o, lse = flash_fwd(q, k, vt)
        x2, cat = pool_out_fwd(x1, z, o, w["w_pool"][l], vec("pool_scale", l), w["wout"][l])
        x3, g2, u2 = ffn_fwd(x2, vec("ffn2_norm", l), wgu2, wd2, l)
        saved.append((x, x1, z, q, k, v, lse, cat, x2, g1, u1, g2, u2))
        x = x3
    dy, loss = loss_head(x, target)

    names = ["ffn1_norm", "ffn1_w_gu", "ffn1_w_down", "mix_norm", "w_in", "q_lat_norm", "kv_lat_norm", "w_uq", "w_uk",
             "w_uv", "q_norm", "k_norm", "w_pool", "pool_scale", "w_out", "ffn2_norm", "ffn2_w_gu", "ffn2_w_down"]
    grads = {n: [None] * depth for n in names if "ffn" not in n or "norm" in n}
    for n in ("ffn1", "ffn2"):
        grads[n + "_w_gu"] = lax.empty((depth, N_CHIPS, D_MODEL, FF_BLOCK), F32)
        grads[n + "_w_down"] = lax.empty((depth, D_FF, D_MODEL), F32)

    def ffn_weight_grads(n, l, h, act, dg, du, dyh):
        gu = matmul_tn(h, dg, tk=2048, into=grads[n + "_w_gu"], at=lambda j: (l, j))
        grads[n + "_w_gu"] = matmul_tn(h, du, tk=2048, into=gu, at=lambda j: (l, j + 2))
        grads[n + "_w_down"] = matmul_tn(act, dyh, into=grads[n + "_w_down"], at=lambda j: (l,))

    for l in reversed(range(depth)):
        x0, x1, z, q, k, v, lse, cat, x2, g1, u1, g2, u2 = saved[l]
        dx2, h, act, dg, du, dyh, dgn = ffn_bwd(x2, dy, vec("ffn2_norm", l), g2, u2, wgu2, wd2, l)
        grads["ffn2_norm"][l] = dgn[0]
        ffn_weight_grads("ffn2", l, h, act, dg, du, dyh)

        do, db, delta = out_bwd(dx2, w["wout"][l], cat)
        dwout = matmul_tn(cat, dx2)
        grads["w_out"][l] = jnp.concatenate(
            [jnp.swapaxes(_unpad_heads(jnp.swapaxes(dwout[:ATT_W], 0, 1), V_HEAD), 0, 1), dwout[ATT_W:]], axis=0)
        dp, mixed, dys, dps = pool_bwd(db, z, w["w_pool"][l], vec("pool_scale", l))
        grads["pool_scale"][l] = dps[0]
        dwp = matmul_tn(mixed, dys)
        grads["w_pool"][l] = jnp.stack([dwp[g * GROUP:(g + 1) * GROUP, g * GROUP:(g + 1) * GROUP]
                                        for g in range(len(POOL_WINDOWS))])
        dq_t, dk, dv = flash_bwd(q, k, v, do, lse, delta.reshape(HEADS, 1, s))
        dz, cqn, ckvn, dqr, dkr, dvb, dqln, dkvln, dqn, dkn = mla_bwd(
            z, dq_t, dk, dv, dp, vec("q_lat_norm", l), vec("kv_lat_norm", l), w["wuq"][l], w["wuk"][l], w["wuv"][l],
            vec("qn", l), vec("kn", l), rc, rs1, rs2)
        grads["q_lat_norm"][l], grads["kv_lat_norm"][l] = dqln[0], dkvln[0]
        grads["q_norm"][l], grads["k_norm"][l] = dqn[0, :QK_HEAD], dkn[0, :QK_HEAD]
        grads["w_uq"][l] = _unpad_heads(matmul_tn(cqn, dqr), QK_HEAD)
        grads["w_uk"][l] = _unpad_heads(matmul_tn(ckvn, dkr), QK_NOPE)
        grads["w_uv"][l] = _unpad_heads(matmul_tn(ckvn, dvb), V_HEAD)
        dx1, h2, dgm = mix_in_bwd(x1, dx2, dz, vec("mix_norm", l), w["win"][l])
        grads["mix_norm"][l] = dgm[0]
        dwin = matmul_tn(h2, dz)
        grads["w_in"][l] = jnp.concatenate(
            [dwin[:, :Z_POOL], dwin[:, Z_KPE + QK_NOPE:Z_KPE + QK_HEAD], dwin[:, Z_POOL:Z_KPE]], axis=1)

        dy, h, act, dg, du, dyh, dgn = ffn_bwd(x0, dx1, vec("ffn1_norm", l), g1, u1, wgu1, wd1, l)
        grads["ffn1_norm"][l] = dgn[0]
        ffn_weight_grads("ffn1", l, h, act, dg, du, dyh)
    return loss, dy, {n: (jnp.stack(g) if isinstance(g, list) else g) for n, g in grads.items()}


HBM_SPEC = pl.BlockSpec(memory_space=pltpu.HBM)


def _place():
    x, y, c = lax.axis_index("x"), lax.axis_index("y"), lax.axis_index("c")
    other_chips = [(1 - x, y), (x, 1 - y), (1 - x, 1 - y)]
    return x, y, c, other_chips


def _exchange(name, arrays, out_shapes, plan, n_local, n_remote):
    n = len(arrays)

    def body(*refs):
        ins, outs = refs[:n], refs[n:n + len(out_shapes)]
        send_sems, recv_sems, local_sems = refs[n + len(out_shapes):]
        local, sends, recvs, forwards = plan(ins, outs, _place())

        def remote(k, src, dst, dev):
            return pltpu.make_async_remote_copy(src_ref=src, dst_ref=dst, send_sem=send_sems.at[k],
                                                recv_sem=recv_sems.at[k], device_id=dev, device_id_type=MESH)

        started = []
        for k, (src, dst) in enumerate(local):
            cp = pltpu.make_async_copy(src, dst, local_sems.at[k])
            cp.start()
            started.append(cp)
        out_going = []
        for k, (src, dst, dev) in enumerate(sends):
            cp = remote(k, src, dst, dev)
            cp.start()
            out_going.append(cp)
        for k, (dst, dev) in enumerate(recvs):
            remote(k, dst, dst, dev).wait_recv()
            if forwards:
                src, fdst, fdev, _ = forwards[k]
                cp = remote(len(sends) + k, src, fdst, fdev)
                cp.start()
                out_going.append(cp)
        for k, (_, _, fdev, landing) in enumerate(forwards):
            remote(len(sends) + k, landing, landing, fdev).wait_recv()
        for cp in out_going:
            cp.wait_send()
        for cp in started:
            cp.wait()

    return pl.pallas_call(
        body, name=name,
        in_specs=[HBM_SPEC] * n,
        out_specs=[HBM_SPEC] * len(out_shapes),
        out_shape=out_shapes,
        scratch_shapes=[pltpu.SemaphoreType.DMA((n_remote,)), pltpu.SemaphoreType.DMA((n_remote,)),
                        pltpu.SemaphoreType.DMA((max(n_local, 1),))],
    )(*arrays)


def _layer_half(depth, which):
    assert depth % 2 == 0
    return pl.ds(which * (depth // 2), depth // 2)


def gather_chips(shards):
    n = len(shards)
    depth = shards[0].shape[0]

    def plan(ins, outs, place):
        x, y, c, chips = place
        me, sib = 2 * x + y, (x, y, 1 - c)
        mine, theirs = _layer_half(depth, c), _layer_half(depth, 1 - c)
        sends = [(ins[p].at[mine], outs[p].at[me, mine], (px, py, c)) for p in range(n) for (px, py) in chips]
        recvs = [(outs[p].at[2 * px + py, mine], (px, py, c)) for p in range(n) for (px, py) in chips]
        forwards = [(outs[p].at[2 * px + py, mine], outs[p].at[2 * px + py, mine], sib, outs[p].at[2 * px + py, theirs])
                    for p in range(n) for (px, py) in chips]
        return [], sends, recvs, forwards

    out_shapes = [jax.ShapeDtypeStruct((N_CHIPS,) + a.shape, a.dtype) for a in shards]
    return _exchange("gather_chips", shards, out_shapes, plan, 0, 6 * n)


def swap_halves(fulls):
    n = len(fulls)
    depth = fulls[0].shape[0]

    def plan(ins, outs, place):
        x, y, c, _ = place
        sib = (x, y, 1 - c)
        sends = [(ins[p].at[_layer_half(depth, 1 - c)], outs[p], sib) for p in range(n)]
        return [], sends, [(outs[p], sib) for p in range(n)], []

    out_shapes = [jax.ShapeDtypeStruct((depth // 2,) + a.shape[1:], a.dtype) for a in fulls]
    return _exchange("swap_halves", fulls, out_shapes, plan, 0, n)


def scatter_chips(fulls):
    n = len(fulls)

    def plan(ins, outs, place):
        x, y, c, chips = place
        sends = [(ins[p].at[:, 2 * px + py], outs[p].at[j], (px, py, c))
                 for p in range(n) for j, (px, py) in enumerate(chips)]
        recvs = [(outs[p].at[j], (px, py, c)) for p in range(n) for j, (px, py) in enumerate(chips)]
        return [], sends, recvs, []

    out_shapes = [jax.ShapeDtypeStruct((3, a.shape[0]) + a.shape[2:], a.dtype) for a in fulls]
    return _exchange("scatter_chips", fulls, out_shapes, plan, 0, 3 * n)


def swap_sibling(arrays):
    n = len(arrays)

    def plan(ins, outs, place):
        x, y, c, _ = place
        sib = (x, y, 1 - c)
        return [], [(ins[p], outs[p], sib) for p in range(n)], [(outs[p], sib) for p in range(n)], []

    return _exchange("swap_sibling", arrays, [jax.ShapeDtypeStruct(a.shape, a.dtype) for a in arrays], plan, 0, n)


def gather_all(vec):
    def plan(ins, outs, place):
        x, y, c, _ = place
        me = 4 * x + 2 * y + c
        flips = [(fx, fy, fc) for fx in (0, 1) for fy in (0, 1) for fc in (0, 1)][1:]
        peers = [(1 - x if fx else x, 1 - y if fy else y, 1 - c if fc else c) for fx, fy, fc in flips]
        local = [(ins[0], outs[0].at[me])]
        sends = [(ins[0], outs[0].at[me], dev) for dev in peers]
        recvs = [(outs[0].at[4 * dev[0] + 2 * dev[1] + dev[2]], dev) for dev in peers]
        return local, sends, recvs, []

    return _exchange("gather_all", [vec], [jax.ShapeDtypeStruct((8,) + vec.shape, vec.dtype)], plan, 1, 7)[0]


def _pick_rows(rows, cols, block_bytes=1024 * 1024):
    best = None
    for t in range(16, rows + 1, 16):
        if rows % t == 0 and t * cols * 4 <= block_bytes:
            best = t
    return best or rows


def sum_rows_of(own, stacked, with_bf16=False):
    rows, cols = own.shape
    n = stacked.shape[0]
    tr = _pick_rows(rows, cols)

    def body(own_ref, st_ref, o_ref, *narrow):
        acc = own_ref[...]
        for k in range(n):
            acc = acc + st_ref[k].astype(F32)
        o_ref[...] = acc
        if with_bf16:
            narrow[0][...] = acc.astype(BF16)

    spec = pl.BlockSpec((tr, cols), lambda i: (i, 0))
    return pl.pallas_call(
        body, name="sum_rows_of",
        grid=(rows // tr,),
        in_specs=[spec, pl.BlockSpec((n, tr, cols), lambda i: (0, i, 0))],
        out_specs=[spec, spec] if with_bf16 else spec,
        out_shape=([jax.ShapeDtypeStruct((rows, cols), F32), jax.ShapeDtypeStruct((rows, cols), BF16)]
                   if with_bf16 else jax.ShapeDtypeStruct((rows, cols), F32)),
        compiler_params=_params("parallel"),
    )(own, stacked)


def sum_stack(stacked):
    n, rows, cols = stacked.shape
    tr = _pick_rows(rows, cols)

    def body(st_ref, o_ref):
        acc = st_ref[0]
        for k in range(1, n):
            acc = acc + st_ref[k]
        o_ref[...] = acc

    return pl.pallas_call(
        body, name="sum_stack",
        grid=(rows // tr,),
        in_specs=[pl.BlockSpec((n, tr, cols), lambda i: (0, i, 0))],
        out_specs=pl.BlockSpec((tr, cols), lambda i: (i, 0)),
        out_shape=jax.ShapeDtypeStruct((rows, cols), F32),
        compiler_params=_params("parallel"),
    )(stacked)


def adamw(w, ga, gb, m, v):
    rows, cols = w.shape
    tr = _pick_rows(rows, cols)
    c1 = 1.0 - ADAM_B1 ** ADAM_STEP
    c2 = 1.0 - ADAM_B2 ** ADAM_STEP
    ins = [w, ga] + ([gb] if gb is not None else []) + [m, v]

    def body(*refs):
        w_ref, ga_ref = refs[0], refs[1]
        m_ref, v_ref, g_out, d_out, m_out, v_out = refs[-6:]
        g = ga_ref[...]
        if gb is not None:
            g = g + refs[2][...]
        mn = ADAM_B1 * m_ref[...] + (1.0 - ADAM_B1) * g
        vn = ADAM_B2 * v_ref[...] + (1.0 - ADAM_B2) * (g * g)
        m_hat = mn / c1
        v_hat = vn / c2
        g_out[...] = g
        d_out[...] = -ADAM_LR * (m_hat / (jnp.sqrt(v_hat) + ADAM_EPS) + ADAM_WD * w_ref[...])
        m_out[...] = mn
        v_out[...] = vn

    spec = pl.BlockSpec((tr, cols), lambda i: (i, 0))
    return pl.pallas_call(
        body, name="adamw",
        grid=(rows // tr,),
        in_specs=[spec] * len(ins),
        out_specs=[spec] * 4,
        out_shape=[jax.ShapeDtypeStruct((rows, cols), F32)] * 4,
        compiler_params=_params("parallel"),
    )(*ins)


WEIGHT_NAMES = ["ffn1_norm", "ffn1_w_gu", "ffn1_w_down", "mix_norm", "w_in", "q_lat_norm", "kv_lat_norm", "w_uq", "w_uk",
                "w_uv", "q_norm", "k_norm", "w_pool", "pool_scale", "w_out", "ffn2_norm", "ffn2_w_gu", "ffn2_w_down"]
COL_SHARDED = ["ffn1_w_gu", "w_in", "w_uq", "w_uk", "w_uv", "ffn2_w_gu"]
ROW_SHARDED = ["ffn1_w_down", "w_out", "ffn2_w_down"]
SHARDED = [n for n in WEIGHT_NAMES if n in COL_SHARDED or n in ROW_SHARDED]
REPLICATED = [n for n in WEIGHT_NAMES if n not in SHARDED]


def _join_chips(g, name):
    _, depth, r, c = g.shape
    if name in COL_SHARDED:
        return jnp.transpose(g, (1, 2, 0, 3)).reshape(depth, r, N_CHIPS * c)
    return jnp.transpose(g, (1, 0, 2, 3)).reshape(depth, N_CHIPS * r, c)


def _split_chips(full, name):
    depth, r, c = full.shape
    if name in COL_SHARDED:
        return jnp.transpose(full.reshape(depth, r, N_CHIPS, c // N_CHIPS), (0, 2, 1, 3))
    return full.reshape(depth, N_CHIPS, r // N_CHIPS, c)


def _pack(vs):
    flat = jnp.concatenate([v.reshape(-1) for v in vs])
    pad = (-flat.shape[0]) % (8 * 128)
    return jnp.pad(flat, (0, pad)).reshape(-1, 128)


def _unpack(packed, like):
    flat = packed.reshape(-1)
    out, at = [], 0
    for v in like:
        out.append(flat[at:at + v.size].reshape(v.shape))
        at += v.size
    return out


def kernel(x, ffn1_norm, ffn1_w_gu, ffn1_w_down, mix_norm, w_in, q_lat_norm, kv_lat_norm, w_uq, w_uk, w_uv, q_norm, k_norm, w_pool, pool_scale, w_out, ffn2_norm, ffn2_w_gu, ffn2_w_down, loss_target, m_ffn1_norm, m_ffn1_w_gu, m_ffn1_w_down, m_mix_norm, m_w_in, m_q_lat_norm, m_kv_lat_norm, m_w_uq, m_w_uk, m_w_uv, m_q_norm, m_k_norm, m_w_pool, m_pool_scale, m_w_out, m_ffn2_norm, m_ffn2_w_gu, m_ffn2_w_down, v_ffn1_norm, v_ffn1_w_gu, v_ffn1_w_down, v_mix_norm, v_w_in, v_q_lat_norm, v_kv_lat_norm, v_w_uq, v_w_uk, v_w_uv, v_q_norm, v_k_norm, v_w_pool, v_pool_scale, v_w_out, v_ffn2_norm, v_ffn2_w_gu, v_ffn2_w_down):
    args = locals()
    weights = {n: args[n] for n in WEIGHT_NAMES}
    moments_m = {n: args["m_" + n] for n in WEIGHT_NAMES}
    moments_v = {n: args["v_" + n] for n in WEIGHT_NAMES}

    chip = 2 * lax.axis_index("x") + lax.axis_index("y")
    core = lax.axis_index("c")
    shards = [weights[n].astype(BF16) for n in SHARDED]
    gathered = {n: lax.dynamic_update_index_in_dim(g, own, chip, 0)
                for n, own, g in zip(SHARDED, shards, gather_chips(shards))}
    full = {n: _join_chips(gathered[n], n) for n in SHARDED if "ffn" not in n}
    for n in REPLICATED:
        full[n] = weights[n].astype(BF16) if n == "w_pool" else weights[n]
    w = layout_weights(full)
    wd1 = _join_chips(gathered["ffn1_w_down"], "ffn1_w_down")
    wd2 = _join_chips(gathered["ffn2_w_down"], "ffn2_w_down")

    loss_part, grad_x, grads = local_step(x[0], loss_target[0], w, gathered["ffn1_w_gu"], wd1, gathered["ffn2_w_gu"], wd2)
    loss = lax.psum(loss_part[0, 0], ("x", "y", "c"))

    split = [grads[n] if n.endswith("w_gu") else _split_chips(grads[n], n) for n in SHARDED]
    half = split[0].shape[0] // 2

    def add(own, stacked, with_bf16=False):
        rows = own.size // own.shape[-1]
        total = sum_rows_of(own.reshape(rows, -1), stacked.reshape(stacked.shape[0], rows, -1), with_bf16)
        return [t.reshape(own.shape) for t in total] if with_bf16 else total.reshape(own.shape)

    from_sibling = swap_halves(split)
    chip_sums = [add(lax.dynamic_slice_in_dim(g4, core * half, half, axis=0), other[None], with_bf16=True)
                 for g4, other in zip(split, from_sibling)]
    landed = scatter_chips([narrow for _, narrow in chip_sums])
    totals = [add(lax.dynamic_index_in_dim(wide, chip, 1, keepdims=False), land)
              for (wide, _), land in zip(chip_sums, landed)]
    out = {}
    for n, mine, theirs in zip(SHARDED, totals, swap_sibling(totals)):
        g = jnp.concatenate([jnp.where(core == 0, mine, theirs), jnp.where(core == 0, theirs, mine)], axis=0)
        shape = weights[n].shape
        flat = lambda a: a.reshape(shape[0] * shape[1], shape[2])
        res = adamw(flat(weights[n]), flat(g), None, flat(moments_m[n]), flat(moments_v[n]))
        out[n] = [r.reshape(shape) for r in res]

    rep_g = [grads[n] for n in REPLICATED]
    total = sum_stack(gather_all(_pack(rep_g)))
    res = adamw(_pack([weights[n] for n in REPLICATED]), total, None,
                _pack([moments_m[n] for n in REPLICATED]), _pack([moments_v[n] for n in REPLICATED]))
    unpacked = [_unpack(r, rep_g) for r in res]
    for k, n in enumerate(REPLICATED):
        out[n] = [u[k] for u in unpacked]

    return (loss, grad_x[None], *[out[n][0] for n in WEIGHT_NAMES], *[out[n][1] for n in WEIGHT_NAMES],
            *[out[n][2] for n in WEIGHT_NAMES], *[out[n][3] for n in WEIGHT_NAMES])
```

```python
import jax
import jax.numpy as jnp
from jax import lax
from jax.experimental import pallas as pl
from jax.experimental.pallas import tpu as pltpu

F32 = jnp.float32
BF16 = jnp.bfloat16
MESH = pl.DeviceIdType.MESH

DEPTH = 4
D_MODEL = 1024
D_FF = 2816
FF_BLOCK = D_FF // 2
N_CHIPS = 4
HEADS = 8
HEAD_PAD = 128
QK_NOPE = 64
QK_ROPE = 32
QK_HEAD = QK_NOPE + QK_ROPE
V_HEAD = 64
Q_LORA = 384
KV_LORA = 256
POOL_W = 512
POOL_WINDOWS = (2, 4, 8, 16)
POOL_HALO = 8
GROUP = 128
D_IN = Q_LORA + KV_LORA + QK_ROPE + POOL_W
Z_W = 1280
Z_CKV = Q_LORA
Z_POOL = Q_LORA + KV_LORA
Z_KPE = Z_POOL + POOL_W
ATT_W = HEADS * HEAD_PAD
CAT_W = ATT_W + POOL_W
EPS = 1e-6
ROPE_THETA = 10000.0
ADAM_LR, ADAM_B1, ADAM_B2, ADAM_EPS, ADAM_WD, ADAM_STEP = 0.001, 0.9, 0.999, 1e-08, 0.01, 10
LOG2_E = 1.4426950408889634
LN_2 = 0.6931471805599453

VMEM_LIMIT = 56 * 1024 * 1024
VMEM_LIMIT_FFN_BWD = 62 * 1024 * 1024


def _params(*sem, vmem=VMEM_LIMIT):
    return pltpu.CompilerParams(dimension_semantics=sem, vmem_limit_bytes=vmem)


def _dot(a, b):
    return jnp.dot(a, b, preferred_element_type=F32)


def _dot_nt(a, b):
    return lax.dot_general(a, b, (((1,), (1,)), ((), ())), preferred_element_type=F32)


def _dot_tn(a, b):
    return lax.dot_general(a, b, (((0,), (0,)), ((), ())), preferred_element_type=F32)


def _norm_fwd(x, n):
    rstd = lax.rsqrt(jnp.sum(x * x, axis=-1, keepdims=True) * (1.0 / n) + EPS)
    return x * rstd, rstd


def _norm_bwd(dy, g, xhat, rstd, n):
    dxhat = dy * g
    dx = rstd * (dxhat - xhat * (jnp.sum(dxhat * xhat, axis=-1, keepdims=True) * (1.0 / n)))
    return dx, dy * xhat


def _row_tile(s, want):
    t = min(s, want)
    assert s % t == 0
    return t


def ffn_fwd(x, gn, wgu4, wd, layer, tm=512):
    s = x.shape[0]
    tm = _row_tile(s, tm)

    def body(x_ref, gn_ref, wg_ref, wu_ref, wd_ref, o_ref, g_ref, u_ref, h_s, acc_s):
        j = pl.program_id(1)

        @pl.when(j == 0)
        def _():
            xhat, _ = _norm_fwd(x_ref[...], D_MODEL)
            h_s[...] = (xhat * gn_ref[...]).astype(BF16)
            acc_s[...] = jnp.zeros_like(acc_s)

        h = h_s[...]
        g = _dot(h, wg_ref[...])
        u = _dot(h, wu_ref[...])
        g_ref[...] = g.astype(BF16)
        u_ref[...] = u.astype(BF16)
        a = (g * jax.nn.sigmoid(g) * u).astype(BF16)
        acc_s[...] += _dot(a, wd_ref[...])

        @pl.when(j == 1)
        def _():
            o_ref[...] = x_ref[...] + 0.5 * acc_s[...]

    return pl.pallas_call(
        body, name="ffn_fwd",
        grid=(s // tm, 2),
        in_specs=[
            pl.BlockSpec((tm, D_MODEL), lambda i, j: (i, 0)),
            pl.BlockSpec((1, D_MODEL), lambda i, j: (0, 0)),
            pl.BlockSpec((None, None, D_MODEL, FF_BLOCK), lambda i, j: (j, layer, 0, 0)),
            pl.BlockSpec((None, None, D_MODEL, FF_BLOCK), lambda i, j: (j + 2, layer, 0, 0)),
            pl.BlockSpec((None, FF_BLOCK, D_MODEL), lambda i, j: (layer, j, 0)),
        ],
        out_specs=[pl.BlockSpec((tm, D_MODEL), lambda i, j: (i, 0)),
                   pl.BlockSpec((tm, FF_BLOCK), lambda i, j: (i, j)), pl.BlockSpec((tm, FF_BLOCK), lambda i, j: (i, j))],
        out_shape=[jax.ShapeDtypeStruct((s, D_MODEL), F32), jax.ShapeDtypeStruct((s, D_FF), BF16),
                   jax.ShapeDtypeStruct((s, D_FF), BF16)],
        scratch_shapes=[pltpu.VMEM((tm, D_MODEL), BF16), pltpu.VMEM((tm, D_MODEL), F32)],
        compiler_params=_params("parallel", "arbitrary"),
    )(x, gn, wgu4, wgu4, wd)


def ffn_bwd(x, dy, gn, g_pre, u_pre, wgu4, wd, layer, tm=512):
    s = x.shape[0]
    tm = _row_tile(s, tm)
    halves = 2 if tm % 32 == 0 else 1

    def body(x_ref, dy_ref, gn_ref, g_ref, u_ref, wg_ref, wu_ref, wd_ref,
             dx_ref, h_ref, a_ref, dg_ref, du_ref, dyh_ref, dgn_ref, dh_s):
        i, j = pl.program_id(0), pl.program_id(1)

        @pl.when(j == 0)
        def _():
            xhat, _ = _norm_fwd(x_ref[...], D_MODEL)
            h_ref[...] = (xhat * gn_ref[...]).astype(BF16)
            dyh_ref[...] = (0.5 * dy_ref[...]).astype(BF16)
            dh_s[...] = jnp.zeros_like(dh_s)

        for r in range(halves):
            rows = slice(r * (tm // halves), (r + 1) * (tm // halves))
            g = g_ref[rows, :].astype(F32)
            u = u_ref[rows, :].astype(F32)
            sg = jax.nn.sigmoid(g)
            silu = g * sg
            a_ref[rows, :] = (silu * u).astype(BF16)
            da = _dot_nt(dyh_ref[rows, :], wd_ref[...])
            dg = (da * u * (sg * (1.0 + g * (1.0 - sg)))).astype(BF16)
            du = (da * silu).astype(BF16)
            dg_ref[rows, :] = dg
            du_ref[rows, :] = du
            dh_s[rows, :] += _dot_nt(dg, wg_ref[...]) + _dot_nt(du, wu_ref[...])

        @pl.when(j == 1)
        def _():
            xhat, rstd = _norm_fwd(x_ref[...], D_MODEL)
            dxn, dgrow = _norm_bwd(dh_s[...], gn_ref[...], xhat, rstd, D_MODEL)
            dx_ref[...] = dy_ref[...] + dxn
            part = jnp.sum(dgrow, axis=0, keepdims=True)

            @pl.when(i == 0)
            def _():
                dgn_ref[...] = part

            @pl.when(i > 0)
            def _():
                dgn_ref[...] += part

    row = lambda i, j: (i, 0)
    return pl.pallas_call(
        body, name="ffn_bwd",
        grid=(s // tm, 2),
        in_specs=[
            pl.BlockSpec((tm, D_MODEL), row),
            pl.BlockSpec((tm, D_MODEL), row),
            pl.BlockSpec((1, D_MODEL), lambda i, j: (0, 0)),
            pl.BlockSpec((tm, FF_BLOCK), lambda i, j: (i, j)),
            pl.BlockSpec((tm, FF_BLOCK), lambda i, j: (i, j)),
            pl.BlockSpec((None, None, D_MODEL, FF_BLOCK), lambda i, j: (j, layer, 0, 0)),
            pl.BlockSpec((None, None, D_MODEL, FF_BLOCK), lambda i, j: (j + 2, layer, 0, 0)),
            pl.BlockSpec((None, FF_BLOCK, D_MODEL), lambda i, j: (layer, j, 0)),
        ],
        out_specs=[
            pl.BlockSpec((tm, D_MODEL), row),
            pl.BlockSpec((tm, D_MODEL), row),
            pl.BlockSpec((tm, FF_BLOCK), lambda i, j: (i, j)),
            pl.BlockSpec((tm, FF_BLOCK), lambda i, j: (i, j)),
            pl.BlockSpec((tm, FF_BLOCK), lambda i, j: (i, j)),
            pl.BlockSpec((tm, D_MODEL), row),
            pl.BlockSpec((1, D_MODEL), lambda i, j: (0, 0)),
        ],
        out_shape=[
            jax.ShapeDtypeStruct((s, D_MODEL), F32),
            jax.ShapeDtypeStruct((s, D_MODEL), BF16),
            jax.ShapeDtypeStruct((s, D_FF), BF16),
            jax.ShapeDtypeStruct((s, D_FF), BF16),
            jax.ShapeDtypeStruct((s, D_FF), BF16),
            jax.ShapeDtypeStruct((s, D_MODEL), BF16),
            jax.ShapeDtypeStruct((1, D_MODEL), F32),
        ],
        scratch_shapes=[pltpu.VMEM((tm, D_MODEL), F32)],
        compiler_params=_params("arbitrary", "arbitrary", vmem=VMEM_LIMIT_FFN_BWD),
    )(x, dy, gn, g_pre, u_pre, wgu4, wgu4, wd)


def matmul_tn(a, b, tk=2048, tn_max=1536, into=None, at=None):
    s, m = a.shape
    n = b.shape[1]
    tk = _row_tile(s, tk)
    tn = n
    if n > tn_max:
        tn = n // 2
    assert n % tn == 0 and tn % 128 == 0
    nk = s // tk

    def body(a_ref, b_ref, *rest):
        o_ref, acc_s = rest[-2:]
        k = pl.program_id(1)

        @pl.when(k == 0)
        def _():
            acc_s[...] = jnp.zeros_like(acc_s)

        acc_s[...] += _dot_tn(a_ref[...].astype(BF16), b_ref[...].astype(BF16))

        @pl.when(k == nk - 1)
        def _():
            o_ref[...] = acc_s[...]

    in_specs = [pl.BlockSpec((tk, m), lambda j, k: (k, 0)), pl.BlockSpec((tk, tn), lambda j, k: (k, j))]
    if into is None:
        operands, aliases = (a, b), {}
        out_specs = pl.BlockSpec((m, tn), lambda j, k: (0, j))
        out_shape = jax.ShapeDtypeStruct((m, n), F32)
    else:
        assert into.shape[-2:] == (m, tn) and into.dtype == F32
        lead = into.ndim - 2
        operands, aliases = (a, b, into), {2: 0}
        in_specs.append(pl.BlockSpec(memory_space=pl.ANY))
        out_specs = pl.BlockSpec((None,) * lead + (m, tn), lambda j, k: tuple(at(j)) + (0, 0))
        out_shape = jax.ShapeDtypeStruct(into.shape, F32)
    return pl.pallas_call(
        body, name="matmul_tn",
        grid=(n // tn, nk),
        in_specs=in_specs,
        out_specs=out_specs,
        out_shape=out_shape,
        scratch_shapes=[pltpu.VMEM((m, tn), F32)],
        input_output_aliases=aliases,
        compiler_params=_params("parallel", "arbitrary"),
    )(*operands)


def _rope_fwd(y, c, s1, s2):
    return y * c + pltpu.roll(y, HEAD_PAD - 16, 1) * s1 + pltpu.roll(y, 16, 1) * s2


def _rope_bwd(dy, c, s1, s2):
    return dy * c + pltpu.roll(dy * s1, 16, 1) + pltpu.roll(dy * s2, HEAD_PAD - 16, 1)


def rope_tables(s):
    pos = jnp.arange(s, dtype=F32)
    inv = ROPE_THETA ** (-jnp.arange(0, QK_ROPE, 2, dtype=F32) / QK_ROPE)
    ang = pos[:, None] * inv[None, :]
    cos, sin = jnp.cos(ang), jnp.sin(ang)
    z16 = jnp.zeros((s, 16), F32)
    c = jnp.concatenate([jnp.ones((s, QK_NOPE), F32), cos, cos, z16, z16], axis=1)
    s1 = jnp.concatenate([jnp.zeros((s, QK_NOPE), F32), -sin, z16, z16, z16], axis=1)
    s2 = jnp.concatenate([jnp.zeros((s, QK_NOPE), F32), z16, sin, z16, z16], axis=1)
    return c, s1, s2


def mix_in_fwd(x, gm, win, qln, kvln, wuq, wuk, wuv, qn, kn, rc, rs1, rs2, tm=512):
    s = x.shape[0]
    tm = _row_tile(s, tm)
    scale = QK_HEAD ** -0.5 * LOG2_E

    def body(x_ref, gm_ref, win_ref, qln_ref, kvln_ref, wuq_ref, wuk_ref, wuv_ref, qn_ref, kn_ref,
             rc_ref, rs1_ref, rs2_ref, z_ref, q_ref, k_ref, v_ref, vt_ref):
        xhat, _ = _norm_fwd(x_ref[...], D_MODEL)
        h = (xhat * gm_ref[...]).astype(BF16)
        z = _dot(h, win_ref[...])
        z_ref[...] = z
        cq, _ = _norm_fwd(z[:, :Q_LORA], Q_LORA)
        cqn = (cq * qln_ref[...]).astype(BF16)
        ckv, _ = _norm_fwd(z[:, Z_CKV:Z_POOL], KV_LORA)
        ckvn = (ckv * kvln_ref[...]).astype(BF16)
        kpe = z[:, Z_KPE:]
        q_raw = _dot(cqn, wuq_ref[...])
        k_raw = _dot(ckvn, wuk_ref[...])
        v = _dot(ckvn, wuv_ref[...])
        v_ref[...] = v.astype(BF16)
        vt_ref[...] = jnp.transpose(v).astype(BF16)
        c, s1, s2 = rc_ref[...], rs1_ref[...], rs2_ref[...]
        for hd in range(HEADS):
            lanes = slice(hd * HEAD_PAD, (hd + 1) * HEAD_PAD)
            qh, _ = _norm_fwd(q_raw[:, lanes], QK_HEAD)
            q_ref[:, lanes] = (_rope_fwd(qh * qn_ref[...], c, s1, s2) * scale).astype(BF16)
            kh, _ = _norm_fwd(k_raw[:, lanes] + kpe, QK_HEAD)
            k_ref[:, lanes] = _rope_fwd(kh * kn_ref[...], c, s1, s2).astype(BF16)

    full = lambda shape: pl.BlockSpec(shape, lambda i: (0,) * len(shape))
    row = lambda w: pl.BlockSpec((tm, w), lambda i: (i, 0))
    return pl.pallas_call(
        body, name="mix_in_fwd",
        grid=(s // tm,),
        in_specs=[row(D_MODEL), full((1, D_MODEL)), full((D_MODEL, Z_W)), full((1, Q_LORA)), full((1, KV_LORA)),
                  full((Q_LORA, ATT_W)), full((KV_LORA, ATT_W)), full((KV_LORA, ATT_W)),
                  full((1, HEAD_PAD)), full((1, HEAD_PAD)), row(HEAD_PAD), row(HEAD_PAD), row(HEAD_PAD)],
        out_specs=[row(Z_W), row(ATT_W), row(ATT_W), row(ATT_W), pl.BlockSpec((ATT_W, tm), lambda i: (0, i))],
        out_shape=[jax.ShapeDtypeStruct((s, Z_W), F32)] + [jax.ShapeDtypeStruct((s, ATT_W), BF16)] * 3 + [
            jax.ShapeDtypeStruct((ATT_W, s), BF16)],
        compiler_params=_params("parallel"),
    )(x, gm, win, qln, kvln, wuq, wuk, wuv, qn, kn, rc, rs1, rs2)


def mla_bwd(z, dq_t, dk, dv, dp, qln, kvln, wuq, wuk, wuv, qn, kn, rc, rs1, rs2, tm=512):
    s = z.shape[0]
    tm = _row_tile(s, tm)
    scale = QK_HEAD ** -0.5

    def body(z_ref, dqt_ref, dk_ref, dv_ref, dp_ref, qln_ref, kvln_ref, wuq_ref, wuk_ref, wuv_ref, qn_ref, kn_ref,
             rc_ref, rs1_ref, rs2_ref,
             dz_ref, cqn_ref, ckvn_ref, dqr_ref, dkr_ref, dvb_ref, dqln_ref, dkvln_ref, dqn_ref, dkn_ref):
        i = pl.program_id(0)
        z = z_ref[...]
        cq, cq_rstd = _norm_fwd(z[:, :Q_LORA], Q_LORA)
        cqn = (cq * qln_ref[...]).astype(BF16)
        ckv, ckv_rstd = _norm_fwd(z[:, Z_CKV:Z_POOL], KV_LORA)
        ckvn = (ckv * kvln_ref[...]).astype(BF16)
        kpe = z[:, Z_KPE:]
        cqn_ref[...] = cqn
        ckvn_ref[...] = ckvn
        q_raw = _dot(cqn, wuq_ref[...])
        k_raw = _dot(ckvn, wuk_ref[...])
        c, s1, s2 = rc_ref[...], rs1_ref[...], rs2_ref[...]
        lane = lax.broadcasted_iota(jnp.int32, (tm, HEAD_PAD), 1)
        rope_lanes = (lane >= QK_NOPE) & (lane < QK_HEAD)
        dkpe = jnp.zeros((tm, HEAD_PAD), F32)
        dqn = jnp.zeros((tm, HEAD_PAD), F32)
        dkn = jnp.zeros((tm, HEAD_PAD), F32)
        for hd in range(HEADS):
            lanes = slice(hd * HEAD_PAD, (hd + 1) * HEAD_PAD)
            qh, q_rstd = _norm_fwd(q_raw[:, lanes], QK_HEAD)
            dqh = jnp.transpose(dqt_ref[lanes, :]) * scale
            dqr, dg = _norm_bwd(_rope_bwd(dqh, c, s1, s2), qn_ref[...], qh, q_rstd, QK_HEAD)
            dqn += dg
            dqr_ref[:, lanes] = dqr.astype(BF16)
            kh, k_rstd = _norm_fwd(k_raw[:, lanes] + kpe, QK_HEAD)
            dkr, dg = _norm_bwd(_rope_bwd(dk_ref[:, lanes], c, s1, s2), kn_ref[...], kh, k_rstd, QK_HEAD)
            dkn += dg
            dkr_ref[:, lanes] = dkr.astype(BF16)
            dkpe += jnp.where(rope_lanes, dkr, 0.0)
        dvb = dv_ref[...].astype(BF16)
        dvb_ref[...] = dvb
        dcqn = _dot_nt(dqr_ref[...], wuq_ref[...])
        dckvn = _dot_nt(dkr_ref[...], wuk_ref[...]) + _dot_nt(dvb, wuv_ref[...])
        dcq, dqln = _norm_bwd(dcqn, qln_ref[...], cq, cq_rstd, Q_LORA)
        dckv, dkvln = _norm_bwd(dckvn, kvln_ref[...], ckv, ckv_rstd, KV_LORA)
        dz_ref[:, :Q_LORA] = dcq.astype(BF16)
        dz_ref[:, Z_CKV:Z_POOL] = dckv.astype(BF16)
        dz_ref[:, Z_POOL:Z_KPE] = dp_ref[...].astype(BF16)
        dz_ref[:, Z_KPE:] = dkpe.astype(BF16)
        parts = [(dqln_ref, dqln), (dkvln_ref, dkvln), (dqn_ref, dqn), (dkn_ref, dkn)]

        @pl.when(i == 0)
        def _():
            for ref, val in parts:
                ref[...] = jnp.sum(val, axis=0, keepdims=True)

        @pl.when(i > 0)
        def _():
            for ref, val in parts:
                ref[...] += jnp.sum(val, axis=0, keepdims=True)

    full = lambda shape: pl.BlockSpec(shape, lambda i: (0,) * len(shape))
    row = lambda w: pl.BlockSpec((tm, w), lambda i: (i, 0))
    return pl.pallas_call(
        body, name="mla_bwd",
        grid=(s // tm,),
        in_specs=[row(Z_W), pl.BlockSpec((ATT_W, tm), lambda i: (0, i)), row(ATT_W), row(ATT_W), row(POOL_W),
                  full((1, Q_LORA)), full((1, KV_LORA)), full((Q_LORA, ATT_W)), full((KV_LORA, ATT_W)),
                  full((KV_LORA, ATT_W)), full((1, HEAD_PAD)), full((1, HEAD_PAD)),
                  row(HEAD_PAD), row(HEAD_PAD), row(HEAD_PAD)],
        out_specs=[row(Z_W), row(Q_LORA), row(KV_LORA), row(ATT_W), row(ATT_W), row(ATT_W),
                   full((1, Q_LORA)), full((1, KV_LORA)), full((1, HEAD_PAD)), full((1, HEAD_PAD))],
        out_shape=[jax.ShapeDtypeStruct((s, Z_W), BF16), jax.ShapeDtypeStruct((s, Q_LORA), BF16),
                   jax.ShapeDtypeStruct((s, KV_LORA), BF16)] + [jax.ShapeDtypeStruct((s, ATT_W), BF16)] * 3 + [
                   jax.ShapeDtypeStruct((1, Q_LORA), F32), jax.ShapeDtypeStruct((1, KV_LORA), F32),
                   jax.ShapeDtypeStruct((1, HEAD_PAD), F32), jax.ShapeDtypeStruct((1, HEAD_PAD), F32)],
        compiler_params=_params("arbitrary"),
    )(z, dq_t, dk, dv, dp, qln, kvln, wuq, wuk, wuv, qn, kn, rc, rs1, rs2)


def mix_in_bwd(x, dx_in, dz, gm, win, tm=512):
    s = x.shape[0]
    tm = _row_tile(s, tm)

    def body(x_ref, dxin_ref, dz_ref, gm_ref, win_ref, dx_ref, h_ref, dgm_ref):
        i = pl.program_id(0)
        xhat, rstd = _norm_fwd(x_ref[...], D_MODEL)
        h_ref[...] = (xhat * gm_ref[...]).astype(BF16)
        dh = _dot_nt(dz_ref[...], win_ref[...])
        dxn, dgrow = _norm_bwd(dh, gm_ref[...], xhat, rstd, D_MODEL)
        dx_ref[...] = dxin_ref[...] + dxn
        part = jnp.sum(dgrow, axis=0, keepdims=True)

        @pl.when(i == 0)
        def _():
            dgm_ref[...] = part

        @pl.when(i > 0)
        def _():
            dgm_ref[...] += part

    full = lambda shape: pl.BlockSpec(shape, lambda i: (0,) * len(shape))
    row = lambda w: pl.BlockSpec((tm, w), lambda i: (i, 0))
    return pl.pallas_call(
        body, name="mix_in_bwd",
        grid=(s // tm,),
        in_specs=[row(D_MODEL), row(D_MODEL), row(Z_W), full((1, D_MODEL)), full((D_MODEL, Z_W))],
        out_specs=[row(D_MODEL), row(D_MODEL), full((1, D_MODEL))],
        out_shape=[jax.ShapeDtypeStruct((s, D_MODEL), F32), jax.ShapeDtypeStruct((s, D_MODEL), BF16),
                   jax.ShapeDtypeStruct((1, D_MODEL), F32)],
        compiler_params=_params("arbitrary"),
    )(x, dx_in, dz, gm, win)


ATT_CHUNK_FWD = 1024
ATT_CHUNK_BWD = 512


def flash_fwd(q, k, vt, tq=1024, tk=16384, chunk=ATT_CHUNK_FWD):
    s = q.shape[0]
    tq, tk = _row_tile(s, tq), _row_tile(s, tk)
    nk = s // tk
    tc = _row_tile(tk, chunk)
    sub = tk // tc

    def body(q_ref, k_ref, vt_ref, o_ref, lse_ref, m_s, l_s, acc_s):
        kk = pl.program_id(2)

        @pl.when(kk == 0)
        def _():
            m_s[...] = jnp.full_like(m_s, -jnp.inf)
            l_s[...] = jnp.zeros_like(l_s)
            acc_s[...] = jnp.zeros_like(acc_s)

        q_t = q_ref[...]

        def scores(c):
            return _dot_nt(k_ref[c * tc:(c + 1) * tc, :], q_t)

        m, l, acc = m_s[...], l_s[...], acc_s[...]
        s_next = scores(0)
        for c in range(sub):
            s_t = s_next
            if c + 1 < sub:
                s_next = scores(c + 1)
            m_new = jnp.maximum(m, jnp.max(s_t, axis=0, keepdims=True))
            alpha = jnp.exp2(m - m_new)
            p_t = jnp.exp2(s_t - m_new)
            l = alpha * l + jnp.sum(p_t, axis=0, keepdims=True)
            acc = alpha * acc + _dot(vt_ref[:, c * tc:(c + 1) * tc], p_t.astype(BF16))
            m = m_new
        m_s[...], l_s[...], acc_s[...] = m, l, acc

        @pl.when(kk == nk - 1)
        def _():
            o_ref[...] = jnp.transpose(acc_s[...] / l_s[...]).astype(BF16)
            lse_ref[...] = m_s[...] + jnp.log2(l_s[...])

    return pl.pallas_call(
        body, name="flash_fwd",
        grid=(HEADS, s // tq, nk),
        in_specs=[pl.BlockSpec((tq, HEAD_PAD), lambda h, i, kk: (i, h)),
                  pl.BlockSpec((tk, HEAD_PAD), lambda h, i, kk: (kk, h)),
                  pl.BlockSpec((HEAD_PAD, tk), lambda h, i, kk: (h, kk))],
        out_specs=[pl.BlockSpec((tq, HEAD_PAD), lambda h, i, kk: (i, h)),
                   pl.BlockSpec((None, 1, tq), lambda h, i, kk: (h, 0, i))],
        out_shape=[jax.ShapeDtypeStruct((s, ATT_W), BF16), jax.ShapeDtypeStruct((HEADS, 1, s), F32)],
        scratch_shapes=[pltpu.VMEM((1, tq), F32), pltpu.VMEM((1, tq), F32), pltpu.VMEM((HEAD_PAD, tq), F32)],
        compiler_params=_params("parallel", "parallel", "arbitrary"),
    )(q, k, vt)


def flash_bwd(q, k, v, do, lse_row, delta_row, tq=8192, tk=1024, chunk=ATT_CHUNK_BWD):
    s = q.shape[0]
    tq, tk = _row_tile(s, tq), _row_tile(s, tk)
    nq = s // tq
    tc = _row_tile(tq, chunk)
    sub = tq // tc

    def body(q_ref, k_ref, v_ref, do_ref, lse_ref, dl_ref, dqt_ref, dk_ref, dv_ref, kt_s, dk_s, dv_s):
        kb, qb = pl.program_id(1), pl.program_id(2)

        @pl.when(qb == 0)
        def _():
            kt_s[...] = jnp.transpose(k_ref[...].astype(F32)).astype(BF16)
            dk_s[...] = jnp.zeros_like(dk_s)
            dv_s[...] = jnp.zeros_like(dv_s)

        k_t, v_t = k_ref[...], v_ref[...]

        def products(c):
            rows = slice(c * tc, (c + 1) * tc)
            return _dot_nt(k_t, q_ref[rows, :]), _dot_nt(v_t, do_ref[rows, :])

        dk, dv = dk_s[...], dv_s[...]
        nxt = products(0)
        contribs = []
        for c in range(sub):
            rows = slice(c * tc, (c + 1) * tc)
            s_t, dp_t = nxt
            if c + 1 < sub:
                nxt = products(c + 1)
            p_t = jnp.exp2(s_t - lse_ref[:, rows])
            ds_t = (p_t * (dp_t - dl_ref[:, rows])).astype(BF16)
            dv = dv + _dot(p_t.astype(BF16), do_ref[rows, :])
            dk = dk + _dot(ds_t, q_ref[rows, :])
            contribs.append(_dot(kt_s[...], ds_t))
        dk_s[...], dv_s[...] = dk, dv
        contrib = jnp.concatenate(contribs, axis=1) if sub > 1 else contribs[0]
        cols = pl.ds(pl.multiple_of(qb * tq, tq), tq)

        @pl.when(kb == 0)
        def _():
            dqt_ref[:, cols] = contrib

        @pl.when(kb > 0)
        def _():
            dqt_ref[:, cols] += contrib

        @pl.when(qb == nq - 1)
        def _():
            dk_ref[...] = dk_s[...] * LN_2
            dv_ref[...] = dv_s[...]

    qspec = pl.BlockSpec((tq, HEAD_PAD), lambda h, kb, qb: (qb, h))
    kspec = pl.BlockSpec((tk, HEAD_PAD), lambda h, kb, qb: (kb, h))
    rowspec = pl.BlockSpec((None, 1, tq), lambda h, kb, qb: (h, 0, qb))
    return pl.pallas_call(
        body, name="flash_bwd",
        grid=(HEADS, s // tk, nq),
        in_specs=[qspec, kspec, kspec, qspec, rowspec, rowspec],
        out_specs=[pl.BlockSpec((HEAD_PAD, s), lambda h, kb, qb: (h, 0)), kspec, kspec],
        out_shape=[jax.ShapeDtypeStruct((ATT_W, s), F32), jax.ShapeDtypeStruct((s, ATT_W), F32),
                   jax.ShapeDtypeStruct((s, ATT_W), F32)],
        scratch_shapes=[pltpu.VMEM((HEAD_PAD, tk), BF16), pltpu.VMEM((tk, HEAD_PAD), F32),
                        pltpu.VMEM((tk, HEAD_PAD), F32)],
        compiler_params=_params("parallel", "arbitrary", "arbitrary"),
    )(q, k, v, do, lse_row, delta_row)


def _halo_specs(tm, w, n_tiles):
    per = tm // POOL_HALO
    last = n_tiles * per - 1
    return [pl.BlockSpec((POOL_HALO, w), lambda i: (jnp.maximum(i * per - 1, 0), 0)),
            pl.BlockSpec((tm, w), lambda i: (i, 0)),
            pl.BlockSpec((POOL_HALO, w), lambda i: (jnp.minimum((i + 1) * per, last), 0))]


def _fill_ext(ext_ref, prev, cur, nxt, i, n_tiles, tm):
    ext_ref[pl.ds(0, POOL_HALO), :] = jnp.where(i > 0, prev, 0.0)
    ext_ref[pl.ds(POOL_HALO, tm), :] = cur
    ext_ref[pl.ds(POOL_HALO + tm, POOL_HALO), :] = jnp.where(i < n_tiles - 1, nxt, 0.0)


def _window_sum(ext_ref, g, lo, hi, tm):
    lanes = pl.ds(g * GROUP, GROUP)
    acc = ext_ref[pl.ds(POOL_HALO + lo, tm), lanes]
    for d in range(lo + 1, hi + 1):
        acc = acc + ext_ref[pl.ds(POOL_HALO + d, tm), lanes]
    return acc


def _pool_mixed(ext_ref, cur, i, s, tm):
    row = i * tm + lax.broadcasted_iota(jnp.int32, (tm, 1), 0)
    out = []
    for g, w in enumerate(POOL_WINDOWS):
        left = w // 2
        right = w - 1 - left
        cnt = (jnp.minimum(row + right + 1, s) - jnp.maximum(row - left, 0)).astype(F32)
        out.append(_window_sum(ext_ref, g, -left, right, tm) / cnt - cur[:, g * GROUP:(g + 1) * GROUP])
    return out


def pool_out_fwd(x, z, o, wpool, pscale, wout, tm=512):
    s = x.shape[0]
    tm = _row_tile(s, tm)
    n_tiles = s // tm

    def body(x_ref, zp_ref, z_ref, zn_ref, o_ref, wp_ref, ps_ref, wo_ref, y_ref, cat_ref, ext_s):
        i = pl.program_id(0)
        cur = z_ref[:, Z_POOL:Z_KPE]
        _fill_ext(ext_s, zp_ref[:, Z_POOL:Z_KPE], cur, zn_ref[:, Z_POOL:Z_KPE], i, n_tiles, tm)
        mixed = _pool_mixed(ext_s, cur, i, s, tm)
        cat_ref[:, :ATT_W] = o_ref[...]
        for g in range(len(POOL_WINDOWS)):
            lanes = slice(g * GROUP, (g + 1) * GROUP)
            yg = _dot(mixed[g].astype(BF16), wp_ref[g])
            cat_ref[:, ATT_W + g * GROUP:ATT_W + (g + 1) * GROUP] = (yg * ps_ref[:, lanes]).astype(BF16)
        y_ref[...] = x_ref[...] + _dot(cat_ref[...], wo_ref[...])

    full = lambda shape: pl.BlockSpec(shape, lambda i: (0,) * len(shape))
    row = lambda w: pl.BlockSpec((tm, w), lambda i: (i, 0))
    return pl.pallas_call(
        body, name="pool_out_fwd",
        grid=(n_tiles,),
        in_specs=[row(D_MODEL)] + _halo_specs(tm, Z_W, n_tiles) + [
            row(ATT_W), full((len(POOL_WINDOWS), GROUP, GROUP)), full((1, POOL_W)), full((CAT_W, D_MODEL))],
        out_specs=[row(D_MODEL), row(CAT_W)],
        out_shape=[jax.ShapeDtypeStruct((s, D_MODEL), F32), jax.ShapeDtypeStruct((s, CAT_W), BF16)],
        scratch_shapes=[pltpu.VMEM((tm + 2 * POOL_HALO, POOL_W), F32)],
        compiler_params=_params("parallel"),
    )(x, z, z, z, o, wpool, pscale, wout)


def out_bwd(dx, wout, cat, tm=512):
    s = dx.shape[0]
    tm = _row_tile(s, tm)

    def body(dx_ref, wo_ref, cat_ref, do_ref, db_ref, dl_ref):
        dcat = _dot_nt(dx_ref[...].astype(BF16), wo_ref[...])
        do_ref[...] = dcat[:, :ATT_W].astype(BF16)
        db_ref[...] = dcat[:, ATT_W:]
        for hd in range(HEADS):
            lanes = slice(hd * HEAD_PAD, (hd + 1) * HEAD_PAD)
            dl_ref[hd] = jnp.sum(dcat[:, lanes] * cat_ref[:, lanes].astype(F32), axis=-1, keepdims=True)

    row = lambda w: pl.BlockSpec((tm, w), lambda i: (i, 0))
    return pl.pallas_call(
        body, name="out_bwd",
        grid=(s // tm,),
        in_specs=[row(D_MODEL), pl.BlockSpec((CAT_W, D_MODEL), lambda i: (0, 0)), row(CAT_W)],
        out_specs=[row(ATT_W), row(POOL_W), pl.BlockSpec((HEADS, tm, 1), lambda i: (0, i, 0))],
        out_shape=[jax.ShapeDtypeStruct((s, ATT_W), BF16), jax.ShapeDtypeStruct((s, POOL_W), F32),
                   jax.ShapeDtypeStruct((HEADS, s, 1), F32)],
        compiler_params=_params("parallel"),
    )(dx, wout, cat)


def pool_bwd(db, z, wpool, pscale, tm=512):
    s = db.shape[0]
    tm = _row_tile(s, tm)
    n_tiles = s // tm
    te = tm + 2 * POOL_HALO

    def body(dbp_ref, db_ref, dbn_ref, zp_ref, z_ref, zn_ref, wp_ref, ps_ref,
             dp_ref, mixed_ref, dys_ref, dps_ref, ext_s, text_s):
        i = pl.program_id(0)
        cur = z_ref[:, Z_POOL:Z_KPE]
        _fill_ext(ext_s, zp_ref[:, Z_POOL:Z_KPE], cur, zn_ref[:, Z_POOL:Z_KPE], i, n_tiles, tm)
        mixed = _pool_mixed(ext_s, cur, i, s, tm)
        _fill_ext(ext_s, dbp_ref[...], db_ref[...], dbn_ref[...], i, n_tiles, tm)
        erow = i * tm - POOL_HALO + lax.broadcasted_iota(jnp.int32, (te, 1), 0)
        dps = []
        for g, w in enumerate(POOL_WINDOWS):
            lanes = slice(g * GROUP, (g + 1) * GROUP)
            left = w // 2
            right = w - 1 - left
            mb = mixed[g].astype(BF16)
            mixed_ref[:, lanes] = mb
            dps.append(jnp.sum(db_ref[:, lanes] * _dot(mb, wp_ref[g]), axis=0, keepdims=True))
            dys_e = (ext_s[:, lanes] * ps_ref[:, lanes]).astype(BF16)
            dys_ref[:, lanes] = (db_ref[:, lanes] * ps_ref[:, lanes]).astype(BF16)
            dmix_e = _dot_nt(dys_e, wp_ref[g])
            cnt = jnp.minimum(erow + right + 1, s) - jnp.maximum(erow - left, 0)
            text_s[:, lanes] = dmix_e / jnp.maximum(cnt, 1).astype(F32)
            dp_ref[:, lanes] = _window_sum(text_s, g, -right, left, tm) - dmix_e[POOL_HALO:POOL_HALO + tm]
        part = jnp.concatenate(dps, axis=1)

        @pl.when(i == 0)
        def _():
            dps_ref[...] = part

        @pl.when(i > 0)
        def _():
            dps_ref[...] += part

    full = lambda shape: pl.BlockSpec(shape, lambda i: (0,) * len(shape))
    row = lambda w: pl.BlockSpec((tm, w), lambda i: (i, 0))
    return pl.pallas_call(
        body, name="pool_bwd",
        grid=(n_tiles,),
        in_specs=_halo_specs(tm, POOL_W, n_tiles) + _halo_specs(tm, Z_W, n_tiles) + [
            full((len(POOL_WINDOWS), GROUP, GROUP)), full((1, POOL_W))],
        out_specs=[row(POOL_W), row(POOL_W), row(POOL_W), full((1, POOL_W))],
        out_shape=[jax.ShapeDtypeStruct((s, POOL_W), F32), jax.ShapeDtypeStruct((s, POOL_W), BF16),
                   jax.ShapeDtypeStruct((s, POOL_W), BF16), jax.ShapeDtypeStruct((1, POOL_W), F32)],
        scratch_shapes=[pltpu.VMEM((te, POOL_W), F32), pltpu.VMEM((te, POOL_W), F32)],
        compiler_params=_params("arbitrary"),
    )(db, db, db, z, z, z, wpool, pscale)


def loss_head(y, target, tm=512):
    s = y.shape[0]
    tm = _row_tile(s, tm)

    def body(y_ref, t_ref, dy_ref, loss_ref):
        i = pl.program_id(0)
        err = y_ref[...] - t_ref[...]
        dy_ref[...] = err * (1.0 / D_MODEL)
        part = 0.5 * jnp.sum(jnp.sum(err * err, axis=-1, keepdims=True) * (1.0 / D_MODEL), axis=0, keepdims=True)
        part = jnp.broadcast_to(part, (1, 128))

        @pl.when(i == 0)
        def _():
            loss_ref[...] = part

        @pl.when(i > 0)
        def _():
            loss_ref[...] += part

    row = pl.BlockSpec((tm, D_MODEL), lambda i: (i, 0))
    return pl.pallas_call(
        body, name="loss_head",
        grid=(s // tm,),
        in_specs=[row, row],
        out_specs=[row, pl.BlockSpec((1, 128), lambda i: (0, 0))],
        out_shape=[jax.ShapeDtypeStruct((s, D_MODEL), F32), jax.ShapeDtypeStruct((1, 128), F32)],
        compiler_params=_params("arbitrary"),
    )(y, target)


def _pad_heads(w, real):
    lead = w.shape[:-1]
    w = w.reshape(lead + (HEADS, real))
    w = jnp.pad(w, [(0, 0)] * len(lead) + [(0, 0), (0, HEAD_PAD - real)])
    return w.reshape(lead + (ATT_W,))


def _unpad_heads(w, real):
    lead = w.shape[:-1]
    return w.reshape(lead + (HEADS, HEAD_PAD))[..., :real].reshape(lead + (HEADS * real,))


def layout_weights(full):
    w = {}
    depth = full["w_in"].shape[0]
    win = full["w_in"]
    zc = lambda n: jnp.zeros((depth, D_MODEL, n), win.dtype)
    o_pe = Q_LORA + KV_LORA
    w["win"] = jnp.concatenate([win[..., :o_pe], win[..., o_pe + QK_ROPE:], zc(QK_NOPE), win[..., o_pe:o_pe + QK_ROPE],
                                zc(HEAD_PAD - QK_HEAD)], axis=-1)
    w["wuq"] = _pad_heads(full["w_uq"], QK_HEAD)
    w["wuk"] = _pad_heads(full["w_uk"], QK_NOPE)
    w["wuv"] = _pad_heads(full["w_uv"], V_HEAD)
    wo = full["w_out"]
    wo_att = jnp.swapaxes(_pad_heads(jnp.swapaxes(wo[:, :HEADS * V_HEAD], 1, 2), V_HEAD), 1, 2)
    w["wout"] = jnp.concatenate([wo_att, wo[:, HEADS * V_HEAD:]], axis=1)
    pad_vec = lambda v: jnp.pad(v, ((0, 0), (0, HEAD_PAD - QK_HEAD)))
    w["qn"], w["kn"] = pad_vec(full["q_norm"]), pad_vec(full["k_norm"])
    for n in ("ffn1_norm", "mix_norm", "ffn2_norm", "q_lat_norm", "kv_lat_norm", "pool_scale", "w_pool"):
        w[n] = full[n]
    return w


def local_step(x, target, w, wgu1, wd1, wgu2, wd2):
    s = x.shape[0]
    depth = w["win"].shape[0]
    rc, rs1, rs2 = rope_tables(s)
    vec = lambda name, l: w[name][l][None, :]
    saved = []
    for l in range(depth):
        x1, g1, u1 = ffn_fwd(x, vec("ffn1_norm", l), wgu1, wd1, l)
        z, q, k, v, vt = mix_in_fwd(x1, vec("mix_norm", l), w["win"][l], vec("q_lat_norm", l), vec("kv_lat_norm", l),
                                    w["wuq"][l], w["wuk"][l], w["wuv"][l], vec("qn", l), vec("kn", l), rc, rs1, rs2)
        o, lse = flash_fwd(q, k, vt)
        x2, cat = pool_out_fwd(x1, z, o, w["w_pool"][l], vec("pool_scale", l), w["wout"][l])
        x3, g2, u2 = ffn_fwd(x2, vec("ffn2_norm", l), wgu2, wd2, l)
        saved.append((x, x1, z, q, k, v, lse, cat, x2, g1, u1, g2, u2))
        x = x3
    dy, loss = loss_head(x, target)

    names = ["ffn1_norm", "ffn1_w_gu", "ffn1_w_down", "mix_norm", "w_in", "q_lat_norm", "kv_lat_norm", "w_uq", "w_uk",
             "w_uv", "q_norm", "k_norm", "w_pool", "pool_scale", "w_out", "ffn2_norm", "ffn2_w_gu", "ffn2_w_down"]
    grads = {n: [None] * depth for n in names if "ffn" not in n or "norm" in n}
    for n in ("ffn1", "ffn2"):
        grads[n + "_w_gu"] = lax.empty((depth, N_CHIPS, D_MODEL, FF_BLOCK), F32)
        grads[n + "_w_down"] = lax.empty((depth, D_FF, D_MODEL), F32)

    def ffn_weight_grads(n, l, h, act, dg, du, dyh):
        gu = matmul_tn(h, dg, into=grads[n + "_w_gu"], at=lambda j: (l, j))
        grads[n + "_w_gu"] = matmul_tn(h, du, into=gu, at=lambda j: (l, j + 2))
        grads[n + "_w_down"] = matmul_tn(act, dyh, tk=1024, into=grads[n + "_w_down"], at=lambda j: (l,))

    for l in reversed(range(depth)):
        x0, x1, z, q, k, v, lse, cat, x2, g1, u1, g2, u2 = saved[l]
        dx2, h, act, dg, du, dyh, dgn = ffn_bwd(x2, dy, vec("ffn2_norm", l), g2, u2, wgu2, wd2, l)
        grads["ffn2_norm"][l] = dgn[0]
        ffn_weight_grads("ffn2", l, h, act, dg, du, dyh)

        do, db, delta = out_bwd(dx2, w["wout"][l], cat)
        dwout = matmul_tn(cat, dx2)
        grads["w_out"][l] = jnp.concatenate(
            [jnp.swapaxes(_unpad_heads(jnp.swapaxes(dwout[:ATT_W], 0, 1), V_HEAD), 0, 1), dwout[ATT_W:]], axis=0)
        dp, mixed, dys, dps = pool_bwd(db, z, w["w_pool"][l], vec("pool_scale", l))
        grads["pool_scale"][l] = dps[0]
        dwp = matmul_tn(mixed, dys)
        grads["w_pool"][l] = jnp.stack([dwp[g * GROUP:(g + 1) * GROUP, g * GROUP:(g + 1) * GROUP]
                                        for g in range(len(POOL_WINDOWS))])
        dq_t, dk, dv = flash_bwd(q, k, v, do, lse, delta.reshape(HEADS, 1, s))
        dz, cqn, ckvn, dqr, dkr, dvb, dqln, dkvln, dqn, dkn = mla_bwd(
            z, dq_t, dk, dv, dp, vec("q_lat_norm", l), vec("kv_lat_norm", l), w["wuq"][l], w["wuk"][l], w["wuv"][l],
            vec("qn", l), vec("kn", l), rc, rs1, rs2)
        grads["q_lat_norm"][l], grads["kv_lat_norm"][l] = dqln[0], dkvln[0]
        grads["q_norm"][l], grads["k_norm"][l] = dqn[0, :QK_HEAD], dkn[0, :QK_HEAD]
        grads["w_uq"][l] = _unpad_heads(matmul_tn(cqn, dqr), QK_HEAD)
        grads["w_uk"][l] = _unpad_heads(matmul_tn(ckvn, dkr), QK_NOPE)
        grads["w_uv"][l] = _unpad_heads(matmul_tn(ckvn, dvb), V_HEAD)
        dx1, h2, dgm = mix_in_bwd(x1, dx2, dz, vec("mix_norm", l), w["win"][l])
        grads["mix_norm"][l] = dgm[0]
        dwin = matmul_tn(h2, dz)
        grads["w_in"][l] = jnp.concatenate(
            [dwin[:, :Z_POOL], dwin[:, Z_KPE + QK_NOPE:Z_KPE + QK_HEAD], dwin[:, Z_POOL:Z_KPE]], axis=1)

        dy, h, act, dg, du, dyh, dgn = ffn_bwd(x0, dx1, vec("ffn1_norm", l), g1, u1, wgu1, wd1, l)
        grads["ffn1_norm"][l] = dgn[0]
        ffn_weight_grads("ffn1", l, h, act, dg, du, dyh)
    return loss, dy, {n: (jnp.stack(g) if isinstance(g, list) else g) for n, g in grads.items()}


HBM_SPEC = pl.BlockSpec(memory_space=pltpu.HBM)


def _place():
    x, y, c = lax.axis_index("x"), lax.axis_index("y"), lax.axis_index("c")
    other_chips = [(1 - x, y), (x, 1 - y), (1 - x, 1 - y)]
    return x, y, c, other_chips


def _exchange(name, arrays, out_shapes, plan, n_local, n_remote):
    n = len(arrays)

    def body(*refs):
        ins, outs = refs[:n], refs[n:n + len(out_shapes)]
        send_sems, recv_sems, local_sems = refs[n + len(out_shapes):]
        local, sends, recvs, forwards = plan(ins, outs, _place())

        def remote(k, src, dst, dev):
            return pltpu.make_async_remote_copy(src_ref=src, dst_ref=dst, send_sem=send_sems.at[k],
                                                recv_sem=recv_sems.at[k], device_id=dev, device_id_type=MESH)

        started = []
        for k, (src, dst) in enumerate(local):
            cp = pltpu.make_async_copy(src, dst, local_sems.at[k])
            cp.start()
            started.append(cp)
        out_going = []
        for k, (src, dst, dev) in enumerate(sends):
            cp = remote(k, src, dst, dev)
            cp.start()
            out_going.append(cp)
        for k, (dst, dev) in enumerate(recvs):
            remote(k, dst, dst, dev).wait_recv()
            if forwards:
                src, fdst, fdev, _ = forwards[k]
                cp = remote(len(sends) + k, src, fdst, fdev)
                cp.start()
                out_going.append(cp)
        for k, (_, _, fdev, landing) in enumerate(forwards):
            remote(len(sends) + k, landing, landing, fdev).wait_recv()
        for cp in out_going:
            cp.wait_send()
        for cp in started:
            cp.wait()

    return pl.pallas_call(
        body, name=name,
        in_specs=[HBM_SPEC] * n,
        out_specs=[HBM_SPEC] * len(out_shapes),
        out_shape=out_shapes,
        scratch_shapes=[pltpu.SemaphoreType.DMA((n_remote,)), pltpu.SemaphoreType.DMA((n_remote,)),
                        pltpu.SemaphoreType.DMA((max(n_local, 1),))],
    )(*arrays)


def _layer_half(depth, which):
    assert depth % 2 == 0
    return pl.ds(which * (depth // 2), depth // 2)


def gather_chips(shards):
    n = len(shards)
    depth = shards[0].shape[0]

    def plan(ins, outs, place):
        x, y, c, chips = place
        me, sib = 2 * x + y, (x, y, 1 - c)
        mine, theirs = _layer_half(depth, c), _layer_half(depth, 1 - c)
        sends = [(ins[p].at[mine], outs[p].at[me, mine], (px, py, c)) for p in range(n) for (px, py) in chips]
        recvs = [(outs[p].at[2 * px + py, mine], (px, py, c)) for p in range(n) for (px, py) in chips]
        forwards = [(outs[p].at[2 * px + py, mine], outs[p].at[2 * px + py, mine], sib, outs[p].at[2 * px + py, theirs])
                    for p in range(n) for (px, py) in chips]
        return [], sends, recvs, forwards

    out_shapes = [jax.ShapeDtypeStruct((N_CHIPS,) + a.shape, a.dtype) for a in shards]
    return _exchange("gather_chips", shards, out_shapes, plan, 0, 6 * n)


def swap_halves(fulls):
    n = len(fulls)
    depth = fulls[0].shape[0]

    def plan(ins, outs, place):
        x, y, c, _ = place
        sib = (x, y, 1 - c)
        sends = [(ins[p].at[_layer_half(depth, 1 - c)], outs[p], sib) for p in range(n)]
        return [], sends, [(outs[p], sib) for p in range(n)], []

    out_shapes = [jax.ShapeDtypeStruct((depth // 2,) + a.shape[1:], a.dtype) for a in fulls]
    return _exchange("swap_halves", fulls, out_shapes, plan, 0, n)


def scatter_chips(fulls):
    n = len(fulls)

    def plan(ins, outs, place):
        x, y, c, chips = place
        sends = [(ins[p].at[:, 2 * px + py], outs[p].at[j], (px, py, c))
                 for p in range(n) for j, (px, py) in enumerate(chips)]
        recvs = [(outs[p].at[j], (px, py, c)) for p in range(n) for j, (px, py) in enumerate(chips)]
        return [], sends, recvs, []

    out_shapes = [jax.ShapeDtypeStruct((3, a.shape[0]) + a.shape[2:], a.dtype) for a in fulls]
    return _exchange("scatter_chips", fulls, out_shapes, plan, 0, 3 * n)


def swap_sibling(arrays):
    n = len(arrays)

    def plan(ins, outs, place):
        x, y, c, _ = place
        sib = (x, y, 1 - c)
        return [], [(ins[p], outs[p], sib) for p in range(n)], [(outs[p], sib) for p in range(n)], []

    return _exchange("swap_sibling", arrays, [jax.ShapeDtypeStruct(a.shape, a.dtype) for a in arrays], plan, 0, n)


def gather_all(vec):
    def plan(ins, outs, place):
        x, y, c, _ = place
        me = 4 * x + 2 * y + c
        flips = [(fx, fy, fc) for fx in (0, 1) for fy in (0, 1) for fc in (0, 1)][1:]
        peers = [(1 - x if fx else x, 1 - y if fy else y, 1 - c if fc else c) for fx, fy, fc in flips]
        local = [(ins[0], outs[0].at[me])]
        sends = [(ins[0], outs[0].at[me], dev) for dev in peers]
        recvs = [(outs[0].at[4 * dev[0] + 2 * dev[1] + dev[2]], dev) for dev in peers]
        return local, sends, recvs, []

    return _exchange("gather_all", [vec], [jax.ShapeDtypeStruct((8,) + vec.shape, vec.dtype)], plan, 1, 7)[0]


def _pick_rows(rows, cols, block_bytes=1024 * 1024):
    best = None
    for t in range(16, rows + 1, 16):
        if rows % t == 0 and t * cols * 4 <= block_bytes:
            best = t
    return best or rows


def sum_rows_of(own, stacked, with_bf16=False):
    rows, cols = own.shape
    n = stacked.shape[0]
    tr = _pick_rows(rows, cols)

    def body(own_ref, st_ref, o_ref, *narrow):
        acc = own_ref[...]
        for k in range(n):
            acc = acc + st_ref[k].astype(F32)
        o_ref[...] = acc
        if with_bf16:
            narrow[0][...] = acc.astype(BF16)

    spec = pl.BlockSpec((tr, cols), lambda i: (i, 0))
    return pl.pallas_call(
        body, name="sum_rows_of",
        grid=(rows // tr,),
        in_specs=[spec, pl.BlockSpec((n, tr, cols), lambda i: (0, i, 0))],
        out_specs=[spec, spec] if with_bf16 else spec,
        out_shape=([jax.ShapeDtypeStruct((rows, cols), F32), jax.ShapeDtypeStruct((rows, cols), BF16)]
                   if with_bf16 else jax.ShapeDtypeStruct((rows, cols), F32)),
        compiler_params=_params("parallel"),
    )(own, stacked)


def sum_stack(stacked):
    n, rows, cols = stacked.shape
    tr = _pick_rows(rows, cols)

    def body(st_ref, o_ref):
        acc = st_ref[0]
        for k in range(1, n):
            acc = acc + st_ref[k]
        o_ref[...] = acc

    return pl.pallas_call(
        body, name="sum_stack",
        grid=(rows // tr,),
        in_specs=[pl.BlockSpec((n, tr, cols), lambda i: (0, i, 0))],
        out_specs=pl.BlockSpec((tr, cols), lambda i: (i, 0)),
        out_shape=jax.ShapeDtypeStruct((rows, cols), F32),
        compiler_params=_params("parallel"),
    )(stacked)


def adamw(w, ga, gb, m, v):
    rows, cols = w.shape
    tr = _pick_rows(rows, cols)
    c1 = 1.0 - ADAM_B1 ** ADAM_STEP
    c2 = 1.0 - ADAM_B2 ** ADAM_STEP
    ins = [w, ga] + ([gb] if gb is not None else []) + [m, v]

    def body(*refs):
        w_ref, ga_ref = refs[0], refs[1]
        m_ref, v_ref, g_out, d_out, m_out, v_out = refs[-6:]
        g = ga_ref[...]
        if gb is not None:
            g = g + refs[2][...]
        mn = ADAM_B1 * m_ref[...] + (1.0 - ADAM_B1) * g
        vn = ADAM_B2 * v_ref[...] + (1.0 - ADAM_B2) * (g * g)
        m_hat = mn / c1
        v_hat = vn / c2
        g_out[...] = g
        d_out[...] = -ADAM_LR * (m_hat / (jnp.sqrt(v_hat) + ADAM_EPS) + ADAM_WD * w_ref[...])
        m_out[...] = mn
        v_out[...] = vn

    spec = pl.BlockSpec((tr, cols), lambda i: (i, 0))
    return pl.pallas_call(
        body, name="adamw",
        grid=(rows // tr,),
        in_specs=[spec] * len(ins),
        out_specs=[spec] * 4,
        out_shape=[jax.ShapeDtypeStruct((rows, cols), F32)] * 4,
        compiler_params=_params("parallel"),
    )(*ins)


WEIGHT_NAMES = ["ffn1_norm", "ffn1_w_gu", "ffn1_w_down", "mix_norm", "w_in", "q_lat_norm", "kv_lat_norm", "w_uq", "w_uk",
                "w_uv", "q_norm", "k_norm", "w_pool", "pool_scale", "w_out", "ffn2_norm", "ffn2_w_gu", "ffn2_w_down"]
COL_SHARDED = ["ffn1_w_gu", "w_in", "w_uq", "w_uk", "w_uv", "ffn2_w_gu"]
ROW_SHARDED = ["ffn1_w_down", "w_out", "ffn2_w_down"]
SHARDED = [n for n in WEIGHT_NAMES if n in COL_SHARDED or n in ROW_SHARDED]
REPLICATED = [n for n in WEIGHT_NAMES if n not in SHARDED]


def _join_chips(g, name):
    _, depth, r, c = g.shape
    if name in COL_SHARDED:
        return jnp.transpose(g, (1, 2, 0, 3)).reshape(depth, r, N_CHIPS * c)
    return jnp.transpose(g, (1, 0, 2, 3)).reshape(depth, N_CHIPS * r, c)


def _split_chips(full, name):
    depth, r, c = full.shape
    if name in COL_SHARDED:
        return jnp.transpose(full.reshape(depth, r, N_CHIPS, c // N_CHIPS), (0, 2, 1, 3))
    return full.reshape(depth, N_CHIPS, r // N_CHIPS, c)


def _pack(vs):
    flat = jnp.concatenate([v.reshape(-1) for v in vs])
    pad = (-flat.shape[0]) % (8 * 128)
    return jnp.pad(flat, (0, pad)).reshape(-1, 128)


def _unpack(packed, like):
    flat = packed.reshape(-1)
    out, at = [], 0
    for v in like:
        out.append(flat[at:at + v.size].reshape(v.shape))
        at += v.size
    return out


def kernel(x, ffn1_norm, ffn1_w_gu, ffn1_w_down, mix_norm, w_in, q_lat_norm, kv_lat_norm, w_uq, w_uk, w_uv, q_norm, k_norm, w_pool, pool_scale, w_out, ffn2_norm, ffn2_w_gu, ffn2_w_down, loss_target, m_ffn1_norm, m_ffn1_w_gu, m_ffn1_w_down, m_mix_norm, m_w_in, m_q_lat_norm, m_kv_lat_norm, m_w_uq, m_w_uk, m_w_uv, m_q_norm, m_k_norm, m_w_pool, m_pool_scale, m_w_out, m_ffn2_norm, m_ffn2_w_gu, m_ffn2_w_down, v_ffn1_norm, v_ffn1_w_gu, v_ffn1_w_down, v_mix_norm, v_w_in, v_q_lat_norm, v_kv_lat_norm, v_w_uq, v_w_uk, v_w_uv, v_q_norm, v_k_norm, v_w_pool, v_pool_scale, v_w_out, v_ffn2_norm, v_ffn2_w_gu, v_ffn2_w_down):
    args = locals()
    weights = {n: args[n] for n in WEIGHT_NAMES}
    moments_m = {n: args["m_" + n] for n in WEIGHT_NAMES}
    moments_v = {n: args["v_" + n] for n in WEIGHT_NAMES}

    chip = 2 * lax.axis_index("x") + lax.axis_index("y")
    core = lax.axis_index("c")
    shards = [weights[n].astype(BF16) for n in SHARDED]
    gathered = {n: lax.dynamic_update_index_in_dim(g, own, chip, 0)
                for n, own, g in zip(SHARDED, shards, gather_chips(shards))}
    full = {n: _join_chips(gathered[n], n) for n in SHARDED if "ffn" not in n}
    for n in REPLICATED:
        full[n] = weights[n].astype(BF16) if n == "w_pool" else weights[n]
    w = layout_weights(full)
    wd1 = _join_chips(gathered["ffn1_w_down"], "ffn1_w_down")
    wd2 = _join_chips(gathered["ffn2_w_down"], "ffn2_w_down")

    loss_part, grad_x, grads = local_step(x[0], loss_target[0], w, gathered["ffn1_w_gu"], wd1, gathered["ffn2_w_gu"], wd2)
    loss = lax.psum(loss_part[0, 0], ("x", "y", "c"))

    split = [grads[n] if n.endswith("w_gu") else _split_chips(grads[n], n) for n in SHARDED]
    half = split[0].shape[0] // 2

    def add(own, stacked, with_bf16=False):
        rows = own.size // own.shape[-1]
        total = sum_rows_of(own.reshape(rows, -1), stacked.reshape(stacked.shape[0], rows, -1), with_bf16)
        return [t.reshape(own.shape) for t in total] if with_bf16 else total.reshape(own.shape)

    from_sibling = swap_halves(split)
    chip_sums = [add(lax.dynamic_slice_in_dim(g4, core * half, half, axis=0), other[None], with_bf16=True)
                 for g4, other in zip(split, from_sibling)]
    landed = scatter_chips([narrow for _, narrow in chip_sums])
    totals = [add(lax.dynamic_index_in_dim(wide, chip, 1, keepdims=False), land)
              for (wide, _), land in zip(chip_sums, landed)]
    out = {}
    for n, mine, theirs in zip(SHARDED, totals, swap_sibling(totals)):
        g = jnp.concatenate([jnp.where(core == 0, mine, theirs), jnp.where(core == 0, theirs, mine)], axis=0)
        shape = weights[n].shape
        flat = lambda a: a.reshape(shape[0] * shape[1], shape[2])
        res = adamw(flat(weights[n]), flat(g), None, flat(moments_m[n]), flat(moments_v[n]))
        out[n] = [r.reshape(shape) for r in res]

    rep_g = [grads[n] for n in REPLICATED]
    total = sum_stack(gather_all(_pack(rep_g)))
    res = adamw(_pack([weights[n] for n in REPLICATED]), total, None,
                _pack([moments_m[n] for n in REPLICATED]), _pack([moments_v[n] for n in REPLICATED]))
    unpacked = [_unpack(r, rep_g) for r in res]
    for k, n in enumerate(REPLICATED):
        out[n] = [u[k] for u in unpacked]

    return (loss, grad_x[None], *[out[n][0] for n in WEIGHT_NAMES], *[out[n][1] for n in WEIGHT_NAMES],
            *[out[n][2] for n in WEIGHT_NAMES], *[out[n][3] for n in WEIGHT_NAMES])
```

```python
import jax
import jax.numpy as jnp
from jax import lax
from jax.experimental import pallas as pl
from jax.experimental.pallas import tpu as pltpu

F32 = jnp.float32
BF16 = jnp.bfloat16
MESH = pl.DeviceIdType.MESH

DEPTH = 4
D_MODEL = 1024
D_FF = 2816
FF_BLOCK = D_FF // 2
N_CHIPS = 4
HEADS = 8
HEAD_PAD = 128
QK_NOPE = 64
QK_ROPE = 32
QK_HEAD = QK_NOPE + QK_ROPE
V_HEAD = 64
Q_LORA = 384
KV_LORA = 256
POOL_W = 512
POOL_WINDOWS = (2, 4, 8, 16)
POOL_HALO = 8
GROUP = 128
D_IN = Q_LORA + KV_LORA + QK_ROPE + POOL_W
Z_W = 1280
Z_CKV = Q_LORA
Z_POOL = Q_LORA + KV_LORA
Z_KPE = Z_POOL + POOL_W
ATT_W = HEADS * HEAD_PAD
CAT_W = ATT_W + POOL_W
EPS = 1e-6
ROPE_THETA = 10000.0
ADAM_LR, ADAM_B1, ADAM_B2, ADAM_EPS, ADAM_WD, ADAM_STEP = 0.001, 0.9, 0.999, 1e-08, 0.01, 10
LOG2_E = 1.4426950408889634
LN_2 = 0.6931471805599453

VMEM_LIMIT = 56 * 1024 * 1024
VMEM_LIMIT_FFN_BWD = 62 * 1024 * 1024


def _params(*sem, vmem=VMEM_LIMIT):
    return pltpu.CompilerParams(dimension_semantics=sem, vmem_limit_bytes=vmem)


def _dot(a, b):
    return jnp.dot(a, b, preferred_element_type=F32)


def _dot_nt(a, b):
    return lax.dot_general(a, b, (((1,), (1,)), ((), ())), preferred_element_type=F32)


def _dot_tn(a, b):
    return lax.dot_general(a, b, (((0,), (0,)), ((), ())), preferred_element_type=F32)


def _norm_fwd(x, n):
    rstd = lax.rsqrt(jnp.sum(x * x, axis=-1, keepdims=True) * (1.0 / n) + EPS)
    return x * rstd, rstd


def _norm_bwd(dy, g, xhat, rstd, n):
    dxhat = dy * g
    dx = rstd * (dxhat - xhat * (jnp.sum(dxhat * xhat, axis=-1, keepdims=True) * (1.0 / n)))
    return dx, dy * xhat


def _row_tile(s, want):
    t = min(s, want)
    assert s % t == 0
    return t


def ffn_fwd(x, gn, wgu4, wd, layer, tm=512):
    s = x.shape[0]
    tm = _row_tile(s, tm)

    def body(x_ref, gn_ref, wg_ref, wu_ref, wd_ref, o_ref, g_ref, u_ref, h_s, acc_s):
        j = pl.program_id(1)

        @pl.when(j == 0)
        def _():
            xhat, _ = _norm_fwd(x_ref[...], D_MODEL)
            h_s[...] = (xhat * gn_ref[...]).astype(BF16)
            acc_s[...] = jnp.zeros_like(acc_s)

        h = h_s[...]
        g = _dot(h, wg_ref[...])
        u = _dot(h, wu_ref[...])
        g_ref[...] = g.astype(BF16)
        u_ref[...] = u.astype(BF16)
        a = (g * jax.nn.sigmoid(g) * u).astype(BF16)
        acc_s[...] += _dot(a, wd_ref[...])

        @pl.when(j == 1)
        def _():
            o_ref[...] = x_ref[...] + 0.5 * acc_s[...]

    return pl.pallas_call(
        body, name="ffn_fwd",
        grid=(s // tm, 2),
        in_specs=[
            pl.BlockSpec((tm, D_MODEL), lambda i, j: (i, 0)),
            pl.BlockSpec((1, D_MODEL), lambda i, j: (0, 0)),
            pl.BlockSpec((None, None, D_MODEL, FF_BLOCK), lambda i, j: (j, layer, 0, 0)),
            pl.BlockSpec((None, None, D_MODEL, FF_BLOCK), lambda i, j: (j + 2, layer, 0, 0)),
            pl.BlockSpec((None, FF_BLOCK, D_MODEL), lambda i, j: (layer, j, 0)),
        ],
        out_specs=[pl.BlockSpec((tm, D_MODEL), lambda i, j: (i, 0)),
                   pl.BlockSpec((tm, FF_BLOCK), lambda i, j: (i, j)), pl.BlockSpec((tm, FF_BLOCK), lambda i, j: (i, j))],
        out_shape=[jax.ShapeDtypeStruct((s, D_MODEL), F32), jax.ShapeDtypeStruct((s, D_FF), BF16),
                   jax.ShapeDtypeStruct((s, D_FF), BF16)],
        scratch_shapes=[pltpu.VMEM((tm, D_MODEL), BF16), pltpu.VMEM((tm, D_MODEL), F32)],
        compiler_params=_params("parallel", "arbitrary"),
    )(x, gn, wgu4, wgu4, wd)


def ffn_bwd(x, dy, gn, g_pre, u_pre, wgu4, wd, layer, tm=512):
    s = x.shape[0]
    tm = _row_tile(s, tm)
    halves = 2 if tm % 32 == 0 else 1

    def body(x_ref, dy_ref, gn_ref, g_ref, u_ref, wg_ref, wu_ref, wd_ref,
             dx_ref, h_ref, a_ref, dg_ref, du_ref, dyh_ref, dgn_ref, dh_s):
        i, j = pl.program_id(0), pl.program_id(1)

        @pl.when(j == 0)
        def _():
            xhat, _ = _norm_fwd(x_ref[...], D_MODEL)
            h_ref[...] = (xhat * gn_ref[...]).astype(BF16)
            dyh_ref[...] = (0.5 * dy_ref[...]).astype(BF16)
            dh_s[...] = jnp.zeros_like(dh_s)

        for r in range(halves):
            rows = slice(r * (tm // halves), (r + 1) * (tm // halves))
            g = g_ref[rows, :].astype(F32)
            u = u_ref[rows, :].astype(F32)
            sg = jax.nn.sigmoid(g)
            silu = g * sg
            a_ref[rows, :] = (silu * u).astype(BF16)
            da = _dot_nt(dyh_ref[rows, :], wd_ref[...])
            dg = (da * u * (sg * (1.0 + g * (1.0 - sg)))).astype(BF16)
            du = (da * silu).astype(BF16)
            dg_ref[rows, :] = dg
            du_ref[rows, :] = du
            dh_s[rows, :] += _dot_nt(dg, wg_ref[...]) + _dot_nt(du, wu_ref[...])

        @pl.when(j == 1)
        def _():
            xhat, rstd = _norm_fwd(x_ref[...], D_MODEL)
            dxn, dgrow = _norm_bwd(dh_s[...], gn_ref[...], xhat, rstd, D_MODEL)
            dx_ref[...] = dy_ref[...] + dxn
            part = jnp.sum(dgrow, axis=0, keepdims=True)

            @pl.when(i == 0)
            def _():
                dgn_ref[...] = part

            @pl.when(i > 0)
            def _():
                dgn_ref[...] += part

    row = lambda i, j: (i, 0)
    return pl.pallas_call(
        body, name="ffn_bwd",
        grid=(s // tm, 2),
        in_specs=[
            pl.BlockSpec((tm, D_MODEL), row),
            pl.BlockSpec((tm, D_MODEL), row),
            pl.BlockSpec((1, D_MODEL), lambda i, j: (0, 0)),
            pl.BlockSpec((tm, FF_BLOCK), lambda i, j: (i, j)),
            pl.BlockSpec((tm, FF_BLOCK), lambda i, j: (i, j)),
            pl.BlockSpec((None, None, D_MODEL, FF_BLOCK), lambda i, j: (j, layer, 0, 0)),
            pl.BlockSpec((None, None, D_MODEL, FF_BLOCK), lambda i, j: (j + 2, layer, 0, 0)),
            pl.BlockSpec((None, FF_BLOCK, D_MODEL), lambda i, j: (layer, j, 0)),
        ],
        out_specs=[
            pl.BlockSpec((tm, D_MODEL), row),
            pl.BlockSpec((tm, D_MODEL), row),
            pl.BlockSpec((tm, FF_BLOCK), lambda i, j: (i, j)),
            pl.BlockSpec((tm, FF_BLOCK), lambda i, j: (i, j)),
            pl.BlockSpec((tm, FF_BLOCK), lambda i, j: (i, j)),
            pl.BlockSpec((tm, D_MODEL), row),
            pl.BlockSpec((1, D_MODEL), lambda i, j: (0, 0)),
        ],
        out_shape=[
            jax.ShapeDtypeStruct((s, D_MODEL), F32),
            jax.ShapeDtypeStruct((s, D_MODEL), BF16),
            jax.ShapeDtypeStruct((s, D_FF), BF16),
            jax.ShapeDtypeStruct((s, D_FF), BF16),
            jax.ShapeDtypeStruct((s, D_FF), BF16),
            jax.ShapeDtypeStruct((s, D_MODEL), BF16),
            jax.ShapeDtypeStruct((1, D_MODEL), F32),
        ],
        scratch_shapes=[pltpu.VMEM((tm, D_MODEL), F32)],
        compiler_params=_params("arbitrary", "arbitrary", vmem=VMEM_LIMIT_FFN_BWD),
    )(x, dy, gn, g_pre, u_pre, wgu4, wgu4, wd)


def matmul_tn(a, b, tk=2048, tn_max=1536, into=None, at=None):
    s, m = a.shape
    n = b.shape[1]
    tk = _row_tile(s, tk)
    tn = n
    if n > tn_max:
        tn = n // 2
    assert n % tn == 0 and tn % 128 == 0
    nk = s // tk

    def body(a_ref, b_ref, *rest):
        o_ref, acc_s = rest[-2:]
        k = pl.program_id(1)

        @pl.when(k == 0)
        def _():
            acc_s[...] = jnp.zeros_like(acc_s)

        acc_s[...] += _dot_tn(a_ref[...].astype(BF16), b_ref[...].astype(BF16))

        @pl.when(k == nk - 1)
        def _():
            o_ref[...] = acc_s[...]

    in_specs = [pl.BlockSpec((tk, m), lambda j, k: (k, 0)), pl.BlockSpec((tk, tn), lambda j, k: (k, j))]
    if into is None:
        operands, aliases = (a, b), {}
        out_specs = pl.BlockSpec((m, tn), lambda j, k: (0, j))
        out_shape = jax.ShapeDtypeStruct((m, n), F32)
    else:
        assert into.shape[-2:] == (m, tn) and into.dtype == F32
        lead = into.ndim - 2
        operands, aliases = (a, b, into), {2: 0}
        in_specs.append(pl.BlockSpec(memory_space=pl.ANY))
        out_specs = pl.BlockSpec((None,) * lead + (m, tn), lambda j, k: tuple(at(j)) + (0, 0))
        out_shape = jax.ShapeDtypeStruct(into.shape, F32)
    return pl.pallas_call(
        body, name="matmul_tn",
        grid=(n // tn, nk),
        in_specs=in_specs,
        out_specs=out_specs,
        out_shape=out_shape,
        scratch_shapes=[pltpu.VMEM((m, tn), F32)],
        input_output_aliases=aliases,
        compiler_params=_params("parallel", "arbitrary"),
    )(*operands)


def _rope_bwd(dy, c, s1, s2):
    return dy * c + pltpu.roll(dy * s1, 16, 1) + pltpu.roll(dy * s2, HEAD_PAD - 16, 1)


def rope_tables(s):
    pos = jnp.arange(s, dtype=F32)
    inv = ROPE_THETA ** (-jnp.arange(0, QK_ROPE, 2, dtype=F32) / QK_ROPE)
    ang = pos[:, None] * inv[None, :]
    cos, sin = jnp.cos(ang), jnp.sin(ang)
    z16 = jnp.zeros((s, 16), F32)
    c = jnp.concatenate([jnp.ones((s, QK_NOPE), F32), cos, cos, z16, z16], axis=1)
    s1 = jnp.concatenate([jnp.zeros((s, QK_NOPE), F32), -sin, z16, z16, z16], axis=1)
    s2 = jnp.concatenate([jnp.zeros((s, QK_NOPE), F32), z16, sin, z16, z16], axis=1)
    return c, s1, s2


def swap_rope_halves(a):
    lead = a.shape[:-1]
    h = a.reshape(lead + (-1, HEAD_PAD))
    half = QK_ROPE // 2
    zeros = lambda n: jnp.zeros(h.shape[:-1] + (n,), a.dtype)
    sw = jnp.concatenate([zeros(QK_NOPE), h[..., QK_NOPE + half:QK_HEAD], h[..., QK_NOPE:QK_NOPE + half],
                          zeros(HEAD_PAD - QK_HEAD)], axis=-1)
    return sw.reshape(a.shape)


def mix_in_fwd(x, gm, win, qln, kvln, wuq, wuq_sw, wuk, wuv, qn, qn_sw, kn, kn_sw, rc, rs, tm=512):
    s = x.shape[0]
    tm = _row_tile(s, tm)
    scale = QK_HEAD ** -0.5 * LOG2_E
    half = QK_ROPE // 2

    def body(x_ref, gm_ref, win_ref, qln_ref, kvln_ref, wuq_ref, wuqs_ref, wuk_ref, wuv_ref,
             qn_ref, qns_ref, kn_ref, kns_ref, rc_ref, rs_ref, z_ref, q_ref, k_ref, v_ref, vt_ref):
        xhat, _ = _norm_fwd(x_ref[...], D_MODEL)
        h = (xhat * gm_ref[...]).astype(BF16)
        z = _dot(h, win_ref[...])
        z_ref[...] = z
        cq, _ = _norm_fwd(z[:, :Q_LORA], Q_LORA)
        cqn = (cq * qln_ref[...]).astype(BF16)
        ckv, _ = _norm_fwd(z[:, Z_CKV:Z_POOL], KV_LORA)
        ckvn = (ckv * kvln_ref[...]).astype(BF16)
        kpe = z[:, Z_KPE:]
        lane = lax.broadcasted_iota(jnp.int32, kpe.shape, 1)
        first = (lane >= QK_NOPE) & (lane < QK_NOPE + half)
        second = (lane >= QK_NOPE + half) & (lane < QK_HEAD)
        kpe_sw = jnp.where(first, pltpu.roll(kpe, HEAD_PAD - half, 1), jnp.where(second, pltpu.roll(kpe, half, 1), 0.0))
        q_raw = _dot(cqn, wuq_ref[...])
        q_sw = _dot(cqn, wuqs_ref[...])
        k_raw = _dot(ckvn, wuk_ref[...])
        v = _dot(ckvn, wuv_ref[...])
        v_ref[...] = v.astype(BF16)
        vt_ref[...] = jnp.transpose(v).astype(BF16)
        c, sn = rc_ref[...], rs_ref[...]
        tq, tq_sw = qn_ref[...] * c * scale, qns_ref[...] * sn * scale
        tk, tk_sw = kn_ref[...] * c, kns_ref[...] * sn
        for hd in range(HEADS):
            lanes = slice(hd * HEAD_PAD, (hd + 1) * HEAD_PAD)
            qr = q_raw[:, lanes]
            _, q_rstd = _norm_fwd(qr, QK_HEAD)
            q_ref[:, lanes] = (q_rstd * (qr * tq + q_sw[:, lanes] * tq_sw)).astype(BF16)
            kr = k_raw[:, lanes] + kpe
            _, k_rstd = _norm_fwd(kr, QK_HEAD)
            k_ref[:, lanes] = (k_rstd * (kr * tk + kpe_sw * tk_sw)).astype(BF16)

    full = lambda shape: pl.BlockSpec(shape, lambda i: (0,) * len(shape))
    row = lambda w: pl.BlockSpec((tm, w), lambda i: (i, 0))
    return pl.pallas_call(
        body, name="mix_in_fwd",
        grid=(s // tm,),
        in_specs=[row(D_MODEL), full((1, D_MODEL)), full((D_MODEL, Z_W)), full((1, Q_LORA)), full((1, KV_LORA)),
                  full((Q_LORA, ATT_W)), full((Q_LORA, ATT_W)), full((KV_LORA, ATT_W)), full((KV_LORA, ATT_W)),
                  full((1, HEAD_PAD)), full((1, HEAD_PAD)), full((1, HEAD_PAD)), full((1, HEAD_PAD)),
                  row(HEAD_PAD), row(HEAD_PAD)],
        out_specs=[row(Z_W), row(ATT_W), row(ATT_W), row(ATT_W), pl.BlockSpec((ATT_W, tm), lambda i: (0, i))],
        out_shape=[jax.ShapeDtypeStruct((s, Z_W), F32)] + [jax.ShapeDtypeStruct((s, ATT_W), BF16)] * 3 + [
            jax.ShapeDtypeStruct((ATT_W, s), BF16)],
        compiler_params=_params("parallel"),
    )(x, gm, win, qln, kvln, wuq, wuq_sw, wuk, wuv, qn, qn_sw, kn, kn_sw, rc, rs)


def mla_bwd(z, dq_t, dk, dv, dp, qln, kvln, wuq, wuk, wuv, qn, kn, rc, rs1, rs2, tm=512):
    s = z.shape[0]
    tm = _row_tile(s, tm)
    scale = QK_HEAD ** -0.5

    def body(z_ref, dqt_ref, dk_ref, dv_ref, dp_ref, qln_ref, kvln_ref, wuq_ref, wuk_ref, wuv_ref, qn_ref, kn_ref,
             rc_ref, rs1_ref, rs2_ref,
             dz_ref, cqn_ref, ckvn_ref, dqr_ref, dkr_ref, dvb_ref, dqln_ref, dkvln_ref, dqn_ref, dkn_ref):
        i = pl.program_id(0)
        z = z_ref[...]
        cq, cq_rstd = _norm_fwd(z[:, :Q_LORA], Q_LORA)
        cqn = (cq * qln_ref[...]).astype(BF16)
        ckv, ckv_rstd = _norm_fwd(z[:, Z_CKV:Z_POOL], KV_LORA)
        ckvn = (ckv * kvln_ref[...]).astype(BF16)
        kpe = z[:, Z_KPE:]
        cqn_ref[...] = cqn
        ckvn_ref[...] = ckvn
        q_raw = _dot(cqn, wuq_ref[...])
        k_raw = _dot(ckvn, wuk_ref[...])
        c, s1, s2 = rc_ref[...], rs1_ref[...], rs2_ref[...]
        lane = lax.broadcasted_iota(jnp.int32, (tm, HEAD_PAD), 1)
        rope_lanes = (lane >= QK_NOPE) & (lane < QK_HEAD)
        dkpe = jnp.zeros((tm, HEAD_PAD), F32)
        dqn = jnp.zeros((tm, HEAD_PAD), F32)
        dkn = jnp.zeros((tm, HEAD_PAD), F32)
        for hd in range(HEADS):
            lanes = slice(hd * HEAD_PAD, (hd + 1) * HEAD_PAD)
            qh, q_rstd = _norm_fwd(q_raw[:, lanes], QK_HEAD)
            dqh = jnp.transpose(dqt_ref[lanes, :]) * scale
            dqr, dg = _norm_bwd(_rope_bwd(dqh, c, s1, s2), qn_ref[...], qh, q_rstd, QK_HEAD)
            dqn += dg
            dqr_ref[:, lanes] = dqr.astype(BF16)
            kh, k_rstd = _norm_fwd(k_raw[:, lanes] + kpe, QK_HEAD)
            dkr, dg = _norm_bwd(_rope_bwd(dk_ref[:, lanes], c, s1, s2), kn_ref[...], kh, k_rstd, QK_HEAD)
            dkn += dg
            dkr_ref[:, lanes] = dkr.astype(BF16)
            dkpe += jnp.where(rope_lanes, dkr, 0.0)
        dvb = dv_ref[...].astype(BF16)
        dvb_ref[...] = dvb
        dcqn = _dot_nt(dqr_ref[...], wuq_ref[...])
        dckvn = _dot_nt(dkr_ref[...], wuk_ref[...]) + _dot_nt(dvb, wuv_ref[...])
        dcq, dqln = _norm_bwd(dcqn, qln_ref[...], cq, cq_rstd, Q_LORA)
        dckv, dkvln = _norm_bwd(dckvn, kvln_ref[...], ckv, ckv_rstd, KV_LORA)
        dz_ref[:, :Q_LORA] = dcq.astype(BF16)
        dz_ref[:, Z_CKV:Z_POOL] = dckv.astype(BF16)
        dz_ref[:, Z_POOL:Z_KPE] = dp_ref[...].astype(BF16)
        dz_ref[:, Z_KPE:] = dkpe.astype(BF16)
        parts = [(dqln_ref, dqln), (dkvln_ref, dkvln), (dqn_ref, dqn), (dkn_ref, dkn)]

        @pl.when(i == 0)
        def _():
            for ref, val in parts:
                ref[...] = jnp.sum(val, axis=0, keepdims=True)

        @pl.when(i > 0)
        def _():
            for ref, val in parts:
                ref[...] += jnp.sum(val, axis=0, keepdims=True)

    full = lambda shape: pl.BlockSpec(shape, lambda i: (0,) * len(shape))
    row = lambda w: pl.BlockSpec((tm, w), lambda i: (i, 0))
    return pl.pallas_call(
        body, name="mla_bwd",
        grid=(s // tm,),
        in_specs=[row(Z_W), pl.BlockSpec((ATT_W, tm), lambda i: (0, i)), row(ATT_W), row(ATT_W), row(POOL_W),
                  full((1, Q_LORA)), full((1, KV_LORA)), full((Q_LORA, ATT_W)), full((KV_LORA, ATT_W)),
                  full((KV_LORA, ATT_W)), full((1, HEAD_PAD)), full((1, HEAD_PAD)),
                  row(HEAD_PAD), row(HEAD_PAD), row(HEAD_PAD)],
        out_specs=[row(Z_W), row(Q_LORA), row(KV_LORA), row(ATT_W), row(ATT_W), row(ATT_W),
                   full((1, Q_LORA)), full((1, KV_LORA)), full((1, HEAD_PAD)), full((1, HEAD_PAD))],
        out_shape=[jax.ShapeDtypeStruct((s, Z_W), BF16), jax.ShapeDtypeStruct((s, Q_LORA), BF16),
                   jax.ShapeDtypeStruct((s, KV_LORA), BF16)] + [jax.ShapeDtypeStruct((s, ATT_W), BF16)] * 3 + [
                   jax.ShapeDtypeStruct((1, Q_LORA), F32), jax.ShapeDtypeStruct((1, KV_LORA), F32),
                   jax.ShapeDtypeStruct((1, HEAD_PAD), F32), jax.ShapeDtypeStruct((1, HEAD_PAD), F32)],
        compiler_params=_params("arbitrary"),
    )(z, dq_t, dk, dv, dp, qln, kvln, wuq, wuk, wuv, qn, kn, rc, rs1, rs2)


def mix_in_bwd(x, dx_in, dz, gm, win, tm=512):
    s = x.shape[0]
    tm = _row_tile(s, tm)

    def body(x_ref, dxin_ref, dz_ref, gm_ref, win_ref, dx_ref, h_ref, dgm_ref):
        i = pl.program_id(0)
        xhat, rstd = _norm_fwd(x_ref[...], D_MODEL)
        h_ref[...] = (xhat * gm_ref[...]).astype(BF16)
        dh = _dot_nt(dz_ref[...], win_ref[...])
        dxn, dgrow = _norm_bwd(dh, gm_ref[...], xhat, rstd, D_MODEL)
        dx_ref[...] = dxin_ref[...] + dxn
        part = jnp.sum(dgrow, axis=0, keepdims=True)

        @pl.when(i == 0)
        def _():
            dgm_ref[...] = part

        @pl.when(i > 0)
        def _():
            dgm_ref[...] += part

    full = lambda shape: pl.BlockSpec(shape, lambda i: (0,) * len(shape))
    row = lambda w: pl.BlockSpec((tm, w), lambda i: (i, 0))
    return pl.pallas_call(
        body, name="mix_in_bwd",
        grid=(s // tm,),
        in_specs=[row(D_MODEL), row(D_MODEL), row(Z_W), full((1, D_MODEL)), full((D_MODEL, Z_W))],
        out_specs=[row(D_MODEL), row(D_MODEL), full((1, D_MODEL))],
        out_shape=[jax.ShapeDtypeStruct((s, D_MODEL), F32), jax.ShapeDtypeStruct((s, D_MODEL), BF16),
                   jax.ShapeDtypeStruct((1, D_MODEL), F32)],
        compiler_params=_params("arbitrary"),
    )(x, dx_in, dz, gm, win)


ATT_CHUNK_FWD = 1024
ATT_CHUNK_BWD = 512


def flash_fwd(q, k, vt, tq=1024, tk=16384, chunk=ATT_CHUNK_FWD):
    s = q.shape[0]
    tq, tk = _row_tile(s, tq), _row_tile(s, tk)
    nk = s // tk
    tc = _row_tile(tk, chunk)
    sub = tk // tc

    def body(q_ref, k_ref, vt_ref, o_ref, lse_ref, m_s, l_s, acc_s):
        kk = pl.program_id(2)

        @pl.when(kk == 0)
        def _():
            m_s[...] = jnp.full_like(m_s, -jnp.inf)
            l_s[...] = jnp.zeros_like(l_s)
            acc_s[...] = jnp.zeros_like(acc_s)

        q_t = q_ref[...]

        def scores(c):
            return _dot_nt(k_ref[c * tc:(c + 1) * tc, :], q_t)

        m, l, acc = m_s[...], l_s[...], acc_s[...]
        s_next = scores(0)
        for c in range(sub):
            s_t = s_next
            if c + 1 < sub:
                s_next = scores(c + 1)
            m_new = jnp.maximum(m, jnp.max(s_t, axis=0, keepdims=True))
            alpha = jnp.exp2(m - m_new)
            p_t = jnp.exp2(s_t - m_new)
            l = alpha * l + jnp.sum(p_t, axis=0, keepdims=True)
            acc = alpha * acc + _dot(vt_ref[:, c * tc:(c + 1) * tc], p_t.astype(BF16))
            m = m_new
        m_s[...], l_s[...], acc_s[...] = m, l, acc

        @pl.when(kk == nk - 1)
        def _():
            o_ref[...] = jnp.transpose(acc_s[...] / l_s[...]).astype(BF16)
            lse_ref[...] = m_s[...] + jnp.log2(l_s[...])

    return pl.pallas_call(
        body, name="flash_fwd",
        grid=(HEADS, s // tq, nk),
        in_specs=[pl.BlockSpec((tq, HEAD_PAD), lambda h, i, kk: (i, h)),
                  pl.BlockSpec((tk, HEAD_PAD), lambda h, i, kk: (kk, h)),
                  pl.BlockSpec((HEAD_PAD, tk), lambda h, i, kk: (h, kk))],
        out_specs=[pl.BlockSpec((tq, HEAD_PAD), lambda h, i, kk: (i, h)),
                   pl.BlockSpec((None, 1, tq), lambda h, i, kk: (h, 0, i))],
        out_shape=[jax.ShapeDtypeStruct((s, ATT_W), BF16), jax.ShapeDtypeStruct((HEADS, 1, s), F32)],
        scratch_shapes=[pltpu.VMEM((1, tq), F32), pltpu.VMEM((1, tq), F32), pltpu.VMEM((HEAD_PAD, tq), F32)],
        compiler_params=_params("parallel", "parallel", "arbitrary"),
    )(q, k, vt)


def flash_bwd(q, k, v, do, lse_row, delta_row, tq=8192, tk=1024, chunk=ATT_CHUNK_BWD):
    s = q.shape[0]
    tq, tk = _row_tile(s, tq), _row_tile(s, tk)
    nq = s // tq
    tc = _row_tile(tq, chunk)
    sub = tq // tc

    def body(q_ref, k_ref, v_ref, do_ref, lse_ref, dl_ref, dqt_ref, dk_ref, dv_ref, kt_s, dk_s, dv_s):
        kb, qb = pl.program_id(1), pl.program_id(2)

        @pl.when(qb == 0)
        def _():
            kt_s[...] = jnp.transpose(k_ref[...].astype(F32)).astype(BF16)
            dk_s[...] = jnp.zeros_like(dk_s)
            dv_s[...] = jnp.zeros_like(dv_s)

        k_t, v_t = k_ref[...], v_ref[...]

        def products(c):
            rows = slice(c * tc, (c + 1) * tc)
            return _dot_nt(k_t, q_ref[rows, :]), _dot_nt(v_t, do_ref[rows, :])

        dk, dv = dk_s[...], dv_s[...]
        nxt = products(0)
        contribs = []
        for c in range(sub):
            rows = slice(c * tc, (c + 1) * tc)
            s_t, dp_t = nxt
            if c + 1 < sub:
                nxt = products(c + 1)
            p_t = jnp.exp2(s_t - lse_ref[:, rows])
            ds_t = (p_t * (dp_t - dl_ref[:, rows])).astype(BF16)
            dv = dv + _dot(p_t.astype(BF16), do_ref[rows, :])
            dk = dk + _dot(ds_t, q_ref[rows, :])
            contribs.append(_dot(kt_s[...], ds_t))
        dk_s[...], dv_s[...] = dk, dv
        contrib = jnp.concatenate(contribs, axis=1) if sub > 1 else contribs[0]
        cols = pl.ds(pl.multiple_of(qb * tq, tq), tq)

        @pl.when(kb == 0)
        def _():
            dqt_ref[:, cols] = contrib

        @pl.when(kb > 0)
        def _():
            dqt_ref[:, cols] += contrib

        @pl.when(qb == nq - 1)
        def _():
            dk_ref[...] = dk_s[...] * LN_2
            dv_ref[...] = dv_s[...]

    qspec = pl.BlockSpec((tq, HEAD_PAD), lambda h, kb, qb: (qb, h))
    kspec = pl.BlockSpec((tk, HEAD_PAD), lambda h, kb, qb: (kb, h))
    rowspec = pl.BlockSpec((None, 1, tq), lambda h, kb, qb: (h, 0, qb))
    return pl.pallas_call(
        body, name="flash_bwd",
        grid=(HEADS, s // tk, nq),
        in_specs=[qspec, kspec, kspec, qspec, rowspec, rowspec],
        out_specs=[pl.BlockSpec((HEAD_PAD, s), lambda h, kb, qb: (h, 0)), kspec, kspec],
        out_shape=[jax.ShapeDtypeStruct((ATT_W, s), F32), jax.ShapeDtypeStruct((s, ATT_W), F32),
                   jax.ShapeDtypeStruct((s, ATT_W), F32)],
        scratch_shapes=[pltpu.VMEM((HEAD_PAD, tk), BF16), pltpu.VMEM((tk, HEAD_PAD), F32),
                        pltpu.VMEM((tk, HEAD_PAD), F32)],
        compiler_params=_params("parallel", "arbitrary", "arbitrary"),
    )(q, k, v, do, lse_row, delta_row)


def _halo_specs(tm, w, n_tiles):
    per = tm // POOL_HALO
    last = n_tiles * per - 1
    return [pl.BlockSpec((POOL_HALO, w), lambda i: (jnp.maximum(i * per - 1, 0), 0)),
            pl.BlockSpec((tm, w), lambda i: (i, 0)),
            pl.BlockSpec((POOL_HALO, w), lambda i: (jnp.minimum((i + 1) * per, last), 0))]


def _fill_ext(ext_ref, prev, cur, nxt, i, n_tiles, tm):
    ext_ref[pl.ds(0, POOL_HALO), :] = jnp.where(i > 0, prev, 0.0)
    ext_ref[pl.ds(POOL_HALO, tm), :] = cur
    ext_ref[pl.ds(POOL_HALO + tm, POOL_HALO), :] = jnp.where(i < n_tiles - 1, nxt, 0.0)


def _window_sum(ext_ref, g, lo, hi, tm):
    lanes = pl.ds(g * GROUP, GROUP)
    acc = ext_ref[pl.ds(POOL_HALO + lo, tm), lanes]
    for d in range(lo + 1, hi + 1):
        acc = acc + ext_ref[pl.ds(POOL_HALO + d, tm), lanes]
    return acc


def _pool_mixed(ext_ref, cur, i, s, tm):
    row = i * tm + lax.broadcasted_iota(jnp.int32, (tm, 1), 0)
    out = []
    for g, w in enumerate(POOL_WINDOWS):
        left = w // 2
        right = w - 1 - left
        cnt = (jnp.minimum(row + right + 1, s) - jnp.maximum(row - left, 0)).astype(F32)
        out.append(_window_sum(ext_ref, g, -left, right, tm) / cnt - cur[:, g * GROUP:(g + 1) * GROUP])
    return out


def pool_out_fwd(x, z, o, wpool, pscale, wout, tm=512):
    s = x.shape[0]
    tm = _row_tile(s, tm)
    n_tiles = s // tm

    def body(x_ref, zp_ref, z_ref, zn_ref, o_ref, wp_ref, ps_ref, wo_ref, y_ref, cat_ref, ext_s):
        i = pl.program_id(0)
        cur = z_ref[:, Z_POOL:Z_KPE]
        _fill_ext(ext_s, zp_ref[:, Z_POOL:Z_KPE], cur, zn_ref[:, Z_POOL:Z_KPE], i, n_tiles, tm)
        mixed = _pool_mixed(ext_s, cur, i, s, tm)
        cat_ref[:, :ATT_W] = o_ref[...]
        for g in range(len(POOL_WINDOWS)):
            lanes = slice(g * GROUP, (g + 1) * GROUP)
            yg = _dot(mixed[g].astype(BF16), wp_ref[g])
            cat_ref[:, ATT_W + g * GROUP:ATT_W + (g + 1) * GROUP] = (yg * ps_ref[:, lanes]).astype(BF16)
        y_ref[...] = x_ref[...] + _dot(cat_ref[...], wo_ref[...])

    full = lambda shape: pl.BlockSpec(shape, lambda i: (0,) * len(shape))
    row = lambda w: pl.BlockSpec((tm, w), lambda i: (i, 0))
    return pl.pallas_call(
        body, name="pool_out_fwd",
        grid=(n_tiles,),
        in_specs=[row(D_MODEL)] + _halo_specs(tm, Z_W, n_tiles) + [
            row(ATT_W), full((len(POOL_WINDOWS), GROUP, GROUP)), full((1, POOL_W)), full((CAT_W, D_MODEL))],
        out_specs=[row(D_MODEL), row(CAT_W)],
        out_shape=[jax.ShapeDtypeStruct((s, D_MODEL), F32), jax.ShapeDtypeStruct((s, CAT_W), BF16)],
        scratch_shapes=[pltpu.VMEM((tm + 2 * POOL_HALO, POOL_W), F32)],
        compiler_params=_params("parallel"),
    )(x, z, z, z, o, wpool, pscale, wout)


def out_bwd(dx, wout, cat, tm=512):
    s = dx.shape[0]
    tm = _row_tile(s, tm)

    def body(dx_ref, wo_ref, cat_ref, do_ref, db_ref, dl_ref):
        dcat = _dot_nt(dx_ref[...].astype(BF16), wo_ref[...])
        do_ref[...] = dcat[:, :ATT_W].astype(BF16)
        db_ref[...] = dcat[:, ATT_W:]
        for hd in range(HEADS):
            lanes = slice(hd * HEAD_PAD, (hd + 1) * HEAD_PAD)
            dl_ref[hd] = jnp.sum(dcat[:, lanes] * cat_ref[:, lanes].astype(F32), axis=-1, keepdims=True)

    row = lambda w: pl.BlockSpec((tm, w), lambda i: (i, 0))
    return pl.pallas_call(
        body, name="out_bwd",
        grid=(s // tm,),
        in_specs=[row(D_MODEL), pl.BlockSpec((CAT_W, D_MODEL), lambda i: (0, 0)), row(CAT_W)],
        out_specs=[row(ATT_W), row(POOL_W), pl.BlockSpec((HEADS, tm, 1), lambda i: (0, i, 0))],
        out_shape=[jax.ShapeDtypeStruct((s, ATT_W), BF16), jax.ShapeDtypeStruct((s, POOL_W), F32),
                   jax.ShapeDtypeStruct((HEADS, s, 1), F32)],
        compiler_params=_params("parallel"),
    )(dx, wout, cat)


def pool_bwd(db, z, wpool, pscale, tm=512):
    s = db.shape[0]
    tm = _row_tile(s, tm)
    n_tiles = s // tm
    te = tm + 2 * POOL_HALO

    def body(dbp_ref, db_ref, dbn_ref, zp_ref, z_ref, zn_ref, wp_ref, ps_ref,
             dp_ref, mixed_ref, dys_ref, dps_ref, ext_s, text_s):
        i = pl.program_id(0)
        cur = z_ref[:, Z_POOL:Z_KPE]
        _fill_ext(ext_s, zp_ref[:, Z_POOL:Z_KPE], cur, zn_ref[:, Z_POOL:Z_KPE], i, n_tiles, tm)
        mixed = _pool_mixed(ext_s, cur, i, s, tm)
        _fill_ext(ext_s, dbp_ref[...], db_ref[...], dbn_ref[...], i, n_tiles, tm)
        erow = i * tm - POOL_HALO + lax.broadcasted_iota(jnp.int32, (te, 1), 0)
        dps = []
        for g, w in enumerate(POOL_WINDOWS):
            lanes = slice(g * GROUP, (g + 1) * GROUP)
            left = w // 2
            right = w - 1 - left
            mb = mixed[g].astype(BF16)
            mixed_ref[:, lanes] = mb
            dps.append(jnp.sum(db_ref[:, lanes] * _dot(mb, wp_ref[g]), axis=0, keepdims=True))
            dys_e = (ext_s[:, lanes] * ps_ref[:, lanes]).astype(BF16)
            dys_ref[:, lanes] = (db_ref[:, lanes] * ps_ref[:, lanes]).astype(BF16)
            dmix_e = _dot_nt(dys_e, wp_ref[g])
            cnt = jnp.minimum(erow + right + 1, s) - jnp.maximum(erow - left, 0)
            text_s[:, lanes] = dmix_e / jnp.maximum(cnt, 1).astype(F32)
            dp_ref[:, lanes] = _window_sum(text_s, g, -right, left, tm) - dmix_e[POOL_HALO:POOL_HALO + tm]
        part = jnp.concatenate(dps, axis=1)

        @pl.when(i == 0)
        def _():
            dps_ref[...] = part

        @pl.when(i > 0)
        def _():
            dps_ref[...] += part

    full = lambda shape: pl.BlockSpec(shape, lambda i: (0,) * len(shape))
    row = lambda w: pl.BlockSpec((tm, w), lambda i: (i, 0))
    return pl.pallas_call(
        body, name="pool_bwd",
        grid=(n_tiles,),
        in_specs=_halo_specs(tm, POOL_W, n_tiles) + _halo_specs(tm, Z_W, n_tiles) + [
            full((len(POOL_WINDOWS), GROUP, GROUP)), full((1, POOL_W))],
        out_specs=[row(POOL_W), row(POOL_W), row(POOL_W), full((1, POOL_W))],
        out_shape=[jax.ShapeDtypeStruct((s, POOL_W), F32), jax.ShapeDtypeStruct((s, POOL_W), BF16),
                   jax.ShapeDtypeStruct((s, POOL_W), BF16), jax.ShapeDtypeStruct((1, POOL_W), F32)],
        scratch_shapes=[pltpu.VMEM((te, POOL_W), F32), pltpu.VMEM((te, POOL_W), F32)],
        compiler_params=_params("arbitrary"),
    )(db, db, db, z, z, z, wpool, pscale)


def loss_head(y, target, tm=512):
    s = y.shape[0]
    tm = _row_tile(s, tm)

    def body(y_ref, t_ref, dy_ref, loss_ref):
        i = pl.program_id(0)
        err = y_ref[...] - t_ref[...]
        dy_ref[...] = err * (1.0 / D_MODEL)
        part = 0.5 * jnp.sum(jnp.sum(err * err, axis=-1, keepdims=True) * (1.0 / D_MODEL), axis=0, keepdims=True)
        part = jnp.broadcast_to(part, (1, 128))

        @pl.when(i == 0)
        def _():
            loss_ref[...] = part

        @pl.when(i > 0)
        def _():
            loss_ref[...] += part

    row = pl.BlockSpec((tm, D_MODEL), lambda i: (i, 0))
    return pl.pallas_call(
        body, name="loss_head",
        grid=(s // tm,),
        in_specs=[row, row],
        out_specs=[row, pl.BlockSpec((1, 128), lambda i: (0, 0))],
        out_shape=[jax.ShapeDtypeStruct((s, D_MODEL), F32), jax.ShapeDtypeStruct((1, 128), F32)],
        compiler_params=_params("arbitrary"),
    )(y, target)


def _pad_heads(w, real):
    lead = w.shape[:-1]
    w = w.reshape(lead + (HEADS, real))
    w = jnp.pad(w, [(0, 0)] * len(lead) + [(0, 0), (0, HEAD_PAD - real)])
    return w.reshape(lead + (ATT_W,))


def _unpad_heads(w, real):
    lead = w.shape[:-1]
    return w.reshape(lead + (HEADS, HEAD_PAD))[..., :real].reshape(lead + (HEADS * real,))


def layout_weights(full):
    w = {}
    depth = full["w_in"].shape[0]
    win = full["w_in"]
    zc = lambda n: jnp.zeros((depth, D_MODEL, n), win.dtype)
    o_pe = Q_LORA + KV_LORA
    w["win"] = jnp.concatenate([win[..., :o_pe], win[..., o_pe + QK_ROPE:], zc(QK_NOPE), win[..., o_pe:o_pe + QK_ROPE],
                                zc(HEAD_PAD - QK_HEAD)], axis=-1)
    w["wuq"] = _pad_heads(full["w_uq"], QK_HEAD)
    w["wuk"] = _pad_heads(full["w_uk"], QK_NOPE)
    w["wuv"] = _pad_heads(full["w_uv"], V_HEAD)
    wo = full["w_out"]
    wo_att = jnp.swapaxes(_pad_heads(jnp.swapaxes(wo[:, :HEADS * V_HEAD], 1, 2), V_HEAD), 1, 2)
    w["wout"] = jnp.concatenate([wo_att, wo[:, HEADS * V_HEAD:]], axis=1)
    pad_vec = lambda v: jnp.pad(v, ((0, 0), (0, HEAD_PAD - QK_HEAD)))
    w["qn"], w["kn"] = pad_vec(full["q_norm"]), pad_vec(full["k_norm"])
    for n in ("wuq", "qn", "kn"):
        w[n + "_sw"] = swap_rope_halves(w[n])
    for n in ("ffn1_norm", "mix_norm", "ffn2_norm", "q_lat_norm", "kv_lat_norm", "pool_scale", "w_pool"):
        w[n] = full[n]
    return w


def local_step(x, target, w, wgu1, wd1, wgu2, wd2):
    s = x.shape[0]
    depth = w["win"].shape[0]
    rc, rs1, rs2 = rope_tables(s)
    rs_both = rs1 + rs2
    vec = lambda name, l: w[name][l][None, :]
    saved = []
    for l in range(depth):
        x1, g1, u1 = ffn_fwd(x, vec("ffn1_norm", l), wgu1, wd1, l)
        z, q, k, v, vt = mix_in_fwd(x1, vec("mix_norm", l), w["win"][l], vec("q_lat_norm", l), vec("kv_lat_norm", l),
                                    w["wuq"][l], w["wuq_sw"][l], w["wuk"][l], w["wuv"][l], vec("qn", l),
                                    vec("qn_sw", l), vec("kn", l), vec("kn_sw", l), rc, rs_both)
        o, lse = flash_fwd(q, k, vt)
        x2, cat = pool_out_fwd(x1, z, o, w["w_pool"][l], vec("pool_scale", l), w["wout"][l])
        x3, g2, u2 = ffn_fwd(x2, vec("ffn2_norm", l), wgu2, wd2, l)
        saved.append((x, x1, z, q, k, v, lse, cat, x2, g1, u1, g2, u2))
        x = x3
    dy, loss = loss_head(x, target)

    names = ["ffn1_norm", "ffn1_w_gu", "ffn1_w_down", "mix_norm", "w_in", "q_lat_norm", "kv_lat_norm", "w_uq", "w_uk",
             "w_uv", "q_norm", "k_norm", "w_pool", "pool_scale", "w_out", "ffn2_norm", "ffn2_w_gu", "ffn2_w_down"]
    grads = {n: [None] * depth for n in names if "ffn" not in n or "norm" in n}
    for n in ("ffn1", "ffn2"):
        grads[n + "_w_gu"] = lax.empty((depth, N_CHIPS, D_MODEL, FF_BLOCK), F32)
        grads[n + "_w_down"] = lax.empty((depth, D_FF, D_MODEL), F32)

    def ffn_weight_grads(n, l, h, act, dg, du, dyh):
        gu = matmul_tn(h, dg, into=grads[n + "_w_gu"], at=lambda j: (l, j))
        grads[n + "_w_gu"] = matmul_tn(h, du, into=gu, at=lambda j: (l, j + 2))
        grads[n + "_w_down"] = matmul_tn(act, dyh, tk=1024, into=grads[n + "_w_down"], at=lambda j: (l,))

    for l in reversed(range(depth)):
        x0, x1, z, q, k, v, lse, cat, x2, g1, u1, g2, u2 = saved[l]
        dx2, h, act, dg, du, dyh, dgn = ffn_bwd(x2, dy, vec("ffn2_norm", l), g2, u2, wgu2, wd2, l)
        grads["ffn2_norm"][l] = dgn[0]
        ffn_weight_grads("ffn2", l, h, act, dg, du, dyh)

        do, db, delta = out_bwd(dx2, w["wout"][l], cat)
        dwout = matmul_tn(cat, dx2)
        grads["w_out"][l] = jnp.concatenate(
            [jnp.swapaxes(_unpad_heads(jnp.swapaxes(dwout[:ATT_W], 0, 1), V_HEAD), 0, 1), dwout[ATT_W:]], axis=0)
        dp, mixed, dys, dps = pool_bwd(db, z, w["w_pool"][l], vec("pool_scale", l))
        grads["pool_scale"][l] = dps[0]
        dwp = matmul_tn(mixed, dys)
        grads["w_pool"][l] = jnp.stack([dwp[g * GROUP:(g + 1) * GROUP, g * GROUP:(g + 1) * GROUP]
                                        for g in range(len(POOL_WINDOWS))])
        dq_t, dk, dv = flash_bwd(q, k, v, do, lse, delta.reshape(HEADS, 1, s))
        dz, cqn, ckvn, dqr, dkr, dvb, dqln, dkvln, dqn, dkn = mla_bwd(
            z, dq_t, dk, dv, dp, vec("q_lat_norm", l), vec("kv_lat_norm", l), w["wuq"][l], w["wuk"][l], w["wuv"][l],
            vec("qn", l), vec("kn", l), rc, rs1, rs2)
        grads["q_lat_norm"][l], grads["kv_lat_norm"][l] = dqln[0], dkvln[0]
        grads["q_norm"][l], grads["k_norm"][l] = dqn[0, :QK_HEAD], dkn[0, :QK_HEAD]
        grads["w_uq"][l] = _unpad_heads(matmul_tn(cqn, dqr), QK_HEAD)
        grads["w_uk"][l] = _unpad_heads(matmul_tn(ckvn, dkr), QK_NOPE)
        grads["w_uv"][l] = _unpad_heads(matmul_tn(ckvn, dvb), V_HEAD)
        dx1, h2, dgm = mix_in_bwd(x1, dx2, dz, vec("mix_norm", l), w["win"][l])
        grads["mix_norm"][l] = dgm[0]
        dwin = matmul_tn(h2, dz)
        grads["w_in"][l] = jnp.concatenate(
            [dwin[:, :Z_POOL], dwin[:, Z_KPE + QK_NOPE:Z_KPE + QK_HEAD], dwin[:, Z_POOL:Z_KPE]], axis=1)

        dy, h, act, dg, du, dyh, dgn = ffn_bwd(x0, dx1, vec("ffn1_norm", l), g1, u1, wgu1, wd1, l)
        grads["ffn1_norm"][l] = dgn[0]
        ffn_weight_grads("ffn1", l, h, act, dg, du, dyh)
    return loss, dy, {n: (jnp.stack(g) if isinstance(g, list) else g) for n, g in grads.items()}


HBM_SPEC = pl.BlockSpec(memory_space=pltpu.HBM)


def _place():
    x, y, c = lax.axis_index("x"), lax.axis_index("y"), lax.axis_index("c")
    other_chips = [(1 - x, y), (x, 1 - y), (1 - x, 1 - y)]
    return x, y, c, other_chips


def _exchange(name, arrays, out_shapes, plan, n_local, n_remote):
    n = len(arrays)

    def body(*refs):
        ins, outs = refs[:n], refs[n:n + len(out_shapes)]
        send_sems, recv_sems, local_sems = refs[n + len(out_shapes):]
        local, sends, recvs, forwards = plan(ins, outs, _place())

        def remote(k, src, dst, dev):
            return pltpu.make_async_remote_copy(src_ref=src, dst_ref=dst, send_sem=send_sems.at[k],
                                                recv_sem=recv_sems.at[k], device_id=dev, device_id_type=MESH)

        started = []
        for k, (src, dst) in enumerate(local):
            cp = pltpu.make_async_copy(src, dst, local_sems.at[k])
            cp.start()
            started.append(cp)
        out_going = []
        for k, (src, dst, dev) in enumerate(sends):
            cp = remote(k, src, dst, dev)
            cp.start()
            out_going.append(cp)
        for k, (dst, dev) in enumerate(recvs):
            remote(k, dst, dst, dev).wait_recv()
            if forwards:
                src, fdst, fdev, _ = forwards[k]
                cp = remote(len(sends) + k, src, fdst, fdev)
                cp.start()
                out_going.append(cp)
        for k, (_, _, fdev, landing) in enumerate(forwards):
            remote(len(sends) + k, landing, landing, fdev).wait_recv()
        for cp in out_going:
            cp.wait_send()
        for cp in started:
            cp.wait()

    return pl.pallas_call(
        body, name=name,
        in_specs=[HBM_SPEC] * n,
        out_specs=[HBM_SPEC] * len(out_shapes),
        out_shape=out_shapes,
        scratch_shapes=[pltpu.SemaphoreType.DMA((n_remote,)), pltpu.SemaphoreType.DMA((n_remote,)),
                        pltpu.SemaphoreType.DMA((max(n_local, 1),))],
    )(*arrays)


def _layer_half(depth, which):
    assert depth % 2 == 0
    return pl.ds(which * (depth // 2), depth // 2)


def gather_chips(shards):
    n = len(shards)
    depth = shards[0].shape[0]

    def plan(ins, outs, place):
        x, y, c, chips = place
        me, sib = 2 * x + y, (x, y, 1 - c)
        mine, theirs = _layer_half(depth, c), _layer_half(depth, 1 - c)
        sends = [(ins[p].at[mine], outs[p].at[me, mine], (px, py, c)) for p in range(n) for (px, py) in chips]
        recvs = [(outs[p].at[2 * px + py, mine], (px, py, c)) for p in range(n) for (px, py) in chips]
        forwards = [(outs[p].at[2 * px + py, mine], outs[p].at[2 * px + py, mine], sib, outs[p].at[2 * px + py, theirs])
                    for p in range(n) for (px, py) in chips]
        return [], sends, recvs, forwards

    out_shapes = [jax.ShapeDtypeStruct((N_CHIPS,) + a.shape, a.dtype) for a in shards]
    return _exchange("gather_chips", shards, out_shapes, plan, 0, 6 * n)


def swap_halves(fulls):
    n = len(fulls)
    depth = fulls[0].shape[0]

    def plan(ins, outs, place):
        x, y, c, _ = place
        sib = (x, y, 1 - c)
        sends = [(ins[p].at[_layer_half(depth, 1 - c)], outs[p], sib) for p in range(n)]
        return [], sends, [(outs[p], sib) for p in range(n)], []

    out_shapes = [jax.ShapeDtypeStruct((depth // 2,) + a.shape[1:], a.dtype) for a in fulls]
    return _exchange("swap_halves", fulls, out_shapes, plan, 0, n)


def scatter_chips(fulls):
    n = len(fulls)

    def plan(ins, outs, place):
        x, y, c, chips = place
        sends = [(ins[p].at[:, 2 * px + py], outs[p].at[j], (px, py, c))
                 for p in range(n) for j, (px, py) in enumerate(chips)]
        recvs = [(outs[p].at[j], (px, py, c)) for p in range(n) for j, (px, py) in enumerate(chips)]
        return [], sends, recvs, []

    out_shapes = [jax.ShapeDtypeStruct((3, a.shape[0]) + a.shape[2:], a.dtype) for a in fulls]
    return _exchange("scatter_chips", fulls, out_shapes, plan, 0, 3 * n)


def swap_sibling(arrays):
    n = len(arrays)

    def plan(ins, outs, place):
        x, y, c, _ = place
        sib = (x, y, 1 - c)
        return [], [(ins[p], outs[p], sib) for p in range(n)], [(outs[p], sib) for p in range(n)], []

    return _exchange("swap_sibling", arrays, [jax.ShapeDtypeStruct(a.shape, a.dtype) for a in arrays], plan, 0, n)


def gather_all(vec):
    def plan(ins, outs, place):
        x, y, c, _ = place
        me = 4 * x + 2 * y + c
        flips = [(fx, fy, fc) for fx in (0, 1) for fy in (0, 1) for fc in (0, 1)][1:]
        peers = [(1 - x if fx else x, 1 - y if fy else y, 1 - c if fc else c) for fx, fy, fc in flips]
        local = [(ins[0], outs[0].at[me])]
        sends = [(ins[0], outs[0].at[me], dev) for dev in peers]
        recvs = [(outs[0].at[4 * dev[0] + 2 * dev[1] + dev[2]], dev) for dev in peers]
        return local, sends, recvs, []

    return _exchange("gather_all", [vec], [jax.ShapeDtypeStruct((8,) + vec.shape, vec.dtype)], plan, 1, 7)[0]


def _pick_rows(rows, cols, block_bytes=1024 * 1024):
    best = None
    for t in range(16, rows + 1, 16):
        if rows % t == 0 and t * cols * 4 <= block_bytes:
            best = t
    return best or rows


def sum_rows_of(own, stacked, with_bf16=False):
    rows, cols = own.shape
    n = stacked.shape[0]
    tr = _pick_rows(rows, cols)

    def body(own_ref, st_ref, o_ref, *narrow):
        acc = own_ref[...]
        for k in range(n):
            acc = acc + st_ref[k].astype(F32)
        o_ref[...] = acc
        if with_bf16:
            narrow[0][...] = acc.astype(BF16)

    spec = pl.BlockSpec((tr, cols), lambda i: (i, 0))
    return pl.pallas_call(
        body, name="sum_rows_of",
        grid=(rows // tr,),
        in_specs=[spec, pl.BlockSpec((n, tr, cols), lambda i: (0, i, 0))],
        out_specs=[spec, spec] if with_bf16 else spec,
        out_shape=([jax.ShapeDtypeStruct((rows, cols), F32), jax.ShapeDtypeStruct((rows, cols), BF16)]
                   if with_bf16 else jax.ShapeDtypeStruct((rows, cols), F32)),
        compiler_params=_params("parallel"),
    )(own, stacked)


def sum_stack(stacked):
    n, rows, cols = stacked.shape
    tr = _pick_rows(rows, cols)

    def body(st_ref, o_ref):
        acc = st_ref[0]
        for k in range(1, n):
            acc = acc + st_ref[k]
        o_ref[...] = acc

    return pl.pallas_call(
        body, name="sum_stack",
        grid=(rows // tr,),
        in_specs=[pl.BlockSpec((n, tr, cols), lambda i: (0, i, 0))],
        out_specs=pl.BlockSpec((tr, cols), lambda i: (i, 0)),
        out_shape=jax.ShapeDtypeStruct((rows, cols), F32),
        compiler_params=_params("parallel"),
    )(stacked)


def adamw(w, ga, gb, m, v):
    rows, cols = w.shape
    tr = _pick_rows(rows, cols)
    c1 = 1.0 - ADAM_B1 ** ADAM_STEP
    c2 = 1.0 - ADAM_B2 ** ADAM_STEP
    ins = [w, ga] + ([gb] if gb is not None else []) + [m, v]

    def body(*refs):
        w_ref, ga_ref = refs[0], refs[1]
        m_ref, v_ref, g_out, d_out, m_out, v_out = refs[-6:]
        g = ga_ref[...]
        if gb is not None:
            g = g + refs[2][...]
        mn = ADAM_B1 * m_ref[...] + (1.0 - ADAM_B1) * g
        vn = ADAM_B2 * v_ref[...] + (1.0 - ADAM_B2) * (g * g)
        m_hat = mn / c1
        v_hat = vn / c2
        g_out[...] = g
        d_out[...] = -ADAM_LR * (m_hat / (jnp.sqrt(v_hat) + ADAM_EPS) + ADAM_WD * w_ref[...])
        m_out[...] = mn
        v_out[...] = vn

    spec = pl.BlockSpec((tr, cols), lambda i: (i, 0))
    return pl.pallas_call(
        body, name="adamw",
        grid=(rows // tr,),
        in_specs=[spec] * len(ins),
        out_specs=[spec] * 4,
        out_shape=[jax.ShapeDtypeStruct((rows, cols), F32)] * 4,
        compiler_params=_params("parallel"),
    )(*ins)


WEIGHT_NAMES = ["ffn1_norm", "ffn1_w_gu", "ffn1_w_down", "mix_norm", "w_in", "q_lat_norm", "kv_lat_norm", "w_uq", "w_uk",
                "w_uv", "q_norm", "k_norm", "w_pool", "pool_scale", "w_out", "ffn2_norm", "ffn2_w_gu", "ffn2_w_down"]
COL_SHARDED = ["ffn1_w_gu", "w_in", "w_uq", "w_uk", "w_uv", "ffn2_w_gu"]
ROW_SHARDED = ["ffn1_w_down", "w_out", "ffn2_w_down"]
SHARDED = [n for n in WEIGHT_NAMES if n in COL_SHARDED or n in ROW_SHARDED]
REPLICATED = [n for n in WEIGHT_NAMES if n not in SHARDED]


def _join_chips(g, name):
    _, depth, r, c = g.shape
    if name in COL_SHARDED:
        return jnp.transpose(g, (1, 2, 0, 3)).reshape(depth, r, N_CHIPS * c)
    return jnp.transpose(g, (1, 0, 2, 3)).reshape(depth, N_CHIPS * r, c)


def _split_chips(full, name):
    depth, r, c = full.shape
    if name in COL_SHARDED:
        return jnp.transpose(full.reshape(depth, r, N_CHIPS, c // N_CHIPS), (0, 2, 1, 3))
    return full.reshape(depth, N_CHIPS, r // N_CHIPS, c)


def _pack(vs):
    flat = jnp.concatenate([v.reshape(-1) for v in vs])
    pad = (-flat.shape[0]) % (8 * 128)
    return jnp.pad(flat, (0, pad)).reshape(-1, 128)


def _unpack(packed, like):
    flat = packed.reshape(-1)
    out, at = [], 0
    for v in like:
        out.append(flat[at:at + v.size].reshape(v.shape))
        at += v.size
    return out


def kernel(x, ffn1_norm, ffn1_w_gu, ffn1_w_down, mix_norm, w_in, q_lat_norm, kv_lat_norm, w_uq, w_uk, w_uv, q_norm, k_norm, w_pool, pool_scale, w_out, ffn2_norm, ffn2_w_gu, ffn2_w_down, loss_target, m_ffn1_norm, m_ffn1_w_gu, m_ffn1_w_down, m_mix_norm, m_w_in, m_q_lat_norm, m_kv_lat_norm, m_w_uq, m_w_uk, m_w_uv, m_q_norm, m_k_norm, m_w_pool, m_pool_scale, m_w_out, m_ffn2_norm, m_ffn2_w_gu, m_ffn2_w_down, v_ffn1_norm, v_ffn1_w_gu, v_ffn1_w_down, v_mix_norm, v_w_in, v_q_lat_norm, v_kv_lat_norm, v_w_uq, v_w_uk, v_w_uv, v_q_norm, v_k_norm, v_w_pool, v_pool_scale, v_w_out, v_ffn2_norm, v_ffn2_w_gu, v_ffn2_w_down):
    args = locals()
    weights = {n: args[n] for n in WEIGHT_NAMES}
    moments_m = {n: args["m_" + n] for n in WEIGHT_NAMES}
    moments_v = {n: args["v_" + n] for n in WEIGHT_NAMES}

    chip = 2 * lax.axis_index("x") + lax.axis_index("y")
    core = lax.axis_index("c")
    shards = [weights[n].astype(BF16) for n in SHARDED]
    gathered = {n: lax.dynamic_update_index_in_dim(g, own, chip, 0)
                for n, own, g in zip(SHARDED, shards, gather_chips(shards))}
    full = {n: _join_chips(gathered[n], n) for n in SHARDED if "ffn" not in n}
    for n in REPLICATED:
        full[n] = weights[n].astype(BF16) if n == "w_pool" else weights[n]
    w = layout_weights(full)
    wd1 = _join_chips(gathered["ffn1_w_down"], "ffn1_w_down")
    wd2 = _join_chips(gathered["ffn2_w_down"], "ffn2_w_down")

    loss_part, grad_x, grads = local_step(x[0], loss_target[0], w, gathered["ffn1_w_gu"], wd1, gathered["ffn2_w_gu"], wd2)
    loss = lax.psum(loss_part[0, 0], ("x", "y", "c"))

    split = [grads[n] if n.endswith("w_gu") else _split_chips(grads[n], n) for n in SHARDED]
    half = split[0].shape[0] // 2

    def add(own, stacked, with_bf16=False):
        rows = own.size // own.shape[-1]
        total = sum_rows_of(own.reshape(rows, -1), stacked.reshape(stacked.shape[0], rows, -1), with_bf16)
        return [t.reshape(own.shape) for t in total] if with_bf16 else total.reshape(own.shape)

    from_sibling = swap_halves(split)
    chip_sums = [add(lax.dynamic_slice_in_dim(g4, core * half, half, axis=0), other[None], with_bf16=True)
                 for g4, other in zip(split, from_sibling)]
    landed = scatter_chips([narrow for _, narrow in chip_sums])
    totals = [add(lax.dynamic_index_in_dim(wide, chip, 1, keepdims=False), land)
              for (wide, _), land in zip(chip_sums, landed)]
    out = {}
    for n, mine, theirs in zip(SHARDED, totals, swap_sibling(totals)):
        g = jnp.concatenate([jnp.where(core == 0, mine, theirs), jnp.where(core == 0, theirs, mine)], axis=0)
        shape = weights[n].shape
        flat = lambda a: a.reshape(shape[0] * shape[1], shape[2])
        res = adamw(flat(weights[n]), flat(g), None, flat(moments_m[n]), flat(moments_v[n]))
        out[n] = [r.reshape(shape) for r in res]

    rep_g = [grads[n] for n in REPLICATED]
    total = sum_stack(gather_all(_pack(rep_g)))
    res = adamw(_pack([weights[n] for n in REPLICATED]), total, None,
                _pack([moments_m[n] for n in REPLICATED]), _pack([moments_v[n] for n in REPLICATED]))
    unpacked = [_unpack(r, rep_g) for r in res]
    for k, n in enumerate(REPLICATED):
        out[n] = [u[k] for u in unpacked]

    return (loss, grad_x[None], *[out[n][0] for n in WEIGHT_NAMES], *[out[n][1] for n in WEIGHT_NAMES],
            *[out[n][2] for n in WEIGHT_NAMES], *[out[n][3] for n in WEIGHT_NAMES])
```

```python
import jax
import jax.numpy as jnp
from jax import lax
from jax.experimental import pallas as pl
from jax.experimental.pallas import tpu as pltpu

F32 = jnp.float32
BF16 = jnp.bfloat16
MESH = pl.DeviceIdType.MESH

DEPTH = 4
D_MODEL = 1024
D_FF = 2816
FF_BLOCK = D_FF // 2
N_CHIPS = 4
HEADS = 8
HEAD_PAD = 128
QK_NOPE = 64
QK_ROPE = 32
QK_HEAD = QK_NOPE + QK_ROPE
V_HEAD = 64
Q_LORA = 384
KV_LORA = 256
POOL_W = 512
POOL_WINDOWS = (2, 4, 8, 16)
POOL_HALO = 8
GROUP = 128
D_IN = Q_LORA + KV_LORA + QK_ROPE + POOL_W
Z_W = 1280
Z_CKV = Q_LORA
Z_POOL = Q_LORA + KV_LORA
Z_KPE = Z_POOL + POOL_W
ATT_W = HEADS * HEAD_PAD
CAT_W = ATT_W + POOL_W
EPS = 1e-6
ROPE_THETA = 10000.0
ADAM_LR, ADAM_B1, ADAM_B2, ADAM_EPS, ADAM_WD, ADAM_STEP = 0.001, 0.9, 0.999, 1e-08, 0.01, 10
LOG2_E = 1.4426950408889634
LN_2 = 0.6931471805599453

VMEM_LIMIT = 56 * 1024 * 1024
VMEM_LIMIT_FFN_BWD = 62 * 1024 * 1024


def _params(*sem, vmem=VMEM_LIMIT):
    return pltpu.CompilerParams(dimension_semantics=sem, vmem_limit_bytes=vmem)


def _dot(a, b):
    return jnp.dot(a, b, preferred_element_type=F32)


def _dot_nt(a, b):
    return lax.dot_general(a, b, (((1,), (1,)), ((), ())), preferred_element_type=F32)


def _dot_tn(a, b):
    return lax.dot_general(a, b, (((0,), (0,)), ((), ())), preferred_element_type=F32)


def _norm_fwd(x, n):
    rstd = lax.rsqrt(jnp.sum(x * x, axis=-1, keepdims=True) * (1.0 / n) + EPS)
    return x * rstd, rstd


def _norm_bwd(dy, g, xhat, rstd, n):
    dxhat = dy * g
    dx = rstd * (dxhat - xhat * (jnp.sum(dxhat * xhat, axis=-1, keepdims=True) * (1.0 / n)))
    return dx, dy * xhat


def _row_tile(s, want):
    t = min(s, want)
    assert s % t == 0
    return t


def ffn_fwd(x, gn, wgu4, wd, layer, tm=512):
    s = x.shape[0]
    tm = _row_tile(s, tm)

    def body(x_ref, gn_ref, wg_ref, wu_ref, wd_ref, o_ref, g_ref, u_ref, h_s, acc_s):
        j = pl.program_id(1)

        @pl.when(j == 0)
        def _():
            xhat, _ = _norm_fwd(x_ref[...], D_MODEL)
            h_s[...] = (xhat * gn_ref[...]).astype(BF16)
            acc_s[...] = jnp.zeros_like(acc_s)

        h = h_s[...]
        g = _dot(h, wg_ref[...])
        u = _dot(h, wu_ref[...])
        g_ref[...] = g.astype(BF16)
        u_ref[...] = u.astype(BF16)
        a = (g * jax.nn.sigmoid(g) * u).astype(BF16)
        acc_s[...] += _dot(a, wd_ref[...])

        @pl.when(j == 1)
        def _():
            o_ref[...] = x_ref[...] + 0.5 * acc_s[...]

    return pl.pallas_call(
        body, name="ffn_fwd",
        grid=(s // tm, 2),
        in_specs=[
            pl.BlockSpec((tm, D_MODEL), lambda i, j: (i, 0)),
            pl.BlockSpec((1, D_MODEL), lambda i, j: (0, 0)),
            pl.BlockSpec((None, None, D_MODEL, FF_BLOCK), lambda i, j: (j, layer, 0, 0)),
            pl.BlockSpec((None, None, D_MODEL, FF_BLOCK), lambda i, j: (j + 2, layer, 0, 0)),
            pl.BlockSpec((None, FF_BLOCK, D_MODEL), lambda i, j: (layer, j, 0)),
        ],
        out_specs=[pl.BlockSpec((tm, D_MODEL), lambda i, j: (i, 0)),
                   pl.BlockSpec((tm, FF_BLOCK), lambda i, j: (i, j)), pl.BlockSpec((tm, FF_BLOCK), lambda i, j: (i, j))],
        out_shape=[jax.ShapeDtypeStruct((s, D_MODEL), F32), jax.ShapeDtypeStruct((s, D_FF), BF16),
                   jax.ShapeDtypeStruct((s, D_FF), BF16)],
        scratch_shapes=[pltpu.VMEM((tm, D_MODEL), BF16), pltpu.VMEM((tm, D_MODEL), F32)],
        compiler_params=_params("parallel", "arbitrary"),
    )(x, gn, wgu4, wgu4, wd)


def ffn_bwd(x, dy, gn, g_pre, u_pre, wgu4, wd, layer, tm=512):
    s = x.shape[0]
    tm = _row_tile(s, tm)
    halves = 2 if tm % 32 == 0 else 1

    def body(x_ref, dy_ref, gn_ref, g_ref, u_ref, wg_ref, wu_ref, wd_ref,
             dx_ref, h_ref, a_ref, dg_ref, du_ref, dyh_ref, dgn_ref, dh_s):
        i, j = pl.program_id(0), pl.program_id(1)

        @pl.when(j == 0)
        def _():
            xhat, _ = _norm_fwd(x_ref[...], D_MODEL)
            h_ref[...] = (xhat * gn_ref[...]).astype(BF16)
            dyh_ref[...] = (0.5 * dy_ref[...]).astype(BF16)
            dh_s[...] = jnp.zeros_like(dh_s)

        for r in range(halves):
            rows = slice(r * (tm // halves), (r + 1) * (tm // halves))
            g = g_ref[rows, :].astype(F32)
            u = u_ref[rows, :].astype(F32)
            sg = jax.nn.sigmoid(g)
            silu = g * sg
            a_ref[rows, :] = (silu * u).astype(BF16)
            da = _dot_nt(dyh_ref[rows, :], wd_ref[...])
            dg = (da * u * (sg * (1.0 + g * (1.0 - sg)))).astype(BF16)
            du = (da * silu).astype(BF16)
            dg_ref[rows, :] = dg
            du_ref[rows, :] = du
            dh_s[rows, :] += _dot_nt(dg, wg_ref[...]) + _dot_nt(du, wu_ref[...])

        @pl.when(j == 1)
        def _():
            xhat, rstd = _norm_fwd(x_ref[...], D_MODEL)
            dxn, dgrow = _norm_bwd(dh_s[...], gn_ref[...], xhat, rstd, D_MODEL)
            dx_ref[...] = dy_ref[...] + dxn
            part = jnp.sum(dgrow, axis=0, keepdims=True)

            @pl.when(i == 0)
            def _():
                dgn_ref[...] = part

            @pl.when(i > 0)
            def _():
                dgn_ref[...] += part

    row = lambda i, j: (i, 0)
    return pl.pallas_call(
        body, name="ffn_bwd",
        grid=(s // tm, 2),
        in_specs=[
            pl.BlockSpec((tm, D_MODEL), row),
            pl.BlockSpec((tm, D_MODEL), row),
            pl.BlockSpec((1, D_MODEL), lambda i, j: (0, 0)),
            pl.BlockSpec((tm, FF_BLOCK), lambda i, j: (i, j)),
            pl.BlockSpec((tm, FF_BLOCK), lambda i, j: (i, j)),
            pl.BlockSpec((None, None, D_MODEL, FF_BLOCK), lambda i, j: (j, layer, 0, 0)),
            pl.BlockSpec((None, None, D_MODEL, FF_BLOCK), lambda i, j: (j + 2, layer, 0, 0)),
            pl.BlockSpec((None, FF_BLOCK, D_MODEL), lambda i, j: (layer, j, 0)),
        ],
        out_specs=[
            pl.BlockSpec((tm, D_MODEL), row),
            pl.BlockSpec((tm, D_MODEL), row),
            pl.BlockSpec((tm, FF_BLOCK), lambda i, j: (i, j)),
            pl.BlockSpec((tm, FF_BLOCK), lambda i, j: (i, j)),
            pl.BlockSpec((tm, FF_BLOCK), lambda i, j: (i, j)),
            pl.BlockSpec((tm, D_MODEL), row),
            pl.BlockSpec((1, D_MODEL), lambda i, j: (0, 0)),
        ],
        out_shape=[
            jax.ShapeDtypeStruct((s, D_MODEL), F32),
            jax.ShapeDtypeStruct((s, D_MODEL), BF16),
            jax.ShapeDtypeStruct((s, D_FF), BF16),
            jax.ShapeDtypeStruct((s, D_FF), BF16),
            jax.ShapeDtypeStruct((s, D_FF), BF16),
            jax.ShapeDtypeStruct((s, D_MODEL), BF16),
            jax.ShapeDtypeStruct((1, D_MODEL), F32),
        ],
        scratch_shapes=[pltpu.VMEM((tm, D_MODEL), F32)],
        compiler_params=_params("arbitrary", "arbitrary", vmem=VMEM_LIMIT_FFN_BWD),
    )(x, dy, gn, g_pre, u_pre, wgu4, wgu4, wd)


def matmul_tn(a, b, tk=2048, tn_max=1536, into=None, at=None):
    s, m = a.shape
    n = b.shape[1]
    tk = _row_tile(s, tk)
    tn = n
    if n > tn_max:
        tn = n // 2
    assert n % tn == 0 and tn % 128 == 0
    nk = s // tk

    def body(a_ref, b_ref, *rest):
        o_ref, acc_s = rest[-2:]
        k = pl.program_id(1)

        @pl.when(k == 0)
        def _():
            acc_s[...] = jnp.zeros_like(acc_s)

        acc_s[...] += _dot_tn(a_ref[...].astype(BF16), b_ref[...].astype(BF16))

        @pl.when(k == nk - 1)
        def _():
            o_ref[...] = acc_s[...]

    in_specs = [pl.BlockSpec((tk, m), lambda j, k: (k, 0)), pl.BlockSpec((tk, tn), lambda j, k: (k, j))]
    if into is None:
        operands, aliases = (a, b), {}
        out_specs = pl.BlockSpec((m, tn), lambda j, k: (0, j))
        out_shape = jax.ShapeDtypeStruct((m, n), F32)
    else:
        assert into.shape[-2:] == (m, tn) and into.dtype == F32
        lead = into.ndim - 2
        operands, aliases = (a, b, into), {2: 0}
        in_specs.append(pl.BlockSpec(memory_space=pl.ANY))
        out_specs = pl.BlockSpec((None,) * lead + (m, tn), lambda j, k: tuple(at(j)) + (0, 0))
        out_shape = jax.ShapeDtypeStruct(into.shape, F32)
    return pl.pallas_call(
        body, name="matmul_tn",
        grid=(n // tn, nk),
        in_specs=in_specs,
        out_specs=out_specs,
        out_shape=out_shape,
        scratch_shapes=[pltpu.VMEM((m, tn), F32)],
        input_output_aliases=aliases,
        compiler_params=_params("parallel", "arbitrary"),
    )(*operands)


def _rope_bwd(dy, c, s1, s2):
    return dy * c + pltpu.roll(dy * s1, 16, 1) + pltpu.roll(dy * s2, HEAD_PAD - 16, 1)


def rope_tables(s):
    pos = jnp.arange(s, dtype=F32)
    inv = ROPE_THETA ** (-jnp.arange(0, QK_ROPE, 2, dtype=F32) / QK_ROPE)
    ang = pos[:, None] * inv[None, :]
    cos, sin = jnp.cos(ang), jnp.sin(ang)
    z16 = jnp.zeros((s, 16), F32)
    c = jnp.concatenate([jnp.ones((s, QK_NOPE), F32), cos, cos, z16, z16], axis=1)
    s1 = jnp.concatenate([jnp.zeros((s, QK_NOPE), F32), -sin, z16, z16, z16], axis=1)
    s2 = jnp.concatenate([jnp.zeros((s, QK_NOPE), F32), z16, sin, z16, z16], axis=1)
    return c, s1, s2


def swap_rope_halves(a):
    lead = a.shape[:-1]
    h = a.reshape(lead + (-1, HEAD_PAD))
    half = QK_ROPE // 2
    zeros = lambda n: jnp.zeros(h.shape[:-1] + (n,), a.dtype)
    sw = jnp.concatenate([zeros(QK_NOPE), h[..., QK_NOPE + half:QK_HEAD], h[..., QK_NOPE:QK_NOPE + half],
                          zeros(HEAD_PAD - QK_HEAD)], axis=-1)
    return sw.reshape(a.shape)


def mix_in_fwd(x, gm, win, qln, kvln, wuq, wuq_sw, wuk, wuv, qn, qn_sw, kn, kn_sw, rc, rs, tm=512):
    s = x.shape[0]
    tm = _row_tile(s, tm)
    scale = QK_HEAD ** -0.5 * LOG2_E
    half = QK_ROPE // 2

    def body(x_ref, gm_ref, win_ref, qln_ref, kvln_ref, wuq_ref, wuqs_ref, wuk_ref, wuv_ref,
             qn_ref, qns_ref, kn_ref, kns_ref, rc_ref, rs_ref, z_ref, q_ref, k_ref, v_ref, vt_ref):
        xhat, _ = _norm_fwd(x_ref[...], D_MODEL)
        h = (xhat * gm_ref[...]).astype(BF16)
        z = _dot(h, win_ref[...])
        z_ref[...] = z
        cq, _ = _norm_fwd(z[:, :Q_LORA], Q_LORA)
        cqn = (cq * qln_ref[...]).astype(BF16)
        ckv, _ = _norm_fwd(z[:, Z_CKV:Z_POOL], KV_LORA)
        ckvn = (ckv * kvln_ref[...]).astype(BF16)
        kpe = z[:, Z_KPE:]
        lane = lax.broadcasted_iota(jnp.int32, kpe.shape, 1)
        first = (lane >= QK_NOPE) & (lane < QK_NOPE + half)
        second = (lane >= QK_NOPE + half) & (lane < QK_HEAD)
        kpe_sw = jnp.where(first, pltpu.roll(kpe, HEAD_PAD - half, 1), jnp.where(second, pltpu.roll(kpe, half, 1), 0.0))
        q_raw = _dot(cqn, wuq_ref[...])
        q_sw = _dot(cqn, wuqs_ref[...])
        k_raw = _dot(ckvn, wuk_ref[...])
        v = _dot(ckvn, wuv_ref[...])
        v_ref[...] = v.astype(BF16)
        vt_ref[...] = jnp.transpose(v).astype(BF16)
        c, sn = rc_ref[...], rs_ref[...]
        tq, tq_sw = qn_ref[...] * c * scale, qns_ref[...] * sn * scale
        tk, tk_sw = kn_ref[...] * c, kns_ref[...] * sn
        for hd in range(HEADS):
            lanes = slice(hd * HEAD_PAD, (hd + 1) * HEAD_PAD)
            qr = q_raw[:, lanes]
            _, q_rstd = _norm_fwd(qr, QK_HEAD)
            q_ref[:, lanes] = (q_rstd * (qr * tq + q_sw[:, lanes] * tq_sw)).astype(BF16)
            kr = k_raw[:, lanes] + kpe
            _, k_rstd = _norm_fwd(kr, QK_HEAD)
            k_ref[:, lanes] = (k_rstd * (kr * tk + kpe_sw * tk_sw)).astype(BF16)

    full = lambda shape: pl.BlockSpec(shape, lambda i: (0,) * len(shape))
    row = lambda w: pl.BlockSpec((tm, w), lambda i: (i, 0))
    return pl.pallas_call(
        body, name="mix_in_fwd",
        grid=(s // tm,),
        in_specs=[row(D_MODEL), full((1, D_MODEL)), full((D_MODEL, Z_W)), full((1, Q_LORA)), full((1, KV_LORA)),
                  full((Q_LORA, ATT_W)), full((Q_LORA, ATT_W)), full((KV_LORA, ATT_W)), full((KV_LORA, ATT_W)),
                  full((1, HEAD_PAD)), full((1, HEAD_PAD)), full((1, HEAD_PAD)), full((1, HEAD_PAD)),
                  row(HEAD_PAD), row(HEAD_PAD)],
        out_specs=[row(Z_W), row(ATT_W), row(ATT_W), row(ATT_W), pl.BlockSpec((ATT_W, tm), lambda i: (0, i))],
        out_shape=[jax.ShapeDtypeStruct((s, Z_W), F32)] + [jax.ShapeDtypeStruct((s, ATT_W), BF16)] * 3 + [
            jax.ShapeDtypeStruct((ATT_W, s), BF16)],
        compiler_params=_params("parallel"),
    )(x, gm, win, qln, kvln, wuq, wuq_sw, wuk, wuv, qn, qn_sw, kn, kn_sw, rc, rs)


def mla_bwd(z, dq_t, dk, dv, dp, qln, kvln, wuq, wuk, wuv, qn, kn, rc, rs1, rs2, tm=512):
    s = z.shape[0]
    tm = _row_tile(s, tm)
    scale = QK_HEAD ** -0.5

    def body(z_ref, dqt_ref, dk_ref, dv_ref, dp_ref, qln_ref, kvln_ref, wuq_ref, wuk_ref, wuv_ref, qn_ref, kn_ref,
             rc_ref, rs1_ref, rs2_ref,
             dz_ref, cqn_ref, ckvn_ref, dqr_ref, dkr_ref, dvb_ref, dqln_ref, dkvln_ref, dqn_ref, dkn_ref):
        i = pl.program_id(0)
        z = z_ref[...]
        cq, cq_rstd = _norm_fwd(z[:, :Q_LORA], Q_LORA)
        cqn = (cq * qln_ref[...]).astype(BF16)
        ckv, ckv_rstd = _norm_fwd(z[:, Z_CKV:Z_POOL], KV_LORA)
        ckvn = (ckv * kvln_ref[...]).astype(BF16)
        kpe = z[:, Z_KPE:]
        cqn_ref[...] = cqn
        ckvn_ref[...] = ckvn
        q_raw = _dot(cqn, wuq_ref[...])
        k_raw = _dot(ckvn, wuk_ref[...])
        c, s1, s2 = rc_ref[...], rs1_ref[...], rs2_ref[...]
        lane = lax.broadcasted_iota(jnp.int32, (tm, HEAD_PAD), 1)
        rope_lanes = (lane >= QK_NOPE) & (lane < QK_HEAD)
        dkpe = jnp.zeros((tm, HEAD_PAD), F32)
        dqn = jnp.zeros((tm, HEAD_PAD), F32)
        dkn = jnp.zeros((tm, HEAD_PAD), F32)
        for hd in range(HEADS):
            lanes = slice(hd * HEAD_PAD, (hd + 1) * HEAD_PAD)
            qh, q_rstd = _norm_fwd(q_raw[:, lanes], QK_HEAD)
            dqh = jnp.transpose(dqt_ref[lanes, :]) * scale
            dqr, dg = _norm_bwd(_rope_bwd(dqh, c, s1, s2), qn_ref[...], qh, q_rstd, QK_HEAD)
            dqn += dg
            dqr_ref[:, lanes] = dqr.astype(BF16)
            kh, k_rstd = _norm_fwd(k_raw[:, lanes] + kpe, QK_HEAD)
            dkr, dg = _norm_bwd(_rope_bwd(dk_ref[:, lanes], c, s1, s2), kn_ref[...], kh, k_rstd, QK_HEAD)
            dkn += dg
            dkr_ref[:, lanes] = dkr.astype(BF16)
            dkpe += jnp.where(rope_lanes, dkr, 0.0)
        dvb = dv_ref[...].astype(BF16)
        dvb_ref[...] = dvb
        dcqn = _dot_nt(dqr_ref[...], wuq_ref[...])
        dckvn = _dot_nt(dkr_ref[...], wuk_ref[...]) + _dot_nt(dvb, wuv_ref[...])
        dcq, dqln = _norm_bwd(dcqn, qln_ref[...], cq, cq_rstd, Q_LORA)
        dckv, dkvln = _norm_bwd(dckvn, kvln_ref[...], ckv, ckv_rstd, KV_LORA)
        dz_ref[:, :Q_LORA] = dcq.astype(BF16)
        dz_ref[:, Z_CKV:Z_POOL] = dckv.astype(BF16)
        dz_ref[:, Z_POOL:Z_KPE] = dp_ref[...].astype(BF16)
        dz_ref[:, Z_KPE:] = dkpe.astype(BF16)
        parts = [(dqln_ref, dqln), (dkvln_ref, dkvln), (dqn_ref, dqn), (dkn_ref, dkn)]

        @pl.when(i == 0)
        def _():
            for ref, val in parts:
                ref[...] = jnp.sum(val, axis=0, keepdims=True)

        @pl.when(i > 0)
        def _():
            for ref, val in parts:
                ref[...] += jnp.sum(val, axis=0, keepdims=True)

    full = lambda shape: pl.BlockSpec(shape, lambda i: (0,) * len(shape))
    row = lambda w: pl.BlockSpec((tm, w), lambda i: (i, 0))
    return pl.pallas_call(
        body, name="mla_bwd",
        grid=(s // tm,),
        in_specs=[row(Z_W), pl.BlockSpec((ATT_W, tm), lambda i: (0, i)), row(ATT_W), row(ATT_W), row(POOL_W),
                  full((1, Q_LORA)), full((1, KV_LORA)), full((Q_LORA, ATT_W)), full((KV_LORA, ATT_W)),
                  full((KV_LORA, ATT_W)), full((1, HEAD_PAD)), full((1, HEAD_PAD)),
                  row(HEAD_PAD), row(HEAD_PAD), row(HEAD_PAD)],
        out_specs=[row(Z_W), row(Q_LORA), row(KV_LORA), row(ATT_W), row(ATT_W), row(ATT_W),
                   full((1, Q_LORA)), full((1, KV_LORA)), full((1, HEAD_PAD)), full((1, HEAD_PAD))],
        out_shape=[jax.ShapeDtypeStruct((s, Z_W), BF16), jax.ShapeDtypeStruct((s, Q_LORA), BF16),
                   jax.ShapeDtypeStruct((s, KV_LORA), BF16)] + [jax.ShapeDtypeStruct((s, ATT_W), BF16)] * 3 + [
                   jax.ShapeDtypeStruct((1, Q_LORA), F32), jax.ShapeDtypeStruct((1, KV_LORA), F32),
                   jax.ShapeDtypeStruct((1, HEAD_PAD), F32), jax.ShapeDtypeStruct((1, HEAD_PAD), F32)],
        compiler_params=_params("arbitrary"),
    )(z, dq_t, dk, dv, dp, qln, kvln, wuq, wuk, wuv, qn, kn, rc, rs1, rs2)


def mix_in_bwd(x, dx_in, dz, gm, win, tm=512):
    s = x.shape[0]
    tm = _row_tile(s, tm)

    def body(x_ref, dxin_ref, dz_ref, gm_ref, win_ref, dx_ref, h_ref, dgm_ref):
        i = pl.program_id(0)
        xhat, rstd = _norm_fwd(x_ref[...], D_MODEL)
        h_ref[...] = (xhat * gm_ref[...]).astype(BF16)
        dh = _dot_nt(dz_ref[...], win_ref[...])
        dxn, dgrow = _norm_bwd(dh, gm_ref[...], xhat, rstd, D_MODEL)
        dx_ref[...] = dxin_ref[...] + dxn
        part = jnp.sum(dgrow, axis=0, keepdims=True)

        @pl.when(i == 0)
        def _():
            dgm_ref[...] = part

        @pl.when(i > 0)
        def _():
            dgm_ref[...] += part

    full = lambda shape: pl.BlockSpec(shape, lambda i: (0,) * len(shape))
    row = lambda w: pl.BlockSpec((tm, w), lambda i: (i, 0))
    return pl.pallas_call(
        body, name="mix_in_bwd",
        grid=(s // tm,),
        in_specs=[row(D_MODEL), row(D_MODEL), row(Z_W), full((1, D_MODEL)), full((D_MODEL, Z_W))],
        out_specs=[row(D_MODEL), row(D_MODEL), full((1, D_MODEL))],
        out_shape=[jax.ShapeDtypeStruct((s, D_MODEL), F32), jax.ShapeDtypeStruct((s, D_MODEL), BF16),
                   jax.ShapeDtypeStruct((1, D_MODEL), F32)],
        compiler_params=_params("arbitrary"),
    )(x, dx_in, dz, gm, win)


ATT_CHUNK_FWD = 1024
ATT_CHUNK_BWD = 512


def flash_fwd(q, k, vt, tq=1024, tk=16384, chunk=ATT_CHUNK_FWD):
    s = q.shape[0]
    tq, tk = _row_tile(s, tq), _row_tile(s, tk)
    nk = s // tk
    tc = _row_tile(tk, chunk)
    sub = tk // tc

    def body(q_ref, k_ref, vt_ref, o_ref, lse_ref, m_s, l_s, acc_s):
        kk = pl.program_id(2)

        @pl.when(kk == 0)
        def _():
            m_s[...] = jnp.full_like(m_s, -jnp.inf)
            l_s[...] = jnp.zeros_like(l_s)
            acc_s[...] = jnp.zeros_like(acc_s)

        q_t = q_ref[...]

        def scores(c):
            return _dot_nt(k_ref[c * tc:(c + 1) * tc, :], q_t)

        m, l, acc = m_s[...], l_s[...], acc_s[...]
        s_next = scores(0)
        for c in range(sub):
            s_t = s_next
            if c + 1 < sub:
                s_next = scores(c + 1)
            m_new = jnp.maximum(m, jnp.max(s_t, axis=0, keepdims=True))
            alpha = jnp.exp2(m - m_new)
            p_t = jnp.exp2(s_t - m_new)
            l = alpha * l + jnp.sum(p_t, axis=0, keepdims=True)
            acc = alpha * acc + _dot(vt_ref[:, c * tc:(c + 1) * tc], p_t.astype(BF16))
            m = m_new
        m_s[...], l_s[...], acc_s[...] = m, l, acc

        @pl.when(kk == nk - 1)
        def _():
            o_ref[...] = jnp.transpose(acc_s[...] / l_s[...]).astype(BF16)
            lse_ref[...] = m_s[...] + jnp.log2(l_s[...])

    return pl.pallas_call(
        body, name="flash_fwd",
        grid=(HEADS, s // tq, nk),
        in_specs=[pl.BlockSpec((tq, HEAD_PAD), lambda h, i, kk: (i, h)),
                  pl.BlockSpec((tk, HEAD_PAD), lambda h, i, kk: (kk, h)),
                  pl.BlockSpec((HEAD_PAD, tk), lambda h, i, kk: (h, kk))],
        out_specs=[pl.BlockSpec((tq, HEAD_PAD), lambda h, i, kk: (i, h)),
                   pl.BlockSpec((None, 1, tq), lambda h, i, kk: (h, 0, i))],
        out_shape=[jax.ShapeDtypeStruct((s, ATT_W), BF16), jax.ShapeDtypeStruct((HEADS, 1, s), F32)],
        scratch_shapes=[pltpu.VMEM((1, tq), F32), pltpu.VMEM((1, tq), F32), pltpu.VMEM((HEAD_PAD, tq), F32)],
        compiler_params=_params("parallel", "parallel", "arbitrary"),
    )(q, k, vt)


def flash_bwd(q, k, v, do, lse_row, delta_row, tq=8192, tk=1024, chunk=ATT_CHUNK_BWD):
    s = q.shape[0]
    tq, tk = _row_tile(s, tq), _row_tile(s, tk)
    nq = s // tq
    tc = _row_tile(tq, chunk)
    sub = tq // tc

    def body(q_ref, k_ref, v_ref, do_ref, lse_ref, dl_ref, dqt_ref, dk_ref, dv_ref, kt_s, dk_s, dv_s):
        kb, qb = pl.program_id(1), pl.program_id(2)

        @pl.when(qb == 0)
        def _():
            kt_s[...] = jnp.transpose(k_ref[...].astype(F32)).astype(BF16)
            dk_s[...] = jnp.zeros_like(dk_s)
            dv_s[...] = jnp.zeros_like(dv_s)

        k_t, v_t = k_ref[...], v_ref[...]

        def products(c):
            rows = slice(c * tc, (c + 1) * tc)
            return _dot_nt(k_t, q_ref[rows, :]), _dot_nt(v_t, do_ref[rows, :])

        dk, dv = dk_s[...], dv_s[...]
        nxt = products(0)
        contribs = []
        for c in range(sub):
            rows = slice(c * tc, (c + 1) * tc)
            s_t, dp_t = nxt
            if c + 1 < sub:
                nxt = products(c + 1)
            p_t = jnp.exp2(s_t - lse_ref[:, rows])
            ds_t = (p_t * (dp_t - dl_ref[:, rows])).astype(BF16)
            dv = dv + _dot(p_t.astype(BF16), do_ref[rows, :])
            dk = dk + _dot(ds_t, q_ref[rows, :])
            contribs.append(_dot(kt_s[...], ds_t))
        dk_s[...], dv_s[...] = dk, dv
        contrib = jnp.concatenate(contribs, axis=1) if sub > 1 else contribs[0]
        cols = pl.ds(pl.multiple_of(qb * tq, tq), tq)

        @pl.when(kb == 0)
        def _():
            dqt_ref[:, cols] = contrib

        @pl.when(kb > 0)
        def _():
            dqt_ref[:, cols] += contrib

        @pl.when(qb == nq - 1)
        def _():
            dk_ref[...] = dk_s[...] * LN_2
            dv_ref[...] = dv_s[...]

    qspec = pl.BlockSpec((tq, HEAD_PAD), lambda h, kb, qb: (qb, h))
    kspec = pl.BlockSpec((tk, HEAD_PAD), lambda h, kb, qb: (kb, h))
    rowspec = pl.BlockSpec((None, 1, tq), lambda h, kb, qb: (h, 0, qb))
    return pl.pallas_call(
        body, name="flash_bwd",
        grid=(HEADS, s // tk, nq),
        in_specs=[qspec, kspec, kspec, qspec, rowspec, rowspec],
        out_specs=[pl.BlockSpec((HEAD_PAD, s), lambda h, kb, qb: (h, 0)), kspec, kspec],
        out_shape=[jax.ShapeDtypeStruct((ATT_W, s), F32), jax.ShapeDtypeStruct((s, ATT_W), F32),
                   jax.ShapeDtypeStruct((s, ATT_W), F32)],
        scratch_shapes=[pltpu.VMEM((HEAD_PAD, tk), BF16), pltpu.VMEM((tk, HEAD_PAD), F32),
                        pltpu.VMEM((tk, HEAD_PAD), F32)],
        compiler_params=_params("parallel", "arbitrary", "arbitrary"),
    )(q, k, v, do, lse_row, delta_row)


def _halo_specs(tm, w, n_tiles):
    per = tm // POOL_HALO
    last = n_tiles * per - 1
    return [pl.BlockSpec((POOL_HALO, w), lambda i: (jnp.maximum(i * per - 1, 0), 0)),
            pl.BlockSpec((tm, w), lambda i: (i, 0)),
            pl.BlockSpec((POOL_HALO, w), lambda i: (jnp.minimum((i + 1) * per, last), 0))]


def _fill_ext(ext_ref, prev, cur, nxt, i, n_tiles, tm):
    ext_ref[pl.ds(0, POOL_HALO), :] = jnp.where(i > 0, prev, 0.0)
    ext_ref[pl.ds(POOL_HALO, tm), :] = cur
    ext_ref[pl.ds(POOL_HALO + tm, POOL_HALO), :] = jnp.where(i < n_tiles - 1, nxt, 0.0)


def _window_sum(ext_ref, g, lo, hi, tm):
    lanes = pl.ds(g * GROUP, GROUP)
    acc = ext_ref[pl.ds(POOL_HALO + lo, tm), lanes]
    for d in range(lo + 1, hi + 1):
        acc = acc + ext_ref[pl.ds(POOL_HALO + d, tm), lanes]
    return acc


def _pool_mixed(ext_ref, cur, i, s, tm):
    row = i * tm + lax.broadcasted_iota(jnp.int32, (tm, 1), 0)
    out = []
    for g, w in enumerate(POOL_WINDOWS):
        left = w // 2
        right = w - 1 - left
        cnt = (jnp.minimum(row + right + 1, s) - jnp.maximum(row - left, 0)).astype(F32)
        out.append(_window_sum(ext_ref, g, -left, right, tm) / cnt - cur[:, g * GROUP:(g + 1) * GROUP])
    return out


def pool_out_fwd(x, z, o, wpool, pscale, wout, tm=512):
    s = x.shape[0]
    tm = _row_tile(s, tm)
    n_tiles = s // tm

    def body(x_ref, zp_ref, z_ref, zn_ref, o_ref, wp_ref, ps_ref, wo_ref, y_ref, cat_ref, ext_s):
        i = pl.program_id(0)
        cur = z_ref[:, Z_POOL:Z_KPE]
        _fill_ext(ext_s, zp_ref[:, Z_POOL:Z_KPE], cur, zn_ref[:, Z_POOL:Z_KPE], i, n_tiles, tm)
        mixed = _pool_mixed(ext_s, cur, i, s, tm)
        cat_ref[:, :ATT_W] = o_ref[...]
        for g in range(len(POOL_WINDOWS)):
            lanes = slice(g * GROUP, (g + 1) * GROUP)
            yg = _dot(mixed[g].astype(BF16), wp_ref[g])
            cat_ref[:, ATT_W + g * GROUP:ATT_W + (g + 1) * GROUP] = (yg * ps_ref[:, lanes]).astype(BF16)
        y_ref[...] = x_ref[...] + _dot(cat_ref[...], wo_ref[...])

    full = lambda shape: pl.BlockSpec(shape, lambda i: (0,) * len(shape))
    row = lambda w: pl.BlockSpec((tm, w), lambda i: (i, 0))
    return pl.pallas_call(
        body, name="pool_out_fwd",
        grid=(n_tiles,),
        in_specs=[row(D_MODEL)] + _halo_specs(tm, Z_W, n_tiles) + [
            row(ATT_W), full((len(POOL_WINDOWS), GROUP, GROUP)), full((1, POOL_W)), full((CAT_W, D_MODEL))],
        out_specs=[row(D_MODEL), row(CAT_W)],
        out_shape=[jax.ShapeDtypeStruct((s, D_MODEL), F32), jax.ShapeDtypeStruct((s, CAT_W), BF16)],
        scratch_shapes=[pltpu.VMEM((tm + 2 * POOL_HALO, POOL_W), F32)],
        compiler_params=_params("parallel"),
    )(x, z, z, z, o, wpool, pscale, wout)


def out_bwd(dx, wout, cat, tm=512):
    s = dx.shape[0]
    tm = _row_tile(s, tm)

    def body(dx_ref, wo_ref, cat_ref, do_ref, db_ref, dl_ref):
        dcat = _dot_nt(dx_ref[...].astype(BF16), wo_ref[...])
        do_ref[...] = dcat[:, :ATT_W].astype(BF16)
        db_ref[...] = dcat[:, ATT_W:]
        for hd in range(HEADS):
            lanes = slice(hd * HEAD_PAD, (hd + 1) * HEAD_PAD)
            dl_ref[hd] = jnp.sum(dcat[:, lanes] * cat_ref[:, lanes].astype(F32), axis=-1, keepdims=True)

    row = lambda w: pl.BlockSpec((tm, w), lambda i: (i, 0))
    return pl.pallas_call(
        body, name="out_bwd",
        grid=(s // tm,),
        in_specs=[row(D_MODEL), pl.BlockSpec((CAT_W, D_MODEL), lambda i: (0, 0)), row(CAT_W)],
        out_specs=[row(ATT_W), row(POOL_W), pl.BlockSpec((HEADS, tm, 1), lambda i: (0, i, 0))],
        out_shape=[jax.ShapeDtypeStruct((s, ATT_W), BF16), jax.ShapeDtypeStruct((s, POOL_W), F32),
                   jax.ShapeDtypeStruct((HEADS, s, 1), F32)],
        compiler_params=_params("parallel"),
    )(dx, wout, cat)


def pool_bwd(db, z, wpool, pscale, tm=512):
    s = db.shape[0]
    tm = _row_tile(s, tm)
    n_tiles = s // tm
    te = tm + 2 * POOL_HALO

    def body(dbp_ref, db_ref, dbn_ref, zp_ref, z_ref, zn_ref, wp_ref, ps_ref,
             dp_ref, mixed_ref, dys_ref, dps_ref, ext_s, text_s):
        i = pl.program_id(0)
        cur = z_ref[:, Z_POOL:Z_KPE]
        _fill_ext(ext_s, zp_ref[:, Z_POOL:Z_KPE], cur, zn_ref[:, Z_POOL:Z_KPE], i, n_tiles, tm)
        mixed = _pool_mixed(ext_s, cur, i, s, tm)
        _fill_ext(ext_s, dbp_ref[...], db_ref[...], dbn_ref[...], i, n_tiles, tm)
        erow = i * tm - POOL_HALO + lax.broadcasted_iota(jnp.int32, (te, 1), 0)
        dps = []
        for g, w in enumerate(POOL_WINDOWS):
            lanes = slice(g * GROUP, (g + 1) * GROUP)
            left = w // 2
            right = w - 1 - left
            mb = mixed[g].astype(BF16)
            mixed_ref[:, lanes] = mb
            dps.append(jnp.sum(db_ref[:, lanes] * _dot(mb, wp_ref[g]), axis=0, keepdims=True))
            dys_e = (ext_s[:, lanes] * ps_ref[:, lanes]).astype(BF16)
            dys_ref[:, lanes] = (db_ref[:, lanes] * ps_ref[:, lanes]).astype(BF16)
            dmix_e = _dot_nt(dys_e, wp_ref[g])
            cnt = jnp.minimum(erow + right + 1, s) - jnp.maximum(erow - left, 0)
            text_s[:, lanes] = dmix_e / jnp.maximum(cnt, 1).astype(F32)
            dp_ref[:, lanes] = _window_sum(text_s, g, -right, left, tm) - dmix_e[POOL_HALO:POOL_HALO + tm]
        part = jnp.concatenate(dps, axis=1)

        @pl.when(i == 0)
        def _():
            dps_ref[...] = part

        @pl.when(i > 0)
        def _():
            dps_ref[...] += part

    full = lambda shape: pl.BlockSpec(shape, lambda i: (0,) * len(shape))
    row = lambda w: pl.BlockSpec((tm, w), lambda i: (i, 0))
    return pl.pallas_call(
        body, name="pool_bwd",
        grid=(n_tiles,),
        in_specs=_halo_specs(tm, POOL_W, n_tiles) + _halo_specs(tm, Z_W, n_tiles) + [
            full((len(POOL_WINDOWS), GROUP, GROUP)), full((1, POOL_W))],
        out_specs=[row(POOL_W), row(POOL_W), row(POOL_W), full((1, POOL_W))],
        out_shape=[jax.ShapeDtypeStruct((s, POOL_W), F32), jax.ShapeDtypeStruct((s, POOL_W), BF16),
                   jax.ShapeDtypeStruct((s, POOL_W), BF16), jax.ShapeDtypeStruct((1, POOL_W), F32)],
        scratch_shapes=[pltpu.VMEM((te, POOL_W), F32), pltpu.VMEM((te, POOL_W), F32)],
        compiler_params=_params("arbitrary"),
    )(db, db, db, z, z, z, wpool, pscale)


def loss_head(y, target, tm=512):
    s = y.shape[0]
    tm = _row_tile(s, tm)

    def body(y_ref, t_ref, dy_ref, loss_ref):
        i = pl.program_id(0)
        err = y_ref[...] - t_ref[...]
        dy_ref[...] = err * (1.0 / D_MODEL)
        part = 0.5 * jnp.sum(jnp.sum(err * err, axis=-1, keepdims=True) * (1.0 / D_MODEL), axis=0, keepdims=True)
        part = jnp.broadcast_to(part, (1, 128))

        @pl.when(i == 0)
        def _():
            loss_ref[...] = part

        @pl.when(i > 0)
        def _():
            loss_ref[...] += part

    row = pl.BlockSpec((tm, D_MODEL), lambda i: (i, 0))
    return pl.pallas_call(
        body, name="loss_head",
        grid=(s // tm,),
        in_specs=[row, row],
        out_specs=[row, pl.BlockSpec((1, 128), lambda i: (0, 0))],
        out_shape=[jax.ShapeDtypeStruct((s, D_MODEL), F32), jax.ShapeDtypeStruct((1, 128), F32)],
        compiler_params=_params("arbitrary"),
    )(y, target)


def _pad_heads(w, real):
    lead = w.shape[:-1]
    w = w.reshape(lead + (HEADS, real))
    w = jnp.pad(w, [(0, 0)] * len(lead) + [(0, 0), (0, HEAD_PAD - real)])
    return w.reshape(lead + (ATT_W,))


def _unpad_heads(w, real):
    lead = w.shape[:-1]
    return w.reshape(lead + (HEADS, HEAD_PAD))[..., :real].reshape(lead + (HEADS * real,))


def layout_weights(full):
    w = {}
    depth = full["w_in"].shape[0]
    win = full["w_in"]
    zc = lambda n: jnp.zeros((depth, D_MODEL, n), win.dtype)
    o_pe = Q_LORA + KV_LORA
    w["win"] = jnp.concatenate([win[..., :o_pe], win[..., o_pe + QK_ROPE:], zc(QK_NOPE), win[..., o_pe:o_pe + QK_ROPE],
                                zc(HEAD_PAD - QK_HEAD)], axis=-1)
    w["wuq"] = _pad_heads(full["w_uq"], QK_HEAD)
    w["wuk"] = _pad_heads(full["w_uk"], QK_NOPE)
    w["wuv"] = _pad_heads(full["w_uv"], V_HEAD)
    wo = full["w_out"]
    wo_att = jnp.swapaxes(_pad_heads(jnp.swapaxes(wo[:, :HEADS * V_HEAD], 1, 2), V_HEAD), 1, 2)
    w["wout"] = jnp.concatenate([wo_att, wo[:, HEADS * V_HEAD:]], axis=1)
    pad_vec = lambda v: jnp.pad(v, ((0, 0), (0, HEAD_PAD - QK_HEAD)))
    w["qn"], w["kn"] = pad_vec(full["q_norm"]), pad_vec(full["k_norm"])
    for n in ("wuq", "qn", "kn"):
        w[n + "_sw"] = swap_rope_halves(w[n])
    for n in ("ffn1_norm", "mix_norm", "ffn2_norm", "q_lat_norm", "kv_lat_norm", "pool_scale", "w_pool"):
        w[n] = full[n]
    return w


def local_step(x, target, w, wgu1, wd1, wgu2, wd2):
    s = x.shape[0]
    depth = w["win"].shape[0]
    rc, rs1, rs2 = rope_tables(s)
    rs_both = rs1 + rs2
    vec = lambda name, l: w[name][l][None, :]
    saved = []
    for l in range(depth):
        x1, g1, u1 = ffn_fwd(x, vec("ffn1_norm", l), wgu1, wd1, l)
        z, q, k, v, vt = mix_in_fwd(x1, vec("mix_norm", l), w["win"][l], vec("q_lat_norm", l), vec("kv_lat_norm", l),
                                    w["wuq"][l], w["wuq_sw"][l], w["wuk"][l], w["wuv"][l], vec("qn", l),
                                    vec("qn_sw", l), vec("kn", l), vec("kn_sw", l), rc, rs_both)
        o, lse = flash_fwd(q, k, vt)
        x2, cat = pool_out_fwd(x1, z, o, w["w_pool"][l], vec("pool_scale", l), w["wout"][l])
        x3, g2, u2 = ffn_fwd(x2, vec("ffn2_norm", l), wgu2, wd2, l)
        saved.append((x, x1, z, q, k, v, lse, cat, x2, g1, u1, g2, u2))
        x = x3
    dy, loss = loss_head(x, target)

    names = ["ffn1_norm", "ffn1_w_gu", "ffn1_w_down", "mix_norm", "w_in", "q_lat_norm", "kv_lat_norm", "w_uq", "w_uk",
             "w_uv", "q_norm", "k_norm", "w_pool", "pool_scale", "w_out", "ffn2_norm", "ffn2_w_gu", "ffn2_w_down"]
    grads = {n: [None] * depth for n in names if "ffn" not in n or "norm" in n}
    for n in ("ffn1", "ffn2"):
        grads[n + "_w_gu"] = lax.empty((depth, N_CHIPS, D_MODEL, FF_BLOCK), F32)
        grads[n + "_w_down"] = lax.empty((depth, D_FF, D_MODEL), F32)

    def ffn_weight_grads(n, l, h, act, dg, du, dyh):
        gu = matmul_tn(h, dg, into=grads[n + "_w_gu"], at=lambda j: (l, j))
        grads[n + "_w_gu"] = matmul_tn(h, du, into=gu, at=lambda j: (l, j + 2))
        grads[n + "_w_down"] = matmul_tn(act, dyh, tk=1024, into=grads[n + "_w_down"], at=lambda j: (l,))

    for l in reversed(range(depth)):
        x0, x1, z, q, k, v, lse, cat, x2, g1, u1, g2, u2 = saved[l]
        dx2, h, act, dg, du, dyh, dgn = ffn_bwd(x2, dy, vec("ffn2_norm", l), g2, u2, wgu2, wd2, l)
        grads["ffn2_norm"][l] = dgn[0]
        ffn_weight_grads("ffn2", l, h, act, dg, du, dyh)

        do, db, delta = out_bwd(dx2, w["wout"][l], cat)
        dwout = matmul_tn(cat, dx2)
        grads["w_out"][l] = jnp.concatenate(
            [jnp.swapaxes(_unpad_heads(jnp.swapaxes(dwout[:ATT_W], 0, 1), V_HEAD), 0, 1), dwout[ATT_W:]], axis=0)
        dp, mixed, dys, dps = pool_bwd(db, z, w["w_pool"][l], vec("pool_scale", l))
        grads["pool_scale"][l] = dps[0]
        dwp = matmul_tn(mixed, dys)
        grads["w_pool"][l] = jnp.stack([dwp[g * GROUP:(g + 1) * GROUP, g * GROUP:(g + 1) * GROUP]
                                        for g in range(len(POOL_WINDOWS))])
        dq_t, dk, dv = flash_bwd(q, k, v, do, lse, delta.reshape(HEADS, 1, s))
        dz, cqn, ckvn, dqr, dkr, dvb, dqln, dkvln, dqn, dkn = mla_bwd(
            z, dq_t, dk, dv, dp, vec("q_lat_norm", l), vec("kv_lat_norm", l), w["wuq"][l], w["wuk"][l], w["wuv"][l],
            vec("qn", l), vec("kn", l), rc, rs1, rs2)
        grads["q_lat_norm"][l], grads["kv_lat_norm"][l] = dqln[0], dkvln[0]
        grads["q_norm"][l], grads["k_norm"][l] = dqn[0, :QK_HEAD], dkn[0, :QK_HEAD]
        grads["w_uq"][l] = _unpad_heads(matmul_tn(cqn, dqr), QK_HEAD)
        grads["w_uk"][l] = _unpad_heads(matmul_tn(ckvn, dkr), QK_NOPE)
        grads["w_uv"][l] = _unpad_heads(matmul_tn(ckvn, dvb), V_HEAD)
        dx1, h2, dgm = mix_in_bwd(x1, dx2, dz, vec("mix_norm", l), w["win"][l])
        grads["mix_norm"][l] = dgm[0]
        dwin = matmul_tn(h2, dz)
        grads["w_in"][l] = jnp.concatenate(
            [dwin[:, :Z_POOL], dwin[:, Z_KPE + QK_NOPE:Z_KPE + QK_HEAD], dwin[:, Z_POOL:Z_KPE]], axis=1)

        dy, h, act, dg, du, dyh, dgn = ffn_bwd(x0, dx1, vec("ffn1_norm", l), g1, u1, wgu1, wd1, l)
        grads["ffn1_norm"][l] = dgn[0]
        ffn_weight_grads("ffn1", l, h, act, dg, du, dyh)
    return loss, dy, {n: (jnp.stack(g) if isinstance(g, list) else g) for n, g in grads.items()}


HBM_SPEC = pl.BlockSpec(memory_space=pltpu.HBM)


def _place():
    x, y, c = lax.axis_index("x"), lax.axis_index("y"), lax.axis_index("c")
    other_chips = [(1 - x, y), (x, 1 - y), (1 - x, 1 - y)]
    return x, y, c, other_chips


def _exchange(name, arrays, out_shapes, plan, n_local, n_remote):
    n = len(arrays)

    def body(*refs):
        ins, outs = refs[:n], refs[n:n + len(out_shapes)]
        send_sems, recv_sems, local_sems = refs[n + len(out_shapes):]
        local, sends, recvs, forwards = plan(ins, outs, _place())

        def remote(k, src, dst, dev):
            return pltpu.make_async_remote_copy(src_ref=src, dst_ref=dst, send_sem=send_sems.at[k],
                                                recv_sem=recv_sems.at[k], device_id=dev, device_id_type=MESH)

        started = []
        for k, (src, dst) in enumerate(local):
            cp = pltpu.make_async_copy(src, dst, local_sems.at[k])
            cp.start()
            started.append(cp)
        out_going = []
        for k, (src, dst, dev) in enumerate(sends):
            cp = remote(k, src, dst, dev)
            cp.start()
            out_going.append(cp)
        for k, (dst, dev) in enumerate(recvs):
            remote(k, dst, dst, dev).wait_recv()
            if forwards:
                src, fdst, fdev, _ = forwards[k]
                cp = remote(len(sends) + k, src, fdst, fdev)
                cp.start()
                out_going.append(cp)
        for k, (_, _, fdev, landing) in enumerate(forwards):
            remote(len(sends) + k, landing, landing, fdev).wait_recv()
        for cp in out_going:
            cp.wait_send()
        for cp in started:
            cp.wait()

    return pl.pallas_call(
        body, name=name,
        in_specs=[HBM_SPEC] * n,
        out_specs=[HBM_SPEC] * len(out_shapes),
        out_shape=out_shapes,
        scratch_shapes=[pltpu.SemaphoreType.DMA((n_remote,)), pltpu.SemaphoreType.DMA((n_remote,)),
                        pltpu.SemaphoreType.DMA((max(n_local, 1),))],
    )(*arrays)


def _layer_half(depth, which):
    assert depth % 2 == 0
    return pl.ds(which * (depth // 2), depth // 2)


def gather_chips(shards):
    n = len(shards)
    depth = shards[0].shape[0]

    def plan(ins, outs, place):
        x, y, c, chips = place
        me, sib = 2 * x + y, (x, y, 1 - c)
        mine, theirs = _layer_half(depth, c), _layer_half(depth, 1 - c)
        sends = [(ins[p].at[mine], outs[p].at[me, mine], (px, py, c)) for p in range(n) for (px, py) in chips]
        recvs = [(outs[p].at[2 * px + py, mine], (px, py, c)) for p in range(n) for (px, py) in chips]
        forwards = [(outs[p].at[2 * px + py, mine], outs[p].at[2 * px + py, mine], sib, outs[p].at[2 * px + py, theirs])
                    for p in range(n) for (px, py) in chips]
        return [], sends, recvs, forwards

    out_shapes = [jax.ShapeDtypeStruct((N_CHIPS,) + a.shape, a.dtype) for a in shards]
    return _exchange("gather_chips", shards, out_shapes, plan, 0, 6 * n)


def swap_halves(fulls):
    n = len(fulls)
    depth = fulls[0].shape[0]

    def plan(ins, outs, place):
        x, y, c, _ = place
        sib = (x, y, 1 - c)
        sends = [(ins[p].at[_layer_half(depth, 1 - c)], outs[p], sib) for p in range(n)]
        return [], sends, [(outs[p], sib) for p in range(n)], []

    out_shapes = [jax.ShapeDtypeStruct((depth // 2,) + a.shape[1:], a.dtype) for a in fulls]
    return _exchange("swap_halves", fulls, out_shapes, plan, 0, n)


def scatter_chips(fulls):
    n = len(fulls)

    def plan(ins, outs, place):
        x, y, c, chips = place
        sends = [(ins[p].at[:, 2 * px + py], outs[p].at[j], (px, py, c))
                 for p in range(n) for j, (px, py) in enumerate(chips)]
        recvs = [(outs[p].at[j], (px, py, c)) for p in range(n) for j, (px, py) in enumerate(chips)]
        return [], sends, recvs, []

    out_shapes = [jax.ShapeDtypeStruct((3, a.shape[0]) + a.shape[2:], a.dtype) for a in fulls]
    return _exchange("scatter_chips", fulls, out_shapes, plan, 0, 3 * n)


def swap_sibling(arrays):
    n = len(arrays)

    def plan(ins, outs, place):
        x, y, c, _ = place
        sib = (x, y, 1 - c)
        return [], [(ins[p], outs[p], sib) for p in range(n)], [(outs[p], sib) for p in range(n)], []

    return _exchange("swap_sibling", arrays, [jax.ShapeDtypeStruct(a.shape, a.dtype) for a in arrays], plan, 0, n)


def gather_all(vec):
    def plan(ins, outs, place):
        x, y, c, _ = place
        me = 4 * x + 2 * y + c
        flips = [(fx, fy, fc) for fx in (0, 1) for fy in (0, 1) for fc in (0, 1)][1:]
        peers = [(1 - x if fx else x, 1 - y if fy else y, 1 - c if fc else c) for fx, fy, fc in flips]
        local = [(ins[0], outs[0].at[me])]
        sends = [(ins[0], outs[0].at[me], dev) for dev in peers]
        recvs = [(outs[0].at[4 * dev[0] + 2 * dev[1] + dev[2]], dev) for dev in peers]
        return local, sends, recvs, []

    return _exchange("gather_all", [vec], [jax.ShapeDtypeStruct((8,) + vec.shape, vec.dtype)], plan, 1, 7)[0]


def _pick_rows(rows, cols, block_bytes=1024 * 1024):
    best = None
    for t in range(16, rows + 1, 16):
        if rows % t == 0 and t * cols * 4 <= block_bytes:
            best = t
    return best or rows


def sum_rows_of(own, stacked, tr, place, own_block, with_bf16=False):
    n, rows, cols = stacked.shape
    assert rows % tr == 0

    def body(place_ref, own_ref, st_ref, o_ref, *narrow):
        acc = own_ref[...]
        for k in range(n):
            acc = acc + st_ref[k].astype(F32)
        o_ref[...] = acc
        if with_bf16:
            narrow[0][...] = acc.astype(BF16)

    spec = pl.BlockSpec((tr, cols), lambda i, p: (i, 0))
    return pl.pallas_call(
        body, name="sum_rows_of",
        grid_spec=pltpu.PrefetchScalarGridSpec(
            num_scalar_prefetch=1, grid=(rows // tr,),
            in_specs=[pl.BlockSpec((tr, cols), lambda i, p: (own_block(i, p[0]), 0)),
                      pl.BlockSpec((n, tr, cols), lambda i, p: (0, i, 0))],
            out_specs=[spec, spec] if with_bf16 else spec),
        out_shape=([jax.ShapeDtypeStruct((rows, cols), F32), jax.ShapeDtypeStruct((rows, cols), BF16)]
                   if with_bf16 else jax.ShapeDtypeStruct((rows, cols), F32)),
        compiler_params=_params("parallel"),
    )(place.astype(jnp.int32).reshape(1), own, stacked)


def sum_stack(stacked):
    n, rows, cols = stacked.shape
    tr = _pick_rows(rows, cols)

    def body(st_ref, o_ref):
        acc = st_ref[0]
        for k in range(1, n):
            acc = acc + st_ref[k]
        o_ref[...] = acc

    return pl.pallas_call(
        body, name="sum_stack",
        grid=(rows // tr,),
        in_specs=[pl.BlockSpec((n, tr, cols), lambda i: (0, i, 0))],
        out_specs=pl.BlockSpec((tr, cols), lambda i: (i, 0)),
        out_shape=jax.ShapeDtypeStruct((rows, cols), F32),
        compiler_params=_params("parallel"),
    )(stacked)


def adamw(w, g_mine, m, v, g_theirs=None, core=None):
    rows, cols = w.shape
    tr = _pick_rows(g_mine.shape[0], cols)
    per_half = g_mine.shape[0] // tr
    c1 = 1.0 - ADAM_B1 ** ADAM_STEP
    c2 = 1.0 - ADAM_B2 ** ADAM_STEP
    halves = g_theirs is not None
    ins = [w, g_mine] + ([g_theirs] if halves else []) + [m, v]
    place = (core if halves else jnp.zeros((), jnp.int32)).astype(jnp.int32).reshape(1)

    def body(place_ref, *refs):
        w_ref, ga_ref = refs[0], refs[1]
        m_ref, v_ref, g_out, d_out, m_out, v_out = refs[-6:]
        g = ga_ref[...]
        if halves:
            g = jnp.where(pl.program_id(0) // per_half == place_ref[0], g, refs[2][...])
        mn = ADAM_B1 * m_ref[...] + (1.0 - ADAM_B1) * g
        vn = ADAM_B2 * v_ref[...] + (1.0 - ADAM_B2) * (g * g)
        m_hat = mn / c1
        v_hat = vn / c2
        g_out[...] = g
        d_out[...] = -ADAM_LR * (m_hat / (jnp.sqrt(v_hat) + ADAM_EPS) + ADAM_WD * w_ref[...])
        m_out[...] = mn
        v_out[...] = vn

    spec = pl.BlockSpec((tr, cols), lambda i, p: (i, 0))
    g_spec = pl.BlockSpec((tr, cols), lambda i, p: (i % per_half, 0))
    return pl.pallas_call(
        body, name="adamw",
        grid_spec=pltpu.PrefetchScalarGridSpec(
            num_scalar_prefetch=1, grid=(rows // tr,),
            in_specs=[spec] + [g_spec] * (2 if halves else 1) + [spec, spec],
            out_specs=[spec] * 4),
        out_shape=[jax.ShapeDtypeStruct((rows, cols), F32)] * 4,
        compiler_params=_params("parallel"),
    )(place, *ins)


WEIGHT_NAMES = ["ffn1_norm", "ffn1_w_gu", "ffn1_w_down", "mix_norm", "w_in", "q_lat_norm", "kv_lat_norm", "w_uq", "w_uk",
                "w_uv", "q_norm", "k_norm", "w_pool", "pool_scale", "w_out", "ffn2_norm", "ffn2_w_gu", "ffn2_w_down"]
COL_SHARDED = ["ffn1_w_gu", "w_in", "w_uq", "w_uk", "w_uv", "ffn2_w_gu"]
ROW_SHARDED = ["ffn1_w_down", "w_out", "ffn2_w_down"]
SHARDED = [n for n in WEIGHT_NAMES if n in COL_SHARDED or n in ROW_SHARDED]
REPLICATED = [n for n in WEIGHT_NAMES if n not in SHARDED]


def _join_chips(g, name):
    _, depth, r, c = g.shape
    if name in COL_SHARDED:
        return jnp.transpose(g, (1, 2, 0, 3)).reshape(depth, r, N_CHIPS * c)
    return jnp.transpose(g, (1, 0, 2, 3)).reshape(depth, N_CHIPS * r, c)


def _split_chips(full, name):
    depth, r, c = full.shape
    if name in COL_SHARDED:
        return jnp.transpose(full.reshape(depth, r, N_CHIPS, c // N_CHIPS), (0, 2, 1, 3))
    return full.reshape(depth, N_CHIPS, r // N_CHIPS, c)


def _pack(vs):
    flat = jnp.concatenate([v.reshape(-1) for v in vs])
    pad = (-flat.shape[0]) % (8 * 128)
    return jnp.pad(flat, (0, pad)).reshape(-1, 128)


def _unpack(packed, like):
    flat = packed.reshape(-1)
    out, at = [], 0
    for v in like:
        out.append(flat[at:at + v.size].reshape(v.shape))
        at += v.size
    return out


def kernel(x, ffn1_norm, ffn1_w_gu, ffn1_w_down, mix_norm, w_in, q_lat_norm, kv_lat_norm, w_uq, w_uk, w_uv, q_norm, k_norm, w_pool, pool_scale, w_out, ffn2_norm, ffn2_w_gu, ffn2_w_down, loss_target, m_ffn1_norm, m_ffn1_w_gu, m_ffn1_w_down, m_mix_norm, m_w_in, m_q_lat_norm, m_kv_lat_norm, m_w_uq, m_w_uk, m_w_uv, m_q_norm, m_k_norm, m_w_pool, m_pool_scale, m_w_out, m_ffn2_norm, m_ffn2_w_gu, m_ffn2_w_down, v_ffn1_norm, v_ffn1_w_gu, v_ffn1_w_down, v_mix_norm, v_w_in, v_q_lat_norm, v_kv_lat_norm, v_w_uq, v_w_uk, v_w_uv, v_q_norm, v_k_norm, v_w_pool, v_pool_scale, v_w_out, v_ffn2_norm, v_ffn2_w_gu, v_ffn2_w_down):
    args = locals()
    weights = {n: args[n] for n in WEIGHT_NAMES}
    moments_m = {n: args["m_" + n] for n in WEIGHT_NAMES}
    moments_v = {n: args["v_" + n] for n in WEIGHT_NAMES}

    chip = 2 * lax.axis_index("x") + lax.axis_index("y")
    core = lax.axis_index("c")
    shards = [weights[n].astype(BF16) for n in SHARDED]
    gathered = {n: lax.dynamic_update_index_in_dim(g, own, chip, 0)
                for n, own, g in zip(SHARDED, shards, gather_chips(shards))}
    full = {n: _join_chips(gathered[n], n) for n in SHARDED if "ffn" not in n}
    for n in REPLICATED:
        full[n] = weights[n].astype(BF16) if n == "w_pool" else weights[n]
    w = layout_weights(full)
    wd1 = _join_chips(gathered["ffn1_w_down"], "ffn1_w_down")
    wd2 = _join_chips(gathered["ffn2_w_down"], "ffn2_w_down")

    loss_part, grad_x, grads = local_step(x[0], loss_target[0], w, gathered["ffn1_w_gu"], wd1, gathered["ffn2_w_gu"], wd2)
    loss = lax.psum(loss_part[0, 0], ("x", "y", "c"))

    split = [grads[n] if n.endswith("w_gu") else _split_chips(grads[n], n) for n in SHARDED]
    half = split[0].shape[0] // 2

    def with_sibling(g4, other):
        cols = g4.shape[-1]
        rows = other.size // cols
        tr = _pick_rows(rows, cols)
        sums = sum_rows_of(g4.reshape(-1, cols), other.reshape(1, rows, cols), tr, core,
                           lambda i, c: i + c * (rows // tr), with_bf16=True)
        return [t.reshape(other.shape) for t in sums]

    def with_chips(wide, land):
        r, cols = wide.shape[-2:]
        tr = _pick_rows(r, cols)
        per = r // tr
        return sum_rows_of(wide.reshape(-1, cols), land.reshape(3, -1, cols), tr, chip,
                           lambda i, b: (i // per) * N_CHIPS * per + b * per + i % per)

    chip_sums = [with_sibling(g4, other) for g4, other in zip(split, swap_halves(split))]
    landed = scatter_chips([narrow for _, narrow in chip_sums])
    totals = [with_chips(wide, land) for (wide, _), land in zip(chip_sums, landed)]
    out = {}
    for n, mine, theirs in zip(SHARDED, totals, swap_sibling(totals)):
        shape = weights[n].shape
        flat = lambda a: a.reshape(shape[0] * shape[1], shape[2])
        res = adamw(flat(weights[n]), mine, flat(moments_m[n]), flat(moments_v[n]), g_theirs=theirs, core=core)
        out[n] = [r.reshape(shape) for r in res]

    rep_g = [grads[n] for n in REPLICATED]
    total = sum_stack(gather_all(_pack(rep_g)))
    res = adamw(_pack([weights[n] for n in REPLICATED]), total,
                _pack([moments_m[n] for n in REPLICATED]), _pack([moments_v[n] for n in REPLICATED]))
    unpacked = [_unpack(r, rep_g) for r in res]
    for k, n in enumerate(REPLICATED):
        out[n] = [u[k] for u in unpacked]

    return (loss, grad_x[None], *[out[n][0] for n in WEIGHT_NAMES], *[out[n][1] for n in WEIGHT_NAMES],
            *[out[n][2] for n in WEIGHT_NAMES], *[out[n][3] for n in WEIGHT_NAMES])
```

```python
import jax
import jax.numpy as jnp
from jax import lax
from jax.experimental import pallas as pl
from jax.experimental.pallas import tpu as pltpu

F32 = jnp.float32
BF16 = jnp.bfloat16
MESH = pl.DeviceIdType.MESH

DEPTH = 4
D_MODEL = 1024
D_FF = 2816
FF_BLOCK = D_FF // 2
N_CHIPS = 4
HEADS = 8
HEAD_PAD = 128
QK_NOPE = 64
QK_ROPE = 32
QK_HEAD = QK_NOPE + QK_ROPE
V_HEAD = 64
Q_LORA = 384
KV_LORA = 256
POOL_W = 512
POOL_WINDOWS = (2, 4, 8, 16)
POOL_HALO = 8
GROUP = 128
D_IN = Q_LORA + KV_LORA + QK_ROPE + POOL_W
Z_W = 1280
Z_CKV = Q_LORA
Z_POOL = Q_LORA + KV_LORA
Z_KPE = Z_POOL + POOL_W
ATT_W = HEADS * HEAD_PAD
CAT_W = ATT_W + POOL_W
EPS = 1e-6
ROPE_THETA = 10000.0
ADAM_LR, ADAM_B1, ADAM_B2, ADAM_EPS, ADAM_WD, ADAM_STEP = 0.001, 0.9, 0.999, 1e-08, 0.01, 10
LOG2_E = 1.4426950408889634
LN_2 = 0.6931471805599453

VMEM_LIMIT = 56 * 1024 * 1024
VMEM_LIMIT_FFN_BWD = 62 * 1024 * 1024


def _params(*sem, vmem=VMEM_LIMIT):
    return pltpu.CompilerParams(dimension_semantics=sem, vmem_limit_bytes=vmem)


def _dot(a, b):
    return jnp.dot(a, b, preferred_element_type=F32)


def _dot_nt(a, b):
    return lax.dot_general(a, b, (((1,), (1,)), ((), ())), preferred_element_type=F32)


def _dot_tn(a, b):
    return lax.dot_general(a, b, (((0,), (0,)), ((), ())), preferred_element_type=F32)


def _norm_fwd(x, n):
    rstd = lax.rsqrt(jnp.sum(x * x, axis=-1, keepdims=True) * (1.0 / n) + EPS)
    return x * rstd, rstd


def _norm_bwd(dy, g, xhat, rstd, n):
    dxhat = dy * g
    dx = rstd * (dxhat - xhat * (jnp.sum(dxhat * xhat, axis=-1, keepdims=True) * (1.0 / n)))
    return dx, dy * xhat


def _row_tile(s, want):
    t = min(s, want)
    assert s % t == 0
    return t


def ffn_fwd(x, gn, wgu4, wd, layer, tm=512):
    s = x.shape[0]
    tm = _row_tile(s, tm)

    def body(x_ref, gn_ref, wg_ref, wu_ref, wd_ref, o_ref, g_ref, u_ref, h_s, acc_s):
        j = pl.program_id(1)

        @pl.when(j == 0)
        def _():
            xhat, _ = _norm_fwd(x_ref[...], D_MODEL)
            h_s[...] = (xhat * gn_ref[...]).astype(BF16)
            acc_s[...] = jnp.zeros_like(acc_s)

        h = h_s[...]
        g = _dot(h, wg_ref[...])
        u = _dot(h, wu_ref[...])
        g_ref[...] = g.astype(BF16)
        u_ref[...] = u.astype(BF16)
        a = (g * jax.nn.sigmoid(g) * u).astype(BF16)
        acc_s[...] += _dot(a, wd_ref[...])

        @pl.when(j == 1)
        def _():
            o_ref[...] = x_ref[...] + 0.5 * acc_s[...]

    return pl.pallas_call(
        body, name="ffn_fwd",
        grid=(s // tm, 2),
        in_specs=[
            pl.BlockSpec((tm, D_MODEL), lambda i, j: (i, 0)),
            pl.BlockSpec((1, D_MODEL), lambda i, j: (0, 0)),
            pl.BlockSpec((None, None, D_MODEL, FF_BLOCK), lambda i, j: (j, layer, 0, 0)),
            pl.BlockSpec((None, None, D_MODEL, FF_BLOCK), lambda i, j: (j + 2, layer, 0, 0)),
            pl.BlockSpec((None, FF_BLOCK, D_MODEL), lambda i, j: (layer, j, 0)),
        ],
        out_specs=[pl.BlockSpec((tm, D_MODEL), lambda i, j: (i, 0)),
                   pl.BlockSpec((tm, FF_BLOCK), lambda i, j: (i, j)), pl.BlockSpec((tm, FF_BLOCK), lambda i, j: (i, j))],
        out_shape=[jax.ShapeDtypeStruct((s, D_MODEL), F32), jax.ShapeDtypeStruct((s, D_FF), BF16),
                   jax.ShapeDtypeStruct((s, D_FF), BF16)],
        scratch_shapes=[pltpu.VMEM((tm, D_MODEL), BF16), pltpu.VMEM((tm, D_MODEL), F32)],
        compiler_params=_params("parallel", "arbitrary"),
    )(x, gn, wgu4, wgu4, wd)


def ffn_bwd(x, dy, gn, g_pre, u_pre, wgu4, wd, layer, tm=512):
    s = x.shape[0]
    tm = _row_tile(s, tm)
    halves = 2 if tm % 32 == 0 else 1

    def body(x_ref, dy_ref, gn_ref, g_ref, u_ref, wg_ref, wu_ref, wd_ref,
             dx_ref, h_ref, a_ref, dg_ref, du_ref, dyh_ref, dgn_ref, dh_s):
        i, j = pl.program_id(0), pl.program_id(1)

        @pl.when(j == 0)
        def _():
            xhat, _ = _norm_fwd(x_ref[...], D_MODEL)
            h_ref[...] = (xhat * gn_ref[...]).astype(BF16)
            dyh_ref[...] = (0.5 * dy_ref[...]).astype(BF16)
            dh_s[...] = jnp.zeros_like(dh_s)

        for r in range(halves):
            rows = slice(r * (tm // halves), (r + 1) * (tm // halves))
            g = g_ref[rows, :].astype(F32)
            u = u_ref[rows, :].astype(F32)
            sg = jax.nn.sigmoid(g)
            silu = g * sg
            a_ref[rows, :] = (silu * u).astype(BF16)
            da = _dot_nt(dyh_ref[rows, :], wd_ref[...])
            dg = (da * u * (sg * (1.0 + g * (1.0 - sg)))).astype(BF16)
            du = (da * silu).astype(BF16)
            dg_ref[rows, :] = dg
            du_ref[rows, :] = du
            dh_s[rows, :] += _dot_nt(dg, wg_ref[...]) + _dot_nt(du, wu_ref[...])

        @pl.when(j == 1)
        def _():
            xhat, rstd = _norm_fwd(x_ref[...], D_MODEL)
            dxn, dgrow = _norm_bwd(dh_s[...], gn_ref[...], xhat, rstd, D_MODEL)
            dx_ref[...] = dy_ref[...] + dxn
            part = jnp.sum(dgrow, axis=0, keepdims=True)

            @pl.when(i == 0)
            def _():
                dgn_ref[...] = part

            @pl.when(i > 0)
            def _():
                dgn_ref[...] += part

    row = lambda i, j: (i, 0)
    return pl.pallas_call(
        body, name="ffn_bwd",
        grid=(s // tm, 2),
        in_specs=[
            pl.BlockSpec((tm, D_MODEL), row),
            pl.BlockSpec((tm, D_MODEL), row),
            pl.BlockSpec((1, D_MODEL), lambda i, j: (0, 0)),
            pl.BlockSpec((tm, FF_BLOCK), lambda i, j: (i, j)),
            pl.BlockSpec((tm, FF_BLOCK), lambda i, j: (i, j)),
            pl.BlockSpec((None, None, D_MODEL, FF_BLOCK), lambda i, j: (j, layer, 0, 0)),
            pl.BlockSpec((None, None, D_MODEL, FF_BLOCK), lambda i, j: (j + 2, layer, 0, 0)),
            pl.BlockSpec((None, FF_BLOCK, D_MODEL), lambda i, j: (layer, j, 0)),
        ],
        out_specs=[
            pl.BlockSpec((tm, D_MODEL), row),
            pl.BlockSpec((tm, D_MODEL), row),
            pl.BlockSpec((tm, FF_BLOCK), lambda i, j: (i, j)),
            pl.BlockSpec((tm, FF_BLOCK), lambda i, j: (i, j)),
            pl.BlockSpec((tm, FF_BLOCK), lambda i, j: (i, j)),
            pl.BlockSpec((tm, D_MODEL), row),
            pl.BlockSpec((1, D_MODEL), lambda i, j: (0, 0)),
        ],
        out_shape=[
            jax.ShapeDtypeStruct((s, D_MODEL), F32),
            jax.ShapeDtypeStruct((s, D_MODEL), BF16),
            jax.ShapeDtypeStruct((s, D_FF), BF16),
            jax.ShapeDtypeStruct((s, D_FF), BF16),
            jax.ShapeDtypeStruct((s, D_FF), BF16),
            jax.ShapeDtypeStruct((s, D_MODEL), BF16),
            jax.ShapeDtypeStruct((1, D_MODEL), F32),
        ],
        scratch_shapes=[pltpu.VMEM((tm, D_MODEL), F32)],
        compiler_params=_params("arbitrary", "arbitrary", vmem=VMEM_LIMIT_FFN_BWD),
    )(x, dy, gn, g_pre, u_pre, wgu4, wgu4, wd)


def matmul_tn(a, b, tk=2048, tn_max=1536, into=None, at=None):
    s, m = a.shape
    n = b.shape[1]
    tk = _row_tile(s, tk)
    tn = n
    if n > tn_max:
        tn = n // 2
    assert n % tn == 0 and tn % 128 == 0
    nk = s // tk

    def body(a_ref, b_ref, *rest):
        o_ref, acc_s = rest[-2:]
        k = pl.program_id(1)

        @pl.when(k == 0)
        def _():
            acc_s[...] = jnp.zeros_like(acc_s)

        acc_s[...] += _dot_tn(a_ref[...].astype(BF16), b_ref[...].astype(BF16))

        @pl.when(k == nk - 1)
        def _():
            o_ref[...] = acc_s[...]

    in_specs = [pl.BlockSpec((tk, m), lambda j, k: (k, 0)), pl.BlockSpec((tk, tn), lambda j, k: (k, j))]
    if into is None:
        operands, aliases = (a, b), {}
        out_specs = pl.BlockSpec((m, tn), lambda j, k: (0, j))
        out_shape = jax.ShapeDtypeStruct((m, n), F32)
    else:
        assert into.shape[-2:] == (m, tn) and into.dtype == F32
        lead = into.ndim - 2
        operands, aliases = (a, b, into), {2: 0}
        in_specs.append(pl.BlockSpec(memory_space=pl.ANY))
        out_specs = pl.BlockSpec((None,) * lead + (m, tn), lambda j, k: tuple(at(j)) + (0, 0))
        out_shape = jax.ShapeDtypeStruct(into.shape, F32)
    return pl.pallas_call(
        body, name="matmul_tn",
        grid=(n // tn, nk),
        in_specs=in_specs,
        out_specs=out_specs,
        out_shape=out_shape,
        scratch_shapes=[pltpu.VMEM((m, tn), F32)],
        input_output_aliases=aliases,
        compiler_params=_params("parallel", "arbitrary"),
    )(*operands)


def _rope_bwd(dy, c, s1, s2):
    return dy * c + pltpu.roll(dy * s1, 16, 1) + pltpu.roll(dy * s2, HEAD_PAD - 16, 1)


def rope_tables(s):
    pos = jnp.arange(s, dtype=F32)
    inv = ROPE_THETA ** (-jnp.arange(0, QK_ROPE, 2, dtype=F32) / QK_ROPE)
    ang = pos[:, None] * inv[None, :]
    cos, sin = jnp.cos(ang), jnp.sin(ang)
    z16 = jnp.zeros((s, 16), F32)
    c = jnp.concatenate([jnp.ones((s, QK_NOPE), F32), cos, cos, z16, z16], axis=1)
    s1 = jnp.concatenate([jnp.zeros((s, QK_NOPE), F32), -sin, z16, z16, z16], axis=1)
    s2 = jnp.concatenate([jnp.zeros((s, QK_NOPE), F32), z16, sin, z16, z16], axis=1)
    return c, s1, s2


def swap_rope_halves(a):
    lead = a.shape[:-1]
    h = a.reshape(lead + (-1, HEAD_PAD))
    half = QK_ROPE // 2
    zeros = lambda n: jnp.zeros(h.shape[:-1] + (n,), a.dtype)
    sw = jnp.concatenate([zeros(QK_NOPE), h[..., QK_NOPE + half:QK_HEAD], h[..., QK_NOPE:QK_NOPE + half],
                          zeros(HEAD_PAD - QK_HEAD)], axis=-1)
    return sw.reshape(a.shape)


def mix_in_fwd(x, gm, win, qln, kvln, wuq, wuq_sw, wuk, wuv, qn, qn_sw, kn, kn_sw, rc, rs, tm=512):
    s = x.shape[0]
    tm = _row_tile(s, tm)
    scale = QK_HEAD ** -0.5 * LOG2_E
    half = QK_ROPE // 2

    def body(x_ref, gm_ref, win_ref, qln_ref, kvln_ref, wuq_ref, wuqs_ref, wuk_ref, wuv_ref,
             qn_ref, qns_ref, kn_ref, kns_ref, rc_ref, rs_ref, z_ref, q_ref, k_ref, v_ref, vt_ref):
        xhat, _ = _norm_fwd(x_ref[...], D_MODEL)
        h = (xhat * gm_ref[...]).astype(BF16)
        z = _dot(h, win_ref[...])
        z_ref[...] = z
        cq, _ = _norm_fwd(z[:, :Q_LORA], Q_LORA)
        cqn = (cq * qln_ref[...]).astype(BF16)
        ckv, _ = _norm_fwd(z[:, Z_CKV:Z_POOL], KV_LORA)
        ckvn = (ckv * kvln_ref[...]).astype(BF16)
        kpe = z[:, Z_KPE:]
        lane = lax.broadcasted_iota(jnp.int32, kpe.shape, 1)
        first = (lane >= QK_NOPE) & (lane < QK_NOPE + half)
        second = (lane >= QK_NOPE + half) & (lane < QK_HEAD)
        kpe_sw = jnp.where(first, pltpu.roll(kpe, HEAD_PAD - half, 1), jnp.where(second, pltpu.roll(kpe, half, 1), 0.0))
        q_raw = _dot(cqn, wuq_ref[...])
        q_sw = _dot(cqn, wuqs_ref[...])
        k_raw = _dot(ckvn, wuk_ref[...])
        v = _dot(ckvn, wuv_ref[...])
        v_ref[...] = v.astype(BF16)
        vt_ref[...] = jnp.transpose(v).astype(BF16)
        c, sn = rc_ref[...], rs_ref[...]
        tq, tq_sw = qn_ref[...] * c * scale, qns_ref[...] * sn * scale
        tk, tk_sw = kn_ref[...] * c, kns_ref[...] * sn
        for hd in range(HEADS):
            lanes = slice(hd * HEAD_PAD, (hd + 1) * HEAD_PAD)
            qr = q_raw[:, lanes]
            _, q_rstd = _norm_fwd(qr, QK_HEAD)
            q_ref[:, lanes] = (q_rstd * (qr * tq + q_sw[:, lanes] * tq_sw)).astype(BF16)
            kr = k_raw[:, lanes] + kpe
            _, k_rstd = _norm_fwd(kr, QK_HEAD)
            k_ref[:, lanes] = (k_rstd * (kr * tk + kpe_sw * tk_sw)).astype(BF16)

    full = lambda shape: pl.BlockSpec(shape, lambda i: (0,) * len(shape))
    row = lambda w: pl.BlockSpec((tm, w), lambda i: (i, 0))
    return pl.pallas_call(
        body, name="mix_in_fwd",
        grid=(s // tm,),
        in_specs=[row(D_MODEL), full((1, D_MODEL)), full((D_MODEL, Z_W)), full((1, Q_LORA)), full((1, KV_LORA)),
                  full((Q_LORA, ATT_W)), full((Q_LORA, ATT_W)), full((KV_LORA, ATT_W)), full((KV_LORA, ATT_W)),
                  full((1, HEAD_PAD)), full((1, HEAD_PAD)), full((1, HEAD_PAD)), full((1, HEAD_PAD)),
                  row(HEAD_PAD), row(HEAD_PAD)],
        out_specs=[row(Z_W), row(ATT_W), row(ATT_W), row(ATT_W), pl.BlockSpec((ATT_W, tm), lambda i: (0, i))],
        out_shape=[jax.ShapeDtypeStruct((s, Z_W), F32)] + [jax.ShapeDtypeStruct((s, ATT_W), BF16)] * 3 + [
            jax.ShapeDtypeStruct((ATT_W, s), BF16)],
        compiler_params=_params("parallel"),
    )(x, gm, win, qln, kvln, wuq, wuq_sw, wuk, wuv, qn, qn_sw, kn, kn_sw, rc, rs)


def mla_bwd(z, dq_t, dk, dv, dp, qln, kvln, wuq, wuk, wuv, qn, kn, rc, rs1, rs2, tm=512):
    s = z.shape[0]
    tm = _row_tile(s, tm)
    scale = QK_HEAD ** -0.5

    def body(z_ref, dqt_ref, dk_ref, dv_ref, dp_ref, qln_ref, kvln_ref, wuq_ref, wuk_ref, wuv_ref, qn_ref, kn_ref,
             rc_ref, rs1_ref, rs2_ref,
             dz_ref, cqn_ref, ckvn_ref, dqr_ref, dkr_ref, dvb_ref, dqln_ref, dkvln_ref, dqn_ref, dkn_ref):
        i = pl.program_id(0)
        z = z_ref[...]
        cq, cq_rstd = _norm_fwd(z[:, :Q_LORA], Q_LORA)
        cqn = (cq * qln_ref[...]).astype(BF16)
        ckv, ckv_rstd = _norm_fwd(z[:, Z_CKV:Z_POOL], KV_LORA)
        ckvn = (ckv * kvln_ref[...]).astype(BF16)
        kpe = z[:, Z_KPE:]
        cqn_ref[...] = cqn
        ckvn_ref[...] = ckvn
        q_raw = _dot(cqn, wuq_ref[...])
        k_raw = _dot(ckvn, wuk_ref[...])
        c, s1, s2 = rc_ref[...], rs1_ref[...], rs2_ref[...]
        lane = lax.broadcasted_iota(jnp.int32, (tm, HEAD_PAD), 1)
        rope_lanes = (lane >= QK_NOPE) & (lane < QK_HEAD)
        dkpe = jnp.zeros((tm, HEAD_PAD), F32)
        dqn = jnp.zeros((tm, HEAD_PAD), F32)
        dkn = jnp.zeros((tm, HEAD_PAD), F32)
        for hd in range(HEADS):
            lanes = slice(hd * HEAD_PAD, (hd + 1) * HEAD_PAD)
            qh, q_rstd = _norm_fwd(q_raw[:, lanes], QK_HEAD)
            dqh = jnp.transpose(dqt_ref[lanes, :]) * scale
            dqr, dg = _norm_bwd(_rope_bwd(dqh, c, s1, s2), qn_ref[...], qh, q_rstd, QK_HEAD)
            dqn += dg
            dqr_ref[:, lanes] = dqr.astype(BF16)
            kh, k_rstd = _norm_fwd(k_raw[:, lanes] + kpe, QK_HEAD)
            dkr, dg = _norm_bwd(_rope_bwd(dk_ref[:, lanes], c, s1, s2), kn_ref[...], kh, k_rstd, QK_HEAD)
            dkn += dg
            dkr_ref[:, lanes] = dkr.astype(BF16)
            dkpe += jnp.where(rope_lanes, dkr, 0.0)
        dvb = dv_ref[...].astype(BF16)
        dvb_ref[...] = dvb
        dcqn = _dot_nt(dqr_ref[...], wuq_ref[...])
        dckvn = _dot_nt(dkr_ref[...], wuk_ref[...]) + _dot_nt(dvb, wuv_ref[...])
        dcq, dqln = _norm_bwd(dcqn, qln_ref[...], cq, cq_rstd, Q_LORA)
        dckv, dkvln = _norm_bwd(dckvn, kvln_ref[...], ckv, ckv_rstd, KV_LORA)
        dz_ref[:, :Q_LORA] = dcq.astype(BF16)
        dz_ref[:, Z_CKV:Z_POOL] = dckv.astype(BF16)
        dz_ref[:, Z_POOL:Z_KPE] = dp_ref[...].astype(BF16)
        dz_ref[:, Z_KPE:] = dkpe.astype(BF16)
        parts = [(dqln_ref, dqln), (dkvln_ref, dkvln), (dqn_ref, dqn), (dkn_ref, dkn)]

        @pl.when(i == 0)
        def _():
            for ref, val in parts:
                ref[...] = jnp.sum(val, axis=0, keepdims=True)

        @pl.when(i > 0)
        def _():
            for ref, val in parts:
                ref[...] += jnp.sum(val, axis=0, keepdims=True)

    full = lambda shape: pl.BlockSpec(shape, lambda i: (0,) * len(shape))
    row = lambda w: pl.BlockSpec((tm, w), lambda i: (i, 0))
    return pl.pallas_call(
        body, name="mla_bwd",
        grid=(s // tm,),
        in_specs=[row(Z_W), pl.BlockSpec((ATT_W, tm), lambda i: (0, i)), row(ATT_W), row(ATT_W), row(POOL_W),
                  full((1, Q_LORA)), full((1, KV_LORA)), full((Q_LORA, ATT_W)), full((KV_LORA, ATT_W)),
                  full((KV_LORA, ATT_W)), full((1, HEAD_PAD)), full((1, HEAD_PAD)),
                  row(HEAD_PAD), row(HEAD_PAD), row(HEAD_PAD)],
        out_specs=[row(Z_W), row(Q_LORA), row(KV_LORA), row(ATT_W), row(ATT_W), row(ATT_W),
                   full((1, Q_LORA)), full((1, KV_LORA)), full((1, HEAD_PAD)), full((1, HEAD_PAD))],
        out_shape=[jax.ShapeDtypeStruct((s, Z_W), BF16), jax.ShapeDtypeStruct((s, Q_LORA), BF16),
                   jax.ShapeDtypeStruct((s, KV_LORA), BF16)] + [jax.ShapeDtypeStruct((s, ATT_W), BF16)] * 3 + [
                   jax.ShapeDtypeStruct((1, Q_LORA), F32), jax.ShapeDtypeStruct((1, KV_LORA), F32),
                   jax.ShapeDtypeStruct((1, HEAD_PAD), F32), jax.ShapeDtypeStruct((1, HEAD_PAD), F32)],
        compiler_params=_params("arbitrary"),
    )(z, dq_t, dk, dv, dp, qln, kvln, wuq, wuk, wuv, qn, kn, rc, rs1, rs2)


def mix_in_bwd(x, dx_in, dz, gm, win, tm=512):
    s = x.shape[0]
    tm = _row_tile(s, tm)

    def body(x_ref, dxin_ref, dz_ref, gm_ref, win_ref, dx_ref, h_ref, dgm_ref):
        i = pl.program_id(0)
        xhat, rstd = _norm_fwd(x_ref[...], D_MODEL)
        h_ref[...] = (xhat * gm_ref[...]).astype(BF16)
        dh = _dot_nt(dz_ref[...], win_ref[...])
        dxn, dgrow = _norm_bwd(dh, gm_ref[...], xhat, rstd, D_MODEL)
        dx_ref[...] = dxin_ref[...] + dxn
        part = jnp.sum(dgrow, axis=0, keepdims=True)

        @pl.when(i == 0)
        def _():
            dgm_ref[...] = part

        @pl.when(i > 0)
        def _():
            dgm_ref[...] += part

    full = lambda shape: pl.BlockSpec(shape, lambda i: (0,) * len(shape))
    row = lambda w: pl.BlockSpec((tm, w), lambda i: (i, 0))
    return pl.pallas_call(
        body, name="mix_in_bwd",
        grid=(s // tm,),
        in_specs=[row(D_MODEL), row(D_MODEL), row(Z_W), full((1, D_MODEL)), full((D_MODEL, Z_W))],
        out_specs=[row(D_MODEL), row(D_MODEL), full((1, D_MODEL))],
        out_shape=[jax.ShapeDtypeStruct((s, D_MODEL), F32), jax.ShapeDtypeStruct((s, D_MODEL), BF16),
                   jax.ShapeDtypeStruct((1, D_MODEL), F32)],
        compiler_params=_params("arbitrary"),
    )(x, dx_in, dz, gm, win)


ATT_CHUNK_FWD = 1024
ATT_CHUNK_BWD = 512


def flash_fwd(q, k, vt, tq=1024, tk=16384, chunk=ATT_CHUNK_FWD):
    s = q.shape[0]
    tq, tk = _row_tile(s, tq), _row_tile(s, tk)
    nk = s // tk
    tc = _row_tile(tk, chunk)
    sub = tk // tc

    def body(q_ref, k_ref, vt_ref, o_ref, lse_ref, m_s, l_s, acc_s):
        kk = pl.program_id(2)

        @pl.when(kk == 0)
        def _():
            m_s[...] = jnp.full_like(m_s, -jnp.inf)
            l_s[...] = jnp.zeros_like(l_s)
            acc_s[...] = jnp.zeros_like(acc_s)

        q_t = q_ref[...]

        def scores(c):
            return _dot_nt(k_ref[c * tc:(c + 1) * tc, :], q_t)

        m, l, acc = m_s[...], l_s[...], acc_s[...]
        s_next = scores(0)
        for c in range(sub):
            s_t = s_next
            if c + 1 < sub:
                s_next = scores(c + 1)
            m_new = jnp.maximum(m, jnp.max(s_t, axis=0, keepdims=True))
            alpha = jnp.exp2(m - m_new)
            p_t = jnp.exp2(s_t - m_new)
            l = alpha * l + jnp.sum(p_t, axis=0, keepdims=True)
            acc = alpha * acc + _dot(vt_ref[:, c * tc:(c + 1) * tc], p_t.astype(BF16))
            m = m_new
        m_s[...], l_s[...], acc_s[...] = m, l, acc

        @pl.when(kk == nk - 1)
        def _():
            o_ref[...] = jnp.transpose(acc_s[...] / l_s[...]).astype(BF16)
            lse_ref[...] = m_s[...] + jnp.log2(l_s[...])

    return pl.pallas_call(
        body, name="flash_fwd",
        grid=(HEADS, s // tq, nk),
        in_specs=[pl.BlockSpec((tq, HEAD_PAD), lambda h, i, kk: (i, h)),
                  pl.BlockSpec((tk, HEAD_PAD), lambda h, i, kk: (kk, h)),
                  pl.BlockSpec((HEAD_PAD, tk), lambda h, i, kk: (h, kk))],
        out_specs=[pl.BlockSpec((tq, HEAD_PAD), lambda h, i, kk: (i, h)),
                   pl.BlockSpec((None, 1, tq), lambda h, i, kk: (h, 0, i))],
        out_shape=[jax.ShapeDtypeStruct((s, ATT_W), BF16), jax.ShapeDtypeStruct((HEADS, 1, s), F32)],
        scratch_shapes=[pltpu.VMEM((1, tq), F32), pltpu.VMEM((1, tq), F32), pltpu.VMEM((HEAD_PAD, tq), F32)],
        compiler_params=_params("parallel", "parallel", "arbitrary"),
    )(q, k, vt)


def flash_bwd(q, k, v, do, lse_row, delta_row, tq=16384, tk=1024, chunk=ATT_CHUNK_BWD):
    s = q.shape[0]
    tq, tk = _row_tile(s, tq), _row_tile(s, tk)
    nq = s // tq
    tc = _row_tile(tq, chunk)
    sub = tq // tc

    def body(q_ref, k_ref, v_ref, do_ref, lse_ref, dl_ref, dqt_ref, dk_ref, dv_ref, kt_s, dk_s, dv_s):
        kb, qb = pl.program_id(1), pl.program_id(2)

        @pl.when(qb == 0)
        def _():
            kt_s[...] = jnp.transpose(k_ref[...].astype(F32)).astype(BF16)
            dk_s[...] = jnp.zeros_like(dk_s)
            dv_s[...] = jnp.zeros_like(dv_s)

        k_t, v_t = k_ref[...], v_ref[...]

        def products(c):
            rows = slice(c * tc, (c + 1) * tc)
            return _dot_nt(k_t, q_ref[rows, :]), _dot_nt(v_t, do_ref[rows, :])

        dk, dv = dk_s[...], dv_s[...]
        nxt = products(0)
        contribs = []
        for c in range(sub):
            rows = slice(c * tc, (c + 1) * tc)
            s_t, dp_t = nxt
            if c + 1 < sub:
                nxt = products(c + 1)
            p_t = jnp.exp2(s_t - lse_ref[:, rows])
            ds_t = (p_t * (dp_t - dl_ref[:, rows])).astype(BF16)
            dv = dv + _dot(p_t.astype(BF16), do_ref[rows, :])
            dk = dk + _dot(ds_t, q_ref[rows, :])
            contribs.append(_dot(kt_s[...], ds_t))
        dk_s[...], dv_s[...] = dk, dv
        contrib = jnp.concatenate(contribs, axis=1) if sub > 1 else contribs[0]
        cols = pl.ds(pl.multiple_of(qb * tq, tq), tq)

        @pl.when(kb == 0)
        def _():
            dqt_ref[:, cols] = contrib

        @pl.when(kb > 0)
        def _():
            dqt_ref[:, cols] += contrib

        @pl.when(qb == nq - 1)
        def _():
            dk_ref[...] = dk_s[...] * LN_2
            dv_ref[...] = dv_s[...]

    qspec = pl.BlockSpec((tq, HEAD_PAD), lambda h, kb, qb: (qb, h))
    kspec = pl.BlockSpec((tk, HEAD_PAD), lambda h, kb, qb: (kb, h))
    rowspec = pl.BlockSpec((None, 1, tq), lambda h, kb, qb: (h, 0, qb))
    return pl.pallas_call(
        body, name="flash_bwd",
        grid=(HEADS, s // tk, nq),
        in_specs=[qspec, kspec, kspec, qspec, rowspec, rowspec],
        out_specs=[pl.BlockSpec((HEAD_PAD, s), lambda h, kb, qb: (h, 0)), kspec, kspec],
        out_shape=[jax.ShapeDtypeStruct((ATT_W, s), F32), jax.ShapeDtypeStruct((s, ATT_W), F32),
                   jax.ShapeDtypeStruct((s, ATT_W), F32)],
        scratch_shapes=[pltpu.VMEM((HEAD_PAD, tk), BF16), pltpu.VMEM((tk, HEAD_PAD), F32),
                        pltpu.VMEM((tk, HEAD_PAD), F32)],
        compiler_params=_params("parallel", "arbitrary", "arbitrary"),
    )(q, k, v, do, lse_row, delta_row)


def _halo_specs(tm, w, n_tiles):
    per = tm // POOL_HALO
    last = n_tiles * per - 1
    return [pl.BlockSpec((POOL_HALO, w), lambda i: (jnp.maximum(i * per - 1, 0), 0)),
            pl.BlockSpec((tm, w), lambda i: (i, 0)),
            pl.BlockSpec((POOL_HALO, w), lambda i: (jnp.minimum((i + 1) * per, last), 0))]


def _fill_ext(ext_ref, prev, cur, nxt, i, n_tiles, tm):
    ext_ref[pl.ds(0, POOL_HALO), :] = jnp.where(i > 0, prev, 0.0)
    ext_ref[pl.ds(POOL_HALO, tm), :] = cur
    ext_ref[pl.ds(POOL_HALO + tm, POOL_HALO), :] = jnp.where(i < n_tiles - 1, nxt, 0.0)


def _window_sum(ext_ref, g, lo, hi, tm):
    lanes = pl.ds(g * GROUP, GROUP)
    acc = ext_ref[pl.ds(POOL_HALO + lo, tm), lanes]
    for d in range(lo + 1, hi + 1):
        acc = acc + ext_ref[pl.ds(POOL_HALO + d, tm), lanes]
    return acc


def _pool_mixed(ext_ref, cur, i, s, tm):
    row = i * tm + lax.broadcasted_iota(jnp.int32, (tm, 1), 0)
    out = []
    for g, w in enumerate(POOL_WINDOWS):
        left = w // 2
        right = w - 1 - left
        cnt = (jnp.minimum(row + right + 1, s) - jnp.maximum(row - left, 0)).astype(F32)
        out.append(_window_sum(ext_ref, g, -left, right, tm) / cnt - cur[:, g * GROUP:(g + 1) * GROUP])
    return out


def pool_out_fwd(x, z, o, wpool, pscale, wout, tm=512):
    s = x.shape[0]
    tm = _row_tile(s, tm)
    n_tiles = s // tm

    def body(x_ref, zp_ref, z_ref, zn_ref, o_ref, wp_ref, ps_ref, wo_ref, y_ref, cat_ref, ext_s):
        i = pl.program_id(0)
        cur = z_ref[:, Z_POOL:Z_KPE]
        _fill_ext(ext_s, zp_ref[:, Z_POOL:Z_KPE], cur, zn_ref[:, Z_POOL:Z_KPE], i, n_tiles, tm)
        mixed = _pool_mixed(ext_s, cur, i, s, tm)
        cat_ref[:, :ATT_W] = o_ref[...]
        for g in range(len(POOL_WINDOWS)):
            lanes = slice(g * GROUP, (g + 1) * GROUP)
            yg = _dot(mixed[g].astype(BF16), wp_ref[g])
            cat_ref[:, ATT_W + g * GROUP:ATT_W + (g + 1) * GROUP] = (yg * ps_ref[:, lanes]).astype(BF16)
        y_ref[...] = x_ref[...] + _dot(cat_ref[...], wo_ref[...])

    full = lambda shape: pl.BlockSpec(shape, lambda i: (0,) * len(shape))
    row = lambda w: pl.BlockSpec((tm, w), lambda i: (i, 0))
    return pl.pallas_call(
        body, name="pool_out_fwd",
        grid=(n_tiles,),
        in_specs=[row(D_MODEL)] + _halo_specs(tm, Z_W, n_tiles) + [
            row(ATT_W), full((len(POOL_WINDOWS), GROUP, GROUP)), full((1, POOL_W)), full((CAT_W, D_MODEL))],
        out_specs=[row(D_MODEL), row(CAT_W)],
        out_shape=[jax.ShapeDtypeStruct((s, D_MODEL), F32), jax.ShapeDtypeStruct((s, CAT_W), BF16)],
        scratch_shapes=[pltpu.VMEM((tm + 2 * POOL_HALO, POOL_W), F32)],
        compiler_params=_params("parallel"),
    )(x, z, z, z, o, wpool, pscale, wout)


def out_bwd(dx, wout, cat, tm=512):
    s = dx.shape[0]
    tm = _row_tile(s, tm)

    def body(dx_ref, wo_ref, cat_ref, do_ref, db_ref, dl_ref):
        dcat = _dot_nt(dx_ref[...].astype(BF16), wo_ref[...])
        do_ref[...] = dcat[:, :ATT_W].astype(BF16)
        db_ref[...] = dcat[:, ATT_W:]
        for hd in range(HEADS):
            lanes = slice(hd * HEAD_PAD, (hd + 1) * HEAD_PAD)
            dl_ref[hd] = jnp.sum(dcat[:, lanes] * cat_ref[:, lanes].astype(F32), axis=-1, keepdims=True)

    row = lambda w: pl.BlockSpec((tm, w), lambda i: (i, 0))
    return pl.pallas_call(
        body, name="out_bwd",
        grid=(s // tm,),
        in_specs=[row(D_MODEL), pl.BlockSpec((CAT_W, D_MODEL), lambda i: (0, 0)), row(CAT_W)],
        out_specs=[row(ATT_W), row(POOL_W), pl.BlockSpec((HEADS, tm, 1), lambda i: (0, i, 0))],
        out_shape=[jax.ShapeDtypeStruct((s, ATT_W), BF16), jax.ShapeDtypeStruct((s, POOL_W), F32),
                   jax.ShapeDtypeStruct((HEADS, s, 1), F32)],
        compiler_params=_params("parallel"),
    )(dx, wout, cat)


def pool_bwd(db, z, wpool, pscale, tm=512):
    s = db.shape[0]
    tm = _row_tile(s, tm)
    n_tiles = s // tm
    te = tm + 2 * POOL_HALO

    def body(dbp_ref, db_ref, dbn_ref, zp_ref, z_ref, zn_ref, wp_ref, ps_ref,
             dp_ref, mixed_ref, dys_ref, dps_ref, ext_s, text_s):
        i = pl.program_id(0)
        cur = z_ref[:, Z_POOL:Z_KPE]
        _fill_ext(ext_s, zp_ref[:, Z_POOL:Z_KPE], cur, zn_ref[:, Z_POOL:Z_KPE], i, n_tiles, tm)
        mixed = _pool_mixed(ext_s, cur, i, s, tm)
        _fill_ext(ext_s, dbp_ref[...], db_ref[...], dbn_ref[...], i, n_tiles, tm)
        erow = i * tm - POOL_HALO + lax.broadcasted_iota(jnp.int32, (te, 1), 0)
        dps = []
        for g, w in enumerate(POOL_WINDOWS):
            lanes = slice(g * GROUP, (g + 1) * GROUP)
            left = w // 2
            right = w - 1 - left
            mb = mixed[g].astype(BF16)
            mixed_ref[:, lanes] = mb
            dps.append(jnp.sum(db_ref[:, lanes] * _dot(mb, wp_ref[g]), axis=0, keepdims=True))
            dys_e = (ext_s[:, lanes] * ps_ref[:, lanes]).astype(BF16)
            dys_ref[:, lanes] = (db_ref[:, lanes] * ps_ref[:, lanes]).astype(BF16)
            dmix_e = _dot_nt(dys_e, wp_ref[g])
            cnt = jnp.minimum(erow + right + 1, s) - jnp.maximum(erow - left, 0)
            text_s[:, lanes] = dmix_e / jnp.maximum(cnt, 1).astype(F32)
            dp_ref[:, lanes] = _window_sum(text_s, g, -right, left, tm) - dmix_e[POOL_HALO:POOL_HALO + tm]
        part = jnp.concatenate(dps, axis=1)

        @pl.when(i == 0)
        def _():
            dps_ref[...] = part

        @pl.when(i > 0)
        def _():
            dps_ref[...] += part

    full = lambda shape: pl.BlockSpec(shape, lambda i: (0,) * len(shape))
    row = lambda w: pl.BlockSpec((tm, w), lambda i: (i, 0))
    return pl.pallas_call(
        body, name="pool_bwd",
        grid=(n_tiles,),
        in_specs=_halo_specs(tm, POOL_W, n_tiles) + _halo_specs(tm, Z_W, n_tiles) + [
            full((len(POOL_WINDOWS), GROUP, GROUP)), full((1, POOL_W))],
        out_specs=[row(POOL_W), row(POOL_W), row(POOL_W), full((1, POOL_W))],
        out_shape=[jax.ShapeDtypeStruct((s, POOL_W), F32), jax.ShapeDtypeStruct((s, POOL_W), BF16),
                   jax.ShapeDtypeStruct((s, POOL_W), BF16), jax.ShapeDtypeStruct((1, POOL_W), F32)],
        scratch_shapes=[pltpu.VMEM((te, POOL_W), F32), pltpu.VMEM((te, POOL_W), F32)],
        compiler_params=_params("arbitrary"),
    )(db, db, db, z, z, z, wpool, pscale)


def loss_head(y, target, tm=512):
    s = y.shape[0]
    tm = _row_tile(s, tm)

    def body(y_ref, t_ref, dy_ref, loss_ref):
        i = pl.program_id(0)
        err = y_ref[...] - t_ref[...]
        dy_ref[...] = err * (1.0 / D_MODEL)
        part = 0.5 * jnp.sum(jnp.sum(err * err, axis=-1, keepdims=True) * (1.0 / D_MODEL), axis=0, keepdims=True)
        part = jnp.broadcast_to(part, (1, 128))

        @pl.when(i == 0)
        def _():
            loss_ref[...] = part

        @pl.when(i > 0)
        def _():
            loss_ref[...] += part

    row = pl.BlockSpec((tm, D_MODEL), lambda i: (i, 0))
    return pl.pallas_call(
        body, name="loss_head",
        grid=(s // tm,),
        in_specs=[row, row],
        out_specs=[row, pl.BlockSpec((1, 128), lambda i: (0, 0))],
        out_shape=[jax.ShapeDtypeStruct((s, D_MODEL), F32), jax.ShapeDtypeStruct((1, 128), F32)],
        compiler_params=_params("arbitrary"),
    )(y, target)


def _pad_heads(w, real):
    lead = w.shape[:-1]
    w = w.reshape(lead + (HEADS, real))
    w = jnp.pad(w, [(0, 0)] * len(lead) + [(0, 0), (0, HEAD_PAD - real)])
    return w.reshape(lead + (ATT_W,))


def _unpad_heads(w, real):
    lead = w.shape[:-1]
    return w.reshape(lead + (HEADS, HEAD_PAD))[..., :real].reshape(lead + (HEADS * real,))


def layout_weights(full):
    w = {}
    depth = full["w_in"].shape[0]
    win = full["w_in"]
    zc = lambda n: jnp.zeros((depth, D_MODEL, n), win.dtype)
    o_pe = Q_LORA + KV_LORA
    w["win"] = jnp.concatenate([win[..., :o_pe], win[..., o_pe + QK_ROPE:], zc(QK_NOPE), win[..., o_pe:o_pe + QK_ROPE],
                                zc(HEAD_PAD - QK_HEAD)], axis=-1)
    w["wuq"] = _pad_heads(full["w_uq"], QK_HEAD)
    w["wuk"] = _pad_heads(full["w_uk"], QK_NOPE)
    w["wuv"] = _pad_heads(full["w_uv"], V_HEAD)
    wo = full["w_out"]
    wo_att = jnp.swapaxes(_pad_heads(jnp.swapaxes(wo[:, :HEADS * V_HEAD], 1, 2), V_HEAD), 1, 2)
    w["wout"] = jnp.concatenate([wo_att, wo[:, HEADS * V_HEAD:]], axis=1)
    pad_vec = lambda v: jnp.pad(v, ((0, 0), (0, HEAD_PAD - QK_HEAD)))
    w["qn"], w["kn"] = pad_vec(full["q_norm"]), pad_vec(full["k_norm"])
    for n in ("wuq", "qn", "kn"):
        w[n + "_sw"] = swap_rope_halves(w[n])
    for n in ("ffn1_norm", "mix_norm", "ffn2_norm", "q_lat_norm", "kv_lat_norm", "pool_scale", "w_pool"):
        w[n] = full[n]
    return w


def local_step(x, target, w, wgu1, wd1, wgu2, wd2):
    s = x.shape[0]
    depth = w["win"].shape[0]
    rc, rs1, rs2 = rope_tables(s)
    rs_both = rs1 + rs2
    vec = lambda name, l: w[name][l][None, :]
    saved = []
    for l in range(depth):
        x1, g1, u1 = ffn_fwd(x, vec("ffn1_norm", l), wgu1, wd1, l)
        z, q, k, v, vt = mix_in_fwd(x1, vec("mix_norm", l), w["win"][l], vec("q_lat_norm", l), vec("kv_lat_norm", l),
                                    w["wuq"][l], w["wuq_sw"][l], w["wuk"][l], w["wuv"][l], vec("qn", l),
                                    vec("qn_sw", l), vec("kn", l), vec("kn_sw", l), rc, rs_both)
        o, lse = flash_fwd(q, k, vt)
        x2, cat = pool_out_fwd(x1, z, o, w["w_pool"][l], vec("pool_scale", l), w["wout"][l])
        x3, g2, u2 = ffn_fwd(x2, vec("ffn2_norm", l), wgu2, wd2, l)
        saved.append((x, x1, z, q, k, v, lse, cat, x2, g1, u1, g2, u2))
        x = x3
    dy, loss = loss_head(x, target)

    names = ["ffn1_norm", "ffn1_w_gu", "ffn1_w_down", "mix_norm", "w_in", "q_lat_norm", "kv_lat_norm", "w_uq", "w_uk",
             "w_uv", "q_norm", "k_norm", "w_pool", "pool_scale", "w_out", "ffn2_norm", "ffn2_w_gu", "ffn2_w_down"]
    grads = {n: [None] * depth for n in names if "ffn" not in n or "norm" in n}
    for n in ("ffn1", "ffn2"):
        grads[n + "_w_gu"] = lax.empty((depth, N_CHIPS, D_MODEL, FF_BLOCK), F32)
        grads[n + "_w_down"] = lax.empty((depth, D_FF, D_MODEL), F32)

    def ffn_weight_grads(n, l, h, act, dg, du, dyh):
        gu = matmul_tn(h, dg, into=grads[n + "_w_gu"], at=lambda j: (l, j))
        grads[n + "_w_gu"] = matmul_tn(h, du, into=gu, at=lambda j: (l, j + 2))
        grads[n + "_w_down"] = matmul_tn(act, dyh, tk=1024, into=grads[n + "_w_down"], at=lambda j: (l,))

    for l in reversed(range(depth)):
        x0, x1, z, q, k, v, lse, cat, x2, g1, u1, g2, u2 = saved[l]
        dx2, h, act, dg, du, dyh, dgn = ffn_bwd(x2, dy, vec("ffn2_norm", l), g2, u2, wgu2, wd2, l)
        grads["ffn2_norm"][l] = dgn[0]
        ffn_weight_grads("ffn2", l, h, act, dg, du, dyh)

        do, db, delta = out_bwd(dx2, w["wout"][l], cat)
        dwout = matmul_tn(cat, dx2)
        grads["w_out"][l] = jnp.concatenate(
            [jnp.swapaxes(_unpad_heads(jnp.swapaxes(dwout[:ATT_W], 0, 1), V_HEAD), 0, 1), dwout[ATT_W:]], axis=0)
        dp, mixed, dys, dps = pool_bwd(db, z, w["w_pool"][l], vec("pool_scale", l))
        grads["pool_scale"][l] = dps[0]
        dwp = matmul_tn(mixed, dys)
        grads["w_pool"][l] = jnp.stack([dwp[g * GROUP:(g + 1) * GROUP, g * GROUP:(g + 1) * GROUP]
                                        for g in range(len(POOL_WINDOWS))])
        dq_t, dk, dv = flash_bwd(q, k, v, do, lse, delta.reshape(HEADS, 1, s))
        dz, cqn, ckvn, dqr, dkr, dvb, dqln, dkvln, dqn, dkn = mla_bwd(
            z, dq_t, dk, dv, dp, vec("q_lat_norm", l), vec("kv_lat_norm", l), w["wuq"][l], w["wuk"][l], w["wuv"][l],
            vec("qn", l), vec("kn", l), rc, rs1, rs2)
        grads["q_lat_norm"][l], grads["kv_lat_norm"][l] = dqln[0], dkvln[0]
        grads["q_norm"][l], grads["k_norm"][l] = dqn[0, :QK_HEAD], dkn[0, :QK_HEAD]
        grads["w_uq"][l] = _unpad_heads(matmul_tn(cqn, dqr), QK_HEAD)
        grads["w_uk"][l] = _unpad_heads(matmul_tn(ckvn, dkr), QK_NOPE)
        grads["w_uv"][l] = _unpad_heads(matmul_tn(ckvn, dvb), V_HEAD)
        dx1, h2, dgm = mix_in_bwd(x1, dx2, dz, vec("mix_norm", l), w["win"][l])
        grads["mix_norm"][l] = dgm[0]
        dwin = matmul_tn(h2, dz)
        grads["w_in"][l] = jnp.concatenate(
            [dwin[:, :Z_POOL], dwin[:, Z_KPE + QK_NOPE:Z_KPE + QK_HEAD], dwin[:, Z_POOL:Z_KPE]], axis=1)

        dy, h, act, dg, du, dyh, dgn = ffn_bwd(x0, dx1, vec("ffn1_norm", l), g1, u1, wgu1, wd1, l)
        grads["ffn1_norm"][l] = dgn[0]
        ffn_weight_grads("ffn1", l, h, act, dg, du, dyh)
    return loss, dy, {n: (jnp.stack(g) if isinstance(g, list) else g) for n, g in grads.items()}


HBM_SPEC = pl.BlockSpec(memory_space=pltpu.HBM)


def _place():
    x, y, c = lax.axis_index("x"), lax.axis_index("y"), lax.axis_index("c")
    other_chips = [(1 - x, y), (x, 1 - y), (1 - x, 1 - y)]
    return x, y, c, other_chips


def _exchange(name, arrays, out_shapes, plan, n_local, n_remote):
    n = len(arrays)

    def body(*refs):
        ins, outs = refs[:n], refs[n:n + len(out_shapes)]
        send_sems, recv_sems, local_sems = refs[n + len(out_shapes):]
        local, sends, recvs, forwards = plan(ins, outs, _place())

        def remote(k, src, dst, dev):
            return pltpu.make_async_remote_copy(src_ref=src, dst_ref=dst, send_sem=send_sems.at[k],
                                                recv_sem=recv_sems.at[k], device_id=dev, device_id_type=MESH)

        started = []
        for k, (src, dst) in enumerate(local):
            cp = pltpu.make_async_copy(src, dst, local_sems.at[k])
            cp.start()
            started.append(cp)
        out_going = []
        for k, (src, dst, dev) in enumerate(sends):
            cp = remote(k, src, dst, dev)
            cp.start()
            out_going.append(cp)
        for k, (dst, dev) in enumerate(recvs):
            remote(k, dst, dst, dev).wait_recv()
            if forwards:
                src, fdst, fdev, _ = forwards[k]
                cp = remote(len(sends) + k, src, fdst, fdev)
                cp.start()
                out_going.append(cp)
        for k, (_, _, fdev, landing) in enumerate(forwards):
            remote(len(sends) + k, landing, landing, fdev).wait_recv()
        for cp in out_going:
            cp.wait_send()
        for cp in started:
            cp.wait()

    return pl.pallas_call(
        body, name=name,
        in_specs=[HBM_SPEC] * n,
        out_specs=[HBM_SPEC] * len(out_shapes),
        out_shape=out_shapes,
        scratch_shapes=[pltpu.SemaphoreType.DMA((n_remote,)), pltpu.SemaphoreType.DMA((n_remote,)),
                        pltpu.SemaphoreType.DMA((max(n_local, 1),))],
    )(*arrays)


def _layer_half(depth, which):
    assert depth % 2 == 0
    return pl.ds(which * (depth // 2), depth // 2)


def gather_chips(shards):
    n = len(shards)
    depth = shards[0].shape[0]

    def plan(ins, outs, place):
        x, y, c, chips = place
        me, sib = 2 * x + y, (x, y, 1 - c)
        mine, theirs = _layer_half(depth, c), _layer_half(depth, 1 - c)
        sends = [(ins[p].at[mine], outs[p].at[me, mine], (px, py, c)) for p in range(n) for (px, py) in chips]
        recvs = [(outs[p].at[2 * px + py, mine], (px, py, c)) for p in range(n) for (px, py) in chips]
        forwards = [(outs[p].at[2 * px + py, mine], outs[p].at[2 * px + py, mine], sib, outs[p].at[2 * px + py, theirs])
                    for p in range(n) for (px, py) in chips]
        return [], sends, recvs, forwards

    out_shapes = [jax.ShapeDtypeStruct((N_CHIPS,) + a.shape, a.dtype) for a in shards]
    return _exchange("gather_chips", shards, out_shapes, plan, 0, 6 * n)


def swap_halves(fulls):
    n = len(fulls)
    depth = fulls[0].shape[0]

    def plan(ins, outs, place):
        x, y, c, _ = place
        sib = (x, y, 1 - c)
        sends = [(ins[p].at[_layer_half(depth, 1 - c)], outs[p], sib) for p in range(n)]
        return [], sends, [(outs[p], sib) for p in range(n)], []

    out_shapes = [jax.ShapeDtypeStruct((depth // 2,) + a.shape[1:], a.dtype) for a in fulls]
    return _exchange("swap_halves", fulls, out_shapes, plan, 0, n)


def scatter_chips(fulls):
    n = len(fulls)

    def plan(ins, outs, place):
        x, y, c, chips = place
        sends = [(ins[p].at[:, 2 * px + py], outs[p].at[j], (px, py, c))
                 for p in range(n) for j, (px, py) in enumerate(chips)]
        recvs = [(outs[p].at[j], (px, py, c)) for p in range(n) for j, (px, py) in enumerate(chips)]
        return [], sends, recvs, []

    out_shapes = [jax.ShapeDtypeStruct((3, a.shape[0]) + a.shape[2:], a.dtype) for a in fulls]
    return _exchange("scatter_chips", fulls, out_shapes, plan, 0, 3 * n)


def swap_sibling(arrays):
    n = len(arrays)

    def plan(ins, outs, place):
        x, y, c, _ = place
        sib = (x, y, 1 - c)
        return [], [(ins[p], outs[p], sib) for p in range(n)], [(outs[p], sib) for p in range(n)], []

    return _exchange("swap_sibling", arrays, [jax.ShapeDtypeStruct(a.shape, a.dtype) for a in arrays], plan, 0, n)


def gather_all(vec):
    def plan(ins, outs, place):
        x, y, c, _ = place
        me = 4 * x + 2 * y + c
        flips = [(fx, fy, fc) for fx in (0, 1) for fy in (0, 1) for fc in (0, 1)][1:]
        peers = [(1 - x if fx else x, 1 - y if fy else y, 1 - c if fc else c) for fx, fy, fc in flips]
        local = [(ins[0], outs[0].at[me])]
        sends = [(ins[0], outs[0].at[me], dev) for dev in peers]
        recvs = [(outs[0].at[4 * dev[0] + 2 * dev[1] + dev[2]], dev) for dev in peers]
        return local, sends, recvs, []

    return _exchange("gather_all", [vec], [jax.ShapeDtypeStruct((8,) + vec.shape, vec.dtype)], plan, 1, 7)[0]


def _pick_rows(rows, cols, block_bytes=1024 * 1024):
    best = None
    for t in range(16, rows + 1, 16):
        if rows % t == 0 and t * cols * 4 <= block_bytes:
            best = t
    return best or rows


def sum_rows_of(own, stacked, tr, place, own_block, with_bf16=False):
    n, rows, cols = stacked.shape
    assert rows % tr == 0

    def body(place_ref, own_ref, st_ref, o_ref, *narrow):
        acc = own_ref[...]
        for k in range(n):
            acc = acc + st_ref[k].astype(F32)
        o_ref[...] = acc
        if with_bf16:
            narrow[0][...] = acc.astype(BF16)

    spec = pl.BlockSpec((tr, cols), lambda i, p: (i, 0))
    return pl.pallas_call(
        body, name="sum_rows_of",
        grid_spec=pltpu.PrefetchScalarGridSpec(
            num_scalar_prefetch=1, grid=(rows // tr,),
            in_specs=[pl.BlockSpec((tr, cols), lambda i, p: (own_block(i, p[0]), 0)),
                      pl.BlockSpec((n, tr, cols), lambda i, p: (0, i, 0))],
            out_specs=[spec, spec] if with_bf16 else spec),
        out_shape=([jax.ShapeDtypeStruct((rows, cols), F32), jax.ShapeDtypeStruct((rows, cols), BF16)]
                   if with_bf16 else jax.ShapeDtypeStruct((rows, cols), F32)),
        compiler_params=_params("parallel"),
    )(place.astype(jnp.int32).reshape(1), own, stacked)


def sum_stack(stacked):
    n, rows, cols = stacked.shape
    tr = _pick_rows(rows, cols)

    def body(st_ref, o_ref):
        acc = st_ref[0]
        for k in range(1, n):
            acc = acc + st_ref[k]
        o_ref[...] = acc

    return pl.pallas_call(
        body, name="sum_stack",
        grid=(rows // tr,),
        in_specs=[pl.BlockSpec((n, tr, cols), lambda i: (0, i, 0))],
        out_specs=pl.BlockSpec((tr, cols), lambda i: (i, 0)),
        out_shape=jax.ShapeDtypeStruct((rows, cols), F32),
        compiler_params=_params("parallel"),
    )(stacked)


def adamw(w, g_mine, m, v, g_theirs=None, core=None):
    rows, cols = w.shape
    tr = _pick_rows(g_mine.shape[0], cols)
    per_half = g_mine.shape[0] // tr
    c1 = 1.0 - ADAM_B1 ** ADAM_STEP
    c2 = 1.0 - ADAM_B2 ** ADAM_STEP
    halves = g_theirs is not None
    ins = [w, g_mine] + ([g_theirs] if halves else []) + [m, v]
    place = (core if halves else jnp.zeros((), jnp.int32)).astype(jnp.int32).reshape(1)

    def body(place_ref, *refs):
        w_ref, ga_ref = refs[0], refs[1]
        m_ref, v_ref, g_out, d_out, m_out, v_out = refs[-6:]
        g = ga_ref[...]
        if halves:
            g = jnp.where(pl.program_id(0) // per_half == place_ref[0], g, refs[2][...])
        mn = ADAM_B1 * m_ref[...] + (1.0 - ADAM_B1) * g
        vn = ADAM_B2 * v_ref[...] + (1.0 - ADAM_B2) * (g * g)
        m_hat = mn / c1
        v_hat = vn / c2
        g_out[...] = g
        d_out[...] = -ADAM_LR * (m_hat / (jnp.sqrt(v_hat) + ADAM_EPS) + ADAM_WD * w_ref[...])
        m_out[...] = mn
        v_out[...] = vn

    spec = pl.BlockSpec((tr, cols), lambda i, p: (i, 0))
    g_spec = pl.BlockSpec((tr, cols), lambda i, p: (i % per_half, 0))
    return pl.pallas_call(
        body, name="adamw",
        grid_spec=pltpu.PrefetchScalarGridSpec(
            num_scalar_prefetch=1, grid=(rows // tr,),
            in_specs=[spec] + [g_spec] * (2 if halves else 1) + [spec, spec],
            out_specs=[spec] * 4),
        out_shape=[jax.ShapeDtypeStruct((rows, cols), F32)] * 4,
        compiler_params=_params("parallel"),
    )(place, *ins)


WEIGHT_NAMES = ["ffn1_norm", "ffn1_w_gu", "ffn1_w_down", "mix_norm", "w_in", "q_lat_norm", "kv_lat_norm", "w_uq", "w_uk",
                "w_uv", "q_norm", "k_norm", "w_pool", "pool_scale", "w_out", "ffn2_norm", "ffn2_w_gu", "ffn2_w_down"]
COL_SHARDED = ["ffn1_w_gu", "w_in", "w_uq", "w_uk", "w_uv", "ffn2_w_gu"]
ROW_SHARDED = ["ffn1_w_down", "w_out", "ffn2_w_down"]
SHARDED = [n for n in WEIGHT_NAMES if n in COL_SHARDED or n in ROW_SHARDED]
REPLICATED = [n for n in WEIGHT_NAMES if n not in SHARDED]


def _join_chips(g, name):
    _, depth, r, c = g.shape
    if name in COL_SHARDED:
        return jnp.transpose(g, (1, 2, 0, 3)).reshape(depth, r, N_CHIPS * c)
    return jnp.transpose(g, (1, 0, 2, 3)).reshape(depth, N_CHIPS * r, c)


def _split_chips(full, name):
    depth, r, c = full.shape
    if name in COL_SHARDED:
        return jnp.transpose(full.reshape(depth, r, N_CHIPS, c // N_CHIPS), (0, 2, 1, 3))
    return full.reshape(depth, N_CHIPS, r // N_CHIPS, c)


def _pack(vs):
    flat = jnp.concatenate([v.reshape(-1) for v in vs])
    pad = (-flat.shape[0]) % (8 * 128)
    return jnp.pad(flat, (0, pad)).reshape(-1, 128)


def _unpack(packed, like):
    flat = packed.reshape(-1)
    out, at = [], 0
    for v in like:
        out.append(flat[at:at + v.size].reshape(v.shape))
        at += v.size
    return out


def kernel(x, ffn1_norm, ffn1_w_gu, ffn1_w_down, mix_norm, w_in, q_lat_norm, kv_lat_norm, w_uq, w_uk, w_uv, q_norm, k_norm, w_pool, pool_scale, w_out, ffn2_norm, ffn2_w_gu, ffn2_w_down, loss_target, m_ffn1_norm, m_ffn1_w_gu, m_ffn1_w_down, m_mix_norm, m_w_in, m_q_lat_norm, m_kv_lat_norm, m_w_uq, m_w_uk, m_w_uv, m_q_norm, m_k_norm, m_w_pool, m_pool_scale, m_w_out, m_ffn2_norm, m_ffn2_w_gu, m_ffn2_w_down, v_ffn1_norm, v_ffn1_w_gu, v_ffn1_w_down, v_mix_norm, v_w_in, v_q_lat_norm, v_kv_lat_norm, v_w_uq, v_w_uk, v_w_uv, v_q_norm, v_k_norm, v_w_pool, v_pool_scale, v_w_out, v_ffn2_norm, v_ffn2_w_gu, v_ffn2_w_down):
    args = locals()
    weights = {n: args[n] for n in WEIGHT_NAMES}
    moments_m = {n: args["m_" + n] for n in WEIGHT_NAMES}
    moments_v = {n: args["v_" + n] for n in WEIGHT_NAMES}

    chip = 2 * lax.axis_index("x") + lax.axis_index("y")
    core = lax.axis_index("c")
    shards = [weights[n].astype(BF16) for n in SHARDED]
    gathered = {n: lax.dynamic_update_index_in_dim(g, own, chip, 0)
                for n, own, g in zip(SHARDED, shards, gather_chips(shards))}
    full = {n: _join_chips(gathered[n], n) for n in SHARDED if "ffn" not in n}
    for n in REPLICATED:
        full[n] = weights[n].astype(BF16) if n == "w_pool" else weights[n]
    w = layout_weights(full)
    wd1 = _join_chips(gathered["ffn1_w_down"], "ffn1_w_down")
    wd2 = _join_chips(gathered["ffn2_w_down"], "ffn2_w_down")

    loss_part, grad_x, grads = local_step(x[0], loss_target[0], w, gathered["ffn1_w_gu"], wd1, gathered["ffn2_w_gu"], wd2)
    loss = lax.psum(loss_part[0, 0], ("x", "y", "c"))

    split = [grads[n] if n.endswith("w_gu") else _split_chips(grads[n], n) for n in SHARDED]
    half = split[0].shape[0] // 2

    def with_sibling(g4, other):
        cols = g4.shape[-1]
        rows = other.size // cols
        tr = _pick_rows(rows, cols)
        sums = sum_rows_of(g4.reshape(-1, cols), other.reshape(1, rows, cols), tr, core,
                           lambda i, c: i + c * (rows // tr), with_bf16=True)
        return [t.reshape(other.shape) for t in sums]

    def with_chips(wide, land):
        r, cols = wide.shape[-2:]
        tr = _pick_rows(r, cols)
        per = r // tr
        return sum_rows_of(wide.reshape(-1, cols), land.reshape(3, -1, cols), tr, chip,
                           lambda i, b: (i // per) * N_CHIPS * per + b * per + i % per)

    chip_sums = [with_sibling(g4, other) for g4, other in zip(split, swap_halves(split))]
    landed = scatter_chips([narrow for _, narrow in chip_sums])
    totals = [with_chips(wide, land) for (wide, _), land in zip(chip_sums, landed)]
    out = {}
    for n, mine, theirs in zip(SHARDED, totals, swap_sibling(totals)):
        shape = weights[n].shape
        flat = lambda a: a.reshape(shape[0] * shape[1], shape[2])
        res = adamw(flat(weights[n]), mine, flat(moments_m[n]), flat(moments_v[n]), g_theirs=theirs, core=core)
        out[n] = [r.reshape(shape) for r in res]

    rep_g = [grads[n] for n in REPLICATED]
    total = sum_stack(gather_all(_pack(rep_g)))
    res = adamw(_pack([weights[n] for n in REPLICATED]), total,
                _pack([moments_m[n] for n in REPLICATED]), _pack([moments_v[n] for n in REPLICATED]))
    unpacked = [_unpack(r, rep_g) for r in res]
    for k, n in enumerate(REPLICATED):
        out[n] = [u[k] for u in unpacked]

    return (loss, grad_x[None], *[out[n][0] for n in WEIGHT_NAMES], *[out[n][1] for n in WEIGHT_NAMES],
            *[out[n][2] for n in WEIGHT_NAMES], *[out[n][3] for n in WEIGHT_NAMES])
```

```python
import jax
import jax.numpy as jnp
from jax import lax
from jax.experimental import pallas as pl
from jax.experimental.pallas import tpu as pltpu

F32 = jnp.float32
BF16 = jnp.bfloat16
MESH = pl.DeviceIdType.MESH

DEPTH = 4
D_MODEL = 1024
D_FF = 2816
FF_BLOCK = D_FF // 2
N_CHIPS = 4
HEADS = 8
HEAD_PAD = 128
QK_NOPE = 64
QK_ROPE = 32
QK_HEAD = QK_NOPE + QK_ROPE
V_HEAD = 64
Q_LORA = 384
KV_LORA = 256
POOL_W = 512
POOL_WINDOWS = (2, 4, 8, 16)
POOL_HALO = 8
GROUP = 128
D_IN = Q_LORA + KV_LORA + QK_ROPE + POOL_W
Z_W = 1280
Z_CKV = Q_LORA
Z_POOL = Q_LORA + KV_LORA
Z_KPE = Z_POOL + POOL_W
ATT_W = HEADS * HEAD_PAD
CAT_W = ATT_W + POOL_W
EPS = 1e-6
ROPE_THETA = 10000.0
ADAM_LR, ADAM_B1, ADAM_B2, ADAM_EPS, ADAM_WD, ADAM_STEP = 0.001, 0.9, 0.999, 1e-08, 0.01, 10
LOG2_E = 1.4426950408889634
LN_2 = 0.6931471805599453

VMEM_LIMIT = 56 * 1024 * 1024
VMEM_LIMIT_FFN_BWD = 62 * 1024 * 1024


def _params(*sem, vmem=VMEM_LIMIT):
    return pltpu.CompilerParams(dimension_semantics=sem, vmem_limit_bytes=vmem)


def _dot(a, b):
    return jnp.dot(a, b, preferred_element_type=F32)


def _dot_nt(a, b):
    return lax.dot_general(a, b, (((1,), (1,)), ((), ())), preferred_element_type=F32)


def _dot_tn(a, b):
    return lax.dot_general(a, b, (((0,), (0,)), ((), ())), preferred_element_type=F32)


def _norm_fwd(x, n):
    rstd = lax.rsqrt(jnp.sum(x * x, axis=-1, keepdims=True) * (1.0 / n) + EPS)
    return x * rstd, rstd


def _norm_bwd(dy, g, xhat, rstd, n):
    dxhat = dy * g
    dx = rstd * (dxhat - xhat * (jnp.sum(dxhat * xhat, axis=-1, keepdims=True) * (1.0 / n)))
    return dx, dy * xhat


def _row_tile(s, want):
    t = min(s, want)
    assert s % t == 0
    return t


def ffn_fwd(x, gn, wgu4, wd, layer, tm=512):
    s = x.shape[0]
    tm = _row_tile(s, tm)

    def body(x_ref, gn_ref, wg_ref, wu_ref, wd_ref, o_ref, g_ref, u_ref, h_s, acc_s):
        j = pl.program_id(1)

        @pl.when(j == 0)
        def _():
            xhat, _ = _norm_fwd(x_ref[...], D_MODEL)
            h_s[...] = (xhat * gn_ref[...]).astype(BF16)
            acc_s[...] = jnp.zeros_like(acc_s)

        h = h_s[...]
        g = _dot(h, wg_ref[...])
        u = _dot(h, wu_ref[...])
        g_ref[...] = g.astype(BF16)
        u_ref[...] = u.astype(BF16)
        a = (g * jax.nn.sigmoid(g) * u).astype(BF16)
        acc_s[...] += _dot(a, wd_ref[...])

        @pl.when(j == 1)
        def _():
            o_ref[...] = x_ref[...] + 0.5 * acc_s[...]

    return pl.pallas_call(
        body, name="ffn_fwd",
        grid=(s // tm, 2),
        in_specs=[
            pl.BlockSpec((tm, D_MODEL), lambda i, j: (i, 0)),
            pl.BlockSpec((1, D_MODEL), lambda i, j: (0, 0)),
            pl.BlockSpec((None, None, D_MODEL, FF_BLOCK), lambda i, j: (j, layer, 0, 0)),
            pl.BlockSpec((None, None, D_MODEL, FF_BLOCK), lambda i, j: (j + 2, layer, 0, 0)),
            pl.BlockSpec((None, FF_BLOCK, D_MODEL), lambda i, j: (layer, j, 0)),
        ],
        out_specs=[pl.BlockSpec((tm, D_MODEL), lambda i, j: (i, 0)),
                   pl.BlockSpec((tm, FF_BLOCK), lambda i, j: (i, j)), pl.BlockSpec((tm, FF_BLOCK), lambda i, j: (i, j))],
        out_shape=[jax.ShapeDtypeStruct((s, D_MODEL), F32), jax.ShapeDtypeStruct((s, D_FF), BF16),
                   jax.ShapeDtypeStruct((s, D_FF), BF16)],
        scratch_shapes=[pltpu.VMEM((tm, D_MODEL), BF16), pltpu.VMEM((tm, D_MODEL), F32)],
        compiler_params=_params("parallel", "arbitrary"),
    )(x, gn, wgu4, wgu4, wd)


def ffn_bwd(x, dy, gn, g_pre, u_pre, wgu4, wd, layer, tm=512):
    s = x.shape[0]
    tm = _row_tile(s, tm)
    halves = 2 if tm % 32 == 0 else 1

    def body(x_ref, dy_ref, gn_ref, g_ref, u_ref, wg_ref, wu_ref, wd_ref,
             dx_ref, h_ref, a_ref, dg_ref, du_ref, dyh_ref, dgn_ref, dh_s):
        i, j = pl.program_id(0), pl.program_id(1)

        @pl.when(j == 0)
        def _():
            xhat, _ = _norm_fwd(x_ref[...], D_MODEL)
            h_ref[...] = (xhat * gn_ref[...]).astype(BF16)
            dyh_ref[...] = (0.5 * dy_ref[...]).astype(BF16)
            dh_s[...] = jnp.zeros_like(dh_s)

        for r in range(halves):
            rows = slice(r * (tm // halves), (r + 1) * (tm // halves))
            g = g_ref[rows, :].astype(F32)
            u = u_ref[rows, :].astype(F32)
            sg = jax.nn.sigmoid(g)
            silu = g * sg
            a_ref[rows, :] = (silu * u).astype(BF16)
            da = _dot_nt(dyh_ref[rows, :], wd_ref[...])
            dg = (da * u * (sg * (1.0 + g * (1.0 - sg)))).astype(BF16)
            du = (da * silu).astype(BF16)
            dg_ref[rows, :] = dg
            du_ref[rows, :] = du
            dh_s[rows, :] += _dot_nt(dg, wg_ref[...]) + _dot_nt(du, wu_ref[...])

        @pl.when(j == 1)
        def _():
            xhat, rstd = _norm_fwd(x_ref[...], D_MODEL)
            dxn, dgrow = _norm_bwd(dh_s[...], gn_ref[...], xhat, rstd, D_MODEL)
            dx_ref[...] = dy_ref[...] + dxn
            part = jnp.sum(dgrow, axis=0, keepdims=True)

            @pl.when(i == 0)
            def _():
                dgn_ref[...] = part

            @pl.when(i > 0)
            def _():
                dgn_ref[...] += part

    row = lambda i, j: (i, 0)
    return pl.pallas_call(
        body, name="ffn_bwd",
        grid=(s // tm, 2),
        in_specs=[
            pl.BlockSpec((tm, D_MODEL), row),
            pl.BlockSpec((tm, D_MODEL), row),
            pl.BlockSpec((1, D_MODEL), lambda i, j: (0, 0)),
            pl.BlockSpec((tm, FF_BLOCK), lambda i, j: (i, j)),
            pl.BlockSpec((tm, FF_BLOCK), lambda i, j: (i, j)),
            pl.BlockSpec((None, None, D_MODEL, FF_BLOCK), lambda i, j: (j, layer, 0, 0)),
            pl.BlockSpec((None, None, D_MODEL, FF_BLOCK), lambda i, j: (j + 2, layer, 0, 0)),
            pl.BlockSpec((None, FF_BLOCK, D_MODEL), lambda i, j: (layer, j, 0)),
        ],
        out_specs=[
            pl.BlockSpec((tm, D_MODEL), row),
            pl.BlockSpec((tm, D_MODEL), row),
            pl.BlockSpec((tm, FF_BLOCK), lambda i, j: (i, j)),
            pl.BlockSpec((tm, FF_BLOCK), lambda i, j: (i, j)),
            pl.BlockSpec((tm, FF_BLOCK), lambda i, j: (i, j)),
            pl.BlockSpec((tm, D_MODEL), row),
            pl.BlockSpec((1, D_MODEL), lambda i, j: (0, 0)),
        ],
        out_shape=[
            jax.ShapeDtypeStruct((s, D_MODEL), F32),
            jax.ShapeDtypeStruct((s, D_MODEL), BF16),
            jax.ShapeDtypeStruct((s, D_FF), BF16),
            jax.ShapeDtypeStruct((s, D_FF), BF16),
            jax.ShapeDtypeStruct((s, D_FF), BF16),
            jax.ShapeDtypeStruct((s, D_MODEL), BF16),
            jax.ShapeDtypeStruct((1, D_MODEL), F32),
        ],
        scratch_shapes=[pltpu.VMEM((tm, D_MODEL), F32)],
        compiler_params=_params("arbitrary", "arbitrary", vmem=VMEM_LIMIT_FFN_BWD),
    )(x, dy, gn, g_pre, u_pre, wgu4, wgu4, wd)


def matmul_tn(a, b, tk=2048, tn_max=1536, into=None, at=None):
    s, m = a.shape
    n = b.shape[1]
    tk = _row_tile(s, tk)
    tn = n
    if n > tn_max:
        tn = n // 2
    assert n % tn == 0 and tn % 128 == 0
    nk = s // tk

    def body(a_ref, b_ref, *rest):
        o_ref, acc_s = rest[-2:]
        k = pl.program_id(1)

        @pl.when(k == 0)
        def _():
            acc_s[...] = jnp.zeros_like(acc_s)

        acc_s[...] += _dot_tn(a_ref[...].astype(BF16), b_ref[...].astype(BF16))

        @pl.when(k == nk - 1)
        def _():
            o_ref[...] = acc_s[...]

    in_specs = [pl.BlockSpec((tk, m), lambda j, k: (k, 0)), pl.BlockSpec((tk, tn), lambda j, k: (k, j))]
    if into is None:
        operands, aliases = (a, b), {}
        out_specs = pl.BlockSpec((m, tn), lambda j, k: (0, j))
        out_shape = jax.ShapeDtypeStruct((m, n), F32)
    else:
        assert into.shape[-2:] == (m, tn) and into.dtype == F32
        lead = into.ndim - 2
        operands, aliases = (a, b, into), {2: 0}
        in_specs.append(pl.BlockSpec(memory_space=pl.ANY))
        out_specs = pl.BlockSpec((None,) * lead + (m, tn), lambda j, k: tuple(at(j)) + (0, 0))
        out_shape = jax.ShapeDtypeStruct(into.shape, F32)
    return pl.pallas_call(
        body, name="matmul_tn",
        grid=(n // tn, nk),
        in_specs=in_specs,
        out_specs=out_specs,
        out_shape=out_shape,
        scratch_shapes=[pltpu.VMEM((m, tn), F32)],
        input_output_aliases=aliases,
        compiler_params=_params("parallel", "arbitrary"),
    )(*operands)


def _rope_bwd(dy, c, s1, s2):
    return dy * c + pltpu.roll(dy * s1, 16, 1) + pltpu.roll(dy * s2, HEAD_PAD - 16, 1)


def rope_tables(s):
    pos = jnp.arange(s, dtype=F32)
    inv = ROPE_THETA ** (-jnp.arange(0, QK_ROPE, 2, dtype=F32) / QK_ROPE)
    ang = pos[:, None] * inv[None, :]
    cos, sin = jnp.cos(ang), jnp.sin(ang)
    z16 = jnp.zeros((s, 16), F32)
    c = jnp.concatenate([jnp.ones((s, QK_NOPE), F32), cos, cos, z16, z16], axis=1)
    s1 = jnp.concatenate([jnp.zeros((s, QK_NOPE), F32), -sin, z16, z16, z16], axis=1)
    s2 = jnp.concatenate([jnp.zeros((s, QK_NOPE), F32), z16, sin, z16, z16], axis=1)
    return c, s1, s2


def swap_rope_halves(a):
    lead = a.shape[:-1]
    h = a.reshape(lead + (-1, HEAD_PAD))
    half = QK_ROPE // 2
    zeros = lambda n: jnp.zeros(h.shape[:-1] + (n,), a.dtype)
    sw = jnp.concatenate([zeros(QK_NOPE), h[..., QK_NOPE + half:QK_HEAD], h[..., QK_NOPE:QK_NOPE + half],
                          zeros(HEAD_PAD - QK_HEAD)], axis=-1)
    return sw.reshape(a.shape)


def mix_in_fwd(x, gm, win, qln, kvln, wuq, wuq_sw, wuk, wuv, qn, qn_sw, kn, kn_sw, rc, rs, tm=512):
    s = x.shape[0]
    tm = _row_tile(s, tm)
    scale = QK_HEAD ** -0.5 * LOG2_E
    half = QK_ROPE // 2

    def body(x_ref, gm_ref, win_ref, qln_ref, kvln_ref, wuq_ref, wuqs_ref, wuk_ref, wuv_ref,
             qn_ref, qns_ref, kn_ref, kns_ref, rc_ref, rs_ref, z_ref, q_ref, k_ref, v_ref, vt_ref):
        xhat, _ = _norm_fwd(x_ref[...], D_MODEL)
        h = (xhat * gm_ref[...]).astype(BF16)
        z = _dot(h, win_ref[...])
        z_ref[...] = z
        cq, _ = _norm_fwd(z[:, :Q_LORA], Q_LORA)
        cqn = (cq * qln_ref[...]).astype(BF16)
        ckv, _ = _norm_fwd(z[:, Z_CKV:Z_POOL], KV_LORA)
        ckvn = (ckv * kvln_ref[...]).astype(BF16)
        kpe = z[:, Z_KPE:]
        lane = lax.broadcasted_iota(jnp.int32, kpe.shape, 1)
        first = (lane >= QK_NOPE) & (lane < QK_NOPE + half)
        second = (lane >= QK_NOPE + half) & (lane < QK_HEAD)
        kpe_sw = jnp.where(first, pltpu.roll(kpe, HEAD_PAD - half, 1), jnp.where(second, pltpu.roll(kpe, half, 1), 0.0))
        q_raw = _dot(cqn, wuq_ref[...])
        q_sw = _dot(cqn, wuqs_ref[...])
        k_raw = _dot(ckvn, wuk_ref[...])
        v = _dot(ckvn, wuv_ref[...])
        v_ref[...] = v.astype(BF16)
        vt_ref[...] = jnp.transpose(v).astype(BF16)
        c, sn = rc_ref[...], rs_ref[...]
        tq, tq_sw = qn_ref[...] * c * scale, qns_ref[...] * sn * scale
        tk, tk_sw = kn_ref[...] * c, kns_ref[...] * sn
        for hd in range(HEADS):
            lanes = slice(hd * HEAD_PAD, (hd + 1) * HEAD_PAD)
            qr = q_raw[:, lanes]
            _, q_rstd = _norm_fwd(qr, QK_HEAD)
            q_ref[:, lanes] = (q_rstd * (qr * tq + q_sw[:, lanes] * tq_sw)).astype(BF16)
            kr = k_raw[:, lanes] + kpe
            _, k_rstd = _norm_fwd(kr, QK_HEAD)
            k_ref[:, lanes] = (k_rstd * (kr * tk + kpe_sw * tk_sw)).astype(BF16)

    full = lambda shape: pl.BlockSpec(shape, lambda i: (0,) * len(shape))
    row = lambda w: pl.BlockSpec((tm, w), lambda i: (i, 0))
    return pl.pallas_call(
        body, name="mix_in_fwd",
        grid=(s // tm,),
        in_specs=[row(D_MODEL), full((1, D_MODEL)), full((D_MODEL, Z_W)), full((1, Q_LORA)), full((1, KV_LORA)),
                  full((Q_LORA, ATT_W)), full((Q_LORA, ATT_W)), full((KV_LORA, ATT_W)), full((KV_LORA, ATT_W)),
                  full((1, HEAD_PAD)), full((1, HEAD_PAD)), full((1, HEAD_PAD)), full((1, HEAD_PAD)),
                  row(HEAD_PAD), row(HEAD_PAD)],
        out_specs=[row(Z_W), row(ATT_W), row(ATT_W), row(ATT_W), pl.BlockSpec((ATT_W, tm), lambda i: (0, i))],
        out_shape=[jax.ShapeDtypeStruct((s, Z_W), F32)] + [jax.ShapeDtypeStruct((s, ATT_W), BF16)] * 3 + [
            jax.ShapeDtypeStruct((ATT_W, s), BF16)],
        compiler_params=_params("parallel"),
    )(x, gm, win, qln, kvln, wuq, wuq_sw, wuk, wuv, qn, qn_sw, kn, kn_sw, rc, rs)


def mla_bwd(z, dq_t, dk, dv, dp, qln, kvln, wuq, wuk, wuv, qn, kn, rc, rs1, rs2, tm=512):
    s = z.shape[0]
    tm = _row_tile(s, tm)
    scale = QK_HEAD ** -0.5

    def body(z_ref, dqt_ref, dk_ref, dv_ref, dp_ref, qln_ref, kvln_ref, wuq_ref, wuk_ref, wuv_ref, qn_ref, kn_ref,
             rc_ref, rs1_ref, rs2_ref,
             dz_ref, cqn_ref, ckvn_ref, dqr_ref, dkr_ref, dvb_ref, dqln_ref, dkvln_ref, dqn_ref, dkn_ref):
        i = pl.program_id(0)
        z = z_ref[...]
        cq, cq_rstd = _norm_fwd(z[:, :Q_LORA], Q_LORA)
        cqn = (cq * qln_ref[...]).astype(BF16)
        ckv, ckv_rstd = _norm_fwd(z[:, Z_CKV:Z_POOL], KV_LORA)
        ckvn = (ckv * kvln_ref[...]).astype(BF16)
        kpe = z[:, Z_KPE:]
        cqn_ref[...] = cqn
        ckvn_ref[...] = ckvn
        q_raw = _dot(cqn, wuq_ref[...])
        k_raw = _dot(ckvn, wuk_ref[...])
        c, s1, s2 = rc_ref[...], rs1_ref[...], rs2_ref[...]
        lane = lax.broadcasted_iota(jnp.int32, (tm, HEAD_PAD), 1)
        rope_lanes = (lane >= QK_NOPE) & (lane < QK_HEAD)
        dkpe = jnp.zeros((tm, HEAD_PAD), F32)
        dqn = jnp.zeros((tm, HEAD_PAD), F32)
        dkn = jnp.zeros((tm, HEAD_PAD), F32)
        for hd in range(HEADS):
            lanes = slice(hd * HEAD_PAD, (hd + 1) * HEAD_PAD)
            qh, q_rstd = _norm_fwd(q_raw[:, lanes], QK_HEAD)
            dqh = jnp.transpose(dqt_ref[lanes, :]) * scale
            dqr, dg = _norm_bwd(_rope_bwd(dqh, c, s1, s2), qn_ref[...], qh, q_rstd, QK_HEAD)
            dqn += dg
            dqr_ref[:, lanes] = dqr.astype(BF16)
            kh, k_rstd = _norm_fwd(k_raw[:, lanes] + kpe, QK_HEAD)
            dkr, dg = _norm_bwd(_rope_bwd(dk_ref[:, lanes], c, s1, s2), kn_ref[...], kh, k_rstd, QK_HEAD)
            dkn += dg
            dkr_ref[:, lanes] = dkr.astype(BF16)
            dkpe += jnp.where(rope_lanes, dkr, 0.0)
        dvb = dv_ref[...].astype(BF16)
        dvb_ref[...] = dvb
        dcqn = _dot_nt(dqr_ref[...], wuq_ref[...])
        dckvn = _dot_nt(dkr_ref[...], wuk_ref[...]) + _dot_nt(dvb, wuv_ref[...])
        dcq, dqln = _norm_bwd(dcqn, qln_ref[...], cq, cq_rstd, Q_LORA)
        dckv, dkvln = _norm_bwd(dckvn, kvln_ref[...], ckv, ckv_rstd, KV_LORA)
        dz_ref[:, :Q_LORA] = dcq.astype(BF16)
        dz_ref[:, Z_CKV:Z_POOL] = dckv.astype(BF16)
        dz_ref[:, Z_POOL:Z_KPE] = dp_ref[...].astype(BF16)
        dz_ref[:, Z_KPE:] = dkpe.astype(BF16)
        parts = [(dqln_ref, dqln), (dkvln_ref, dkvln), (dqn_ref, dqn), (dkn_ref, dkn)]

        @pl.when(i == 0)
        def _():
            for ref, val in parts:
                ref[...] = jnp.sum(val, axis=0, keepdims=True)

        @pl.when(i > 0)
        def _():
            for ref, val in parts:
                ref[...] += jnp.sum(val, axis=0, keepdims=True)

    full = lambda shape: pl.BlockSpec(shape, lambda i: (0,) * len(shape))
    row = lambda w: pl.BlockSpec((tm, w), lambda i: (i, 0))
    return pl.pallas_call(
        body, name="mla_bwd",
        grid=(s // tm,),
        in_specs=[row(Z_W), pl.BlockSpec((ATT_W, tm), lambda i: (0, i)), row(ATT_W), row(ATT_W), row(POOL_W),
                  full((1, Q_LORA)), full((1, KV_LORA)), full((Q_LORA, ATT_W)), full((KV_LORA, ATT_W)),
                  full((KV_LORA, ATT_W)), full((1, HEAD_PAD)), full((1, HEAD_PAD)),
                  row(HEAD_PAD), row(HEAD_PAD), row(HEAD_PAD)],
        out_specs=[row(Z_W), row(Q_LORA), row(KV_LORA), row(ATT_W), row(ATT_W), row(ATT_W),
                   full((1, Q_LORA)), full((1, KV_LORA)), full((1, HEAD_PAD)), full((1, HEAD_PAD))],
        out_shape=[jax.ShapeDtypeStruct((s, Z_W), BF16), jax.ShapeDtypeStruct((s, Q_LORA), BF16),
                   jax.ShapeDtypeStruct((s, KV_LORA), BF16)] + [jax.ShapeDtypeStruct((s, ATT_W), BF16)] * 3 + [
                   jax.ShapeDtypeStruct((1, Q_LORA), F32), jax.ShapeDtypeStruct((1, KV_LORA), F32),
                   jax.ShapeDtypeStruct((1, HEAD_PAD), F32), jax.ShapeDtypeStruct((1, HEAD_PAD), F32)],
        compiler_params=_params("arbitrary"),
    )(z, dq_t, dk, dv, dp, qln, kvln, wuq, wuk, wuv, qn, kn, rc, rs1, rs2)


def mix_in_bwd(x, dx_in, dz, gm, win, tm=512):
    s = x.shape[0]
    tm = _row_tile(s, tm)

    def body(x_ref, dxin_ref, dz_ref, gm_ref, win_ref, dx_ref, h_ref, dgm_ref):
        i = pl.program_id(0)
        xhat, rstd = _norm_fwd(x_ref[...], D_MODEL)
        h_ref[...] = (xhat * gm_ref[...]).astype(BF16)
        dh = _dot_nt(dz_ref[...], win_ref[...])
        dxn, dgrow = _norm_bwd(dh, gm_ref[...], xhat, rstd, D_MODEL)
        dx_ref[...] = dxin_ref[...] + dxn
        part = jnp.sum(dgrow, axis=0, keepdims=True)

        @pl.when(i == 0)
        def _():
            dgm_ref[...] = part

        @pl.when(i > 0)
        def _():
            dgm_ref[...] += part

    full = lambda shape: pl.BlockSpec(shape, lambda i: (0,) * len(shape))
    row = lambda w: pl.BlockSpec((tm, w), lambda i: (i, 0))
    return pl.pallas_call(
        body, name="mix_in_bwd",
        grid=(s // tm,),
        in_specs=[row(D_MODEL), row(D_MODEL), row(Z_W), full((1, D_MODEL)), full((D_MODEL, Z_W))],
        out_specs=[row(D_MODEL), row(D_MODEL), full((1, D_MODEL))],
        out_shape=[jax.ShapeDtypeStruct((s, D_MODEL), F32), jax.ShapeDtypeStruct((s, D_MODEL), BF16),
                   jax.ShapeDtypeStruct((1, D_MODEL), F32)],
        compiler_params=_params("arbitrary"),
    )(x, dx_in, dz, gm, win)


ATT_CHUNK_FWD = 1024
ATT_CHUNK_BWD = 256


def flash_fwd(q, k, vt, tq=1024, tk=16384, chunk=ATT_CHUNK_FWD):
    s = q.shape[0]
    tq, tk = _row_tile(s, tq), _row_tile(s, tk)
    nk = s // tk
    tc = _row_tile(tk, chunk)
    sub = tk // tc

    def body(q_ref, k_ref, vt_ref, o_ref, lse_ref, m_s, l_s, acc_s):
        kk = pl.program_id(2)

        @pl.when(kk == 0)
        def _():
            m_s[...] = jnp.full_like(m_s, -jnp.inf)
            l_s[...] = jnp.zeros_like(l_s)
            acc_s[...] = jnp.zeros_like(acc_s)

        q_t = q_ref[...]

        def scores(c):
            return _dot_nt(k_ref[c * tc:(c + 1) * tc, :], q_t)

        m, l, acc = m_s[...], l_s[...], acc_s[...]
        s_next = scores(0)
        for c in range(sub):
            s_t = s_next
            if c + 1 < sub:
                s_next = scores(c + 1)
            m_new = jnp.maximum(m, jnp.max(s_t, axis=0, keepdims=True))
            alpha = jnp.exp2(m - m_new)
            p_t = jnp.exp2(s_t - m_new)
            l = alpha * l + jnp.sum(p_t, axis=0, keepdims=True)
            acc = alpha * acc + _dot(vt_ref[:, c * tc:(c + 1) * tc], p_t.astype(BF16))
            m = m_new
        m_s[...], l_s[...], acc_s[...] = m, l, acc

        @pl.when(kk == nk - 1)
        def _():
            o_ref[...] = jnp.transpose(acc_s[...] / l_s[...]).astype(BF16)
            lse_ref[...] = m_s[...] + jnp.log2(l_s[...])

    return pl.pallas_call(
        body, name="flash_fwd",
        grid=(HEADS, s // tq, nk),
        in_specs=[pl.BlockSpec((tq, HEAD_PAD), lambda h, i, kk: (i, h)),
                  pl.BlockSpec((tk, HEAD_PAD), lambda h, i, kk: (kk, h)),
                  pl.BlockSpec((HEAD_PAD, tk), lambda h, i, kk: (h, kk))],
        out_specs=[pl.BlockSpec((tq, HEAD_PAD), lambda h, i, kk: (i, h)),
                   pl.BlockSpec((None, 1, tq), lambda h, i, kk: (h, 0, i))],
        out_shape=[jax.ShapeDtypeStruct((s, ATT_W), BF16), jax.ShapeDtypeStruct((HEADS, 1, s), F32)],
        scratch_shapes=[pltpu.VMEM((1, tq), F32), pltpu.VMEM((1, tq), F32), pltpu.VMEM((HEAD_PAD, tq), F32)],
        compiler_params=_params("parallel", "parallel", "arbitrary"),
    )(q, k, vt)


def flash_bwd(q, k, v, do, lse_row, delta_row, tq=4096, tk=4096, chunk=ATT_CHUNK_BWD):
    s = q.shape[0]
    tq, tk = _row_tile(s, tq), _row_tile(s, tk)
    nq = s // tq
    tc = _row_tile(tq, chunk)
    sub = tq // tc

    def body(q_ref, k_ref, v_ref, do_ref, lse_ref, dl_ref, dqt_ref, dk_ref, dv_ref, kt_s, dk_s, dv_s):
        kb, qb = pl.program_id(1), pl.program_id(2)

        @pl.when(qb == 0)
        def _():
            kt_s[...] = jnp.transpose(k_ref[...].astype(F32)).astype(BF16)
            dk_s[...] = jnp.zeros_like(dk_s)
            dv_s[...] = jnp.zeros_like(dv_s)

        k_t, v_t = k_ref[...], v_ref[...]

        def products(c):
            rows = slice(c * tc, (c + 1) * tc)
            return _dot_nt(k_t, q_ref[rows, :]), _dot_nt(v_t, do_ref[rows, :])

        dk, dv = dk_s[...], dv_s[...]
        nxt = products(0)
        contribs = []
        for c in range(sub):
            rows = slice(c * tc, (c + 1) * tc)
            s_t, dp_t = nxt
            if c + 1 < sub:
                nxt = products(c + 1)
            p_t = jnp.exp2(s_t - lse_ref[:, rows])
            ds_t = (p_t * (dp_t - dl_ref[:, rows])).astype(BF16)
            dv = dv + _dot(p_t.astype(BF16), do_ref[rows, :])
            dk = dk + _dot(ds_t, q_ref[rows, :])
            contribs.append(_dot(kt_s[...], ds_t))
        dk_s[...], dv_s[...] = dk, dv
        contrib = jnp.concatenate(contribs, axis=1) if sub > 1 else contribs[0]
        cols = pl.ds(pl.multiple_of(qb * tq, tq), tq)

        @pl.when(kb == 0)
        def _():
            dqt_ref[:, cols] = contrib

        @pl.when(kb > 0)
        def _():
            dqt_ref[:, cols] += contrib

        @pl.when(qb == nq - 1)
        def _():
            dk_ref[...] = dk_s[...] * LN_2
            dv_ref[...] = dv_s[...]

    qspec = pl.BlockSpec((tq, HEAD_PAD), lambda h, kb, qb: (qb, h))
    kspec = pl.BlockSpec((tk, HEAD_PAD), lambda h, kb, qb: (kb, h))
    rowspec = pl.BlockSpec((None, 1, tq), lambda h, kb, qb: (h, 0, qb))
    return pl.pallas_call(
        body, name="flash_bwd",
        grid=(HEADS, s // tk, nq),
        in_specs=[qspec, kspec, kspec, qspec, rowspec, rowspec],
        out_specs=[pl.BlockSpec((HEAD_PAD, s), lambda h, kb, qb: (h, 0)), kspec, kspec],
        out_shape=[jax.ShapeDtypeStruct((ATT_W, s), F32), jax.ShapeDtypeStruct((s, ATT_W), F32),
                   jax.ShapeDtypeStruct((s, ATT_W), F32)],
        scratch_shapes=[pltpu.VMEM((HEAD_PAD, tk), BF16), pltpu.VMEM((tk, HEAD_PAD), F32),
                        pltpu.VMEM((tk, HEAD_PAD), F32)],
        compiler_params=_params("parallel", "arbitrary", "arbitrary"),
    )(q, k, v, do, lse_row, delta_row)


def _halo_specs(tm, w, n_tiles):
    per = tm // POOL_HALO
    last = n_tiles * per - 1
    return [pl.BlockSpec((POOL_HALO, w), lambda i: (jnp.maximum(i * per - 1, 0), 0)),
            pl.BlockSpec((tm, w), lambda i: (i, 0)),
            pl.BlockSpec((POOL_HALO, w), lambda i: (jnp.minimum((i + 1) * per, last), 0))]


def _fill_ext(ext_ref, prev, cur, nxt, i, n_tiles, tm):
    ext_ref[pl.ds(0, POOL_HALO), :] = jnp.where(i > 0, prev, 0.0)
    ext_ref[pl.ds(POOL_HALO, tm), :] = cur
    ext_ref[pl.ds(POOL_HALO + tm, POOL_HALO), :] = jnp.where(i < n_tiles - 1, nxt, 0.0)


def _window_sum(ext_ref, g, lo, hi, tm):
    lanes = pl.ds(g * GROUP, GROUP)
    acc = ext_ref[pl.ds(POOL_HALO + lo, tm), lanes]
    for d in range(lo + 1, hi + 1):
        acc = acc + ext_ref[pl.ds(POOL_HALO + d, tm), lanes]
    return acc


def _pool_mixed(ext_ref, cur, i, s, tm):
    row = i * tm + lax.broadcasted_iota(jnp.int32, (tm, 1), 0)
    out = []
    for g, w in enumerate(POOL_WINDOWS):
        left = w // 2
        right = w - 1 - left
        cnt = (jnp.minimum(row + right + 1, s) - jnp.maximum(row - left, 0)).astype(F32)
        out.append(_window_sum(ext_ref, g, -left, right, tm) / cnt - cur[:, g * GROUP:(g + 1) * GROUP])
    return out


def pool_out_fwd(x, z, o, wpool, pscale, wout, tm=512):
    s = x.shape[0]
    tm = _row_tile(s, tm)
    n_tiles = s // tm

    def body(x_ref, zp_ref, z_ref, zn_ref, o_ref, wp_ref, ps_ref, wo_ref, y_ref, cat_ref, ext_s):
        i = pl.program_id(0)
        cur = z_ref[:, Z_POOL:Z_KPE]
        _fill_ext(ext_s, zp_ref[:, Z_POOL:Z_KPE], cur, zn_ref[:, Z_POOL:Z_KPE], i, n_tiles, tm)
        mixed = _pool_mixed(ext_s, cur, i, s, tm)
        cat_ref[:, :ATT_W] = o_ref[...]
        for g in range(len(POOL_WINDOWS)):
            lanes = slice(g * GROUP, (g + 1) * GROUP)
            yg = _dot(mixed[g].astype(BF16), wp_ref[g])
            cat_ref[:, ATT_W + g * GROUP:ATT_W + (g + 1) * GROUP] = (yg * ps_ref[:, lanes]).astype(BF16)
        y_ref[...] = x_ref[...] + _dot(cat_ref[...], wo_ref[...])

    full = lambda shape: pl.BlockSpec(shape, lambda i: (0,) * len(shape))
    row = lambda w: pl.BlockSpec((tm, w), lambda i: (i, 0))
    return pl.pallas_call(
        body, name="pool_out_fwd",
        grid=(n_tiles,),
        in_specs=[row(D_MODEL)] + _halo_specs(tm, Z_W, n_tiles) + [
            row(ATT_W), full((len(POOL_WINDOWS), GROUP, GROUP)), full((1, POOL_W)), full((CAT_W, D_MODEL))],
        out_specs=[row(D_MODEL), row(CAT_W)],
        out_shape=[jax.ShapeDtypeStruct((s, D_MODEL), F32), jax.ShapeDtypeStruct((s, CAT_W), BF16)],
        scratch_shapes=[pltpu.VMEM((tm + 2 * POOL_HALO, POOL_W), F32)],
        compiler_params=_params("parallel"),
    )(x, z, z, z, o, wpool, pscale, wout)


def out_bwd(dx, wout, cat, tm=512):
    s = dx.shape[0]
    tm = _row_tile(s, tm)

    def body(dx_ref, wo_ref, cat_ref, do_ref, db_ref, dl_ref):
        dcat = _dot_nt(dx_ref[...].astype(BF16), wo_ref[...])
        do_ref[...] = dcat[:, :ATT_W].astype(BF16)
        db_ref[...] = dcat[:, ATT_W:]
        for hd in range(HEADS):
            lanes = slice(hd * HEAD_PAD, (hd + 1) * HEAD_PAD)
            dl_ref[hd] = jnp.sum(dcat[:, lanes] * cat_ref[:, lanes].astype(F32), axis=-1, keepdims=True)

    row = lambda w: pl.BlockSpec((tm, w), lambda i: (i, 0))
    return pl.pallas_call(
        body, name="out_bwd",
        grid=(s // tm,),
        in_specs=[row(D_MODEL), pl.BlockSpec((CAT_W, D_MODEL), lambda i: (0, 0)), row(CAT_W)],
        out_specs=[row(ATT_W), row(POOL_W), pl.BlockSpec((HEADS, tm, 1), lambda i: (0, i, 0))],
        out_shape=[jax.ShapeDtypeStruct((s, ATT_W), BF16), jax.ShapeDtypeStruct((s, POOL_W), F32),
                   jax.ShapeDtypeStruct((HEADS, s, 1), F32)],
        compiler_params=_params("parallel"),
    )(dx, wout, cat)


def pool_bwd(db, z, wpool, pscale, tm=512):
    s = db.shape[0]
    tm = _row_tile(s, tm)
    n_tiles = s // tm
    te = tm + 2 * POOL_HALO

    def body(dbp_ref, db_ref, dbn_ref, zp_ref, z_ref, zn_ref, wp_ref, ps_ref,
             dp_ref, mixed_ref, dys_ref, dps_ref, ext_s, text_s):
        i = pl.program_id(0)
        cur = z_ref[:, Z_POOL:Z_KPE]
        _fill_ext(ext_s, zp_ref[:, Z_POOL:Z_KPE], cur, zn_ref[:, Z_POOL:Z_KPE], i, n_tiles, tm)
        mixed = _pool_mixed(ext_s, cur, i, s, tm)
        _fill_ext(ext_s, dbp_ref[...], db_ref[...], dbn_ref[...], i, n_tiles, tm)
        erow = i * tm - POOL_HALO + lax.broadcasted_iota(jnp.int32, (te, 1), 0)
        dps = []
        for g, w in enumerate(POOL_WINDOWS):
            lanes = slice(g * GROUP, (g + 1) * GROUP)
            left = w // 2
            right = w - 1 - left
            mb = mixed[g].astype(BF16)
            mixed_ref[:, lanes] = mb
            dps.append(jnp.sum(db_ref[:, lanes] * _dot(mb, wp_ref[g]), axis=0, keepdims=True))
            dys_e = (ext_s[:, lanes] * ps_ref[:, lanes]).astype(BF16)
            dys_ref[:, lanes] = (db_ref[:, lanes] * ps_ref[:, lanes]).astype(BF16)
            dmix_e = _dot_nt(dys_e, wp_ref[g])
            cnt = jnp.minimum(erow + right + 1, s) - jnp.maximum(erow - left, 0)
            text_s[:, lanes] = dmix_e / jnp.maximum(cnt, 1).astype(F32)
            dp_ref[:, lanes] = _window_sum(text_s, g, -right, left, tm) - dmix_e[POOL_HALO:POOL_HALO + tm]
        part = jnp.concatenate(dps, axis=1)

        @pl.when(i == 0)
        def _():
            dps_ref[...] = part

        @pl.when(i > 0)
        def _():
            dps_ref[...] += part

    full = lambda shape: pl.BlockSpec(shape, lambda i: (0,) * len(shape))
    row = lambda w: pl.BlockSpec((tm, w), lambda i: (i, 0))
    return pl.pallas_call(
        body, name="pool_bwd",
        grid=(n_tiles,),
        in_specs=_halo_specs(tm, POOL_W, n_tiles) + _halo_specs(tm, Z_W, n_tiles) + [
            full((len(POOL_WINDOWS), GROUP, GROUP)), full((1, POOL_W))],
        out_specs=[row(POOL_W), row(POOL_W), row(POOL_W), full((1, POOL_W))],
        out_shape=[jax.ShapeDtypeStruct((s, POOL_W), F32), jax.ShapeDtypeStruct((s, POOL_W), BF16),
                   jax.ShapeDtypeStruct((s, POOL_W), BF16), jax.ShapeDtypeStruct((1, POOL_W), F32)],
        scratch_shapes=[pltpu.VMEM((te, POOL_W), F32), pltpu.VMEM((te, POOL_W), F32)],
        compiler_params=_params("arbitrary"),
    )(db, db, db, z, z, z, wpool, pscale)


def loss_head(y, target, tm=512):
    s = y.shape[0]
    tm = _row_tile(s, tm)

    def body(y_ref, t_ref, dy_ref, loss_ref):
        i = pl.program_id(0)
        err = y_ref[...] - t_ref[...]
        dy_ref[...] = err * (1.0 / D_MODEL)
        part = 0.5 * jnp.sum(jnp.sum(err * err, axis=-1, keepdims=True) * (1.0 / D_MODEL), axis=0, keepdims=True)
        part = jnp.broadcast_to(part, (1, 128))

        @pl.when(i == 0)
        def _():
            loss_ref[...] = part

        @pl.when(i > 0)
        def _():
            loss_ref[...] += part

    row = pl.BlockSpec((tm, D_MODEL), lambda i: (i, 0))
    return pl.pallas_call(
        body, name="loss_head",
        grid=(s // tm,),
        in_specs=[row, row],
        out_specs=[row, pl.BlockSpec((1, 128), lambda i: (0, 0))],
        out_shape=[jax.ShapeDtypeStruct((s, D_MODEL), F32), jax.ShapeDtypeStruct((1, 128), F32)],
        compiler_params=_params("arbitrary"),
    )(y, target)


def _pad_heads(w, real):
    lead = w.shape[:-1]
    w = w.reshape(lead + (HEADS, real))
    w = jnp.pad(w, [(0, 0)] * len(lead) + [(0, 0), (0, HEAD_PAD - real)])
    return w.reshape(lead + (ATT_W,))


def _unpad_heads(w, real):
    lead = w.shape[:-1]
    return w.reshape(lead + (HEADS, HEAD_PAD))[..., :real].reshape(lead + (HEADS * real,))


def layout_weights(full):
    w = {}
    depth = full["w_in"].shape[0]
    win = full["w_in"]
    zc = lambda n: jnp.zeros((depth, D_MODEL, n), win.dtype)
    o_pe = Q_LORA + KV_LORA
    w["win"] = jnp.concatenate([win[..., :o_pe], win[..., o_pe + QK_ROPE:], zc(QK_NOPE), win[..., o_pe:o_pe + QK_ROPE],
                                zc(HEAD_PAD - QK_HEAD)], axis=-1)
    w["wuq"] = _pad_heads(full["w_uq"], QK_HEAD)
    w["wuk"] = _pad_heads(full["w_uk"], QK_NOPE)
    w["wuv"] = _pad_heads(full["w_uv"], V_HEAD)
    wo = full["w_out"]
    wo_att = jnp.swapaxes(_pad_heads(jnp.swapaxes(wo[:, :HEADS * V_HEAD], 1, 2), V_HEAD), 1, 2)
    w["wout"] = jnp.concatenate([wo_att, wo[:, HEADS * V_HEAD:]], axis=1)
    pad_vec = lambda v: jnp.pad(v, ((0, 0), (0, HEAD_PAD - QK_HEAD)))
    w["qn"], w["kn"] = pad_vec(full["q_norm"]), pad_vec(full["k_norm"])
    for n in ("wuq", "qn", "kn"):
        w[n + "_sw"] = swap_rope_halves(w[n])
    for n in ("ffn1_norm", "mix_norm", "ffn2_norm", "q_lat_norm", "kv_lat_norm", "pool_scale", "w_pool"):
        w[n] = full[n]
    return w


def local_step(x, target, w, wgu1, wd1, wgu2, wd2):
    s = x.shape[0]
    depth = w["win"].shape[0]
    rc, rs1, rs2 = rope_tables(s)
    rs_both = rs1 + rs2
    vec = lambda name, l: w[name][l][None, :]
    saved = []
    for l in range(depth):
        x1, g1, u1 = ffn_fwd(x, vec("ffn1_norm", l), wgu1, wd1, l)
        z, q, k, v, vt = mix_in_fwd(x1, vec("mix_norm", l), w["win"][l], vec("q_lat_norm", l), vec("kv_lat_norm", l),
                                    w["wuq"][l], w["wuq_sw"][l], w["wuk"][l], w["wuv"][l], vec("qn", l),
                                    vec("qn_sw", l), vec("kn", l), vec("kn_sw", l), rc, rs_both)
        o, lse = flash_fwd(q, k, vt)
        x2, cat = pool_out_fwd(x1, z, o, w["w_pool"][l], vec("pool_scale", l), w["wout"][l])
        x3, g2, u2 = ffn_fwd(x2, vec("ffn2_norm", l), wgu2, wd2, l)
        saved.append((x, x1, z, q, k, v, lse, cat, x2, g1, u1, g2, u2))
        x = x3
    dy, loss = loss_head(x, target)

    names = ["ffn1_norm", "ffn1_w_gu", "ffn1_w_down", "mix_norm", "w_in", "q_lat_norm", "kv_lat_norm", "w_uq", "w_uk",
             "w_uv", "q_norm", "k_norm", "w_pool", "pool_scale", "w_out", "ffn2_norm", "ffn2_w_gu", "ffn2_w_down"]
    grads = {n: [None] * depth for n in names if "ffn" not in n or "norm" in n}
    for n in ("ffn1", "ffn2"):
        grads[n + "_w_gu"] = lax.empty((depth, N_CHIPS, D_MODEL, FF_BLOCK), F32)
        grads[n + "_w_down"] = lax.empty((depth, D_FF, D_MODEL), F32)

    def ffn_weight_grads(n, l, h, act, dg, du, dyh):
        gu = matmul_tn(h, dg, into=grads[n + "_w_gu"], at=lambda j: (l, j))
        grads[n + "_w_gu"] = matmul_tn(h, du, into=gu, at=lambda j: (l, j + 2))
        grads[n + "_w_down"] = matmul_tn(act, dyh, tk=1024, into=grads[n + "_w_down"], at=lambda j: (l,))

    for l in reversed(range(depth)):
        x0, x1, z, q, k, v, lse, cat, x2, g1, u1, g2, u2 = saved[l]
        dx2, h, act, dg, du, dyh, dgn = ffn_bwd(x2, dy, vec("ffn2_norm", l), g2, u2, wgu2, wd2, l)
        grads["ffn2_norm"][l] = dgn[0]
        ffn_weight_grads("ffn2", l, h, act, dg, du, dyh)

        do, db, delta = out_bwd(dx2, w["wout"][l], cat)
        dwout = matmul_tn(cat, dx2)
        grads["w_out"][l] = jnp.concatenate(
            [jnp.swapaxes(_unpad_heads(jnp.swapaxes(dwout[:ATT_W], 0, 1), V_HEAD), 0, 1), dwout[ATT_W:]], axis=0)
        dp, mixed, dys, dps = pool_bwd(db, z, w["w_pool"][l], vec("pool_scale", l))
        grads["pool_scale"][l] = dps[0]
        dwp = matmul_tn(mixed, dys)
        grads["w_pool"][l] = jnp.stack([dwp[g * GROUP:(g + 1) * GROUP, g * GROUP:(g + 1) * GROUP]
                                        for g in range(len(POOL_WINDOWS))])
        dq_t, dk, dv = flash_bwd(q, k, v, do, lse, delta.reshape(HEADS, 1, s))
        dz, cqn, ckvn, dqr, dkr, dvb, dqln, dkvln, dqn, dkn = mla_bwd(
            z, dq_t, dk, dv, dp, vec("q_lat_norm", l), vec("kv_lat_norm", l), w["wuq"][l], w["wuk"][l], w["wuv"][l],
            vec("qn", l), vec("kn", l), rc, rs1, rs2)
        grads["q_lat_norm"][l], grads["kv_lat_norm"][l] = dqln[0], dkvln[0]
        grads["q_norm"][l], grads["k_norm"][l] = dqn[0, :QK_HEAD], dkn[0, :QK_HEAD]
        grads["w_uq"][l] = _unpad_heads(matmul_tn(cqn, dqr), QK_HEAD)
        grads["w_uk"][l] = _unpad_heads(matmul_tn(ckvn, dkr), QK_NOPE)
        grads["w_uv"][l] = _unpad_heads(matmul_tn(ckvn, dvb), V_HEAD)
        dx1, h2, dgm = mix_in_bwd(x1, dx2, dz, vec("mix_norm", l), w["win"][l])
        grads["mix_norm"][l] = dgm[0]
        dwin = matmul_tn(h2, dz)
        grads["w_in"][l] = jnp.concatenate(
            [dwin[:, :Z_POOL], dwin[:, Z_KPE + QK_NOPE:Z_KPE + QK_HEAD], dwin[:, Z_POOL:Z_KPE]], axis=1)

        dy, h, act, dg, du, dyh, dgn = ffn_bwd(x0, dx1, vec("ffn1_norm", l), g1, u1, wgu1, wd1, l)
        grads["ffn1_norm"][l] = dgn[0]
        ffn_weight_grads("ffn1", l, h, act, dg, du, dyh)
    return loss, dy, {n: (jnp.stack(g) if isinstance(g, list) else g) for n, g in grads.items()}


HBM_SPEC = pl.BlockSpec(memory_space=pltpu.HBM)


def _place():
    x, y, c = lax.axis_index("x"), lax.axis_index("y"), lax.axis_index("c")
    other_chips = [(1 - x, y), (x, 1 - y), (1 - x, 1 - y)]
    return x, y, c, other_chips


def _exchange(name, arrays, out_shapes, plan, n_local, n_remote):
    n = len(arrays)

    def body(*refs):
        ins, outs = refs[:n], refs[n:n + len(out_shapes)]
        send_sems, recv_sems, local_sems = refs[n + len(out_shapes):]
        local, sends, recvs, forwards = plan(ins, outs, _place())

        def remote(k, src, dst, dev):
            return pltpu.make_async_remote_copy(src_ref=src, dst_ref=dst, send_sem=send_sems.at[k],
                                                recv_sem=recv_sems.at[k], device_id=dev, device_id_type=MESH)

        started = []
        for k, (src, dst) in enumerate(local):
            cp = pltpu.make_async_copy(src, dst, local_sems.at[k])
            cp.start()
            started.append(cp)
        out_going = []
        for k, (src, dst, dev) in enumerate(sends):
            cp = remote(k, src, dst, dev)
            cp.start()
            out_going.append(cp)
        for k, (dst, dev) in enumerate(recvs):
            remote(k, dst, dst, dev).wait_recv()
            if forwards:
                src, fdst, fdev, _ = forwards[k]
                cp = remote(len(sends) + k, src, fdst, fdev)
                cp.start()
                out_going.append(cp)
        for k, (_, _, fdev, landing) in enumerate(forwards):
            remote(len(sends) + k, landing, landing, fdev).wait_recv()
        for cp in out_going:
            cp.wait_send()
        for cp in started:
            cp.wait()

    return pl.pallas_call(
        body, name=name,
        in_specs=[HBM_SPEC] * n,
        out_specs=[HBM_SPEC] * len(out_shapes),
        out_shape=out_shapes,
        scratch_shapes=[pltpu.SemaphoreType.DMA((n_remote,)), pltpu.SemaphoreType.DMA((n_remote,)),
                        pltpu.SemaphoreType.DMA((max(n_local, 1),))],
    )(*arrays)


def _layer_half(depth, which):
    assert depth % 2 == 0
    return pl.ds(which * (depth // 2), depth // 2)


def gather_chips(shards):
    n = len(shards)
    depth = shards[0].shape[0]

    def plan(ins, outs, place):
        x, y, c, chips = place
        me, sib = 2 * x + y, (x, y, 1 - c)
        mine, theirs = _layer_half(depth, c), _layer_half(depth, 1 - c)
        sends = [(ins[p].at[mine], outs[p].at[me, mine], (px, py, c)) for p in range(n) for (px, py) in chips]
        recvs = [(outs[p].at[2 * px + py, mine], (px, py, c)) for p in range(n) for (px, py) in chips]
        forwards = [(outs[p].at[2 * px + py, mine], outs[p].at[2 * px + py, mine], sib, outs[p].at[2 * px + py, theirs])
                    for p in range(n) for (px, py) in chips]
        return [], sends, recvs, forwards

    out_shapes = [jax.ShapeDtypeStruct((N_CHIPS,) + a.shape, a.dtype) for a in shards]
    return _exchange("gather_chips", shards, out_shapes, plan, 0, 6 * n)


def swap_halves(fulls):
    n = len(fulls)
    depth = fulls[0].shape[0]

    def plan(ins, outs, place):
        x, y, c, _ = place
        sib = (x, y, 1 - c)
        sends = [(ins[p].at[_layer_half(depth, 1 - c)], outs[p], sib) for p in range(n)]
        return [], sends, [(outs[p], sib) for p in range(n)], []

    out_shapes = [jax.ShapeDtypeStruct((depth // 2,) + a.shape[1:], a.dtype) for a in fulls]
    return _exchange("swap_halves", fulls, out_shapes, plan, 0, n)


def scatter_chips(fulls):
    n = len(fulls)

    def plan(ins, outs, place):
        x, y, c, chips = place
        sends = [(ins[p].at[:, 2 * px + py], outs[p].at[j], (px, py, c))
                 for p in range(n) for j, (px, py) in enumerate(chips)]
        recvs = [(outs[p].at[j], (px, py, c)) for p in range(n) for j, (px, py) in enumerate(chips)]
        return [], sends, recvs, []

    out_shapes = [jax.ShapeDtypeStruct((3, a.shape[0]) + a.shape[2:], a.dtype) for a in fulls]
    return _exchange("scatter_chips", fulls, out_shapes, plan, 0, 3 * n)


def swap_sibling(arrays):
    n = len(arrays)

    def plan(ins, outs, place):
        x, y, c, _ = place
        sib = (x, y, 1 - c)
        return [], [(ins[p], outs[p], sib) for p in range(n)], [(outs[p], sib) for p in range(n)], []

    return _exchange("swap_sibling", arrays, [jax.ShapeDtypeStruct(a.shape, a.dtype) for a in arrays], plan, 0, n)


def gather_all(vec):
    def plan(ins, outs, place):
        x, y, c, _ = place
        me = 4 * x + 2 * y + c
        flips = [(fx, fy, fc) for fx in (0, 1) for fy in (0, 1) for fc in (0, 1)][1:]
        peers = [(1 - x if fx else x, 1 - y if fy else y, 1 - c if fc else c) for fx, fy, fc in flips]
        local = [(ins[0], outs[0].at[me])]
        sends = [(ins[0], outs[0].at[me], dev) for dev in peers]
        recvs = [(outs[0].at[4 * dev[0] + 2 * dev[1] + dev[2]], dev) for dev in peers]
        return local, sends, recvs, []

    return _exchange("gather_all", [vec], [jax.ShapeDtypeStruct((8,) + vec.shape, vec.dtype)], plan, 1, 7)[0]


def _pick_rows(rows, cols, block_bytes=1024 * 1024):
    best = None
    for t in range(16, rows + 1, 16):
        if rows % t == 0 and t * cols * 4 <= block_bytes:
            best = t
    return best or rows


def sum_rows_of(own, stacked, tr, place, own_block, with_bf16=False):
    n, rows, cols = stacked.shape
    assert rows % tr == 0

    def body(place_ref, own_ref, st_ref, o_ref, *narrow):
        acc = own_ref[...]
        for k in range(n):
            acc = acc + st_ref[k].astype(F32)
        o_ref[...] = acc
        if with_bf16:
            narrow[0][...] = acc.astype(BF16)

    spec = pl.BlockSpec((tr, cols), lambda i, p: (i, 0))
    return pl.pallas_call(
        body, name="sum_rows_of",
        grid_spec=pltpu.PrefetchScalarGridSpec(
            num_scalar_prefetch=1, grid=(rows // tr,),
            in_specs=[pl.BlockSpec((tr, cols), lambda i, p: (own_block(i, p[0]), 0)),
                      pl.BlockSpec((n, tr, cols), lambda i, p: (0, i, 0))],
            out_specs=[spec, spec] if with_bf16 else spec),
        out_shape=([jax.ShapeDtypeStruct((rows, cols), F32), jax.ShapeDtypeStruct((rows, cols), BF16)]
                   if with_bf16 else jax.ShapeDtypeStruct((rows, cols), F32)),
        compiler_params=_params("parallel"),
    )(place.astype(jnp.int32).reshape(1), own, stacked)


def sum_stack(stacked):
    n, rows, cols = stacked.shape
    tr = _pick_rows(rows, cols)

    def body(st_ref, o_ref):
        acc = st_ref[0]
        for k in range(1, n):
            acc = acc + st_ref[k]
        o_ref[...] = acc

    return pl.pallas_call(
        body, name="sum_stack",
        grid=(rows // tr,),
        in_specs=[pl.BlockSpec((n, tr, cols), lambda i: (0, i, 0))],
        out_specs=pl.BlockSpec((tr, cols), lambda i: (i, 0)),
        out_shape=jax.ShapeDtypeStruct((rows, cols), F32),
        compiler_params=_params("parallel"),
    )(stacked)


def adamw(w, g_mine, m, v, g_theirs=None, core=None):
    rows, cols = w.shape
    tr = _pick_rows(g_mine.shape[0], cols)
    per_half = g_mine.shape[0] // tr
    c1 = 1.0 - ADAM_B1 ** ADAM_STEP
    c2 = 1.0 - ADAM_B2 ** ADAM_STEP
    halves = g_theirs is not None
    ins = [w, g_mine] + ([g_theirs] if halves else []) + [m, v]
    place = (core if halves else jnp.zeros((), jnp.int32)).astype(jnp.int32).reshape(1)

    def body(place_ref, *refs):
        w_ref, ga_ref = refs[0], refs[1]
        m_ref, v_ref, g_out, d_out, m_out, v_out = refs[-6:]
        g = ga_ref[...]
        if halves:
            g = jnp.where(pl.program_id(0) // per_half == place_ref[0], g, refs[2][...])
        mn = ADAM_B1 * m_ref[...] + (1.0 - ADAM_B1) * g
        vn = ADAM_B2 * v_ref[...] + (1.0 - ADAM_B2) * (g * g)
        m_hat = mn / c1
        v_hat = vn / c2
        g_out[...] = g
        d_out[...] = -ADAM_LR * (m_hat / (jnp.sqrt(v_hat) + ADAM_EPS) + ADAM_WD * w_ref[...])
        m_out[...] = mn
        v_out[...] = vn

    spec = pl.BlockSpec((tr, cols), lambda i, p: (i, 0))
    g_spec = pl.BlockSpec((tr, cols), lambda i, p: (i % per_half, 0))
    return pl.pallas_call(
        body, name="adamw",
        grid_spec=pltpu.PrefetchScalarGridSpec(
            num_scalar_prefetch=1, grid=(rows // tr,),
            in_specs=[spec] + [g_spec] * (2 if halves else 1) + [spec, spec],
            out_specs=[spec] * 4),
        out_shape=[jax.ShapeDtypeStruct((rows, cols), F32)] * 4,
        compiler_params=_params("parallel"),
    )(place, *ins)


WEIGHT_NAMES = ["ffn1_norm", "ffn1_w_gu", "ffn1_w_down", "mix_norm", "w_in", "q_lat_norm", "kv_lat_norm", "w_uq", "w_uk",
                "w_uv", "q_norm", "k_norm", "w_pool", "pool_scale", "w_out", "ffn2_norm", "ffn2_w_gu", "ffn2_w_down"]
COL_SHARDED = ["ffn1_w_gu", "w_in", "w_uq", "w_uk", "w_uv", "ffn2_w_gu"]
ROW_SHARDED = ["ffn1_w_down", "w_out", "ffn2_w_down"]
SHARDED = [n for n in WEIGHT_NAMES if n in COL_SHARDED or n in ROW_SHARDED]
REPLICATED = [n for n in WEIGHT_NAMES if n not in SHARDED]


def _join_chips(g, name):
    _, depth, r, c = g.shape
    if name in COL_SHARDED:
        return jnp.transpose(g, (1, 2, 0, 3)).reshape(depth, r, N_CHIPS * c)
    return jnp.transpose(g, (1, 0, 2, 3)).reshape(depth, N_CHIPS * r, c)


def _split_chips(full, name):
    depth, r, c = full.shape
    if name in COL_SHARDED:
        return jnp.transpose(full.reshape(depth, r, N_CHIPS, c // N_CHIPS), (0, 2, 1, 3))
    return full.reshape(depth, N_CHIPS, r // N_CHIPS, c)


def _pack(vs):
    flat = jnp.concatenate([v.reshape(-1) for v in vs])
    pad = (-flat.shape[0]) % (8 * 128)
    return jnp.pad(flat, (0, pad)).reshape(-1, 128)


def _unpack(packed, like):
    flat = packed.reshape(-1)
    out, at = [], 0
    for v in like:
        out.append(flat[at:at + v.size].reshape(v.shape))
        at += v.size
    return out


def kernel(x, ffn1_norm, ffn1_w_gu, ffn1_w_down, mix_norm, w_in, q_lat_norm, kv_lat_norm, w_uq, w_uk, w_uv, q_norm, k_norm, w_pool, pool_scale, w_out, ffn2_norm, ffn2_w_gu, ffn2_w_down, loss_target, m_ffn1_norm, m_ffn1_w_gu, m_ffn1_w_down, m_mix_norm, m_w_in, m_q_lat_norm, m_kv_lat_norm, m_w_uq, m_w_uk, m_w_uv, m_q_norm, m_k_norm, m_w_pool, m_pool_scale, m_w_out, m_ffn2_norm, m_ffn2_w_gu, m_ffn2_w_down, v_ffn1_norm, v_ffn1_w_gu, v_ffn1_w_down, v_mix_norm, v_w_in, v_q_lat_norm, v_kv_lat_norm, v_w_uq, v_w_uk, v_w_uv, v_q_norm, v_k_norm, v_w_pool, v_pool_scale, v_w_out, v_ffn2_norm, v_ffn2_w_gu, v_ffn2_w_down):
    args = locals()
    weights = {n: args[n] for n in WEIGHT_NAMES}
    moments_m = {n: args["m_" + n] for n in WEIGHT_NAMES}
    moments_v = {n: args["v_" + n] for n in WEIGHT_NAMES}

    chip = 2 * lax.axis_index("x") + lax.axis_index("y")
    core = lax.axis_index("c")
    shards = [weights[n].astype(BF16) for n in SHARDED]
    gathered = {n: lax.dynamic_update_index_in_dim(g, own, chip, 0)
                for n, own, g in zip(SHARDED, shards, gather_chips(shards))}
    full = {n: _join_chips(gathered[n], n) for n in SHARDED if "ffn" not in n}
    for n in REPLICATED:
        full[n] = weights[n].astype(BF16) if n == "w_pool" else weights[n]
    w = layout_weights(full)
    wd1 = _join_chips(gathered["ffn1_w_down"], "ffn1_w_down")
    wd2 = _join_chips(gathered["ffn2_w_down"], "ffn2_w_down")

    loss_part, grad_x, grads = local_step(x[0], loss_target[0], w, gathered["ffn1_w_gu"], wd1, gathered["ffn2_w_gu"], wd2)
    loss = lax.psum(loss_part[0, 0], ("x", "y", "c"))

    split = [grads[n] if n.endswith("w_gu") else _split_chips(grads[n], n) for n in SHARDED]
    half = split[0].shape[0] // 2

    def with_sibling(g4, other):
        cols = g4.shape[-1]
        rows = other.size // cols
        tr = _pick_rows(rows, cols)
        sums = sum_rows_of(g4.reshape(-1, cols), other.reshape(1, rows, cols), tr, core,
                           lambda i, c: i + c * (rows // tr), with_bf16=True)
        return [t.reshape(other.shape) for t in sums]

    def with_chips(wide, land):
        r, cols = wide.shape[-2:]
        tr = _pick_rows(r, cols)
        per = r // tr
        return sum_rows_of(wide.reshape(-1, cols), land.reshape(3, -1, cols), tr, chip,
                           lambda i, b: (i // per) * N_CHIPS * per + b * per + i % per)

    chip_sums = [with_sibling(g4, other) for g4, other in zip(split, swap_halves(split))]
    landed = scatter_chips([narrow for _, narrow in chip_sums])
    totals = [with_chips(wide, land) for (wide, _), land in zip(chip_sums, landed)]
    out = {}
    for n, mine, theirs in zip(SHARDED, totals, swap_sibling(totals)):
        shape = weights[n].shape
        flat = lambda a: a.reshape(shape[0] * shape[1], shape[2])
        res = adamw(flat(weights[n]), mine, flat(moments_m[n]), flat(moments_v[n]), g_theirs=theirs, core=core)
        out[n] = [r.reshape(shape) for r in res]

    rep_g = [grads[n] for n in REPLICATED]
    total = sum_stack(gather_all(_pack(rep_g)))
    res = adamw(_pack([weights[n] for n in REPLICATED]), total,
                _pack([moments_m[n] for n in REPLICATED]), _pack([moments_v[n] for n in REPLICATED]))
    unpacked = [_unpack(r, rep_g) for r in res]
    for k, n in enumerate(REPLICATED):
        out[n] = [u[k] for u in unpacked]

    return (loss, grad_x[None], *[out[n][0] for n in WEIGHT_NAMES], *[out[n][1] for n in WEIGHT_NAMES],
            *[out[n][2] for n in WEIGHT_NAMES], *[out[n][3] for n in WEIGHT_NAMES])
```
